```python
import math
import functools
import jax
import jax.numpy as jnp
from jax import lax
import numpy as np

D_MODEL = 1024
BATCH = 32
SEQ = 2048
DEPTH = 1
DEC_BATCH = 128
DEC_SEQ = 4
PAST_LEN = 8192
PAGE_SIZE = 128

D_ATTN = D_MODEL // 2
D_SSM = D_MODEL - D_ATTN
HEAD_DIM = 64
N_HEADS = D_ATTN // HEAD_DIM
DILATED_PATTERNS = ((128, 1), (512, 4), (2048, 16))
MAX_WINDOW = max(w for w, _ in DILATED_PATTERNS)
ATTN_BLOCK = 128
SSM_GROUP_CH = 16
SSM_GROUPS = D_SSM // SSM_GROUP_CH
SSM_STATE = 64
DT_MIN = 1e-3
DT_MAX = 1e-1
N_EXPERT_GROUPS = 4
EXPERTS_PER_GROUP = 8
N_EXPERTS = N_EXPERT_GROUPS * EXPERTS_PER_GROUP
TOP_K_FINE = 2
D_EXPERT = 512
MOE_BLOCK = 128
D_IN_PROJ = 3 * D_ATTN + D_SSM
EPS = 1e-6

kernel_name = 'hymba_dilated_s5_hmoe_step'


def rms_norm(x, g):
    xf = x.astype(jnp.float32)
    y = xf * lax.rsqrt(jnp.mean(xf * xf, axis=-1, keepdims=True) + EPS)
    return y * g.astype(jnp.float32)


def adaln(c, w, b):
    m = jnp.einsum('bd,de->be', jax.nn.silu(c.astype(jnp.float32)), w.astype(jnp.float32)) + b.astype(jnp.float32)
    shift, scale, gate = jnp.split(m[:, None, :], 3, axis=-1)
    return shift, scale, gate


def _attend(s, v, spec):
    m = jnp.max(s, axis=-1, keepdims=True)
    p = jnp.exp(s - m)
    l = jnp.sum(p, axis=-1)
    o = jnp.einsum(spec, p, v.astype(jnp.float32)) / l[..., None]
    return o, m[..., 0] + jnp.log(l)


def _combine_branches(outs, lses):
    w = jax.nn.softmax(jnp.stack(lses, axis=0), axis=0)
    return jnp.sum(w[..., None] * jnp.stack(outs, axis=0), axis=0)


def _to_subsequences(t, dil):
    b, s, h, dh = t.shape
    return t.reshape(b, s // dil, dil, h, dh).transpose(0, 2, 1, 3, 4).reshape(b * dil, s // dil, h, dh)


def _from_subsequences(t, b, dil):
    n, ls = t.shape[:2]
    rest = t.shape[2:]
    t = t.reshape((b, dil, ls) + rest)
    t = jnp.moveaxis(t, 1, 2)
    return t.reshape((b, ls * dil) + rest)


def _banded_subsequence_attention(q, k, v, n_keys):
    n, L, h, dh = q.shape
    blk = ATTN_BLOCK
    nblk = -(-L // blk)
    lp = nblk * blk
    qb = jnp.pad(q, ((0, 0), (0, lp - L), (0, 0), (0, 0))).reshape(n, nblk, blk, h, dh)

    def band(t):
        tp = jnp.pad(t, ((0, 0), (blk, lp - L), (0, 0), (0, 0)))
        prev = tp[:, :lp].reshape(n, nblk, blk, h, dh)
        cur = tp[:, blk:].reshape(n, nblk, blk, h, dh)
        return jnp.concatenate([prev, cur], axis=2)

    kb = band(k)
    vb = band(v)
    qq = np.arange(blk)[:, None]
    kk = np.arange(2 * blk)[None, :]
    dist = qq + blk - kk
    key_pos = np.arange(nblk)[:, None, None] * blk + kk[None] - blk
    mask = (dist >= 0) & (dist < n_keys) & (key_pos >= 0)
    s = jnp.einsum('nbqhd,nbkhd->nbqhk', qb, kb, preferred_element_type=jnp.float32) * (HEAD_DIM ** -0.5)
    s = jnp.where(jnp.asarray(mask)[None, :, :, None, :], s, -jnp.inf)
    o, lse = _attend(s, vb, 'nbqhk,nbkhd->nbqhd')
    return o.reshape(n, lp, h, dh)[:, :L], lse.reshape(n, lp, h)[:, :L]


def dilated_attention_prompt(q, k, v):
    b = q.shape[0]
    outs, lses = [], []
    for window, dil in DILATED_PATTERNS:
        o, lse = _banded_subsequence_attention(_to_subsequences(q, dil), _to_subsequences(k, dil),
                                               _to_subsequences(v, dil), window // dil + 1)
        outs.append(_from_subsequences(o, b, dil))
        lses.append(_from_subsequences(lse, b, dil))
    return _combine_branches(outs, lses)


def dilated_attention_sample(q, k, v, k_buf, v_buf):
    b, t, h, dh = q.shape
    w = k_buf.shape[1]
    k_ext = jnp.concatenate([k_buf.astype(k.dtype), k], axis=1)
    v_ext = jnp.concatenate([v_buf.astype(v.dtype), v], axis=1)
    outs, lses = [], []
    for window, dil in DILATED_PATTERNS:
        nk = window // dil + 1
        idx = w + np.arange(t)[:, None] - dil * np.arange(nk)[None, :]
        valid = jnp.asarray(idx >= 0)
        flat = jnp.asarray(np.clip(idx, 0, None).reshape(-1))
        kg = jnp.take(k_ext, flat, axis=1).reshape(b, t, nk, h, dh)
        vg = jnp.take(v_ext, flat, axis=1).reshape(b, t, nk, h, dh)
        s = jnp.einsum('bthd,btkhd->bthk', q, kg, preferred_element_type=jnp.float32) * (HEAD_DIM ** -0.5)
        s = jnp.where(valid[None, :, None, :], s, -jnp.inf)
        o, lse = _attend(s, vg, 'bthk,btkhd->bthd')
        outs.append(o)
        lses.append(lse)
    return _combine_branches(outs, lses)


def _scan_combine(e1, e2):
    a1, b1 = e1
    a2, b2 = e2
    return a1 * a2, a2 * b1 + b2


def s5_mixer(u, h0_re, h0_im, lam_re, lam_im, log_dt, b_re, b_im, c_re, c_im, d, w_glu, b_glu):
    f32 = jnp.float32
    bn, sn = u.shape[:2]
    lam = lax.complex(lam_re.astype(f32), lam_im.astype(f32))
    dt = jnp.exp(log_dt.astype(f32))[:, None]
    a_bar = jnp.exp(lam * dt)
    b_bar = ((a_bar - 1.0) / lam)[..., None] * lax.complex(b_re.astype(f32), b_im.astype(f32))
    ug = u.astype(f32).reshape(bn, sn, SSM_GROUPS, SSM_GROUP_CH)
    bu = jnp.einsum('gpc,bsgc->bsgp', b_bar, ug.astype(jnp.complex64))
    a_seq = jnp.broadcast_to(a_bar, (1, sn) + a_bar.shape)
    a_cum, hs = lax.associative_scan(_scan_combine, (a_seq, bu), axis=1)
    if h0_re is not None:
        hs = hs + a_cum * lax.complex(h0_re.astype(f32), h0_im.astype(f32))[:, None]
    c_mat = lax.complex(c_re.astype(f32), c_im.astype(f32))
    y = jnp.einsum('gcp,bsgp->bsgc', c_mat, hs).real + d.astype(f32) * ug
    z = jax.nn.gelu(y.reshape(bn, sn, D_SSM))
    out = z * jax.nn.sigmoid(jnp.einsum('bsc,ce->bse', z, w_glu.astype(f32)) + b_glu.astype(f32))
    h_last = hs[:, -1]
    return out, h_last.real, h_last.imag


def grouped_swiglu(xf, eid, gates, w_gate, w_up, w_down):
    t, dm = xf.shape
    n = t * TOP_K_FINE
    flat_e = eid.reshape(-1).astype(jnp.int32)
    flat_tok = jnp.repeat(jnp.arange(t, dtype=jnp.int32), TOP_K_FINE)
    flat_g = gates.reshape(-1)
    order = jnp.argsort(flat_e)
    se = flat_e[order]
    stok = flat_tok[order]
    sg = flat_g[order]
    counts = jnp.zeros((N_EXPERTS,), jnp.int32).at[flat_e].add(1)
    start = jnp.cumsum(counts) - counts
    padded = (counts + MOE_BLOCK - 1) // MOE_BLOCK * MOE_BLOCK
    pend = jnp.cumsum(padded)
    pstart = pend - padded
    dest = pstart[se] + (jnp.arange(n, dtype=jnp.int32) - start[se])
    nblk = -(-n // MOE_BLOCK) + N_EXPERTS
    xs = jnp.zeros((nblk * MOE_BLOCK, dm), xf.dtype).at[dest].set(xf[stok])
    blk_e = jnp.minimum(jnp.searchsorted(pend, jnp.arange(nblk, dtype=jnp.int32) * MOE_BLOCK, side='right'),
                        N_EXPERTS - 1)

    def expert_block(args):
        xb, e = args
        hid = jax.nn.silu(xb @ w_gate[e]) * (xb @ w_up[e])
        return hid @ w_down[e]

    ys = lax.map(expert_block, (xs.reshape(nblk, MOE_BLOCK, dm), blk_e)).reshape(nblk * MOE_BLOCK, dm)
    contrib = ys[dest].astype(jnp.float32) * sg[:, None]
    return jax.ops.segment_sum(contrib, stok, num_segments=t)


def hier_moe(h, w_rc, b_rc, w_rf, b_rf, w_gate, w_up, w_down):
    f32 = jnp.float32
    t = h.shape[0]
    hf = h.astype(f32)
    p_coarse = jax.nn.softmax(hf @ w_rc.astype(f32) + b_rc.astype(f32), axis=-1)
    grp = jnp.argmax(p_coarse, axis=-1)
    p_grp = jnp.max(p_coarse, axis=-1)
    logits_f = (hf @ w_rf.astype(f32) + b_rf.astype(f32)).reshape(t, N_EXPERT_GROUPS, EXPERTS_PER_GROUP)
    lf = logits_f[jnp.arange(t), grp]
    top_v, top_i = lax.top_k(lf, TOP_K_FINE)
    gates = p_grp[:, None] * jax.nn.softmax(top_v, axis=-1)
    eid = grp[:, None] * EXPERTS_PER_GROUP + top_i
    return grouped_swiglu(h, eid, gates, w_gate, w_up, w_down)


def layer_forward(x, c, attn_fn, h0_re, h0_im, lw):
    dt = x.dtype
    bn, sn, _ = x.shape
    shift, scale, gate = adaln(c, lw['w_ada_mix'], lw['b_ada_mix'])
    h = (rms_norm(x, lw['g_mix']) * (1.0 + scale) + shift).astype(dt)
    proj = jnp.einsum('bsd,de->bse', h, lw['w_in'])
    q, k, v, u = jnp.split(proj, [D_ATTN, 2 * D_ATTN, 3 * D_ATTN], axis=-1)
    q = q.reshape(bn, sn, N_HEADS, HEAD_DIM)
    k = k.reshape(bn, sn, N_HEADS, HEAD_DIM)
    v = v.reshape(bn, sn, N_HEADS, HEAD_DIM)
    o_attn = attn_fn(q, k, v).reshape(bn, sn, D_ATTN)
    o_ssm, hl_re, hl_im = s5_mixer(u, h0_re, h0_im, lw['lam_re'], lw['lam_im'], lw['log_dt'], lw['b_re'],
                                   lw['b_im'], lw['c_re'], lw['c_im'], lw['d'], lw['w_glu'], lw['b_glu'])
    merged = jnp.concatenate([rms_norm(o_attn, lw['g_attn_out']), rms_norm(o_ssm, lw['g_ssm_out'])],
                             axis=-1).astype(dt)
    x = (x + gate * jnp.einsum('bse,ed->bsd', merged, lw['w_out'])).astype(dt)
    shift, scale, gate = adaln(c, lw['w_ada_ffn'], lw['b_ada_ffn'])
    h = (rms_norm(x, lw['g_ffn']) * (1.0 + scale) + shift).astype(dt)
    y = hier_moe(h.reshape(bn * sn, D_MODEL), lw['w_rc'], lw['b_rc'], lw['w_rf'], lw['b_rf'],
                 lw['w_gate'], lw['w_up'], lw['w_down']).reshape(bn, sn, D_MODEL)
    x = (x + gate * y).astype(dt)
    return x, k, v, hl_re, hl_im


def setup_inputs(seed: int = 0) -> dict:
    key = jax.random.key(seed)
    ks = iter(jax.random.split(key, 48))
    f32 = jnp.float32

    def nrm(shape, s):
        return jax.random.normal(next(ks), shape, f32) * s

    L = DEPTH
    D = D_MODEL
    wbuf = min(MAX_WINDOW, PAST_LEN)
    n_idx = jnp.arange(SSM_STATE, dtype=f32)
    return {
        'x_prompt': nrm((BATCH, SEQ, D), 1.0),
        'x_sample': nrm((DEC_BATCH, DEC_SEQ, D), 1.0),
        'cache_k_win': nrm((L, DEC_BATCH, wbuf, N_HEADS, HEAD_DIM), 1.0),
        'cache_v_win': nrm((L, DEC_BATCH, wbuf, N_HEADS, HEAD_DIM), 1.0),
        'state_ssm_re': nrm((L, DEC_BATCH, SSM_GROUPS, SSM_STATE), 0.5),
        'state_ssm_im': nrm((L, DEC_BATCH, SSM_GROUPS, SSM_STATE), 0.5),
        'c_prompt': nrm((BATCH, D), 1.0),
        'c_sample': nrm((DEC_BATCH, D), 1.0),
        'g_mix': 1.0 + nrm((L, D), 0.02),
        'w_ada_mix': nrm((L, D, 3 * D), 0.5 * D ** -0.5),
        'b_ada_mix': nrm((L, 3 * D), 0.02),
        'w_in': nrm((L, D, D_IN_PROJ), D ** -0.5),
        'w_out': nrm((L, D_ATTN + D_SSM, D), (D_ATTN + D_SSM) ** -0.5),
        'g_attn_out': 1.0 + nrm((L, D_ATTN), 0.02),
        'g_ssm_out': 1.0 + nrm((L, D_SSM), 0.02),
        'ssm_lambda_re': -0.5 + nrm((L, SSM_GROUPS, SSM_STATE), 0.01),
        'ssm_lambda_im': math.pi * n_idx + nrm((L, SSM_GROUPS, SSM_STATE), 0.01),
        'ssm_log_dt': jax.random.uniform(next(ks), (L, SSM_GROUPS), f32, math.log(DT_MIN), math.log(DT_MAX)),
        'ssm_b_re': nrm((L, SSM_GROUPS, SSM_STATE, SSM_GROUP_CH), (2 * SSM_GROUP_CH) ** -0.5),
        'ssm_b_im': nrm((L, SSM_GROUPS, SSM_STATE, SSM_GROUP_CH), (2 * SSM_GROUP_CH) ** -0.5),
        'ssm_c_re': nrm((L, SSM_GROUPS, SSM_GROUP_CH, SSM_STATE), (2 * SSM_STATE) ** -0.5),
        'ssm_c_im': nrm((L, SSM_GROUPS, SSM_GROUP_CH, SSM_STATE), (2 * SSM_STATE) ** -0.5),
        'ssm_d': nrm((L, SSM_GROUPS, SSM_GROUP_CH), 0.5),
        'w_glu': nrm((L, D_SSM, D_SSM), D_SSM ** -0.5),
        'b_glu': nrm((L, D_SSM), 0.02),
        'g_ffn': 1.0 + nrm((L, D), 0.02),
        'w_ada_ffn': nrm((L, D, 3 * D), 0.5 * D ** -0.5),
        'b_ada_ffn': nrm((L, 3 * D), 0.02),
        'w_router_coarse': nrm((L, D, N_EXPERT_GROUPS), D ** -0.5),
        'b_router_coarse': nrm((L, N_EXPERT_GROUPS), 0.01),
        'w_router_fine': nrm((L, D, N_EXPERTS), D ** -0.5),
        'b_router_fine': nrm((L, N_EXPERTS), 0.01),
        'w_expert_gate': nrm((L, N_EXPERTS, D, D_EXPERT), D ** -0.5),
        'w_expert_up': nrm((L, N_EXPERTS, D, D_EXPERT), D ** -0.5),
        'w_expert_down': nrm((L, N_EXPERTS, D_EXPERT, D), D_EXPERT ** -0.5),
        'g_final': 1.0 + nrm((D,), 0.02),
    }


def reference(x_prompt, x_sample, cache_k_win, cache_v_win, state_ssm_re, state_ssm_im, c_prompt, c_sample,
              g_mix, w_ada_mix, b_ada_mix, w_in, w_out, g_attn_out, g_ssm_out,
              ssm_lambda_re, ssm_lambda_im, ssm_log_dt, ssm_b_re, ssm_b_im, ssm_c_re, ssm_c_im, ssm_d,
              w_glu, b_glu, g_ffn, w_ada_ffn, b_ada_ffn,
              w_router_coarse, b_router_coarse, w_router_fine, b_router_fine,
              w_expert_gate, w_expert_up, w_expert_down, g_final):
    keep = min(MAX_WINDOW, x_prompt.shape[1])
    xp = x_prompt
    xs = x_sample
    kp_rows, vp_rows, srp, sip = [], [], [], []
    ks_rows, vs_rows, srs, sis = [], [], [], []
    for l in range(DEPTH):
        lw = {
            'g_mix': g_mix[l], 'w_ada_mix': w_ada_mix[l], 'b_ada_mix': b_ada_mix[l],
            'w_in': w_in[l], 'w_out': w_out[l], 'g_attn_out': g_attn_out[l], 'g_ssm_out': g_ssm_out[l],
            'lam_re': ssm_lambda_re[l], 'lam_im': ssm_lambda_im[l], 'log_dt': ssm_log_dt[l],
            'b_re': ssm_b_re[l], 'b_im': ssm_b_im[l], 'c_re': ssm_c_re[l], 'c_im': ssm_c_im[l], 'd': ssm_d[l],
            'w_glu': w_glu[l], 'b_glu': b_glu[l],
            'g_ffn': g_ffn[l], 'w_ada_ffn': w_ada_ffn[l], 'b_ada_ffn': b_ada_ffn[l],
            'w_rc': w_router_coarse[l], 'b_rc': b_router_coarse[l],
            'w_rf': w_router_fine[l], 'b_rf': b_router_fine[l],
            'w_gate': w_expert_gate[l], 'w_up': w_expert_up[l], 'w_down': w_expert_down[l],
        }
        xp, kp, vp, hr, hi = layer_forward(xp, c_prompt, dilated_attention_prompt, None, None, lw)
        kp_rows.append(kp[:, kp.shape[1] - keep:])
        vp_rows.append(vp[:, vp.shape[1] - keep:])
        srp.append(hr)
        sip.append(hi)
        attn_s = functools.partial(dilated_attention_sample, k_buf=cache_k_win[l], v_buf=cache_v_win[l])
        xs, ksn, vsn, hr, hi = layer_forward(xs, c_sample, attn_s, state_ssm_re[l], state_ssm_im[l], lw)
        ks_rows.append(ksn)
        vs_rows.append(vsn)
        srs.append(hr)
        sis.append(hi)
    y_prompt = rms_norm(xp, g_final).astype(x_prompt.dtype)
    y_sample = rms_norm(xs, g_final).astype(x_sample.dtype)
    return (y_prompt, y_sample, jnp.stack(kp_rows), jnp.stack(vp_rows), jnp.stack(srp), jnp.stack(sip),
            jnp.stack(ks_rows), jnp.stack(vs_rows), jnp.stack(srs), jnp.stack(sis))
```

```python
import functools
import math

import numpy as np
import jax
import jax.numpy as jnp
from jax import lax
from jax.experimental import pallas as pl
from jax.experimental.pallas import tpu as pltpu

F32 = jnp.float32
BF16 = jnp.bfloat16
HIGHEST = lax.Precision.HIGHEST

D_MODEL = 1024
D_ATTN = 512
D_SSM = 512
HEAD_DIM = 64
N_HEADS = 8
DILATED_PATTERNS = ((128, 1), (512, 4), (2048, 16))
SSM_GROUP_CH = 16
SSM_GROUPS = 32
SSM_STATE = 64
N_EXPERT_GROUPS = 4
EXPERTS_PER_GROUP = 8
N_EXPERTS = 32
D_EXPERT = 512
D_IN_PROJ = 3 * D_ATTN + D_SSM
EPS = 1e-6

LANES = 128
SUBLANES = 8
VMEM_LIMIT = 48 * 1024 * 1024

ATTN_BLOCK = 128
NEG_BIG = -1e30
SSM_TT = 128
SSM_PITCH = SSM_TT + 8
ROUTER_COLS = 128
MOE_TILE = 256


def _cparams(sem, vmem=VMEM_LIMIT):
    return pltpu.CompilerParams(dimension_semantics=sem, vmem_limit_bytes=vmem)


def _adaln_kernel(c_ref, w_ref, b_ref, o_ref):
    c = c_ref[...]
    s = c * jax.nn.sigmoid(c)
    o_ref[...] = jnp.dot(s, w_ref[...], precision=HIGHEST, preferred_element_type=F32) + b_ref[...]


def _adaln(c, w, b):
    r, d = c.shape
    n = w.shape[1]
    tn = 768
    return pl.pallas_call(
        _adaln_kernel,
        grid=(n // tn,),
        in_specs=[pl.BlockSpec((r, d), lambda j: (0, 0)),
                  pl.BlockSpec((d, tn), lambda j: (0, j)),
                  pl.BlockSpec((1, tn), lambda j: (0, j))],
        out_specs=pl.BlockSpec((r, tn), lambda j: (0, j)),
        out_shape=jax.ShapeDtypeStruct((r, n), F32),
        compiler_params=_cparams(("arbitrary",)),
    )(c, w, b.reshape(1, n))


def _inproj_kernel(x_ref, sh_ref, sc_ref, g_ref, w_ref, q_ref, k_ref, v_ref, u_ref):
    x = x_ref[...]
    ms = jnp.mean(x * x, axis=-1, keepdims=True)
    h = x * lax.rsqrt(ms + EPS) * g_ref[...]
    h = h * (1.0 + sc_ref[...]) + sh_ref[...]
    p = jnp.dot(h.astype(BF16), w_ref[...], preferred_element_type=F32)
    q_ref[...] = p[:, :D_ATTN] * (HEAD_DIM ** -0.5)
    k_ref[...] = p[:, D_ATTN:2 * D_ATTN]
    v_ref[...] = p[:, 2 * D_ATTN:3 * D_ATTN]
    u_ref[...] = p[:, 3 * D_ATTN:]


def _mod_spec(mod, tm):
    d = mod.shape[-1]
    if mod.shape[1] == 1:
        return pl.BlockSpec((None, 1, d), lambda b, i: (b, 0, 0))
    return pl.BlockSpec((None, tm, d), lambda b, i: (b, i, 0))


def _inproj(x, shift, scale, g, w_bf16, tm):
    nb, s, d = x.shape
    row = lambda n: pl.BlockSpec((None, tm, n), lambda b, i: (b, i, 0))
    out = jax.ShapeDtypeStruct((nb, s, D_ATTN), F32)
    return pl.pallas_call(
        _inproj_kernel,
        grid=(nb, s // tm),
        in_specs=[row(d), _mod_spec(shift, tm), _mod_spec(scale, tm),
                  pl.BlockSpec((1, d), lambda b, i: (0, 0)),
                  pl.BlockSpec((d, D_IN_PROJ), lambda b, i: (0, 0))],
        out_specs=[row(D_ATTN), row(D_ATTN), row(D_ATTN), row(D_SSM)],
        out_shape=[out, out, out, jax.ShapeDtypeStruct((nb, s, D_SSM), F32)],
        compiler_params=_cparams(("parallel", "parallel")),
    )(x, shift, scale, g.reshape(1, d), w_bf16)


def _band_bias():
    qi = np.arange(ATTN_BLOCK)[:, None]
    kj = np.arange(ATTN_BLOCK)[None, :]
    cur = kj <= qi
    prev = kj >= qi
    to_bias = lambda m: np.where(m, 0.0, NEG_BIG).astype(np.float32)
    return to_bias(np.concatenate([prev, cur], axis=1)), to_bias(cur)


def _attn_prompt_kernel(q_ref, k_ref, v_ref, bias2_ref, bias1_ref, o_ref, acc_scr, m_scr, l_scr):
    s_len = q_ref.shape[0]
    lane = lax.broadcasted_iota(jnp.int32, (1, LANES), 1)
    head0 = lane < HEAD_DIM

    def rows(ref, start, n, d):
        if d == 1:
            return ref[pl.ds(start, n), :]
        return ref[pl.ds(start, n, stride=d), :]

    def tile(br, d, qstart, first):
        qrows = rows(q_ref, qstart, ATTN_BLOCK, d)
        if first:
            kstart, nk, bias = qstart, ATTN_BLOCK, bias1_ref[...]
        else:
            kstart, nk, bias = qstart - d * ATTN_BLOCK, 2 * ATTN_BLOCK, bias2_ref[...]
        kb = rows(k_ref, kstart, nk, d).astype(BF16)
        vb = rows(v_ref, kstart, nk, d).astype(BF16)
        parts = []
        for hh in range(2):
            msk = head0 if hh == 0 else jnp.logical_not(head0)
            qh = jnp.where(msk, qrows, 0.0).astype(BF16)
            s = lax.dot_general(qh, kb, (((1,), (1,)), ((), ())), preferred_element_type=F32) + bias
            m = jnp.max(s, axis=-1, keepdims=True)
            p = jnp.exp(s - m)
            l = jnp.sum(p, axis=-1, keepdims=True)
            pv = jnp.dot(p.astype(BF16), vb, preferred_element_type=F32)
            parts.append((pv, m, l))
        acc = jnp.where(head0, parts[0][0], parts[1][0])
        m = jnp.where(head0, parts[0][1], parts[1][1])
        l = jnp.where(head0, parts[0][2], parts[1][2])
        if d == 1:
            dst = pl.ds(qstart, ATTN_BLOCK)
        else:
            dst = pl.ds(qstart, ATTN_BLOCK, stride=d)
        acc_scr[br, dst, :] = acc
        m_scr[br, dst, :] = m
        l_scr[br, dst, :] = l

    for br, (window, d) in enumerate(DILATED_PATTERNS):
        assert window // d == ATTN_BLOCK
        nblk = s_len // (d * ATTN_BLOCK)

        def first_body(r, carry, br=br, d=d):
            tile(br, d, r, True)
            return carry

        if d == 1:
            tile(br, d, 0, True)
        else:
            lax.fori_loop(0, d, first_body, 0)
        if nblk > 1:
            def rest_body(n, carry, br=br, d=d, nblk=nblk):
                r = n // (nblk - 1)
                jb = 1 + n % (nblk - 1)
                qstart = r + d * ATTN_BLOCK * jb
                if d == 1:
                    qstart = pl.multiple_of(qstart, ATTN_BLOCK)
                tile(br, d, qstart, False)
                return carry

            lax.fori_loop(0, d * (nblk - 1), rest_body, 0)

    ch = 256

    def merge_body(i, carry):
        sl = pl.ds(pl.multiple_of(i * ch, ch), ch)
        ms = [m_scr[br, sl, :] for br in range(3)]
        mx = jnp.maximum(jnp.maximum(ms[0], ms[1]), ms[2])
        num = jnp.zeros((ch, LANES), F32)
        den = jnp.zeros((ch, LANES), F32)
        for br in range(3):
            e = jnp.exp(ms[br] - mx)
            num = num + e * acc_scr[br, sl, :]
            den = den + e * l_scr[br, sl, :]
        o_ref[sl, :] = num / den
        return carry

    lax.fori_loop(0, s_len // ch, merge_body, 0)


def _attn_prompt(q, k, v):
    nb, s, _ = q.shape
    assert s % (16 * ATTN_BLOCK) == 0
    bias2, bias1 = _band_bias()
    blk = pl.BlockSpec((None, s, LANES), lambda b, p: (b, 0, p))
    const = lambda a: pl.BlockSpec(a.shape, lambda b, p: (0, 0))
    return pl.pallas_call(
        _attn_prompt_kernel,
        grid=(nb, D_ATTN // LANES),
        in_specs=[blk, blk, blk, const(bias2), const(bias1)],
        out_specs=blk,
        out_shape=jax.ShapeDtypeStruct((nb, s, D_ATTN), F32),
        scratch_shapes=[pltpu.VMEM((3, s, LANES), F32)] * 3,
        compiler_params=_cparams(("parallel", "parallel")),
    )(q, k, v, jnp.asarray(bias2), jnp.asarray(bias1))


def _sample_bias(t_new, w_buf):
    t = np.arange(t_new)[:, None]
    idx = np.arange(w_buf + t_new)[None, :]
    dist = w_buf + t - idx
    mult = np.zeros(dist.shape, np.int64)
    for window, d in DILATED_PATTERNS:
        mult += ((dist >= 0) & (dist % d == 0) & (dist <= window)).astype(np.int64)
    bias = np.where(mult > 0, np.log(np.maximum(mult, 1)), NEG_BIG)
    bias = np.concatenate([bias, bias], axis=0).astype(np.float32)
    return bias[:, :w_buf], bias[:, w_buf:]


def _attn_sample_kernel(q_ref, kn_ref, vn_ref, kc_ref, vc_ref, bias_ref, biasn_ref, o_ref):
    t_new = q_ref.shape[0]
    lane = lax.broadcasted_iota(jnp.int32, (1, LANES), 1)
    head0 = lane < HEAD_DIM
    bias_n = biasn_ref[...]
    for p in range(D_ATTN // LANES):
        cols = slice(p * LANES, (p + 1) * LANES)
        q2 = q_ref[:, cols]
        qq = jnp.concatenate([jnp.where(head0, q2, 0.0), jnp.where(head0, 0.0, q2)], axis=0)
        kc = kc_ref[:, cols].astype(BF16)
        vc = vc_ref[:, cols].astype(BF16)
        kn = kn_ref[:, cols]
        vn = vn_ref[:, cols]
        s_c = lax.dot_general(qq.astype(BF16), kc, (((1,), (1,)), ((), ())),
                              preferred_element_type=F32) + bias_ref[...]
        qq_r = qq.astype(BF16).astype(F32)
        s_n = []
        for j in range(t_new):
            kj = kn[j:j + 1, :].astype(BF16).astype(F32)
            sj = jnp.sum(qq_r * kj, axis=-1, keepdims=True)
            s_n.append(sj + bias_n[:, j:j + 1])
        m = jnp.max(s_c, axis=-1, keepdims=True)
        for sj in s_n:
            m = jnp.maximum(m, sj)
        p_c = jnp.exp(s_c - m)
        l = jnp.sum(p_c, axis=-1, keepdims=True)
        acc = jnp.dot(p_c.astype(BF16), vc, preferred_element_type=F32)
        for j, sj in enumerate(s_n):
            pj = jnp.exp(sj - m)
            l = l + pj
            acc = acc + pj * vn[j:j + 1, :]
        o = acc / l
        o_ref[:, cols] = jnp.where(head0, o[:t_new], o[t_new:])


def _attn_sample(q, k_new, v_new, cache_k, cache_v):
    nb, t_new, _ = q.shape
    w_buf = cache_k.shape[1]
    bias, bias_new = _sample_bias(t_new, w_buf)
    new = pl.BlockSpec((None, t_new, D_ATTN), lambda b: (b, 0, 0))
    buf = pl.BlockSpec((None, w_buf, D_ATTN), lambda b: (b, 0, 0))
    const = lambda a: pl.BlockSpec(a.shape, lambda b: (0, 0))
    return pl.pallas_call(
        _attn_sample_kernel,
        grid=(nb,),
        in_specs=[new, new, new, buf, buf, const(bias), const(bias_new)],
        out_specs=new,
        out_shape=jax.ShapeDtypeStruct((nb, t_new, D_ATTN), F32),
        compiler_params=_cparams(("parallel",)),
    )(q, k_new, v_new, cache_k, cache_v, jnp.asarray(bias), jnp.asarray(bias_new))


def _s5_params(lam_re, lam_im, log_dt, b_re, b_im, c_re, c_im):
    f32 = F32
    dt = jnp.exp(log_dt.astype(f32))[:, None]
    lr, li = lam_re.astype(f32), lam_im.astype(f32)
    ea = jnp.exp(lr * dt)
    a_re, a_im = ea * jnp.cos(li * dt), ea * jnp.sin(li * dt)
    den = lr * lr + li * li
    co_re = ((a_re - 1.0) * lr + a_im * li) / den
    co_im = (a_im * lr - (a_re - 1.0) * li) / den
    bb_re = co_re[..., None] * b_re - co_im[..., None] * b_im
    bb_im = co_re[..., None] * b_im + co_im[..., None] * b_re
    eye = jnp.eye(8, dtype=f32)

    def b_blocks(bb):
        t = bb.reshape(4, 8, SSM_STATE, SSM_GROUP_CH)
        return jnp.einsum('ab,kapc->kacbp', eye, t).reshape(4, 8 * SSM_GROUP_CH, 8 * SSM_STATE)

    def c_blocks(cc):
        t = cc.reshape(4, 8, SSM_GROUP_CH, SSM_STATE)
        return jnp.einsum('ab,kacp->kbpac', eye, t).reshape(4, 8 * SSM_STATE, 8 * SSM_GROUP_CH)

    b_mat = jnp.concatenate([b_blocks(bb_re), b_blocks(bb_im)], axis=2).astype(BF16)
    c_mat = jnp.concatenate([c_blocks(c_re.astype(f32)), -c_blocks(c_im.astype(f32))], axis=1).astype(BF16)
    return a_re.reshape(-1), a_im.reshape(-1), b_mat, c_mat


def _gelu_tanh(y):
    return 0.5 * y * (1.0 + jnp.tanh(math.sqrt(2.0 / math.pi) * (y + 0.044715 * (y * y * y))))


def _s5_epilogue(y, u, d_ref, wglu_ref, bglu_ref, g_ref):
    y = y + d_ref[...] * u
    z = _gelu_tanh(y)
    gate = jnp.dot(z.astype(BF16), wglu_ref[...], preferred_element_type=F32) + bglu_ref[...]
    out = z * jax.nn.sigmoid(gate)
    ms = jnp.mean(out * out, axis=-1, keepdims=True)
    return out * lax.rsqrt(ms + EPS) * g_ref[...]


def _s5_prompt_kernel(u_ref, are_ref, aim_ref, b_ref, c_ref, d_ref, wglu_ref, bglu_ref, g_ref,
                      o_ref, hre_ref, him_ref, scr, hst):
    nseq, tt, _ = u_ref.shape
    n_slab = D_SSM * SSM_STATE // SSM_GROUP_CH // LANES
    ti = pl.program_id(1)

    @pl.when(ti == 0)
    def _():
        hst[...] = jnp.zeros_like(hst)

    u2 = u_ref[...].reshape(nseq * tt, D_SSM)
    ub = u2.astype(BF16)
    for kc in range(4):
        bu = jnp.dot(ub[:, kc * LANES:(kc + 1) * LANES], b_ref[kc], preferred_element_type=F32)
        for part in range(2):
            for j in range(4):
                col = part * 512 + j * LANES
                for b in range(nseq):
                    scr[part * n_slab + 4 * kc + j, b * SSM_PITCH:b * SSM_PITCH + tt, :] = (
                        bu[b * tt:(b + 1) * tt, col:col + LANES])

    grp = 4
    for sg in range(n_slab // grp):
        slabs = [sg * grp + i for i in range(grp)]
        ar = [are_ref[s] for s in slabs]
        ai = [aim_ref[s] for s in slabs]

        def step(t, carry, slabs=slabs, ar=ar, ai=ai):
            hr, hi = carry
            nhr, nhi = [], []
            for i, s in enumerate(slabs):
                sel = pl.ds(t, nseq, stride=SSM_PITCH)
                br = scr[s, sel, :]
                bi = scr[n_slab + s, sel, :]
                r = ar[i] * hr[i] - ai[i] * hi[i] + br
                im = ar[i] * hi[i] + ai[i] * hr[i] + bi
                scr[s, sel, :] = r
                scr[n_slab + s, sel, :] = im
                nhr.append(r)
                nhi.append(im)
            return tuple(nhr), tuple(nhi)

        init = (tuple(hst[s] for s in slabs), tuple(hst[n_slab + s] for s in slabs))
        hr, hi = lax.fori_loop(0, tt, step, init, unroll=4)
        for i, s in enumerate(slabs):
            hst[s] = hr[i]
            hst[n_slab + s] = hi[i]

    ys = []
    for kc in range(4):
        slabs = [4 * kc + j for j in range(4)] + [n_slab + 4 * kc + j for j in range(4)]
        lhs = jnp.concatenate(
            [jnp.concatenate([scr[s, b * SSM_PITCH:b * SSM_PITCH + tt, :].astype(BF16) for s in slabs], axis=1)
             for b in range(nseq)], axis=0)
        ys.append(jnp.dot(lhs, c_ref[kc], preferred_element_type=F32))
    y = jnp.concatenate(ys, axis=1)
    o_ref[...] = _s5_epilogue(y, u2, d_ref, wglu_ref, bglu_ref, g_ref).reshape(nseq, tt, D_SSM)

    @pl.when(ti == pl.num_programs(1) - 1)
    def _():
        for s in range(n_slab):
            hre_ref[:, s * LANES:(s + 1) * LANES] = hst[s]
            him_ref[:, s * LANES:(s + 1) * LANES] = hst[n_slab + s]


def _s5_prompt(u, a_re, a_im, b_mat, c_mat, d, w_glu_bf16, b_glu, g_out):
    nb, s, _ = u.shape
    nseq = SUBLANES
    assert nb % nseq == 0 and s % SSM_TT == 0
    n_state = a_re.shape[0]
    n_slab = n_state // LANES
    bcast = lambda a: jnp.broadcast_to(a.reshape(n_slab, 1, LANES), (n_slab, nseq, LANES))
    const = lambda a: pl.BlockSpec(a.shape, lambda b, i: (0,) * a.ndim)
    args = (u, bcast(a_re), bcast(a_im), b_mat, c_mat, d.reshape(1, D_SSM), w_glu_bf16,
            b_glu.reshape(1, D_SSM), g_out.reshape(1, D_SSM))
    st = jax.ShapeDtypeStruct((nb, n_state), F32)
    return pl.pallas_call(
        _s5_prompt_kernel,
        grid=(nb // nseq, s // SSM_TT),
        in_specs=[pl.BlockSpec((nseq, SSM_TT, D_SSM), lambda b, i: (b, i, 0))] + [const(a) for a in args[1:]],
        out_specs=[pl.BlockSpec((nseq, SSM_TT, D_SSM), lambda b, i: (b, i, 0)),
                   pl.BlockSpec((nseq, n_state), lambda b, i: (b, 0)),
                   pl.BlockSpec((nseq, n_state), lambda b, i: (b, 0))],
        out_shape=[jax.ShapeDtypeStruct((nb, s, D_SSM), F32), st, st],
        scratch_shapes=[pltpu.VMEM((2 * n_slab, nseq * SSM_PITCH, LANES), F32),
                        pltpu.VMEM((2 * n_slab, nseq, LANES), F32)],
        compiler_params=_cparams(("parallel", "arbitrary")),
    )(*args)


def _s5_sample_kernel(u_ref, h0re_ref, h0im_ref, are_ref, aim_ref, b_ref, c_ref, d_ref, wglu_ref, bglu_ref, g_ref,
                      o_ref, hre_ref, him_ref):
    t_new = u_ref.shape[0]
    hre = h0re_ref[...]
    him = h0im_ref[...]
    are, aim = are_ref[...], aim_ref[...]
    half = 4 * LANES
    for t in range(t_new):
        u = u_ref[t]
        ub = u.astype(BF16)
        bus = [jnp.dot(ub[:, kc * LANES:(kc + 1) * LANES], b_ref[kc], preferred_element_type=F32) for kc in range(4)]
        bre = jnp.concatenate([bu[:, :half] for bu in bus], axis=1)
        bim = jnp.concatenate([bu[:, half:] for bu in bus], axis=1)
        hre, him = are * hre - aim * him + bre, are * him + aim * hre + bim
        ys = []
        for kc in range(4):
            lhs = jnp.concatenate([hre[:, kc * half:(kc + 1) * half], him[:, kc * half:(kc + 1) * half]], axis=1)
            ys.append(jnp.dot(lhs.astype(BF16), c_ref[kc], preferred_element_type=F32))
        y = jnp.concatenate(ys, axis=1)
        o_ref[t] = _s5_epilogue(y, u, d_ref, wglu_ref, bglu_ref, g_ref)
    hre_ref[...] = hre
    him_ref[...] = him


def _s5_sample(u_tm, h0_re, h0_im, a_re, a_im, b_mat, c_mat, d, w_glu_bf16, b_glu, g_out):
    t_new, nb, _ = u_tm.shape
    n_state = a_re.shape[0]
    tb = 64
    assert nb % tb == 0
    const = lambda a: pl.BlockSpec(a.shape, lambda b: (0,) * a.ndim)
    args = (u_tm, h0_re, h0_im, a_re.reshape(1, n_state), a_im.reshape(1, n_state), b_mat, c_mat,
            d.reshape(1, D_SSM), w_glu_bf16, b_glu.reshape(1, D_SSM), g_out.reshape(1, D_SSM))
    st_spec = pl.BlockSpec((tb, n_state), lambda b: (b, 0))
    st = jax.ShapeDtypeStruct((nb, n_state), F32)
    return pl.pallas_call(
        _s5_sample_kernel,
        grid=(nb // tb,),
        in_specs=[pl.BlockSpec((t_new, tb, D_SSM), lambda b: (0, b, 0)), st_spec, st_spec]
                 + [const(a) for a in args[3:]],
        out_specs=[pl.BlockSpec((t_new, tb, D_SSM), lambda b: (0, b, 0)), st_spec, st_spec],
        out_shape=[jax.ShapeDtypeStruct((t_new, nb, D_SSM), F32), st, st],
        compiler_params=_cparams(("parallel",)),
    )(*args)


def _outproj_kernel(x_ref, oa_ref, os_ref, gate_ref, sh_ref, sc_ref, ga_ref, gf_ref, w_ref, wr_hi_ref, wr_lo_ref,
                    br_ref, x1_ref, h_ref, lg_ref):
    oa = oa_ref[...]
    ms = jnp.mean(oa * oa, axis=-1, keepdims=True)
    na = oa * lax.rsqrt(ms + EPS) * ga_ref[...]
    merged = jnp.concatenate([na, os_ref[...]], axis=-1).astype(BF16)
    x1 = x_ref[...] + gate_ref[...] * jnp.dot(merged, w_ref[...], preferred_element_type=F32)
    x1_ref[...] = x1
    ms = jnp.mean(x1 * x1, axis=-1, keepdims=True)
    h = x1 * lax.rsqrt(ms + EPS) * gf_ref[...]
    h = h * (1.0 + sc_ref[...]) + sh_ref[...]
    h_hi = h.astype(BF16)
    h_ref[...] = h_hi
    h_lo = (h - h_hi.astype(F32)).astype(BF16)
    lg = jnp.dot(h_hi, wr_hi_ref[...], preferred_element_type=F32)
    lg = lg + jnp.dot(h_hi, wr_lo_ref[...], preferred_element_type=F32)
    lg = lg + jnp.dot(h_lo, wr_hi_ref[...], preferred_element_type=F32)
    lg_ref[...] = lg + br_ref[...]


def _outproj(x, o_attn, o_ssm, gate, shift, scale, g_attn, g_ffn, w_out_bf16, w_router, b_router, tm):
    nb, s, d = x.shape
    row = lambda n: pl.BlockSpec((None, tm, n), lambda b, i: (b, i, 0))
    const = lambda a: pl.BlockSpec(a.shape, lambda b, i: (0,) * a.ndim)
    wr_hi = w_router.astype(BF16)
    wr_lo = (w_router - wr_hi.astype(F32)).astype(BF16)
    consts = (g_attn.reshape(1, D_ATTN), g_ffn.reshape(1, d), w_out_bf16, wr_hi, wr_lo,
              b_router.reshape(1, ROUTER_COLS))
    return pl.pallas_call(
        _outproj_kernel,
        grid=(nb, s // tm),
        in_specs=[row(d), row(D_ATTN), row(D_SSM), _mod_spec(gate, tm), _mod_spec(shift, tm), _mod_spec(scale, tm)]
                 + [const(a) for a in consts],
        out_specs=[row(d), row(d), row(ROUTER_COLS)],
        out_shape=[jax.ShapeDtypeStruct((nb, s, d), F32), jax.ShapeDtypeStruct((nb, s, d), BF16),
                   jax.ShapeDtypeStruct((nb, s, ROUTER_COLS), F32)],
        compiler_params=_cparams(("parallel", "parallel")),
    )(x, o_attn, o_ssm, gate, shift, scale, *consts)


def _expert_kernel(blk_e_ref, nvalid_ref, x_ref, wg_ref, wu_ref, wd_ref, y_ref):
    i = pl.program_id(0)

    @pl.when(i < nvalid_ref[0])
    def _():
        x = x_ref[...]
        a = jnp.dot(x, wg_ref[...], preferred_element_type=F32)
        b = jnp.dot(x, wu_ref[...], preferred_element_type=F32)
        hid = (a * jax.nn.sigmoid(a) * b).astype(BF16)
        y_ref[...] = jnp.dot(hid, wd_ref[...], preferred_element_type=F32).astype(y_ref.dtype)

    @pl.when(i >= nvalid_ref[0])
    def _():
        y_ref[...] = jnp.zeros_like(y_ref)


def _experts(xs, blk_e, nvalid, w_gate, w_up, w_down):
    n, d = xs.shape
    nblk = n // MOE_TILE
    wspec = lambda shp: pl.BlockSpec((None,) + shp, lambda i, be, nv: (be[i], 0, 0))
    return pl.pallas_call(
        _expert_kernel,
        grid_spec=pltpu.PrefetchScalarGridSpec(
            num_scalar_prefetch=2,
            grid=(nblk,),
            in_specs=[pl.BlockSpec((MOE_TILE, d), lambda i, be, nv: (i, 0)),
                      wspec((d, D_EXPERT)), wspec((d, D_EXPERT)), wspec((D_EXPERT, d))],
            out_specs=pl.BlockSpec((MOE_TILE, d), lambda i, be, nv: (i, 0)),
        ),
        out_shape=jax.ShapeDtypeStruct((n, d), BF16),
        compiler_params=_cparams(("arbitrary",)),
    )(blk_e, nvalid, xs, w_gate, w_up, w_down)


def _combine_kernel(x1_ref, y0_ref, y1_ref, gts_ref, gate_ref, gfin_ref, o_ref):
    gts = gts_ref[...]
    moe = gts[:, 0:1] * y0_ref[...].astype(F32) + gts[:, 1:2] * y1_ref[...].astype(F32)
    x2 = x1_ref[...] + gate_ref[...] * moe
    ms = jnp.mean(x2 * x2, axis=-1, keepdims=True)
    o_ref[...] = x2 * lax.rsqrt(ms + EPS) * gfin_ref[...]


def _combine(x1, y0, y1, gts, gate, g_final, tm):
    nb, s, d = x1.shape
    row = lambda n: pl.BlockSpec((None, tm, n), lambda b, i: (b, i, 0))
    return pl.pallas_call(
        _combine_kernel,
        grid=(nb, s // tm),
        in_specs=[row(d), row(d), row(d), row(2), _mod_spec(gate, tm), pl.BlockSpec((1, d), lambda b, i: (0, 0))],
        out_specs=row(d),
        out_shape=jax.ShapeDtypeStruct((nb, s, d), F32),
        compiler_params=_cparams(("parallel", "parallel")),
    )(x1, y0, y1, gts, gate, g_final.reshape(1, d))


def _route(logits):
    t = logits.shape[0]
    lc = logits[:, :N_EXPERT_GROUPS]
    lf = logits[:, N_EXPERT_GROUPS:N_EXPERT_GROUPS + N_EXPERTS].reshape(t, N_EXPERT_GROUPS, EXPERTS_PER_GROUP)
    p_coarse = jax.nn.softmax(lc, axis=-1)
    grp = jnp.argmax(p_coarse, axis=-1)
    p_grp = jnp.max(p_coarse, axis=-1)
    lfg = jnp.take_along_axis(lf, grp[:, None, None], axis=1)[:, 0]
    top_v, top_i = lax.top_k(lfg, 2)
    gates = p_grp[:, None] * jax.nn.softmax(top_v, axis=-1)
    eid = grp[:, None].astype(jnp.int32) * EXPERTS_PER_GROUP + top_i.astype(jnp.int32)
    return eid, gates


def _dispatch_slots(eid):
    t = eid.shape[0]
    flat_e = eid.reshape(-1)
    onehot = (flat_e[:, None] == jnp.arange(N_EXPERTS, dtype=jnp.int32)[None, :]).astype(jnp.int32)
    rank = jnp.sum((jnp.cumsum(onehot, axis=0) - onehot) * onehot, axis=1)
    counts = jnp.sum(onehot, axis=0)
    padded = (counts + MOE_TILE - 1) // MOE_TILE * MOE_TILE
    pend = jnp.cumsum(padded)
    pstart = pend - padded
    slot = (pstart[flat_e] + rank).astype(jnp.int32)
    nblk = -(-2 * t // MOE_TILE) + N_EXPERTS
    blk_start = jnp.arange(nblk, dtype=jnp.int32) * MOE_TILE
    blk_e = jnp.minimum(jnp.searchsorted(pend, blk_start, side='right'), N_EXPERTS - 1).astype(jnp.int32)
    nvalid = (pend[-1] // MOE_TILE).astype(jnp.int32).reshape(1)
    return slot.reshape(t, 2), blk_e, nvalid, nblk


def _moe(h_bf16, logits, w_gate, w_up, w_down):
    t, d = h_bf16.shape
    eid, gates = _route(logits)
    slot, blk_e, nvalid, nblk = _dispatch_slots(eid)
    src = jnp.zeros((nblk * MOE_TILE,), jnp.int32).at[slot.reshape(-1)].set(
        jnp.repeat(jnp.arange(t, dtype=jnp.int32), 2))
    xs = jnp.take(h_bf16, src, axis=0)
    ys = _experts(xs, blk_e, nvalid, w_gate, w_up, w_down)
    y0 = jnp.take(ys, slot[:, 0], axis=0)
    y1 = jnp.take(ys, slot[:, 1], axis=0)
    return y0, y1, gates


def _layer(x, mods_mix, mods_ffn, attn_fn, s5_fn, wts, tm):
    nb, s, d = x.shape
    q, k, v, u = _inproj(x, mods_mix[0], mods_mix[1], wts['g_mix'], wts['w_in'], tm)
    o_attn = attn_fn(q, k, v)
    o_ssm, h_re, h_im = s5_fn(u)
    x1, h2, logits = _outproj(x, o_attn, o_ssm, mods_mix[2], mods_ffn[0], mods_ffn[1], wts['g_attn_out'],
                              wts['g_ffn'], wts['w_out'], wts['w_router'], wts['b_router'], tm)
    y0, y1, gates = _moe(h2.reshape(nb * s, d), logits.reshape(nb * s, ROUTER_COLS),
                         wts['w_gate'], wts['w_up'], wts['w_down'])
    y = _combine(x1, y0.reshape(nb, s, d), y1.reshape(nb, s, d), gates.reshape(nb, s, 2), mods_ffn[2],
                 wts['g_final'], tm)
    return y, k, v, h_re, h_im


def kernel(x_prompt, x_sample, cache_k_win, cache_v_win, state_ssm_re, state_ssm_im, c_prompt, c_sample, g_mix, w_ada_mix, b_ada_mix, w_in, w_out, g_attn_out, g_ssm_out, ssm_lambda_re, ssm_lambda_im, ssm_log_dt, ssm_b_re, ssm_b_im, ssm_c_re, ssm_c_im, ssm_d, w_glu, b_glu, g_ffn, w_ada_ffn, b_ada_ffn, w_router_coarse, b_router_coarse, w_router_fine, b_router_fine, w_expert_gate, w_expert_up, w_expert_down, g_final):
    depth = g_mix.shape[0]
    assert depth == 1, "single-layer step"
    l = 0
    bp, sp, d = x_prompt.shape
    bs, ts, _ = x_sample.shape
    keep = min(max(w for w, _ in DILATED_PATTERNS), sp)

    c_all = jnp.concatenate([c_prompt, c_sample], axis=0).astype(F32)
    m_mix = _adaln(c_all, w_ada_mix[l], b_ada_mix[l])
    m_ffn = _adaln(c_all, w_ada_ffn[l], b_ada_ffn[l])

    def mods(m, lo, hi, per_token):
        parts = jnp.split(m[lo:hi], 3, axis=-1)
        if per_token:
            return tuple(jnp.repeat(p, ts, axis=0)[None] for p in parts)
        return tuple(p[:, None, :] for p in parts)

    pad = ROUTER_COLS - N_EXPERT_GROUPS - N_EXPERTS
    w_router = jnp.concatenate([w_router_coarse[l], w_router_fine[l], jnp.zeros((d, pad), F32)], axis=1)
    b_router = jnp.concatenate([b_router_coarse[l], b_router_fine[l], jnp.zeros((pad,), F32)])
    wts = {
        'g_mix': g_mix[l], 'w_in': w_in[l].astype(BF16), 'w_out': w_out[l].astype(BF16),
        'g_attn_out': g_attn_out[l], 'g_ffn': g_ffn[l], 'w_router': w_router, 'b_router': b_router,
        'w_gate': w_expert_gate[l].astype(BF16), 'w_up': w_expert_up[l].astype(BF16),
        'w_down': w_expert_down[l].astype(BF16), 'g_final': g_final,
    }
    a_re, a_im, b_mat, c_mat = _s5_params(ssm_lambda_re[l], ssm_lambda_im[l], ssm_log_dt[l], ssm_b_re[l],
                                          ssm_b_im[l], ssm_c_re[l], ssm_c_im[l])
    s5_w = (a_re, a_im, b_mat, c_mat, ssm_d[l].reshape(-1), w_glu[l].astype(BF16), b_glu[l], g_ssm_out[l])

    yp, kp, vp, hrp, hip = _layer(
        x_prompt, mods(m_mix, 0, bp, False), mods(m_ffn, 0, bp, False), _attn_prompt,
        lambda u: _s5_prompt(u, *s5_w), wts, tm=512)

    n_state = SSM_GROUPS * SSM_STATE
    ck = cache_k_win[l].reshape(bs, -1, D_ATTN)
    cv = cache_v_win[l].reshape(bs, -1, D_ATTN)

    def attn_s(q, k, v):
        r = lambda a: a.reshape(bs, ts, D_ATTN)
        return _attn_sample(r(q), r(k), r(v), ck, cv).reshape(1, bs * ts, D_ATTN)

    def s5_s(u):
        u_tm = u.reshape(bs, ts, D_SSM).transpose(1, 0, 2)
        o, hr, hi = _s5_sample(u_tm, state_ssm_re[l].reshape(bs, n_state), state_ssm_im[l].reshape(bs, n_state),
                               *s5_w)
        return o.transpose(1, 0, 2).reshape(1, bs * ts, D_SSM), hr, hi

    ys, ks, vs, hrs, his = _layer(
        x_sample.reshape(1, bs * ts, d), mods(m_mix, bp, bp + bs, True), mods(m_ffn, bp, bp + bs, True),
        attn_s, s5_s, wts, tm=bs * ts)

    heads = lambda a, b, s: a.reshape(1, b, s, N_HEADS, HEAD_DIM)
    state = lambda a, b: a.reshape(1, b, SSM_GROUPS, SSM_STATE)
    return (yp, ys.reshape(bs, ts, d),
            heads(kp[:, sp - keep:], bp, keep), heads(vp[:, sp - keep:], bp, keep), state(hrp, bp), state(hip, bp),
            heads(ks, bs, ts), heads(vs, bs, ts), state(hrs, bs), state(his, bs))
```

```python
import functools
import math

import numpy as np
import jax
import jax.numpy as jnp
from jax import lax
from jax.experimental import pallas as pl
from jax.experimental.pallas import tpu as pltpu

F32 = jnp.float32
BF16 = jnp.bfloat16
HIGHEST = lax.Precision.HIGHEST

D_MODEL = 1024
D_ATTN = 512
D_SSM = 512
HEAD_DIM = 64
N_HEADS = 8
DILATED_PATTERNS = ((128, 1), (512, 4), (2048, 16))
SSM_GROUP_CH = 16
SSM_GROUPS = 32
SSM_STATE = 64
N_EXPERT_GROUPS = 4
EXPERTS_PER_GROUP = 8
N_EXPERTS = 32
D_EXPERT = 512
D_IN_PROJ = 3 * D_ATTN + D_SSM
EPS = 1e-6

LANES = 128
SUBLANES = 8
VMEM_LIMIT = 48 * 1024 * 1024

ATTN_BLOCK = 128
ATTN_UNROLL = 8
NEG_BIG = -1e30
SSM_TT = 128
SSM_PITCH = SSM_TT + 8
ROUTER_COLS = 128
MOE_TILE = 256


def _cparams(sem, vmem=VMEM_LIMIT):
    return pltpu.CompilerParams(dimension_semantics=sem, vmem_limit_bytes=vmem)


def _adaln_kernel(c_ref, w_ref, b_ref, o_ref):
    c = c_ref[...]
    s = c * jax.nn.sigmoid(c)
    o_ref[...] = jnp.dot(s, w_ref[...], precision=HIGHEST, preferred_element_type=F32) + b_ref[...]


def _adaln(c, w, b):
    r, d = c.shape
    n = w.shape[1]
    tn = 768
    return pl.pallas_call(
        _adaln_kernel,
        grid=(n // tn,),
        in_specs=[pl.BlockSpec((r, d), lambda j: (0, 0)),
                  pl.BlockSpec((d, tn), lambda j: (0, j)),
                  pl.BlockSpec((1, tn), lambda j: (0, j))],
        out_specs=pl.BlockSpec((r, tn), lambda j: (0, j)),
        out_shape=jax.ShapeDtypeStruct((r, n), F32),
        compiler_params=_cparams(("arbitrary",)),
    )(c, w, b.reshape(1, n))


def _inproj_kernel(x_ref, sh_ref, sc_ref, g_ref, w_ref, q_ref, k_ref, v_ref, u_ref):
    x = x_ref[...]
    ms = jnp.mean(x * x, axis=-1, keepdims=True)
    h = x * lax.rsqrt(ms + EPS) * g_ref[...]
    h = h * (1.0 + sc_ref[...]) + sh_ref[...]
    p = jnp.dot(h.astype(BF16), w_ref[...], preferred_element_type=F32)
    q_ref[...] = p[:, :D_ATTN] * (HEAD_DIM ** -0.5)
    k_ref[...] = p[:, D_ATTN:2 * D_ATTN]
    v_ref[...] = p[:, 2 * D_ATTN:3 * D_ATTN]
    u_ref[...] = p[:, 3 * D_ATTN:]


def _mod_spec(mod, tm):
    d = mod.shape[-1]
    if mod.shape[1] == 1:
        return pl.BlockSpec((None, 1, d), lambda b, i: (b, 0, 0))
    return pl.BlockSpec((None, tm, d), lambda b, i: (b, i, 0))


def _inproj(x, shift, scale, g, w_bf16, tm):
    nb, s, d = x.shape
    row = lambda n: pl.BlockSpec((None, tm, n), lambda b, i: (b, i, 0))
    out = jax.ShapeDtypeStruct((nb, s, D_ATTN), F32)
    return pl.pallas_call(
        _inproj_kernel,
        grid=(nb, s // tm),
        in_specs=[row(d), _mod_spec(shift, tm), _mod_spec(scale, tm),
                  pl.BlockSpec((1, d), lambda b, i: (0, 0)),
                  pl.BlockSpec((d, D_IN_PROJ), lambda b, i: (0, 0))],
        out_specs=[row(D_ATTN), row(D_ATTN), row(D_ATTN), row(D_SSM)],
        out_shape=[out, out, out, jax.ShapeDtypeStruct((nb, s, D_SSM), F32)],
        compiler_params=_cparams(("parallel", "parallel")),
    )(x, shift, scale, g.reshape(1, d), w_bf16)


def _band_bias():
    qi = np.arange(ATTN_BLOCK)[:, None]
    kj = np.arange(ATTN_BLOCK)[None, :]
    cur = kj <= qi
    prev = kj >= qi
    to_bias = lambda m: np.tile(np.where(m, 0.0, NEG_BIG).astype(np.float32), (2, 1))
    return to_bias(np.concatenate([prev, cur], axis=1)), to_bias(cur)


def _attn_prompt_kernel(q_ref, k_ref, v_ref, bias2_ref, bias1_ref, o_ref, acc_scr, m_scr, l_scr):
    s_len = q_ref.shape[0]
    lane = lax.broadcasted_iota(jnp.int32, (1, LANES), 1)
    head0 = lane < HEAD_DIM

    def rows(ref, start, n, d):
        if d == 1:
            return ref[pl.ds(start, n), :]
        return ref[pl.ds(start, n, stride=d), :]

    def tiles(items, first):
        nk = ATTN_BLOCK if first else 2 * ATTN_BLOCK
        bias = bias1_ref[...] if first else bias2_ref[...]
        one = jnp.ones((), BF16)
        loaded = []
        for _, d, qstart in items:
            kstart = qstart if first else qstart - d * ATTN_BLOCK
            loaded.append((rows(q_ref, qstart, ATTN_BLOCK, d), rows(k_ref, kstart, nk, d),
                           rows(v_ref, kstart, nk, d)))
        results = []
        for qrows, krows, vrows in loaded:
            kb = krows.astype(BF16)
            vb = vrows.astype(BF16)
            q2 = jnp.concatenate([jnp.where(head0, qrows, 0.0), jnp.where(head0, 0.0, qrows)], axis=0).astype(BF16)
            s = lax.dot_general(q2, kb, (((1,), (1,)), ((), ())), preferred_element_type=F32) + bias
            m = jnp.max(s, axis=-1, keepdims=True)
            p = jnp.exp(s - m).astype(BF16)
            pv0 = jnp.dot(p[:ATTN_BLOCK], jnp.where(head0, vb, one), preferred_element_type=F32)
            pv1 = jnp.dot(p[ATTN_BLOCK:], jnp.where(head0, one, vb), preferred_element_type=F32)
            results.append((jnp.where(head0, pv0, pv1), jnp.where(head0, m[:ATTN_BLOCK], m[ATTN_BLOCK:]),
                            jnp.where(head0, pv1, pv0)))
        for (br, d, qstart), (acc, m, l) in zip(items, results):
            dst = pl.ds(qstart, ATTN_BLOCK) if d == 1 else pl.ds(qstart, ATTN_BLOCK, stride=d)
            acc_scr[br, dst, :] = acc
            m_scr[br, dst, :] = m
            l_scr[br, dst, :] = l

    def largest_divisor(n):
        return max(f for f in range(1, ATTN_UNROLL + 1) if n % f == 0)

    grouped = []
    for br, (window, d) in enumerate(DILATED_PATTERNS):
        assert window // d == ATTN_BLOCK
        if d <= ATTN_UNROLL:
            grouped += [(br, d, r) for r in range(d)]
        else:
            un = largest_divisor(d)

            def first_body(i, carry, br=br, d=d, un=un):
                tiles([(br, d, i * un + j) for j in range(un)], True)
                return carry

            lax.fori_loop(0, d // un, first_body, 0)
    for i in range(0, len(grouped), ATTN_UNROLL):
        tiles(grouped[i:i + ATTN_UNROLL], True)

    for br, (window, d) in enumerate(DILATED_PATTERNS):
        nblk = s_len // (d * ATTN_BLOCK)
        n_rest = d * (nblk - 1)
        if n_rest:
            un = largest_divisor(n_rest)

            def rest_body(i, carry, br=br, d=d, nblk=nblk, un=un):
                items = []
                for j in range(un):
                    n = i * un + j
                    qstart = n // (nblk - 1) + d * ATTN_BLOCK * (1 + n % (nblk - 1))
                    items.append((br, d, pl.multiple_of(qstart, ATTN_BLOCK) if d == 1 else qstart))
                tiles(items, False)
                return carry

            lax.fori_loop(0, n_rest // un, rest_body, 0)

    ch = 256

    def merge_body(i, carry):
        sl = pl.ds(pl.multiple_of(i * ch, ch), ch)
        ms = [m_scr[br, sl, :] for br in range(3)]
        mx = jnp.maximum(jnp.maximum(ms[0], ms[1]), ms[2])
        num = jnp.zeros((ch, LANES), F32)
        den = jnp.zeros((ch, LANES), F32)
        for br in range(3):
            e = jnp.exp(ms[br] - mx)
            num = num + e * acc_scr[br, sl, :]
            den = den + e * pltpu.roll(l_scr[br, sl, :], HEAD_DIM, 1)
        o_ref[sl, :] = num / den
        return carry

    lax.fori_loop(0, s_len // ch, merge_body, 0)


def _attn_prompt(q, k, v):
    nb, s, _ = q.shape
    assert s % (16 * ATTN_BLOCK) == 0
    bias2, bias1 = _band_bias()
    blk = pl.BlockSpec((None, s, LANES), lambda b, p: (b, 0, p))
    const = lambda a: pl.BlockSpec(a.shape, lambda b, p: (0, 0))
    return pl.pallas_call(
        _attn_prompt_kernel,
        grid=(nb, D_ATTN // LANES),
        in_specs=[blk, blk, blk, const(bias2), const(bias1)],
        out_specs=blk,
        out_shape=jax.ShapeDtypeStruct((nb, s, D_ATTN), F32),
        scratch_shapes=[pltpu.VMEM((3, s, LANES), F32)] * 3,
        compiler_params=_cparams(("parallel", "parallel")),
    )(q, k, v, jnp.asarray(bias2), jnp.asarray(bias1))


def _sample_bias(t_new, w_buf):
    t = np.arange(t_new)[:, None]
    idx = np.arange(w_buf + t_new)[None, :]
    dist = w_buf + t - idx
    mult = np.zeros(dist.shape, np.int64)
    for window, d in DILATED_PATTERNS:
        mult += ((dist >= 0) & (dist % d == 0) & (dist <= window)).astype(np.int64)
    bias = np.where(mult > 0, np.log(np.maximum(mult, 1)), NEG_BIG)
    bias = np.concatenate([bias, bias], axis=0).astype(np.float32)
    return bias[:, :w_buf], bias[:, w_buf:]


def _attn_sample_kernel(q_ref, kn_ref, vn_ref, kc_ref, vc_ref, bias_ref, biasn_ref, o_ref):
    t_new = q_ref.shape[0]
    lane = lax.broadcasted_iota(jnp.int32, (1, LANES), 1)
    head0 = lane < HEAD_DIM
    bias_n = biasn_ref[...]
    for p in range(D_ATTN // LANES):
        cols = slice(p * LANES, (p + 1) * LANES)
        q2 = q_ref[:, cols]
        qq = jnp.concatenate([jnp.where(head0, q2, 0.0), jnp.where(head0, 0.0, q2)], axis=0)
        kc = kc_ref[:, cols].astype(BF16)
        vc = vc_ref[:, cols].astype(BF16)
        kn = kn_ref[:, cols]
        vn = vn_ref[:, cols]
        s_c = lax.dot_general(qq.astype(BF16), kc, (((1,), (1,)), ((), ())),
                              preferred_element_type=F32) + bias_ref[...]
        qq_r = qq.astype(BF16).astype(F32)
        s_n = []
        for j in range(t_new):
            kj = kn[j:j + 1, :].astype(BF16).astype(F32)
            sj = jnp.sum(qq_r * kj, axis=-1, keepdims=True)
            s_n.append(sj + bias_n[:, j:j + 1])
        m = jnp.max(s_c, axis=-1, keepdims=True)
        for sj in s_n:
            m = jnp.maximum(m, sj)
        p_c = jnp.exp(s_c - m)
        l = jnp.sum(p_c, axis=-1, keepdims=True)
        acc = jnp.dot(p_c.astype(BF16), vc, preferred_element_type=F32)
        for j, sj in enumerate(s_n):
            pj = jnp.exp(sj - m)
            l = l + pj
            acc = acc + pj * vn[j:j + 1, :]
        o = acc / l
        o_ref[:, cols] = jnp.where(head0, o[:t_new], o[t_new:])


def _attn_sample(q, k_new, v_new, cache_k, cache_v):
    nb, t_new, _ = q.shape
    w_buf = cache_k.shape[1]
    bias, bias_new = _sample_bias(t_new, w_buf)
    new = pl.BlockSpec((None, t_new, D_ATTN), lambda b: (b, 0, 0))
    buf = pl.BlockSpec((None, w_buf, D_ATTN), lambda b: (b, 0, 0))
    const = lambda a: pl.BlockSpec(a.shape, lambda b: (0, 0))
    return pl.pallas_call(
        _attn_sample_kernel,
        grid=(nb,),
        in_specs=[new, new, new, buf, buf, const(bias), const(bias_new)],
        out_specs=new,
        out_shape=jax.ShapeDtypeStruct((nb, t_new, D_ATTN), F32),
        compiler_params=_cparams(("parallel",)),
    )(q, k_new, v_new, cache_k, cache_v, jnp.asarray(bias), jnp.asarray(bias_new))


def _s5_params(lam_re, lam_im, log_dt, b_re, b_im, c_re, c_im):
    f32 = F32
    dt = jnp.exp(log_dt.astype(f32))[:, None]
    lr, li = lam_re.astype(f32), lam_im.astype(f32)
    ea = jnp.exp(lr * dt)
    a_re, a_im = ea * jnp.cos(li * dt), ea * jnp.sin(li * dt)
    den = lr * lr + li * li
    co_re = ((a_re - 1.0) * lr + a_im * li) / den
    co_im = (a_im * lr - (a_re - 1.0) * li) / den
    bb_re = co_re[..., None] * b_re - co_im[..., None] * b_im
    bb_im = co_re[..., None] * b_im + co_im[..., None] * b_re
    eye = jnp.eye(8, dtype=f32)

    def b_blocks(bb):
        t = bb.reshape(4, 8, SSM_STATE, SSM_GROUP_CH)
        return jnp.einsum('ab,kapc->kacbp', eye, t).reshape(4, 8 * SSM_GROUP_CH, 8 * SSM_STATE)

    def c_blocks(cc):
        t = cc.reshape(4, 8, SSM_GROUP_CH, SSM_STATE)
        return jnp.einsum('ab,kacp->kbpac', eye, t).reshape(4, 8 * SSM_STATE, 8 * SSM_GROUP_CH)

    b_mat = jnp.concatenate([b_blocks(bb_re), b_blocks(bb_im)], axis=2).astype(BF16)
    c_mat = jnp.concatenate([c_blocks(c_re.astype(f32)), -c_blocks(c_im.astype(f32))], axis=1).astype(BF16)
    return a_re.reshape(-1), a_im.reshape(-1), b_mat, c_mat


def _gelu_tanh(y):
    return 0.5 * y * (1.0 + jnp.tanh(math.sqrt(2.0 / math.pi) * (y + 0.044715 * (y * y * y))))


def _s5_epilogue(y, u, d_ref, wglu_ref, bglu_ref, g_ref):
    y = y + d_ref[...] * u
    z = _gelu_tanh(y)
    gate = jnp.dot(z.astype(BF16), wglu_ref[...], preferred_element_type=F32) + bglu_ref[...]
    out = z * jax.nn.sigmoid(gate)
    ms = jnp.mean(out * out, axis=-1, keepdims=True)
    return out * lax.rsqrt(ms + EPS) * g_ref[...]


def _s5_prompt_kernel(u_ref, are_ref, aim_ref, b_ref, c_ref, d_ref, wglu_ref, bglu_ref, g_ref,
                      o_ref, hre_ref, him_ref, scr, hst):
    nseq, tt, _ = u_ref.shape
    n_slab = D_SSM * SSM_STATE // SSM_GROUP_CH // LANES
    ti = pl.program_id(1)

    @pl.when(ti == 0)
    def _():
        hst[...] = jnp.zeros_like(hst)

    u2 = u_ref[...].reshape(nseq * tt, D_SSM)
    ub = u2.astype(BF16)
    for kc in range(4):
        bu = jnp.dot(ub[:, kc * LANES:(kc + 1) * LANES], b_ref[kc], preferred_element_type=F32)
        for part in range(2):
            for j in range(4):
                col = part * 512 + j * LANES
                for b in range(nseq):
                    scr[part * n_slab + 4 * kc + j, b * SSM_PITCH:b * SSM_PITCH + tt, :] = (
                        bu[b * tt:(b + 1) * tt, col:col + LANES])

    grp = 4
    for sg in range(n_slab // grp):
        slabs = [sg * grp + i for i in range(grp)]
        ar = [are_ref[s] for s in slabs]
        ai = [aim_ref[s] for s in slabs]

        def step(t, carry, slabs=slabs, ar=ar, ai=ai):
            hr, hi = carry
            nhr, nhi = [], []
            for i, s in enumerate(slabs):
                sel = pl.ds(t, nseq, stride=SSM_PITCH)
                br = scr[s, sel, :]
                bi = scr[n_slab + s, sel, :]
                r = ar[i] * hr[i] - ai[i] * hi[i] + br
                im = ar[i] * hi[i] + ai[i] * hr[i] + bi
                scr[s, sel, :] = r
                scr[n_slab + s, sel, :] = im
                nhr.append(r)
                nhi.append(im)
            return tuple(nhr), tuple(nhi)

        init = (tuple(hst[s] for s in slabs), tuple(hst[n_slab + s] for s in slabs))
        hr, hi = lax.fori_loop(0, tt, step, init, unroll=4)
        for i, s in enumerate(slabs):
            hst[s] = hr[i]
            hst[n_slab + s] = hi[i]

    ys = []
    for kc in range(4):
        slabs = [4 * kc + j for j in range(4)] + [n_slab + 4 * kc + j for j in range(4)]
        lhs = jnp.concatenate(
            [jnp.concatenate([scr[s, b * SSM_PITCH:b * SSM_PITCH + tt, :].astype(BF16) for s in slabs], axis=1)
             for b in range(nseq)], axis=0)
        ys.append(jnp.dot(lhs, c_ref[kc], preferred_element_type=F32))
    y = jnp.concatenate(ys, axis=1)
    o_ref[...] = _s5_epilogue(y, u2, d_ref, wglu_ref, bglu_ref, g_ref).reshape(nseq, tt, D_SSM)

    @pl.when(ti == pl.num_programs(1) - 1)
    def _():
        for s in range(n_slab):
            hre_ref[:, s * LANES:(s + 1) * LANES] = hst[s]
            him_ref[:, s * LANES:(s + 1) * LANES] = hst[n_slab + s]


def _s5_prompt(u, a_re, a_im, b_mat, c_mat, d, w_glu_bf16, b_glu, g_out):
    nb, s, _ = u.shape
    nseq = SUBLANES
    assert nb % nseq == 0 and s % SSM_TT == 0
    n_state = a_re.shape[0]
    n_slab = n_state // LANES
    bcast = lambda a: jnp.broadcast_to(a.reshape(n_slab, 1, LANES), (n_slab, nseq, LANES))
    const = lambda a: pl.BlockSpec(a.shape, lambda b, i: (0,) * a.ndim)
    args = (u, bcast(a_re), bcast(a_im), b_mat, c_mat, d.reshape(1, D_SSM), w_glu_bf16,
            b_glu.reshape(1, D_SSM), g_out.reshape(1, D_SSM))
    st = jax.ShapeDtypeStruct((nb, n_state), F32)
    return pl.pallas_call(
        _s5_prompt_kernel,
        grid=(nb // nseq, s // SSM_TT),
        in_specs=[pl.BlockSpec((nseq, SSM_TT, D_SSM), lambda b, i: (b, i, 0))] + [const(a) for a in args[1:]],
        out_specs=[pl.BlockSpec((nseq, SSM_TT, D_SSM), lambda b, i: (b, i, 0)),
                   pl.BlockSpec((nseq, n_state), lambda b, i: (b, 0)),
                   pl.BlockSpec((nseq, n_state), lambda b, i: (b, 0))],
        out_shape=[jax.ShapeDtypeStruct((nb, s, D_SSM), F32), st, st],
        scratch_shapes=[pltpu.VMEM((2 * n_slab, nseq * SSM_PITCH, LANES), F32),
                        pltpu.VMEM((2 * n_slab, nseq, LANES), F32)],
        compiler_params=_cparams(("parallel", "arbitrary")),
    )(*args)


def _s5_sample_kernel(u_ref, h0re_ref, h0im_ref, are_ref, aim_ref, b_ref, c_ref, d_ref, wglu_ref, bglu_ref, g_ref,
                      o_ref, hre_ref, him_ref):
    t_new = u_ref.shape[0]
    hre = h0re_ref[...]
    him = h0im_ref[...]
    are, aim = are_ref[...], aim_ref[...]
    half = 4 * LANES
    for t in range(t_new):
        u = u_ref[t]
        ub = u.astype(BF16)
        bus = [jnp.dot(ub[:, kc * LANES:(kc + 1) * LANES], b_ref[kc], preferred_element_type=F32) for kc in range(4)]
        bre = jnp.concatenate([bu[:, :half] for bu in bus], axis=1)
        bim = jnp.concatenate([bu[:, half:] for bu in bus], axis=1)
        hre, him = are * hre - aim * him + bre, are * him + aim * hre + bim
        ys = []
        for kc in range(4):
            lhs = jnp.concatenate([hre[:, kc * half:(kc + 1) * half], him[:, kc * half:(kc + 1) * half]], axis=1)
            ys.append(jnp.dot(lhs.astype(BF16), c_ref[kc], preferred_element_type=F32))
        y = jnp.concatenate(ys, axis=1)
        o_ref[t] = _s5_epilogue(y, u, d_ref, wglu_ref, bglu_ref, g_ref)
    hre_ref[...] = hre
    him_ref[...] = him


def _s5_sample(u_tm, h0_re, h0_im, a_re, a_im, b_mat, c_mat, d, w_glu_bf16, b_glu, g_out):
    t_new, nb, _ = u_tm.shape
    n_state = a_re.shape[0]
    tb = 64
    assert nb % tb == 0
    const = lambda a: pl.BlockSpec(a.shape, lambda b: (0,) * a.ndim)
    args = (u_tm, h0_re, h0_im, a_re.reshape(1, n_state), a_im.reshape(1, n_state), b_mat, c_mat,
            d.reshape(1, D_SSM), w_glu_bf16, b_glu.reshape(1, D_SSM), g_out.reshape(1, D_SSM))
    st_spec = pl.BlockSpec((tb, n_state), lambda b: (b, 0))
    st = jax.ShapeDtypeStruct((nb, n_state), F32)
    return pl.pallas_call(
        _s5_sample_kernel,
        grid=(nb // tb,),
        in_specs=[pl.BlockSpec((t_new, tb, D_SSM), lambda b: (0, b, 0)), st_spec, st_spec]
                 + [const(a) for a in args[3:]],
        out_specs=[pl.BlockSpec((t_new, tb, D_SSM), lambda b: (0, b, 0)), st_spec, st_spec],
        out_shape=[jax.ShapeDtypeStruct((t_new, nb, D_SSM), F32), st, st],
        compiler_params=_cparams(("parallel",)),
    )(*args)


def _outproj_kernel(x_ref, oa_ref, os_ref, gate_ref, sh_ref, sc_ref, ga_ref, gf_ref, w_ref, wr_hi_ref, wr_lo_ref,
                    br_ref, x1_ref, h_ref, lg_ref):
    oa = oa_ref[...]
    ms = jnp.mean(oa * oa, axis=-1, keepdims=True)
    na = oa * lax.rsqrt(ms + EPS) * ga_ref[...]
    merged = jnp.concatenate([na, os_ref[...]], axis=-1).astype(BF16)
    x1 = x_ref[...] + gate_ref[...] * jnp.dot(merged, w_ref[...], preferred_element_type=F32)
    x1_ref[...] = x1
    ms = jnp.mean(x1 * x1, axis=-1, keepdims=True)
    h = x1 * lax.rsqrt(ms + EPS) * gf_ref[...]
    h = h * (1.0 + sc_ref[...]) + sh_ref[...]
    h_hi = h.astype(BF16)
    h_ref[...] = h_hi
    h_lo = (h - h_hi.astype(F32)).astype(BF16)
    lg = jnp.dot(h_hi, wr_hi_ref[...], preferred_element_type=F32)
    lg = lg + jnp.dot(h_hi, wr_lo_ref[...], preferred_element_type=F32)
    lg = lg + jnp.dot(h_lo, wr_hi_ref[...], preferred_element_type=F32)
    lg_ref[...] = lg + br_ref[...]


def _outproj(x, o_attn, o_ssm, gate, shift, scale, g_attn, g_ffn, w_out_bf16, w_router, b_router, tm):
    nb, s, d = x.shape
    row = lambda n: pl.BlockSpec((None, tm, n), lambda b, i: (b, i, 0))
    const = lambda a: pl.BlockSpec(a.shape, lambda b, i: (0,) * a.ndim)
    wr_hi = w_router.astype(BF16)
    wr_lo = (w_router - wr_hi.astype(F32)).astype(BF16)
    consts = (g_attn.reshape(1, D_ATTN), g_ffn.reshape(1, d), w_out_bf16, wr_hi, wr_lo,
              b_router.reshape(1, ROUTER_COLS))
    return pl.pallas_call(
        _outproj_kernel,
        grid=(nb, s // tm),
        in_specs=[row(d), row(D_ATTN), row(D_SSM), _mod_spec(gate, tm), _mod_spec(shift, tm), _mod_spec(scale, tm)]
                 + [const(a) for a in consts],
        out_specs=[row(d), row(d), row(ROUTER_COLS)],
        out_shape=[jax.ShapeDtypeStruct((nb, s, d), F32), jax.ShapeDtypeStruct((nb, s, d), BF16),
                   jax.ShapeDtypeStruct((nb, s, ROUTER_COLS), F32)],
        compiler_params=_cparams(("parallel", "parallel")),
    )(x, o_attn, o_ssm, gate, shift, scale, *consts)


def _expert_kernel(blk_e_ref, nvalid_ref, x_ref, wg_ref, wu_ref, wd_ref, y_ref):
    i = pl.program_id(0)

    @pl.when(i < nvalid_ref[0])
    def _():
        x = x_ref[...]
        a = jnp.dot(x, wg_ref[...], preferred_element_type=F32)
        b = jnp.dot(x, wu_ref[...], preferred_element_type=F32)
        hid = (a * jax.nn.sigmoid(a) * b).astype(BF16)
        y_ref[...] = jnp.dot(hid, wd_ref[...], preferred_element_type=F32).astype(y_ref.dtype)

    @pl.when(i >= nvalid_ref[0])
    def _():
        y_ref[...] = jnp.zeros_like(y_ref)


def _experts(xs, blk_e, nvalid, w_gate, w_up, w_down):
    n, d = xs.shape
    nblk = n // MOE_TILE
    wspec = lambda shp: pl.BlockSpec((None,) + shp, lambda i, be, nv: (be[i], 0, 0))
    return pl.pallas_call(
        _expert_kernel,
        grid_spec=pltpu.PrefetchScalarGridSpec(
            num_scalar_prefetch=2,
            grid=(nblk,),
            in_specs=[pl.BlockSpec((MOE_TILE, d), lambda i, be, nv: (i, 0)),
                      wspec((d, D_EXPERT)), wspec((d, D_EXPERT)), wspec((D_EXPERT, d))],
            out_specs=pl.BlockSpec((MOE_TILE, d), lambda i, be, nv: (i, 0)),
        ),
        out_shape=jax.ShapeDtypeStruct((n, d), BF16),
        compiler_params=_cparams(("arbitrary",)),
    )(blk_e, nvalid, xs, w_gate, w_up, w_down)


def _combine_kernel(x1_ref, y0_ref, y1_ref, gts_ref, gate_ref, gfin_ref, o_ref):
    gts = gts_ref[...]
    moe = gts[:, 0:1] * y0_ref[...].astype(F32) + gts[:, 1:2] * y1_ref[...].astype(F32)
    x2 = x1_ref[...] + gate_ref[...] * moe
    ms = jnp.mean(x2 * x2, axis=-1, keepdims=True)
    o_ref[...] = x2 * lax.rsqrt(ms + EPS) * gfin_ref[...]


def _combine(x1, y0, y1, gts, gate, g_final, tm):
    nb, s, d = x1.shape
    row = lambda n: pl.BlockSpec((None, tm, n), lambda b, i: (b, i, 0))
    return pl.pallas_call(
        _combine_kernel,
        grid=(nb, s // tm),
        in_specs=[row(d), row(d), row(d), row(2), _mod_spec(gate, tm), pl.BlockSpec((1, d), lambda b, i: (0, 0))],
        out_specs=row(d),
        out_shape=jax.ShapeDtypeStruct((nb, s, d), F32),
        compiler_params=_cparams(("parallel", "parallel")),
    )(x1, y0, y1, gts, gate, g_final.reshape(1, d))


def _route(logits):
    t = logits.shape[0]
    lc = logits[:, :N_EXPERT_GROUPS]
    lf = logits[:, N_EXPERT_GROUPS:N_EXPERT_GROUPS + N_EXPERTS].reshape(t, N_EXPERT_GROUPS, EXPERTS_PER_GROUP)
    p_coarse = jax.nn.softmax(lc, axis=-1)
    grp = jnp.argmax(p_coarse, axis=-1)
    p_grp = jnp.max(p_coarse, axis=-1)
    lfg = jnp.take_along_axis(lf, grp[:, None, None], axis=1)[:, 0]
    top_v, top_i = lax.top_k(lfg, 2)
    gates = p_grp[:, None] * jax.nn.softmax(top_v, axis=-1)
    eid = grp[:, None].astype(jnp.int32) * EXPERTS_PER_GROUP + top_i.astype(jnp.int32)
    return eid, gates


def _dispatch_slots(eid):
    t = eid.shape[0]
    flat_e = eid.reshape(-1)
    onehot = (flat_e[:, None] == jnp.arange(N_EXPERTS, dtype=jnp.int32)[None, :]).astype(jnp.int32)
    rank = jnp.sum((jnp.cumsum(onehot, axis=0) - onehot) * onehot, axis=1)
    counts = jnp.sum(onehot, axis=0)
    padded = (counts + MOE_TILE - 1) // MOE_TILE * MOE_TILE
    pend = jnp.cumsum(padded)
    pstart = pend - padded
    slot = (pstart[flat_e] + rank).astype(jnp.int32)
    nblk = -(-2 * t // MOE_TILE) + N_EXPERTS
    blk_start = jnp.arange(nblk, dtype=jnp.int32) * MOE_TILE
    blk_e = jnp.minimum(jnp.searchsorted(pend, blk_start, side='right'), N_EXPERTS - 1).astype(jnp.int32)
    nvalid = (pend[-1] // MOE_TILE).astype(jnp.int32).reshape(1)
    return slot.reshape(t, 2), blk_e, nvalid, nblk


def _moe(h_bf16, logits, w_gate, w_up, w_down):
    t, d = h_bf16.shape
    eid, gates = _route(logits)
    slot, blk_e, nvalid, nblk = _dispatch_slots(eid)
    src = jnp.zeros((nblk * MOE_TILE,), jnp.int32).at[slot.reshape(-1)].set(
        jnp.repeat(jnp.arange(t, dtype=jnp.int32), 2))
    xs = jnp.take(h_bf16, src, axis=0)
    ys = _experts(xs, blk_e, nvalid, w_gate, w_up, w_down)
    y0 = jnp.take(ys, slot[:, 0], axis=0)
    y1 = jnp.take(ys, slot[:, 1], axis=0)
    return y0, y1, gates


def _layer(x, mods_mix, mods_ffn, attn_fn, s5_fn, wts, tm):
    nb, s, d = x.shape
    q, k, v, u = _inproj(x, mods_mix[0], mods_mix[1], wts['g_mix'], wts['w_in'], tm)
    o_attn = attn_fn(q, k, v)
    o_ssm, h_re, h_im = s5_fn(u)
    x1, h2, logits = _outproj(x, o_attn, o_ssm, mods_mix[2], mods_ffn[0], mods_ffn[1], wts['g_attn_out'],
                              wts['g_ffn'], wts['w_out'], wts['w_router'], wts['b_router'], tm)
    y0, y1, gates = _moe(h2.reshape(nb * s, d), logits.reshape(nb * s, ROUTER_COLS),
                         wts['w_gate'], wts['w_up'], wts['w_down'])
    y = _combine(x1, y0.reshape(nb, s, d), y1.reshape(nb, s, d), gates.reshape(nb, s, 2), mods_ffn[2],
                 wts['g_final'], tm)
    return y, k, v, h_re, h_im


def kernel(x_prompt, x_sample, cache_k_win, cache_v_win, state_ssm_re, state_ssm_im, c_prompt, c_sample, g_mix, w_ada_mix, b_ada_mix, w_in, w_out, g_attn_out, g_ssm_out, ssm_lambda_re, ssm_lambda_im, ssm_log_dt, ssm_b_re, ssm_b_im, ssm_c_re, ssm_c_im, ssm_d, w_glu, b_glu, g_ffn, w_ada_ffn, b_ada_ffn, w_router_coarse, b_router_coarse, w_router_fine, b_router_fine, w_expert_gate, w_expert_up, w_expert_down, g_final):
    depth = g_mix.shape[0]
    assert depth == 1, "single-layer step"
    l = 0
    bp, sp, d = x_prompt.shape
    bs, ts, _ = x_sample.shape
    keep = min(max(w for w, _ in DILATED_PATTERNS), sp)

    c_all = jnp.concatenate([c_prompt, c_sample], axis=0).astype(F32)
    m_mix = _adaln(c_all, w_ada_mix[l], b_ada_mix[l])
    m_ffn = _adaln(c_all, w_ada_ffn[l], b_ada_ffn[l])

    def mods(m, lo, hi, per_token):
        parts = jnp.split(m[lo:hi], 3, axis=-1)
        if per_token:
            return tuple(jnp.repeat(p, ts, axis=0)[None] for p in parts)
        return tuple(p[:, None, :] for p in parts)

    pad = ROUTER_COLS - N_EXPERT_GROUPS - N_EXPERTS
    w_router = jnp.concatenate([w_router_coarse[l], w_router_fine[l], jnp.zeros((d, pad), F32)], axis=1)
    b_router = jnp.concatenate([b_router_coarse[l], b_router_fine[l], jnp.zeros((pad,), F32)])
    wts = {
        'g_mix': g_mix[l], 'w_in': w_in[l].astype(BF16), 'w_out': w_out[l].astype(BF16),
        'g_attn_out': g_attn_out[l], 'g_ffn': g_ffn[l], 'w_router': w_router, 'b_router': b_router,
        'w_gate': w_expert_gate[l].astype(BF16), 'w_up': w_expert_up[l].astype(BF16),
        'w_down': w_expert_down[l].astype(BF16), 'g_final': g_final,
    }
    a_re, a_im, b_mat, c_mat = _s5_params(ssm_lambda_re[l], ssm_lambda_im[l], ssm_log_dt[l], ssm_b_re[l],
                                          ssm_b_im[l], ssm_c_re[l], ssm_c_im[l])
    s5_w = (a_re, a_im, b_mat, c_mat, ssm_d[l].reshape(-1), w_glu[l].astype(BF16), b_glu[l], g_ssm_out[l])

    yp, kp, vp, hrp, hip = _layer(
        x_prompt, mods(m_mix, 0, bp, False), mods(m_ffn, 0, bp, False), _attn_prompt,
        lambda u: _s5_prompt(u, *s5_w), wts, tm=512)

    n_state = SSM_GROUPS * SSM_STATE
    ck = cache_k_win[l].reshape(bs, -1, D_ATTN)
    cv = cache_v_win[l].reshape(bs, -1, D_ATTN)

    def attn_s(q, k, v):
        r = lambda a: a.reshape(bs, ts, D_ATTN)
        return _attn_sample(r(q), r(k), r(v), ck, cv).reshape(1, bs * ts, D_ATTN)

    def s5_s(u):
        u_tm = u.reshape(bs, ts, D_SSM).transpose(1, 0, 2)
        o, hr, hi = _s5_sample(u_tm, state_ssm_re[l].reshape(bs, n_state), state_ssm_im[l].reshape(bs, n_state),
                               *s5_w)
        return o.transpose(1, 0, 2).reshape(1, bs * ts, D_SSM), hr, hi

    ys, ks, vs, hrs, his = _layer(
        x_sample.reshape(1, bs * ts, d), mods(m_mix, bp, bp + bs, True), mods(m_ffn, bp, bp + bs, True),
        attn_s, s5_s, wts, tm=bs * ts)

    heads = lambda a, b, s: a.reshape(1, b, s, N_HEADS, HEAD_DIM)
    state = lambda a, b: a.reshape(1, b, SSM_GROUPS, SSM_STATE)
    return (yp, ys.reshape(bs, ts, d),
            heads(kp[:, sp - keep:], bp, keep), heads(vp[:, sp - keep:], bp, keep), state(hrp, bp), state(hip, bp),
            heads(ks, bs, ts), heads(vs, bs, ts), state(hrs, bs), state(his, bs))
```

```python
import functools
import math

import numpy as np
import jax
import jax.numpy as jnp
from jax import lax
from jax.experimental import pallas as pl
from jax.experimental.pallas import tpu as pltpu
from jax.experimental.pallas import tpu_sc as plsc

F32 = jnp.float32
BF16 = jnp.bfloat16
HIGHEST = lax.Precision.HIGHEST

D_MODEL = 1024
D_ATTN = 512
D_SSM = 512
HEAD_DIM = 64
N_HEADS = 8
DILATED_PATTERNS = ((128, 1), (512, 4), (2048, 16))
SSM_GROUP_CH = 16
SSM_GROUPS = 32
SSM_STATE = 64
N_EXPERT_GROUPS = 4
EXPERTS_PER_GROUP = 8
N_EXPERTS = 32
D_EXPERT = 512
D_IN_PROJ = 3 * D_ATTN + D_SSM
EPS = 1e-6

LANES = 128
SUBLANES = 8
VMEM_LIMIT = 48 * 1024 * 1024

ATTN_BLOCK = 128
ATTN_UNROLL = 8
NEG_BIG = -1e30
SSM_TT = 128
SSM_PITCH = SSM_TT + 8
ROUTER_COLS = 128
MOE_TILE = 256
SC_WINDOW = 128
ROWS_PER_TOKEN = D_MODEL // 2 // LANES
INFO_EXPERT, INFO_RANK, INFO_GATE = 0, 2, 4


def _cparams(sem, vmem=VMEM_LIMIT):
    return pltpu.CompilerParams(dimension_semantics=sem, vmem_limit_bytes=vmem)


def _adaln_kernel(c_ref, w_ref, b_ref, o_ref):
    c = c_ref[...]
    s = c * jax.nn.sigmoid(c)
    o_ref[...] = jnp.dot(s, w_ref[...], precision=HIGHEST, preferred_element_type=F32) + b_ref[...]


def _adaln(c, w, b):
    r, d = c.shape
    n = w.shape[1]
    tn = 768
    return pl.pallas_call(
        _adaln_kernel,
        grid=(n // tn,),
        in_specs=[pl.BlockSpec((r, d), lambda j: (0, 0)),
                  pl.BlockSpec((d, tn), lambda j: (0, j)),
                  pl.BlockSpec((1, tn), lambda j: (0, j))],
        out_specs=pl.BlockSpec((r, tn), lambda j: (0, j)),
        out_shape=jax.ShapeDtypeStruct((r, n), F32),
        compiler_params=_cparams(("arbitrary",)),
    )(c, w, b.reshape(1, n))


def _inproj_kernel(x_ref, sh_ref, sc_ref, g_ref, w_ref, q_ref, k_ref, v_ref, u_ref):
    x = x_ref[...]
    ms = jnp.mean(x * x, axis=-1, keepdims=True)
    h = x * lax.rsqrt(ms + EPS) * g_ref[...]
    h = h * (1.0 + sc_ref[...]) + sh_ref[...]
    p = jnp.dot(h.astype(BF16), w_ref[...], preferred_element_type=F32)
    q_ref[...] = p[:, :D_ATTN] * (HEAD_DIM ** -0.5)
    k_ref[...] = p[:, D_ATTN:2 * D_ATTN]
    v_ref[...] = p[:, 2 * D_ATTN:3 * D_ATTN]
    u_ref[...] = p[:, 3 * D_ATTN:]


def _mod_spec(mod, tm):
    d = mod.shape[-1]
    if mod.shape[1] == 1:
        return pl.BlockSpec((None, 1, d), lambda b, i: (b, 0, 0))
    return pl.BlockSpec((None, tm, d), lambda b, i: (b, i, 0))


def _inproj(x, shift, scale, g, w_bf16, tm):
    nb, s, d = x.shape
    row = lambda n: pl.BlockSpec((None, tm, n), lambda b, i: (b, i, 0))
    out = jax.ShapeDtypeStruct((nb, s, D_ATTN), F32)
    return pl.pallas_call(
        _inproj_kernel,
        grid=(nb, s // tm),
        in_specs=[row(d), _mod_spec(shift, tm), _mod_spec(scale, tm),
                  pl.BlockSpec((1, d), lambda b, i: (0, 0)),
                  pl.BlockSpec((d, D_IN_PROJ), lambda b, i: (0, 0))],
        out_specs=[row(D_ATTN), row(D_ATTN), row(D_ATTN), row(D_SSM)],
        out_shape=[out, out, out, jax.ShapeDtypeStruct((nb, s, D_SSM), F32)],
        compiler_params=_cparams(("parallel", "parallel")),
    )(x, shift, scale, g.reshape(1, d), w_bf16)


def _band_bias():
    qi = np.arange(ATTN_BLOCK)[:, None]
    kj = np.arange(ATTN_BLOCK)[None, :]
    cur = kj <= qi
    prev = kj >= qi
    to_bias = lambda m: np.tile(np.where(m, 0.0, NEG_BIG).astype(np.float32), (2, 1))
    return to_bias(np.concatenate([prev, cur], axis=1)), to_bias(cur)


def _attn_prompt_kernel(q_ref, k_ref, v_ref, bias2_ref, bias1_ref, o_ref, acc_scr, m_scr, l_scr):
    s_len = q_ref.shape[0]
    lane = lax.broadcasted_iota(jnp.int32, (1, LANES), 1)
    head0 = lane < HEAD_DIM

    def rows(ref, start, n, d):
        if d == 1:
            return ref[pl.ds(start, n), :]
        return ref[pl.ds(start, n, stride=d), :]

    def tiles(items, first):
        nk = ATTN_BLOCK if first else 2 * ATTN_BLOCK
        bias = bias1_ref[...] if first else bias2_ref[...]
        one = jnp.ones((), BF16)
        loaded = []
        for _, d, qstart in items:
            kstart = qstart if first else qstart - d * ATTN_BLOCK
            loaded.append((rows(q_ref, qstart, ATTN_BLOCK, d), rows(k_ref, kstart, nk, d),
                           rows(v_ref, kstart, nk, d)))
        results = []
        for qrows, krows, vrows in loaded:
            kb = krows.astype(BF16)
            vb = vrows.astype(BF16)
            q2 = jnp.concatenate([jnp.where(head0, qrows, 0.0), jnp.where(head0, 0.0, qrows)], axis=0).astype(BF16)
            s = lax.dot_general(q2, kb, (((1,), (1,)), ((), ())), preferred_element_type=F32) + bias
            m = jnp.max(s, axis=-1, keepdims=True)
            p = jnp.exp(s - m).astype(BF16)
            pv0 = jnp.dot(p[:ATTN_BLOCK], jnp.where(head0, vb, one), preferred_element_type=F32)
            pv1 = jnp.dot(p[ATTN_BLOCK:], jnp.where(head0, one, vb), preferred_element_type=F32)
            results.append((jnp.where(head0, pv0, pv1), jnp.where(head0, m[:ATTN_BLOCK], m[ATTN_BLOCK:]),
                            jnp.where(head0, pv1, pv0)))
        for (br, d, qstart), (acc, m, l) in zip(items, results):
            dst = pl.ds(qstart, ATTN_BLOCK) if d == 1 else pl.ds(qstart, ATTN_BLOCK, stride=d)
            acc_scr[br, dst, :] = acc
            m_scr[br, dst, :] = m
            l_scr[br, dst, :] = l

    def largest_divisor(n):
        return max(f for f in range(1, ATTN_UNROLL + 1) if n % f == 0)

    grouped = []
    for br, (window, d) in enumerate(DILATED_PATTERNS):
        assert window // d == ATTN_BLOCK
        if d <= ATTN_UNROLL:
            grouped += [(br, d, r) for r in range(d)]
        else:
            un = largest_divisor(d)

            def first_body(i, carry, br=br, d=d, un=un):
                tiles([(br, d, i * un + j) for j in range(un)], True)
                return carry

            lax.fori_loop(0, d // un, first_body, 0)
    for i in range(0, len(grouped), ATTN_UNROLL):
        tiles(grouped[i:i + ATTN_UNROLL], True)

    for br, (window, d) in enumerate(DILATED_PATTERNS):
        nblk = s_len // (d * ATTN_BLOCK)
        n_rest = d * (nblk - 1)
        if n_rest:
            un = largest_divisor(n_rest)

            def rest_body(i, carry, br=br, d=d, nblk=nblk, un=un):
                items = []
                for j in range(un):
                    n = i * un + j
                    qstart = n // (nblk - 1) + d * ATTN_BLOCK * (1 + n % (nblk - 1))
                    items.append((br, d, pl.multiple_of(qstart, ATTN_BLOCK) if d == 1 else qstart))
                tiles(items, False)
                return carry

            lax.fori_loop(0, n_rest // un, rest_body, 0)

    ch = 256

    def merge_body(i, carry):
        sl = pl.ds(pl.multiple_of(i * ch, ch), ch)
        ms = [m_scr[br, sl, :] for br in range(3)]
        mx = jnp.maximum(jnp.maximum(ms[0], ms[1]), ms[2])
        num = jnp.zeros((ch, LANES), F32)
        den = jnp.zeros((ch, LANES), F32)
        for br in range(3):
            e = jnp.exp(ms[br] - mx)
            num = num + e * acc_scr[br, sl, :]
            den = den + e * pltpu.roll(l_scr[br, sl, :], HEAD_DIM, 1)
        o_ref[sl, :] = num / den
        return carry

    lax.fori_loop(0, s_len // ch, merge_body, 0)


def _attn_prompt(q, k, v):
    nb, s, _ = q.shape
    assert s % (16 * ATTN_BLOCK) == 0
    bias2, bias1 = _band_bias()
    blk = pl.BlockSpec((None, s, LANES), lambda b, p: (b, 0, p))
    const = lambda a: pl.BlockSpec(a.shape, lambda b, p: (0, 0))
    return pl.pallas_call(
        _attn_prompt_kernel,
        grid=(nb, D_ATTN // LANES),
        in_specs=[blk, blk, blk, const(bias2), const(bias1)],
        out_specs=blk,
        out_shape=jax.ShapeDtypeStruct((nb, s, D_ATTN), F32),
        scratch_shapes=[pltpu.VMEM((3, s, LANES), F32)] * 3,
        compiler_params=_cparams(("parallel", "parallel")),
    )(q, k, v, jnp.asarray(bias2), jnp.asarray(bias1))


def _sample_bias(t_new, w_buf):
    t = np.arange(t_new)[:, None]
    idx = np.arange(w_buf + t_new)[None, :]
    dist = w_buf + t - idx
    mult = np.zeros(dist.shape, np.int64)
    for window, d in DILATED_PATTERNS:
        mult += ((dist >= 0) & (dist % d == 0) & (dist <= window)).astype(np.int64)
    bias = np.where(mult > 0, np.log(np.maximum(mult, 1)), NEG_BIG)
    bias = np.concatenate([bias, bias], axis=0).astype(np.float32)
    return bias[:, :w_buf], bias[:, w_buf:]


def _attn_sample_kernel(q_ref, kn_ref, vn_ref, kc_ref, vc_ref, bias_ref, biasn_ref, o_ref):
    t_new = q_ref.shape[0]
    lane = lax.broadcasted_iota(jnp.int32, (1, LANES), 1)
    head0 = lane < HEAD_DIM
    bias_n = biasn_ref[...]
    for p in range(D_ATTN // LANES):
        cols = slice(p * LANES, (p + 1) * LANES)
        q2 = q_ref[:, cols]
        qq = jnp.concatenate([jnp.where(head0, q2, 0.0), jnp.where(head0, 0.0, q2)], axis=0)
        kc = kc_ref[:, cols].astype(BF16)
        vc = vc_ref[:, cols].astype(BF16)
        kn = kn_ref[:, cols]
        vn = vn_ref[:, cols]
        s_c = lax.dot_general(qq.astype(BF16), kc, (((1,), (1,)), ((), ())),
                              preferred_element_type=F32) + bias_ref[...]
        qq_r = qq.astype(BF16).astype(F32)
        s_n = []
        for j in range(t_new):
            kj = kn[j:j + 1, :].astype(BF16).astype(F32)
            sj = jnp.sum(qq_r * kj, axis=-1, keepdims=True)
            s_n.append(sj + bias_n[:, j:j + 1])
        m = jnp.max(s_c, axis=-1, keepdims=True)
        for sj in s_n:
            m = jnp.maximum(m, sj)
        p_c = jnp.exp(s_c - m)
        l = jnp.sum(p_c, axis=-1, keepdims=True)
        acc = jnp.dot(p_c.astype(BF16), vc, preferred_element_type=F32)
        for j, sj in enumerate(s_n):
            pj = jnp.exp(sj - m)
            l = l + pj
            acc = acc + pj * vn[j:j + 1, :]
        o = acc / l
        o_ref[:, cols] = jnp.where(head0, o[:t_new], o[t_new:])


def _attn_sample(q, k_new, v_new, cache_k, cache_v):
    nb, t_new, _ = q.shape
    w_buf = cache_k.shape[1]
    bias, bias_new = _sample_bias(t_new, w_buf)
    new = pl.BlockSpec((None, t_new, D_ATTN), lambda b: (b, 0, 0))
    buf = pl.BlockSpec((None, w_buf, D_ATTN), lambda b: (b, 0, 0))
    const = lambda a: pl.BlockSpec(a.shape, lambda b: (0, 0))
    return pl.pallas_call(
        _attn_sample_kernel,
        grid=(nb,),
        in_specs=[new, new, new, buf, buf, const(bias), const(bias_new)],
        out_specs=new,
        out_shape=jax.ShapeDtypeStruct((nb, t_new, D_ATTN), F32),
        compiler_params=_cparams(("parallel",)),
    )(q, k_new, v_new, cache_k, cache_v, jnp.asarray(bias), jnp.asarray(bias_new))


def _s5_params(lam_re, lam_im, log_dt, b_re, b_im, c_re, c_im):
    f32 = F32
    dt = jnp.exp(log_dt.astype(f32))[:, None]
    lr, li = lam_re.astype(f32), lam_im.astype(f32)
    ea = jnp.exp(lr * dt)
    a_re, a_im = ea * jnp.cos(li * dt), ea * jnp.sin(li * dt)
    den = lr * lr + li * li
    co_re = ((a_re - 1.0) * lr + a_im * li) / den
    co_im = (a_im * lr - (a_re - 1.0) * li) / den
    bb_re = co_re[..., None] * b_re - co_im[..., None] * b_im
    bb_im = co_re[..., None] * b_im + co_im[..., None] * b_re
    eye = jnp.eye(8, dtype=f32)

    def b_blocks(bb):
        t = bb.reshape(4, 8, SSM_STATE, SSM_GROUP_CH)
        return jnp.einsum('ab,kapc->kacbp', eye, t).reshape(4, 8 * SSM_GROUP_CH, 8 * SSM_STATE)

    def c_blocks(cc):
        t = cc.reshape(4, 8, SSM_GROUP_CH, SSM_STATE)
        return jnp.einsum('ab,kacp->kbpac', eye, t).reshape(4, 8 * SSM_STATE, 8 * SSM_GROUP_CH)

    b_mat = jnp.concatenate([b_blocks(bb_re), b_blocks(bb_im)], axis=2).astype(BF16)
    c_mat = jnp.concatenate([c_blocks(c_re.astype(f32)), -c_blocks(c_im.astype(f32))], axis=1).astype(BF16)
    return a_re.reshape(-1), a_im.reshape(-1), b_mat, c_mat


def _gelu_tanh(y):
    return 0.5 * y * (1.0 + jnp.tanh(math.sqrt(2.0 / math.pi) * (y + 0.044715 * (y * y * y))))


def _s5_epilogue(y, u, d_ref, wglu_ref, bglu_ref, g_ref):
    y = y + d_ref[...] * u
    z = _gelu_tanh(y)
    gate = jnp.dot(z.astype(BF16), wglu_ref[...], preferred_element_type=F32) + bglu_ref[...]
    out = z * jax.nn.sigmoid(gate)
    ms = jnp.mean(out * out, axis=-1, keepdims=True)
    return out * lax.rsqrt(ms + EPS) * g_ref[...]


def _s5_prompt_kernel(u_ref, are_ref, aim_ref, b_ref, c_ref, d_ref, wglu_ref, bglu_ref, g_ref,
                      o_ref, hre_ref, him_ref, scr, hst):
    nseq, tt, _ = u_ref.shape
    n_slab = D_SSM * SSM_STATE // SSM_GROUP_CH // LANES
    ti = pl.program_id(1)

    @pl.when(ti == 0)
    def _():
        hst[...] = jnp.zeros_like(hst)

    u2 = u_ref[...].reshape(nseq * tt, D_SSM)
    ub = u2.astype(BF16)
    for kc in range(4):
        bu = jnp.dot(ub[:, kc * LANES:(kc + 1) * LANES], b_ref[kc], preferred_element_type=F32)
        for part in range(2):
            for j in range(4):
                col = part * 512 + j * LANES
                for b in range(nseq):
                    scr[part * n_slab + 4 * kc + j, b * SSM_PITCH:b * SSM_PITCH + tt, :] = (
                        bu[b * tt:(b + 1) * tt, col:col + LANES])

    grp = 4
    for sg in range(n_slab // grp):
        slabs = [sg * grp + i for i in range(grp)]
        ar = [are_ref[s] for s in slabs]
        ai = [aim_ref[s] for s in slabs]

        def step(t, carry, slabs=slabs, ar=ar, ai=ai):
            hr, hi = carry
            nhr, nhi = [], []
            for i, s in enumerate(slabs):
                sel = pl.ds(t, nseq, stride=SSM_PITCH)
                br = scr[s, sel, :]
                bi = scr[n_slab + s, sel, :]
                r = ar[i] * hr[i] - ai[i] * hi[i] + br
                im = ar[i] * hi[i] + ai[i] * hr[i] + bi
                scr[s, sel, :] = r
                scr[n_slab + s, sel, :] = im
                nhr.append(r)
                nhi.append(im)
            return tuple(nhr), tuple(nhi)

        init = (tuple(hst[s] for s in slabs), tuple(hst[n_slab + s] for s in slabs))
        hr, hi = lax.fori_loop(0, tt, step, init, unroll=4)
        for i, s in enumerate(slabs):
            hst[s] = hr[i]
            hst[n_slab + s] = hi[i]

    ys = []
    for kc in range(4):
        slabs = [4 * kc + j for j in range(4)] + [n_slab + 4 * kc + j for j in range(4)]
        lhs = jnp.concatenate(
            [jnp.concatenate([scr[s, b * SSM_PITCH:b * SSM_PITCH + tt, :].astype(BF16) for s in slabs], axis=1)
             for b in range(nseq)], axis=0)
        ys.append(jnp.dot(lhs, c_ref[kc], preferred_element_type=F32))
    y = jnp.concatenate(ys, axis=1)
    o_ref[...] = _s5_epilogue(y, u2, d_ref, wglu_ref, bglu_ref, g_ref).reshape(nseq, tt, D_SSM)

    @pl.when(ti == pl.num_programs(1) - 1)
    def _():
        for s in range(n_slab):
            hre_ref[:, s * LANES:(s + 1) * LANES] = hst[s]
            him_ref[:, s * LANES:(s + 1) * LANES] = hst[n_slab + s]


def _s5_prompt(u, a_re, a_im, b_mat, c_mat, d, w_glu_bf16, b_glu, g_out):
    nb, s, _ = u.shape
    nseq = SUBLANES
    assert nb % nseq == 0 and s % SSM_TT == 0
    n_state = a_re.shape[0]
    n_slab = n_state // LANES
    bcast = lambda a: jnp.broadcast_to(a.reshape(n_slab, 1, LANES), (n_slab, nseq, LANES))
    const = lambda a: pl.BlockSpec(a.shape, lambda b, i: (0,) * a.ndim)
    args = (u, bcast(a_re), bcast(a_im), b_mat, c_mat, d.reshape(1, D_SSM), w_glu_bf16,
            b_glu.reshape(1, D_SSM), g_out.reshape(1, D_SSM))
    st = jax.ShapeDtypeStruct((nb, n_state), F32)
    return pl.pallas_call(
        _s5_prompt_kernel,
        grid=(nb // nseq, s // SSM_TT),
        in_specs=[pl.BlockSpec((nseq, SSM_TT, D_SSM), lambda b, i: (b, i, 0))] + [const(a) for a in args[1:]],
        out_specs=[pl.BlockSpec((nseq, SSM_TT, D_SSM), lambda b, i: (b, i, 0)),
                   pl.BlockSpec((nseq, n_state), lambda b, i: (b, 0)),
                   pl.BlockSpec((nseq, n_state), lambda b, i: (b, 0))],
        out_shape=[jax.ShapeDtypeStruct((nb, s, D_SSM), F32), st, st],
        scratch_shapes=[pltpu.VMEM((2 * n_slab, nseq * SSM_PITCH, LANES), F32),
                        pltpu.VMEM((2 * n_slab, nseq, LANES), F32)],
        compiler_params=_cparams(("parallel", "arbitrary")),
    )(*args)


def _s5_sample_kernel(u_ref, h0re_ref, h0im_ref, are_ref, aim_ref, b_ref, c_ref, d_ref, wglu_ref, bglu_ref, g_ref,
                      o_ref, hre_ref, him_ref):
    t_new = u_ref.shape[0]
    hre = h0re_ref[...]
    him = h0im_ref[...]
    are, aim = are_ref[...], aim_ref[...]
    half = 4 * LANES
    for t in range(t_new):
        u = u_ref[t]
        ub = u.astype(BF16)
        bus = [jnp.dot(ub[:, kc * LANES:(kc + 1) * LANES], b_ref[kc], preferred_element_type=F32) for kc in range(4)]
        bre = jnp.concatenate([bu[:, :half] for bu in bus], axis=1)
        bim = jnp.concatenate([bu[:, half:] for bu in bus], axis=1)
        hre, him = are * hre - aim * him + bre, are * him + aim * hre + bim
        ys = []
        for kc in range(4):
            lhs = jnp.concatenate([hre[:, kc * half:(kc + 1) * half], him[:, kc * half:(kc + 1) * half]], axis=1)
            ys.append(jnp.dot(lhs.astype(BF16), c_ref[kc], preferred_element_type=F32))
        y = jnp.concatenate(ys, axis=1)
        o_ref[t] = _s5_epilogue(y, u, d_ref, wglu_ref, bglu_ref, g_ref)
    hre_ref[...] = hre
    him_ref[...] = him


def _s5_sample(u_tm, h0_re, h0_im, a_re, a_im, b_mat, c_mat, d, w_glu_bf16, b_glu, g_out):
    t_new, nb, _ = u_tm.shape
    n_state = a_re.shape[0]
    tb = 64
    assert nb % tb == 0
    const = lambda a: pl.BlockSpec(a.shape, lambda b: (0,) * a.ndim)
    args = (u_tm, h0_re, h0_im, a_re.reshape(1, n_state), a_im.reshape(1, n_state), b_mat, c_mat,
            d.reshape(1, D_SSM), w_glu_bf16, b_glu.reshape(1, D_SSM), g_out.reshape(1, D_SSM))
    st_spec = pl.BlockSpec((tb, n_state), lambda b: (b, 0))
    st = jax.ShapeDtypeStruct((nb, n_state), F32)
    return pl.pallas_call(
        _s5_sample_kernel,
        grid=(nb // tb,),
        in_specs=[pl.BlockSpec((t_new, tb, D_SSM), lambda b: (0, b, 0)), st_spec, st_spec]
                 + [const(a) for a in args[3:]],
        out_specs=[pl.BlockSpec((t_new, tb, D_SSM), lambda b: (0, b, 0)), st_spec, st_spec],
        out_shape=[jax.ShapeDtypeStruct((t_new, nb, D_SSM), F32), st, st],
        compiler_params=_cparams(("parallel",)),
    )(*args)


def _store_packed(ref, val):
    rows, d = val.shape
    half = d // 2
    bits = pltpu.bitcast(val.astype(BF16).astype(F32), jnp.uint32)
    words = bits[:, :half] | lax.shift_right_logical(bits[:, half:], jnp.uint32(16))
    for c in range(ROWS_PER_TOKEN):
        ref[pl.ds(c, rows, stride=ROWS_PER_TOKEN), :] = words[:, c * LANES:(c + 1) * LANES]


def _load_packed(ref, rows):
    hi, lo = [], []
    for c in range(ROWS_PER_TOKEN):
        w = ref[pl.ds(c, rows, stride=ROWS_PER_TOKEN), :]
        hi.append(pltpu.bitcast(w & jnp.uint32(0xFFFF0000), F32))
        lo.append(pltpu.bitcast(lax.shift_left(w, jnp.uint32(16)), F32))
    return jnp.concatenate(hi + lo, axis=1)


def _outproj_kernel(x_ref, oa_ref, os_ref, gate_ref, sh_ref, sc_ref, ga_ref, gf_ref, w_ref, wr_hi_ref, wr_lo_ref,
                    br_ref, x1_ref, h_ref, lg_ref):
    oa = oa_ref[...]
    ms = jnp.mean(oa * oa, axis=-1, keepdims=True)
    na = oa * lax.rsqrt(ms + EPS) * ga_ref[...]
    merged = jnp.concatenate([na, os_ref[...]], axis=-1).astype(BF16)
    x1 = x_ref[...] + gate_ref[...] * jnp.dot(merged, w_ref[...], preferred_element_type=F32)
    x1_ref[...] = x1
    ms = jnp.mean(x1 * x1, axis=-1, keepdims=True)
    h = x1 * lax.rsqrt(ms + EPS) * gf_ref[...]
    h = h * (1.0 + sc_ref[...]) + sh_ref[...]
    h_hi = h.astype(BF16)
    _store_packed(h_ref, h)
    h_lo = (h - h_hi.astype(F32)).astype(BF16)
    lg = jnp.dot(h_hi, wr_hi_ref[...], preferred_element_type=F32)
    lg = lg + jnp.dot(h_hi, wr_lo_ref[...], preferred_element_type=F32)
    lg = lg + jnp.dot(h_lo, wr_hi_ref[...], preferred_element_type=F32)
    lg_ref[...] = lg + br_ref[...]


def _outproj(x, o_attn, o_ssm, gate, shift, scale, g_attn, g_ffn, w_out_bf16, w_router, b_router, tm):
    nb, s, d = x.shape
    row = lambda n: pl.BlockSpec((None, tm, n), lambda b, i: (b, i, 0))
    const = lambda a: pl.BlockSpec(a.shape, lambda b, i: (0,) * a.ndim)
    wr_hi = w_router.astype(BF16)
    wr_lo = (w_router - wr_hi.astype(F32)).astype(BF16)
    consts = (g_attn.reshape(1, D_ATTN), g_ffn.reshape(1, d), w_out_bf16, wr_hi, wr_lo,
              b_router.reshape(1, ROUTER_COLS))
    return pl.pallas_call(
        _outproj_kernel,
        grid=(nb, s // tm),
        in_specs=[row(d), row(D_ATTN), row(D_SSM), _mod_spec(gate, tm), _mod_spec(shift, tm), _mod_spec(scale, tm)]
                 + [const(a) for a in consts],
        out_specs=[row(d), pl.BlockSpec((tm * ROWS_PER_TOKEN, LANES), lambda b, i: (b * (s // tm) + i, 0)),
                   row(ROUTER_COLS)],
        out_shape=[jax.ShapeDtypeStruct((nb, s, d), F32),
                   jax.ShapeDtypeStruct((nb * s * ROWS_PER_TOKEN, LANES), jnp.uint32),
                   jax.ShapeDtypeStruct((nb, s, ROUTER_COLS), F32)],
        compiler_params=_cparams(("parallel", "parallel")),
    )(x, o_attn, o_ssm, gate, shift, scale, *consts)


def _expert_kernel(blk_e_ref, nvalid_ref, x_ref, wg_ref, wu_ref, wd_ref, y_ref):
    i = pl.program_id(0)

    @pl.when(i < nvalid_ref[0])
    def _():
        x = _load_packed(x_ref, MOE_TILE).astype(BF16)
        a = jnp.dot(x, wg_ref[...], preferred_element_type=F32)
        b = jnp.dot(x, wu_ref[...], preferred_element_type=F32)
        hid = (a * jax.nn.sigmoid(a) * b).astype(BF16)
        _store_packed(y_ref, jnp.dot(hid, wd_ref[...], preferred_element_type=F32))

    @pl.when(i >= nvalid_ref[0])
    def _():
        y_ref[...] = jnp.zeros_like(y_ref)


def _experts(xs, blk_e, nvalid, w_gate, w_up, w_down):
    nblk = xs.shape[0] // (MOE_TILE * ROWS_PER_TOKEN)
    d = w_gate.shape[1]
    wspec = lambda shp: pl.BlockSpec((None,) + shp, lambda i, be, nv: (be[i], 0, 0))
    tile = pl.BlockSpec((MOE_TILE * ROWS_PER_TOKEN, LANES), lambda i, be, nv: (i, 0))
    return pl.pallas_call(
        _expert_kernel,
        grid_spec=pltpu.PrefetchScalarGridSpec(
            num_scalar_prefetch=2,
            grid=(nblk,),
            in_specs=[tile, wspec((d, D_EXPERT)), wspec((d, D_EXPERT)), wspec((D_EXPERT, d))],
            out_specs=tile,
        ),
        out_shape=jax.ShapeDtypeStruct(xs.shape, jnp.uint32),
        compiler_params=_cparams(("arbitrary",)),
    )(blk_e, nvalid, xs, w_gate, w_up, w_down)


def _combine_kernel(x1_ref, y0_ref, y1_ref, info_ref, gate_ref, gfin_ref, o_ref):
    tm = x1_ref.shape[0]
    info = pltpu.bitcast(info_ref[...], F32)
    g0 = info[:, INFO_GATE:INFO_GATE + 1]
    g1 = info[:, INFO_GATE + 1:INFO_GATE + 2]
    moe = g0 * _load_packed(y0_ref, tm) + g1 * _load_packed(y1_ref, tm)
    x2 = x1_ref[...] + gate_ref[...] * moe
    ms = jnp.mean(x2 * x2, axis=-1, keepdims=True)
    o_ref[...] = x2 * lax.rsqrt(ms + EPS) * gfin_ref[...]


def _combine(x1, y0, y1, info, gate, g_final, tm):
    nb, s, d = x1.shape
    row = lambda n: pl.BlockSpec((None, tm, n), lambda b, i: (b, i, 0))
    packed = pl.BlockSpec((tm * ROWS_PER_TOKEN, LANES), lambda b, i: (b * (s // tm) + i, 0))
    return pl.pallas_call(
        _combine_kernel,
        grid=(nb, s // tm),
        in_specs=[row(d), packed, packed, row(ROUTER_COLS), _mod_spec(gate, tm),
                  pl.BlockSpec((1, d), lambda b, i: (0, 0))],
        out_specs=row(d),
        out_shape=jax.ShapeDtypeStruct((nb, s, d), F32),
        compiler_params=_cparams(("parallel", "parallel")),
    )(x1, y0, y1, info, gate, g_final.reshape(1, d))


def _sc_window(n_rows):
    assert n_rows % SC_WINDOW == 0
    return SC_WINDOW


def _sc_gather_rows(table, idx):
    n = idx.shape[0]
    w = _sc_window(n)
    mesh = plsc.VectorSubcoreMesh(core_axis_name="core", subcore_axis_name="subcore")

    @functools.partial(pl.kernel, out_type=jax.ShapeDtypeStruct((n, LANES), table.dtype), mesh=mesh)
    def gather_kernel(x_hbm, i_hbm, o_hbm):
        def body(i_vmem, o_vmem):
            pltpu.sync_copy(x_hbm.at[i_vmem.at[0]], o_vmem)

        pltpu.emit_pipeline(
            body, grid=(n // w,),
            in_specs=[pl.BlockSpec((1, w), lambda i: (0, i))],
            out_specs=[pl.BlockSpec((w, LANES), lambda i: (i, 0))],
            core_axis_name=("core", "subcore"), dimension_semantics=(pltpu.PARALLEL,),
        )(i_hbm, o_hbm)

    return gather_kernel(table, idx.reshape(1, n))


def _sc_scatter_rows(rows, idx0, idx1, n_out):
    n = rows.shape[0]
    w = _sc_window(n)
    mesh = plsc.VectorSubcoreMesh(core_axis_name="core", subcore_axis_name="subcore")

    @functools.partial(pl.kernel, out_type=jax.ShapeDtypeStruct((n_out, LANES), rows.dtype), mesh=mesh)
    def scatter_kernel(x_hbm, i0_hbm, i1_hbm, o_hbm):
        def body(x_vmem, i0_vmem, i1_vmem):
            pltpu.sync_copy(x_vmem, o_hbm.at[i0_vmem.at[0]])
            pltpu.sync_copy(x_vmem, o_hbm.at[i1_vmem.at[0]])

        pltpu.emit_pipeline(
            body, grid=(n // w,),
            in_specs=[pl.BlockSpec((w, LANES), lambda i: (i, 0)),
                      pl.BlockSpec((1, w), lambda i: (0, i)),
                      pl.BlockSpec((1, w), lambda i: (0, i))],
            out_specs=[],
            core_axis_name=("core", "subcore"), dimension_semantics=(pltpu.PARALLEL,),
        )(x_hbm, i0_hbm, i1_hbm)

    return scatter_kernel(rows, idx0.reshape(1, n), idx1.reshape(1, n))


def _route_kernel(lg_ref, tri_ref, info_ref, cnt_ref, carry):
    i = pl.program_id(0)

    @pl.when(i == 0)
    def _():
        carry[...] = jnp.zeros_like(carry)

    lg = lg_ref[...]
    lane = lax.broadcasted_iota(jnp.int32, lg.shape, 1)
    lane_f = lane.astype(F32)
    none = float(ROUTER_COLS)
    ninf = float("-inf")
    first = lambda cond: jnp.min(jnp.where(cond, lane_f, none), axis=-1, keepdims=True)

    is_c = lane < N_EXPERT_GROUPS
    lc = jnp.where(is_c, lg, ninf)
    mc = jnp.max(lc, axis=-1, keepdims=True)
    p_grp = 1.0 / jnp.sum(jnp.exp(lc - mc), axis=-1, keepdims=True)
    grp = first(lc == mc)
    fine = lane - N_EXPERT_GROUPS
    fine_grp = lax.shift_right_arithmetic(fine, jnp.int32(EXPERTS_PER_GROUP.bit_length() - 1))
    in_grp = (fine >= 0) & (fine < N_EXPERTS) & (fine_grp.astype(F32) == grp)
    lf = jnp.where(in_grp, lg, ninf)
    v1 = jnp.max(lf, axis=-1, keepdims=True)
    i1 = first(lf == v1)
    lf2 = jnp.where(lane_f == i1, ninf, lf)
    v2 = jnp.max(lf2, axis=-1, keepdims=True)
    i2 = first(lf2 == v2)
    b = jnp.exp(v2 - v1)
    g0 = p_grp / (1.0 + b)
    g1 = p_grp * b / (1.0 + b)

    hit1 = lane_f == i1
    hit2 = lane_f == i2
    onehot = jnp.where(hit1 | hit2, 1.0, 0.0)
    before = jnp.dot(tri_ref[...], onehot.astype(BF16), preferred_element_type=F32) + carry[0:1, :]
    r1 = jnp.sum(jnp.where(hit1, before, 0.0), axis=-1, keepdims=True)
    r2 = jnp.sum(jnp.where(hit2, before, 0.0), axis=-1, keepdims=True)
    carry[...] = carry[...] + jnp.sum(onehot, axis=0, keepdims=True)

    as_int = lambda v: jnp.broadcast_to(v, lg.shape).astype(jnp.int32)
    as_bits = lambda v: pltpu.bitcast(jnp.broadcast_to(v, lg.shape), jnp.int32)
    info = jnp.zeros(lg.shape, jnp.int32)
    fields = ((INFO_EXPERT, as_int(i1 - N_EXPERT_GROUPS)), (INFO_EXPERT + 1, as_int(i2 - N_EXPERT_GROUPS)),
              (INFO_RANK, as_int(r1)), (INFO_RANK + 1, as_int(r2)), (INFO_GATE, as_bits(g0)),
              (INFO_GATE + 1, as_bits(g1)))
    for col, val in fields:
        info = jnp.where(lane == col, val, info)
    info_ref[...] = info

    @pl.when(i == pl.num_programs(0) - 1)
    def _():
        cnt_ref[...] = carry[...]


def _route(logits):
    t = logits.shape[0]
    tr = min(t, 512)
    assert t % tr == 0
    tri = jnp.asarray(np.tril(np.ones((tr, tr), np.float32), -1), BF16)
    info, cnt = pl.pallas_call(
        _route_kernel,
        grid=(t // tr,),
        in_specs=[pl.BlockSpec((tr, ROUTER_COLS), lambda i: (i, 0)), pl.BlockSpec((tr, tr), lambda i: (0, 0))],
        out_specs=[pl.BlockSpec((tr, ROUTER_COLS), lambda i: (i, 0)),
                   pl.BlockSpec((SUBLANES, ROUTER_COLS), lambda i: (0, 0))],
        out_shape=[jax.ShapeDtypeStruct((t, ROUTER_COLS), jnp.int32),
                   jax.ShapeDtypeStruct((SUBLANES, ROUTER_COLS), F32)],
        scratch_shapes=[pltpu.VMEM((SUBLANES, ROUTER_COLS), F32)],
        compiler_params=_cparams(("arbitrary",)),
    )(logits, tri)
    counts = cnt[0, N_EXPERT_GROUPS:N_EXPERT_GROUPS + N_EXPERTS].astype(jnp.int32)
    return info, counts


def _moe(h_packed, logits, w_gate, w_up, w_down):
    t = logits.shape[0]
    info, counts = _route(logits)
    padded = (counts + MOE_TILE - 1) // MOE_TILE * MOE_TILE
    pend = jnp.cumsum(padded)
    pstart = pend - padded
    nblk = -(-2 * t // MOE_TILE) + N_EXPERTS
    blk_start = jnp.arange(nblk, dtype=jnp.int32) * MOE_TILE
    blk_e = jnp.minimum(jnp.searchsorted(pend, blk_start, side='right'), N_EXPERTS - 1).astype(jnp.int32)
    nvalid = (pend[-1] // MOE_TILE).astype(jnp.int32).reshape(1)
    eid = info[:, INFO_EXPERT:INFO_EXPERT + 2]
    slot = jnp.take(pstart, eid).astype(jnp.int32) + info[:, INFO_RANK:INFO_RANK + 2]
    rows = slot[:, :, None] * ROWS_PER_TOKEN + jnp.arange(ROWS_PER_TOKEN, dtype=jnp.int32)
    idx0 = rows[:, 0].reshape(-1)
    idx1 = rows[:, 1].reshape(-1)
    xs = _sc_scatter_rows(h_packed, idx0, idx1, nblk * MOE_TILE * ROWS_PER_TOKEN)
    ys = _experts(xs, blk_e, nvalid, w_gate, w_up, w_down)
    return _sc_gather_rows(ys, idx0), _sc_gather_rows(ys, idx1), info


def _layer(x, mods_mix, mods_ffn, attn_fn, s5_fn, wts, tm):
    nb, s, d = x.shape
    q, k, v, u = _inproj(x, mods_mix[0], mods_mix[1], wts['g_mix'], wts['w_in'], tm)
    o_attn = attn_fn(q, k, v)
    o_ssm, h_re, h_im = s5_fn(u)
    x1, h2, logits = _outproj(x, o_attn, o_ssm, mods_mix[2], mods_ffn[0], mods_ffn[1], wts['g_attn_out'],
                              wts['g_ffn'], wts['w_out'], wts['w_router'], wts['b_router'], tm)
    y0, y1, info = _moe(h2, logits.reshape(nb * s, ROUTER_COLS), wts['w_gate'], wts['w_up'], wts['w_down'])
    y = _combine(x1, y0, y1, info.reshape(nb, s, ROUTER_COLS), mods_ffn[2], wts['g_final'], tm)
    return y, k, v, h_re, h_im


def kernel(x_prompt, x_sample, cache_k_win, cache_v_win, state_ssm_re, state_ssm_im, c_prompt, c_sample, g_mix, w_ada_mix, b_ada_mix, w_in, w_out, g_attn_out, g_ssm_out, ssm_lambda_re, ssm_lambda_im, ssm_log_dt, ssm_b_re, ssm_b_im, ssm_c_re, ssm_c_im, ssm_d, w_glu, b_glu, g_ffn, w_ada_ffn, b_ada_ffn, w_router_coarse, b_router_coarse, w_router_fine, b_router_fine, w_expert_gate, w_expert_up, w_expert_down, g_final):
    depth = g_mix.shape[0]
    assert depth == 1, "single-layer step"
    l = 0
    bp, sp, d = x_prompt.shape
    bs, ts, _ = x_sample.shape
    keep = min(max(w for w, _ in DILATED_PATTERNS), sp)

    c_all = jnp.concatenate([c_prompt, c_sample], axis=0).astype(F32)
    m_mix = _adaln(c_all, w_ada_mix[l], b_ada_mix[l])
    m_ffn = _adaln(c_all, w_ada_ffn[l], b_ada_ffn[l])

    def mods(m, lo, hi, per_token):
        parts = jnp.split(m[lo:hi], 3, axis=-1)
        if per_token:
            return tuple(jnp.repeat(p, ts, axis=0)[None] for p in parts)
        return tuple(p[:, None, :] for p in parts)

    pad = ROUTER_COLS - N_EXPERT_GROUPS - N_EXPERTS
    w_router = jnp.concatenate([w_router_coarse[l], w_router_fine[l], jnp.zeros((d, pad), F32)], axis=1)
    b_router = jnp.concatenate([b_router_coarse[l], b_router_fine[l], jnp.zeros((pad,), F32)])
    wts = {
        'g_mix': g_mix[l], 'w_in': w_in[l].astype(BF16), 'w_out': w_out[l].astype(BF16),
        'g_attn_out': g_attn_out[l], 'g_ffn': g_ffn[l], 'w_router': w_router, 'b_router': b_router,
        'w_gate': w_expert_gate[l].astype(BF16), 'w_up': w_expert_up[l].astype(BF16),
        'w_down': w_expert_down[l].astype(BF16), 'g_final': g_final,
    }
    a_re, a_im, b_mat, c_mat = _s5_params(ssm_lambda_re[l], ssm_lambda_im[l], ssm_log_dt[l], ssm_b_re[l],
                                          ssm_b_im[l], ssm_c_re[l], ssm_c_im[l])
    s5_w = (a_re, a_im, b_mat, c_mat, ssm_d[l].reshape(-1), w_glu[l].astype(BF16), b_glu[l], g_ssm_out[l])

    yp, kp, vp, hrp, hip = _layer(
        x_prompt, mods(m_mix, 0, bp, False), mods(m_ffn, 0, bp, False), _attn_prompt,
        lambda u: _s5_prompt(u, *s5_w), wts, tm=512)

    n_state = SSM_GROUPS * SSM_STATE
    ck = cache_k_win[l].reshape(bs, -1, D_ATTN)
    cv = cache_v_win[l].reshape(bs, -1, D_ATTN)

    def attn_s(q, k, v):
        r = lambda a: a.reshape(bs, ts, D_ATTN)
        return _attn_sample(r(q), r(k), r(v), ck, cv).reshape(1, bs * ts, D_ATTN)

    def s5_s(u):
        u_tm = u.reshape(bs, ts, D_SSM).transpose(1, 0, 2)
        o, hr, hi = _s5_sample(u_tm, state_ssm_re[l].reshape(bs, n_state), state_ssm_im[l].reshape(bs, n_state),
                               *s5_w)
        return o.transpose(1, 0, 2).reshape(1, bs * ts, D_SSM), hr, hi

    ys, ks, vs, hrs, his = _layer(
        x_sample.reshape(1, bs * ts, d), mods(m_mix, bp, bp + bs, True), mods(m_ffn, bp, bp + bs, True),
        attn_s, s5_s, wts, tm=bs * ts)

    heads = lambda a, b, s: a.reshape(1, b, s, N_HEADS, HEAD_DIM)
    state = lambda a, b: a.reshape(1, b, SSM_GROUPS, SSM_STATE)
    return (yp, ys.reshape(bs, ts, d),
            heads(kp[:, sp - keep:], bp, keep), heads(vp[:, sp - keep:], bp, keep), state(hrp, bp), state(hip, bp),
            heads(ks, bs, ts), heads(vs, bs, ts), state(hrs, bs), state(his, bs))
```

```python
import functools
import math

import numpy as np
import jax
import jax.numpy as jnp
from jax import lax
from jax.experimental import pallas as pl
from jax.experimental.pallas import tpu as pltpu
from jax.experimental.pallas import tpu_sc as plsc

F32 = jnp.float32
BF16 = jnp.bfloat16
HIGHEST = lax.Precision.HIGHEST

D_MODEL = 1024
D_ATTN = 512
D_SSM = 512
HEAD_DIM = 64
N_HEADS = 8
DILATED_PATTERNS = ((128, 1), (512, 4), (2048, 16))
SSM_GROUP_CH = 16
SSM_GROUPS = 32
SSM_STATE = 64
N_EXPERT_GROUPS = 4
EXPERTS_PER_GROUP = 8
N_EXPERTS = 32
D_EXPERT = 512
D_IN_PROJ = 3 * D_ATTN + D_SSM
EPS = 1e-6

LANES = 128
SUBLANES = 8
VMEM_LIMIT = 48 * 1024 * 1024

ATTN_BLOCK = 128
ATTN_UNROLL = 8
NEG_BIG = -1e30
SSM_TT = 128
SSM_PITCH = SSM_TT + 8
ROUTER_COLS = 128
MOE_TILE = 512
SC_WINDOW = 128
ROWS_PER_TOKEN = D_MODEL // 2 // LANES
INFO_EXPERT, INFO_RANK, INFO_GATE = 0, 2, 4


def _cparams(sem, vmem=VMEM_LIMIT):
    return pltpu.CompilerParams(dimension_semantics=sem, vmem_limit_bytes=vmem)


def _adaln_kernel(c_ref, w_ref, b_ref, o_ref):
    c = c_ref[...]
    s = c * jax.nn.sigmoid(c)
    o_ref[...] = jnp.dot(s, w_ref[...], precision=HIGHEST, preferred_element_type=F32) + b_ref[...]


def _adaln(c, w, b):
    r, d = c.shape
    n = w.shape[1]
    tn = 768
    return pl.pallas_call(
        _adaln_kernel,
        grid=(n // tn,),
        in_specs=[pl.BlockSpec((r, d), lambda j: (0, 0)),
                  pl.BlockSpec((d, tn), lambda j: (0, j)),
                  pl.BlockSpec((1, tn), lambda j: (0, j))],
        out_specs=pl.BlockSpec((r, tn), lambda j: (0, j)),
        out_shape=jax.ShapeDtypeStruct((r, n), F32),
        compiler_params=_cparams(("arbitrary",)),
    )(c, w, b.reshape(1, n))


def _inproj_kernel(x_ref, sh_ref, sc_ref, g_ref, w_ref, q_ref, k_ref, v_ref, u_ref):
    x = x_ref[...]
    ms = jnp.mean(x * x, axis=-1, keepdims=True)
    h = x * lax.rsqrt(ms + EPS) * g_ref[...]
    h = h * (1.0 + sc_ref[...]) + sh_ref[...]
    p = jnp.dot(h.astype(BF16), w_ref[...], preferred_element_type=F32)
    q_ref[...] = p[:, :D_ATTN] * (HEAD_DIM ** -0.5)
    k_ref[...] = p[:, D_ATTN:2 * D_ATTN]
    v_ref[...] = p[:, 2 * D_ATTN:3 * D_ATTN]
    u_ref[...] = p[:, 3 * D_ATTN:]


def _mod_spec(mod, tm):
    d = mod.shape[-1]
    if mod.shape[1] == 1:
        return pl.BlockSpec((None, 1, d), lambda b, i: (b, 0, 0))
    return pl.BlockSpec((None, tm, d), lambda b, i: (b, i, 0))


def _inproj(x, shift, scale, g, w_bf16, tm):
    nb, s, d = x.shape
    row = lambda n: pl.BlockSpec((None, tm, n), lambda b, i: (b, i, 0))
    out = jax.ShapeDtypeStruct((nb, s, D_ATTN), F32)
    return pl.pallas_call(
        _inproj_kernel,
        grid=(nb, s // tm),
        in_specs=[row(d), _mod_spec(shift, tm), _mod_spec(scale, tm),
                  pl.BlockSpec((1, d), lambda b, i: (0, 0)),
                  pl.BlockSpec((d, D_IN_PROJ), lambda b, i: (0, 0))],
        out_specs=[row(D_ATTN), row(D_ATTN), row(D_ATTN), row(D_SSM)],
        out_shape=[out, out, out, jax.ShapeDtypeStruct((nb, s, D_SSM), F32)],
        compiler_params=_cparams(("parallel", "parallel")),
    )(x, shift, scale, g.reshape(1, d), w_bf16)


def _band_bias():
    qi = np.arange(ATTN_BLOCK)[:, None]
    kj = np.arange(ATTN_BLOCK)[None, :]
    cur = kj <= qi
    prev = kj >= qi
    to_bias = lambda m: np.tile(np.where(m, 0.0, NEG_BIG).astype(np.float32), (2, 1))
    return to_bias(np.concatenate([prev, cur], axis=1)), to_bias(cur)


def _attn_prompt_kernel(q_ref, k_ref, v_ref, bias2_ref, bias1_ref, o_ref, acc_scr, m_scr, l_scr):
    s_len = q_ref.shape[0]
    lane = lax.broadcasted_iota(jnp.int32, (1, LANES), 1)
    head0 = lane < HEAD_DIM

    def rows(ref, start, n, d):
        if d == 1:
            return ref[pl.ds(start, n), :]
        return ref[pl.ds(start, n, stride=d), :]

    def tiles(items, first):
        nk = ATTN_BLOCK if first else 2 * ATTN_BLOCK
        bias = bias1_ref[...] if first else bias2_ref[...]
        one = jnp.ones((), BF16)
        loaded = []
        for _, d, qstart in items:
            kstart = qstart if first else qstart - d * ATTN_BLOCK
            loaded.append((rows(q_ref, qstart, ATTN_BLOCK, d), rows(k_ref, kstart, nk, d),
                           rows(v_ref, kstart, nk, d)))
        results = []
        for qrows, krows, vrows in loaded:
            kb = krows.astype(BF16)
            vb = vrows.astype(BF16)
            q2 = jnp.concatenate([jnp.where(head0, qrows, 0.0), jnp.where(head0, 0.0, qrows)], axis=0).astype(BF16)
            s = lax.dot_general(q2, kb, (((1,), (1,)), ((), ())), preferred_element_type=F32) + bias
            m = jnp.max(s, axis=-1, keepdims=True)
            p = jnp.exp(s - m).astype(BF16)
            pv0 = jnp.dot(p[:ATTN_BLOCK], jnp.where(head0, vb, one), preferred_element_type=F32)
            pv1 = jnp.dot(p[ATTN_BLOCK:], jnp.where(head0, one, vb), preferred_element_type=F32)
            results.append((jnp.where(head0, pv0, pv1), jnp.where(head0, m[:ATTN_BLOCK], m[ATTN_BLOCK:]),
                            jnp.where(head0, pv1, pv0)))
        for (br, d, qstart), (acc, m, l) in zip(items, results):
            dst = pl.ds(qstart, ATTN_BLOCK) if d == 1 else pl.ds(qstart, ATTN_BLOCK, stride=d)
            acc_scr[br, dst, :] = acc
            m_scr[br, dst, :] = m
            l_scr[br, dst, :] = l

    def largest_divisor(n):
        return max(f for f in range(1, ATTN_UNROLL + 1) if n % f == 0)

    grouped = []
    for br, (window, d) in enumerate(DILATED_PATTERNS):
        assert window // d == ATTN_BLOCK
        if d <= ATTN_UNROLL:
            grouped += [(br, d, r) for r in range(d)]
        else:
            un = largest_divisor(d)

            def first_body(i, carry, br=br, d=d, un=un):
                tiles([(br, d, i * un + j) for j in range(un)], True)
                return carry

            lax.fori_loop(0, d // un, first_body, 0)
    for i in range(0, len(grouped), ATTN_UNROLL):
        tiles(grouped[i:i + ATTN_UNROLL], True)

    for br, (window, d) in enumerate(DILATED_PATTERNS):
        nblk = s_len // (d * ATTN_BLOCK)
        n_rest = d * (nblk - 1)
        if n_rest:
            un = largest_divisor(n_rest)

            def rest_body(i, carry, br=br, d=d, nblk=nblk, un=un):
                items = []
                for j in range(un):
                    n = i * un + j
                    qstart = n // (nblk - 1) + d * ATTN_BLOCK * (1 + n % (nblk - 1))
                    items.append((br, d, pl.multiple_of(qstart, ATTN_BLOCK) if d == 1 else qstart))
                tiles(items, False)
                return carry

            lax.fori_loop(0, n_rest // un, rest_body, 0)

    ch = 256

    def merge_body(i, carry):
        sl = pl.ds(pl.multiple_of(i * ch, ch), ch)
        ms = [m_scr[br, sl, :] for br in range(3)]
        mx = jnp.maximum(jnp.maximum(ms[0], ms[1]), ms[2])
        num = jnp.zeros((ch, LANES), F32)
        den = jnp.zeros((ch, LANES), F32)
        for br in range(3):
            e = jnp.exp(ms[br] - mx)
            num = num + e * acc_scr[br, sl, :]
            den = den + e * pltpu.roll(l_scr[br, sl, :], HEAD_DIM, 1)
        o_ref[sl, :] = num / den
        return carry

    lax.fori_loop(0, s_len // ch, merge_body, 0)


def _attn_prompt(q, k, v):
    nb, s, _ = q.shape
    assert s % (16 * ATTN_BLOCK) == 0
    bias2, bias1 = _band_bias()
    blk = pl.BlockSpec((None, s, LANES), lambda b, p: (b, 0, p))
    const = lambda a: pl.BlockSpec(a.shape, lambda b, p: (0, 0))
    return pl.pallas_call(
        _attn_prompt_kernel,
        grid=(nb, D_ATTN // LANES),
        in_specs=[blk, blk, blk, const(bias2), const(bias1)],
        out_specs=blk,
        out_shape=jax.ShapeDtypeStruct((nb, s, D_ATTN), F32),
        scratch_shapes=[pltpu.VMEM((3, s, LANES), F32)] * 3,
        compiler_params=_cparams(("parallel", "parallel")),
    )(q, k, v, jnp.asarray(bias2), jnp.asarray(bias1))


def _sample_plan(t_new, w_buf):
    group = max(d for _, d in DILATED_PATTERNS)
    reach = max(w for w, d in DILATED_PATTERNS if d < group)
    assert w_buf % group == 0 and group % t_new == 0 and reach % group == 0 and w_buf > reach
    n_tail = reach // group
    n_head = w_buf // group - n_tail
    assert n_head % n_tail == 0
    pos = np.concatenate([
        (np.arange(n_head)[:, None] * group + np.arange(t_new)[None, :]).reshape(-1),
        n_head * group + np.arange(n_tail * group),
        w_buf + np.arange(t_new),
    ])
    n_keys = -(-pos.size * N_HEADS // LANES) * LANES // N_HEADS
    pos = np.concatenate([pos, np.full(n_keys - pos.size, -1)])
    t = np.arange(t_new)[:, None]
    dist = w_buf + t - pos[None, :]
    mult = np.zeros(dist.shape, np.int64)
    for window, d in DILATED_PATTERNS:
        mult += ((pos[None, :] >= 0) & (dist >= 0) & (dist % d == 0) & (dist <= window)).astype(np.int64)
    logm = np.where(mult > 0, np.log(np.maximum(mult, 1)), NEG_BIG)
    same_head = np.eye(N_HEADS, dtype=bool)
    bias = np.where(same_head[None, :, None, :], logm[:, None, :, None], NEG_BIG)
    bias = bias.reshape(t_new * N_HEADS, n_keys * N_HEADS).astype(np.float32)
    return group, n_head, n_tail, bias


def _attn_sample_kernel(q_ref, kn_ref, vn_ref, kh_ref, kt_ref, vh_ref, vt_ref, bias_ref, o_ref):
    n_rows = bias_ref.shape[1]

    def keys(head_ref, tail_ref, new_ref):
        flat = lambda a: a.reshape(-1, HEAD_DIM)
        parts = [flat(head_ref[...]), flat(tail_ref[...]), new_ref[...]]
        pad = n_rows - sum(p.shape[0] for p in parts)
        return jnp.concatenate(parts + [jnp.zeros((pad, HEAD_DIM), F32)], axis=0).astype(BF16)

    s = lax.dot_general(q_ref[...].astype(BF16), keys(kh_ref, kt_ref, kn_ref), (((1,), (1,)), ((), ())),
                        preferred_element_type=F32) + bias_ref[...]
    m = jnp.max(s, axis=-1, keepdims=True)
    p = jnp.exp(s - m)
    l = jnp.sum(p, axis=-1, keepdims=True)
    o_ref[...] = jnp.dot(p.astype(BF16), keys(vh_ref, vt_ref, vn_ref), preferred_element_type=F32) / l


def _attn_sample(q, k_new, v_new, cache_k, cache_v):
    nb, t_new, _ = q.shape
    w_buf = cache_k.shape[1]
    group, n_head, n_tail, bias = _sample_plan(t_new, w_buf)
    rows = lambda a: a.reshape(nb, t_new * N_HEADS, HEAD_DIM)
    grouped = lambda a: a.reshape(nb, w_buf // group, group, N_HEADS, HEAD_DIM)
    new = pl.BlockSpec((None, t_new * N_HEADS, HEAD_DIM), lambda b: (b, 0, 0))
    head = pl.BlockSpec((None, n_head, t_new, N_HEADS, HEAD_DIM), lambda b: (b, 0, 0, 0, 0))
    tail = pl.BlockSpec((None, n_tail, group, N_HEADS, HEAD_DIM), lambda b: (b, n_head // n_tail, 0, 0, 0))
    out = pl.pallas_call(
        _attn_sample_kernel,
        grid=(nb,),
        in_specs=[new, new, new, head, tail, head, tail, pl.BlockSpec(bias.shape, lambda b: (0, 0))],
        out_specs=new,
        out_shape=jax.ShapeDtypeStruct((nb, t_new * N_HEADS, HEAD_DIM), F32),
        compiler_params=_cparams(("parallel",)),
    )(rows(q), rows(k_new), rows(v_new), grouped(cache_k), grouped(cache_k), grouped(cache_v), grouped(cache_v),
      jnp.asarray(bias))
    return out.reshape(nb, t_new, D_ATTN)


def _s5_params(lam_re, lam_im, log_dt, b_re, b_im, c_re, c_im):
    f32 = F32
    dt = jnp.exp(log_dt.astype(f32))[:, None]
    lr, li = lam_re.astype(f32), lam_im.astype(f32)
    ea = jnp.exp(lr * dt)
    a_re, a_im = ea * jnp.cos(li * dt), ea * jnp.sin(li * dt)
    den = lr * lr + li * li
    co_re = ((a_re - 1.0) * lr + a_im * li) / den
    co_im = (a_im * lr - (a_re - 1.0) * li) / den
    bb_re = co_re[..., None] * b_re - co_im[..., None] * b_im
    bb_im = co_re[..., None] * b_im + co_im[..., None] * b_re
    eye = jnp.eye(8, dtype=f32)

    def b_blocks(bb):
        t = bb.reshape(4, 8, SSM_STATE, SSM_GROUP_CH)
        return jnp.einsum('ab,kapc->kacbp', eye, t).reshape(4, 8 * SSM_GROUP_CH, 8 * SSM_STATE)

    def c_blocks(cc):
        t = cc.reshape(4, 8, SSM_GROUP_CH, SSM_STATE)
        return jnp.einsum('ab,kacp->kbpac', eye, t).reshape(4, 8 * SSM_STATE, 8 * SSM_GROUP_CH)

    b_mat = jnp.concatenate([b_blocks(bb_re), b_blocks(bb_im)], axis=2).astype(BF16)
    c_mat = jnp.concatenate([c_blocks(c_re.astype(f32)), -c_blocks(c_im.astype(f32))], axis=1).astype(BF16)
    return a_re.reshape(-1), a_im.reshape(-1), b_mat, c_mat


def _gelu_tanh(y):
    return 0.5 * y * (1.0 + jnp.tanh(math.sqrt(2.0 / math.pi) * (y + 0.044715 * (y * y * y))))


def _s5_epilogue(y, u, d_ref, wglu_ref, bglu_ref, g_ref):
    y = y + d_ref[...] * u
    z = _gelu_tanh(y)
    gate = jnp.dot(z.astype(BF16), wglu_ref[...], preferred_element_type=F32) + bglu_ref[...]
    out = z * jax.nn.sigmoid(gate)
    ms = jnp.mean(out * out, axis=-1, keepdims=True)
    return out * lax.rsqrt(ms + EPS) * g_ref[...]


def _s5_prompt_kernel(u_ref, are_ref, aim_ref, b_ref, c_ref, d_ref, wglu_ref, bglu_ref, g_ref,
                      o_ref, hre_ref, him_ref, scr, hst):
    nseq, tt, _ = u_ref.shape
    n_slab = D_SSM * SSM_STATE // SSM_GROUP_CH // LANES
    ti = pl.program_id(1)

    @pl.when(ti == 0)
    def _():
        hst[...] = jnp.zeros_like(hst)

    u2 = u_ref[...].reshape(nseq * tt, D_SSM)
    ub = u2.astype(BF16)
    for kc in range(4):
        bu = jnp.dot(ub[:, kc * LANES:(kc + 1) * LANES], b_ref[kc], preferred_element_type=F32)
        for part in range(2):
            for j in range(4):
                col = part * 512 + j * LANES
                for b in range(nseq):
                    scr[part * n_slab + 4 * kc + j, b * SSM_PITCH:b * SSM_PITCH + tt, :] = (
                        bu[b * tt:(b + 1) * tt, col:col + LANES])

    grp = 4
    for sg in range(n_slab // grp):
        slabs = [sg * grp + i for i in range(grp)]
        ar = [are_ref[s] for s in slabs]
        ai = [aim_ref[s] for s in slabs]

        def step(t, carry, slabs=slabs, ar=ar, ai=ai):
            hr, hi = carry
            nhr, nhi = [], []
            for i, s in enumerate(slabs):
                sel = pl.ds(t, nseq, stride=SSM_PITCH)
                br = scr[s, sel, :]
                bi = scr[n_slab + s, sel, :]
                r = ar[i] * hr[i] - ai[i] * hi[i] + br
                im = ar[i] * hi[i] + ai[i] * hr[i] + bi
                scr[s, sel, :] = r
                scr[n_slab + s, sel, :] = im
                nhr.append(r)
                nhi.append(im)
            return tuple(nhr), tuple(nhi)

        init = (tuple(hst[s] for s in slabs), tuple(hst[n_slab + s] for s in slabs))
        hr, hi = lax.fori_loop(0, tt, step, init, unroll=4)
        for i, s in enumerate(slabs):
            hst[s] = hr[i]
            hst[n_slab + s] = hi[i]

    ys = []
    for kc in range(4):
        slabs = [4 * kc + j for j in range(4)] + [n_slab + 4 * kc + j for j in range(4)]
        lhs = jnp.concatenate(
            [jnp.concatenate([scr[s, b * SSM_PITCH:b * SSM_PITCH + tt, :].astype(BF16) for s in slabs], axis=1)
             for b in range(nseq)], axis=0)
        ys.append(jnp.dot(lhs, c_ref[kc], preferred_element_type=F32))
    y = jnp.concatenate(ys, axis=1)
    o_ref[...] = _s5_epilogue(y, u2, d_ref, wglu_ref, bglu_ref, g_ref).reshape(nseq, tt, D_SSM)

    @pl.when(ti == pl.num_programs(1) - 1)
    def _():
        for s in range(n_slab):
            hre_ref[:, s * LANES:(s + 1) * LANES] = hst[s]
            him_ref[:, s * LANES:(s + 1) * LANES] = hst[n_slab + s]


def _s5_prompt(u, a_re, a_im, b_mat, c_mat, d, w_glu_bf16, b_glu, g_out):
    nb, s, _ = u.shape
    nseq = SUBLANES
    assert nb % nseq == 0 and s % SSM_TT == 0
    n_state = a_re.shape[0]
    n_slab = n_state // LANES
    bcast = lambda a: jnp.broadcast_to(a.reshape(n_slab, 1, LANES), (n_slab, nseq, LANES))
    const = lambda a: pl.BlockSpec(a.shape, lambda b, i: (0,) * a.ndim)
    args = (u, bcast(a_re), bcast(a_im), b_mat, c_mat, d.reshape(1, D_SSM), w_glu_bf16,
            b_glu.reshape(1, D_SSM), g_out.reshape(1, D_SSM))
    st = jax.ShapeDtypeStruct((nb, n_state), F32)
    return pl.pallas_call(
        _s5_prompt_kernel,
        grid=(nb // nseq, s // SSM_TT),
        in_specs=[pl.BlockSpec((nseq, SSM_TT, D_SSM), lambda b, i: (b, i, 0))] + [const(a) for a in args[1:]],
        out_specs=[pl.BlockSpec((nseq, SSM_TT, D_SSM), lambda b, i: (b, i, 0)),
                   pl.BlockSpec((nseq, n_state), lambda b, i: (b, 0)),
                   pl.BlockSpec((nseq, n_state), lambda b, i: (b, 0))],
        out_shape=[jax.ShapeDtypeStruct((nb, s, D_SSM), F32), st, st],
        scratch_shapes=[pltpu.VMEM((2 * n_slab, nseq * SSM_PITCH, LANES), F32),
                        pltpu.VMEM((2 * n_slab, nseq, LANES), F32)],
        compiler_params=_cparams(("parallel", "arbitrary")),
    )(*args)


def _s5_sample_kernel(u_ref, h0re_ref, h0im_ref, are_ref, aim_ref, b_ref, c_ref, d_ref, wglu_ref, bglu_ref, g_ref,
                      o_ref, hre_ref, him_ref):
    t_new = u_ref.shape[0]
    hre = h0re_ref[...]
    him = h0im_ref[...]
    are, aim = are_ref[...], aim_ref[...]
    half = 4 * LANES
    for t in range(t_new):
        u = u_ref[t]
        ub = u.astype(BF16)
        bus = [jnp.dot(ub[:, kc * LANES:(kc + 1) * LANES], b_ref[kc], preferred_element_type=F32) for kc in range(4)]
        bre = jnp.concatenate([bu[:, :half] for bu in bus], axis=1)
        bim = jnp.concatenate([bu[:, half:] for bu in bus], axis=1)
        hre, him = are * hre - aim * him + bre, are * him + aim * hre + bim
        ys = []
        for kc in range(4):
            lhs = jnp.concatenate([hre[:, kc * half:(kc + 1) * half], him[:, kc * half:(kc + 1) * half]], axis=1)
            ys.append(jnp.dot(lhs.astype(BF16), c_ref[kc], preferred_element_type=F32))
        y = jnp.concatenate(ys, axis=1)
        o_ref[t] = _s5_epilogue(y, u, d_ref, wglu_ref, bglu_ref, g_ref)
    hre_ref[...] = hre
    him_ref[...] = him


def _s5_sample(u_tm, h0_re, h0_im, a_re, a_im, b_mat, c_mat, d, w_glu_bf16, b_glu, g_out):
    t_new, nb, _ = u_tm.shape
    n_state = a_re.shape[0]
    tb = 64
    assert nb % tb == 0
    const = lambda a: pl.BlockSpec(a.shape, lambda b: (0,) * a.ndim)
    args = (u_tm, h0_re, h0_im, a_re.reshape(1, n_state), a_im.reshape(1, n_state), b_mat, c_mat,
            d.reshape(1, D_SSM), w_glu_bf16, b_glu.reshape(1, D_SSM), g_out.reshape(1, D_SSM))
    st_spec = pl.BlockSpec((tb, n_state), lambda b: (b, 0))
    st = jax.ShapeDtypeStruct((nb, n_state), F32)
    return pl.pallas_call(
        _s5_sample_kernel,
        grid=(nb // tb,),
        in_specs=[pl.BlockSpec((t_new, tb, D_SSM), lambda b: (0, b, 0)), st_spec, st_spec]
                 + [const(a) for a in args[3:]],
        out_specs=[pl.BlockSpec((t_new, tb, D_SSM), lambda b: (0, b, 0)), st_spec, st_spec],
        out_shape=[jax.ShapeDtypeStruct((t_new, nb, D_SSM), F32), st, st],
        compiler_params=_cparams(("parallel",)),
    )(*args)


def _store_packed(ref, val):
    rows, d = val.shape
    half = d // 2
    bits = pltpu.bitcast(val.astype(BF16).astype(F32), jnp.uint32)
    words = bits[:, :half] | lax.shift_right_logical(bits[:, half:], jnp.uint32(16))
    for c in range(ROWS_PER_TOKEN):
        ref[pl.ds(c, rows, stride=ROWS_PER_TOKEN), :] = words[:, c * LANES:(c + 1) * LANES]


def _load_packed(ref, rows):
    hi, lo = [], []
    for c in range(ROWS_PER_TOKEN):
        w = ref[pl.ds(c, rows, stride=ROWS_PER_TOKEN), :]
        hi.append(pltpu.bitcast(w & jnp.uint32(0xFFFF0000), F32))
        lo.append(pltpu.bitcast(lax.shift_left(w, jnp.uint32(16)), F32))
    return jnp.concatenate(hi + lo, axis=1)


def _outproj_kernel(x_ref, oa_ref, os_ref, gate_ref, sh_ref, sc_ref, ga_ref, gf_ref, w_ref, wr_hi_ref, wr_lo_ref,
                    br_ref, x1_ref, h_ref, lg_ref):
    oa = oa_ref[...]
    ms = jnp.mean(oa * oa, axis=-1, keepdims=True)
    na = oa * lax.rsqrt(ms + EPS) * ga_ref[...]
    merged = jnp.concatenate([na, os_ref[...]], axis=-1).astype(BF16)
    x1 = x_ref[...] + gate_ref[...] * jnp.dot(merged, w_ref[...], preferred_element_type=F32)
    x1_ref[...] = x1
    ms = jnp.mean(x1 * x1, axis=-1, keepdims=True)
    h = x1 * lax.rsqrt(ms + EPS) * gf_ref[...]
    h = h * (1.0 + sc_ref[...]) + sh_ref[...]
    h_hi = h.astype(BF16)
    _store_packed(h_ref, h)
    h_lo = (h - h_hi.astype(F32)).astype(BF16)
    lg = jnp.dot(h_hi, wr_hi_ref[...], preferred_element_type=F32)
    lg = lg + jnp.dot(h_hi, wr_lo_ref[...], preferred_element_type=F32)
    lg = lg + jnp.dot(h_lo, wr_hi_ref[...], preferred_element_type=F32)
    lg_ref[...] = lg + br_ref[...]


def _outproj(x, o_attn, o_ssm, gate, shift, scale, g_attn, g_ffn, w_out_bf16, w_router, b_router, tm):
    nb, s, d = x.shape
    row = lambda n: pl.BlockSpec((None, tm, n), lambda b, i: (b, i, 0))
    const = lambda a: pl.BlockSpec(a.shape, lambda b, i: (0,) * a.ndim)
    wr_hi = w_router.astype(BF16)
    wr_lo = (w_router - wr_hi.astype(F32)).astype(BF16)
    consts = (g_attn.reshape(1, D_ATTN), g_ffn.reshape(1, d), w_out_bf16, wr_hi, wr_lo,
              b_router.reshape(1, ROUTER_COLS))
    return pl.pallas_call(
        _outproj_kernel,
        grid=(nb, s // tm),
        in_specs=[row(d), row(D_ATTN), row(D_SSM), _mod_spec(gate, tm), _mod_spec(shift, tm), _mod_spec(scale, tm)]
                 + [const(a) for a in consts],
        out_specs=[row(d), pl.BlockSpec((tm * ROWS_PER_TOKEN, LANES), lambda b, i: (b * (s // tm) + i, 0)),
                   row(ROUTER_COLS)],
        out_shape=[jax.ShapeDtypeStruct((nb, s, d), F32),
                   jax.ShapeDtypeStruct((nb * s * ROWS_PER_TOKEN, LANES), jnp.uint32),
                   jax.ShapeDtypeStruct((nb, s, ROUTER_COLS), F32)],
        compiler_params=_cparams(("parallel", "parallel")),
    )(x, o_attn, o_ssm, gate, shift, scale, *consts)


def _expert_kernel(blk_e_ref, nvalid_ref, x_ref, wg_ref, wu_ref, wd_ref, y_ref):
    i = pl.program_id(0)

    @pl.when(i < nvalid_ref[0])
    def _():
        x = _load_packed(x_ref, MOE_TILE).astype(BF16)
        a = jnp.dot(x, wg_ref[...], preferred_element_type=F32)
        b = jnp.dot(x, wu_ref[...], preferred_element_type=F32)
        hid = (a * jax.nn.sigmoid(a) * b).astype(BF16)
        _store_packed(y_ref, jnp.dot(hid, wd_ref[...], preferred_element_type=F32))

    @pl.when(i >= nvalid_ref[0])
    def _():
        y_ref[...] = jnp.zeros_like(y_ref)


def _experts(xs, blk_e, nvalid, w_gate, w_up, w_down):
    nblk = xs.shape[0] // (MOE_TILE * ROWS_PER_TOKEN)
    d = w_gate.shape[1]
    wspec = lambda shp: pl.BlockSpec((None,) + shp, lambda i, be, nv: (be[i], 0, 0))
    tile = pl.BlockSpec((MOE_TILE * ROWS_PER_TOKEN, LANES), lambda i, be, nv: (i, 0))
    return pl.pallas_call(
        _expert_kernel,
        grid_spec=pltpu.PrefetchScalarGridSpec(
            num_scalar_prefetch=2,
            grid=(nblk,),
            in_specs=[tile, wspec((d, D_EXPERT)), wspec((d, D_EXPERT)), wspec((D_EXPERT, d))],
            out_specs=tile,
        ),
        out_shape=jax.ShapeDtypeStruct(xs.shape, jnp.uint32),
        compiler_params=_cparams(("arbitrary",)),
    )(blk_e, nvalid, xs, w_gate, w_up, w_down)


def _combine_kernel(x1_ref, y0_ref, y1_ref, info_ref, gate_ref, gfin_ref, o_ref):
    tm = x1_ref.shape[0]
    info = pltpu.bitcast(info_ref[...], F32)
    g0 = info[:, INFO_GATE:INFO_GATE + 1]
    g1 = info[:, INFO_GATE + 1:INFO_GATE + 2]
    moe = g0 * _load_packed(y0_ref, tm) + g1 * _load_packed(y1_ref, tm)
    x2 = x1_ref[...] + gate_ref[...] * moe
    ms = jnp.mean(x2 * x2, axis=-1, keepdims=True)
    o_ref[...] = x2 * lax.rsqrt(ms + EPS) * gfin_ref[...]


def _combine(x1, y0, y1, info, gate, g_final, tm):
    nb, s, d = x1.shape
    row = lambda n: pl.BlockSpec((None, tm, n), lambda b, i: (b, i, 0))
    packed = pl.BlockSpec((tm * ROWS_PER_TOKEN, LANES), lambda b, i: (b * (s // tm) + i, 0))
    return pl.pallas_call(
        _combine_kernel,
        grid=(nb, s // tm),
        in_specs=[row(d), packed, packed, row(ROUTER_COLS), _mod_spec(gate, tm),
                  pl.BlockSpec((1, d), lambda b, i: (0, 0))],
        out_specs=row(d),
        out_shape=jax.ShapeDtypeStruct((nb, s, d), F32),
        compiler_params=_cparams(("parallel", "parallel")),
    )(x1, y0, y1, info, gate, g_final.reshape(1, d))


def _sc_window(n_rows):
    assert n_rows % SC_WINDOW == 0
    return SC_WINDOW


def _sc_gather_rows(table, idx):
    n = idx.shape[0]
    w = _sc_window(n)
    mesh = plsc.VectorSubcoreMesh(core_axis_name="core", subcore_axis_name="subcore")

    @functools.partial(pl.kernel, out_type=jax.ShapeDtypeStruct((n, LANES), table.dtype), mesh=mesh)
    def gather_kernel(x_hbm, i_hbm, o_hbm):
        def body(i_vmem, o_vmem):
            pltpu.sync_copy(x_hbm.at[i_vmem.at[0]], o_vmem)

        pltpu.emit_pipeline(
            body, grid=(n // w,),
            in_specs=[pl.BlockSpec((1, w), lambda i: (0, i))],
            out_specs=[pl.BlockSpec((w, LANES), lambda i: (i, 0))],
            core_axis_name=("core", "subcore"), dimension_semantics=(pltpu.PARALLEL,),
        )(i_hbm, o_hbm)

    return gather_kernel(table, idx.reshape(1, n))


def _sc_scatter_rows(rows, idx0, idx1, n_out):
    n = rows.shape[0]
    w = _sc_window(n)
    mesh = plsc.VectorSubcoreMesh(core_axis_name="core", subcore_axis_name="subcore")

    @functools.partial(pl.kernel, out_type=jax.ShapeDtypeStruct((n_out, LANES), rows.dtype), mesh=mesh)
    def scatter_kernel(x_hbm, i0_hbm, i1_hbm, o_hbm):
        def body(x_vmem, i0_vmem, i1_vmem):
            pltpu.sync_copy(x_vmem, o_hbm.at[i0_vmem.at[0]])
            pltpu.sync_copy(x_vmem, o_hbm.at[i1_vmem.at[0]])

        pltpu.emit_pipeline(
            body, grid=(n // w,),
            in_specs=[pl.BlockSpec((w, LANES), lambda i: (i, 0)),
                      pl.BlockSpec((1, w), lambda i: (0, i)),
                      pl.BlockSpec((1, w), lambda i: (0, i))],
            out_specs=[],
            core_axis_name=("core", "subcore"), dimension_semantics=(pltpu.PARALLEL,),
        )(x_hbm, i0_hbm, i1_hbm)

    return scatter_kernel(rows, idx0.reshape(1, n), idx1.reshape(1, n))


def _route_kernel(lg_ref, tri_ref, info_ref, cnt_ref, carry):
    i = pl.program_id(0)

    @pl.when(i == 0)
    def _():
        carry[...] = jnp.zeros_like(carry)

    lg = lg_ref[...]
    lane = lax.broadcasted_iota(jnp.int32, lg.shape, 1)
    lane_f = lane.astype(F32)
    none = float(ROUTER_COLS)
    ninf = float("-inf")
    first = lambda cond: jnp.min(jnp.where(cond, lane_f, none), axis=-1, keepdims=True)

    is_c = lane < N_EXPERT_GROUPS
    lc = jnp.where(is_c, lg, ninf)
    mc = jnp.max(lc, axis=-1, keepdims=True)
    p_grp = 1.0 / jnp.sum(jnp.exp(lc - mc), axis=-1, keepdims=True)
    grp = first(lc == mc)
    fine = lane - N_EXPERT_GROUPS
    fine_grp = lax.shift_right_arithmetic(fine, jnp.int32(EXPERTS_PER_GROUP.bit_length() - 1))
    in_grp = (fine >= 0) & (fine < N_EXPERTS) & (fine_grp.astype(F32) == grp)
    lf = jnp.where(in_grp, lg, ninf)
    v1 = jnp.max(lf, axis=-1, keepdims=True)
    i1 = first(lf == v1)
    lf2 = jnp.where(lane_f == i1, ninf, lf)
    v2 = jnp.max(lf2, axis=-1, keepdims=True)
    i2 = first(lf2 == v2)
    b = jnp.exp(v2 - v1)
    g0 = p_grp / (1.0 + b)
    g1 = p_grp * b / (1.0 + b)

    hit1 = lane_f == i1
    hit2 = lane_f == i2
    onehot = jnp.where(hit1 | hit2, 1.0, 0.0)
    before = jnp.dot(tri_ref[...], onehot.astype(BF16), preferred_element_type=F32) + carry[0:1, :]
    r1 = jnp.sum(jnp.where(hit1, before, 0.0), axis=-1, keepdims=True)
    r2 = jnp.sum(jnp.where(hit2, before, 0.0), axis=-1, keepdims=True)
    carry[...] = carry[...] + jnp.sum(onehot, axis=0, keepdims=True)

    as_int = lambda v: jnp.broadcast_to(v, lg.shape).astype(jnp.int32)
    as_bits = lambda v: pltpu.bitcast(jnp.broadcast_to(v, lg.shape), jnp.int32)
    info = jnp.zeros(lg.shape, jnp.int32)
    fields = ((INFO_EXPERT, as_int(i1 - N_EXPERT_GROUPS)), (INFO_EXPERT + 1, as_int(i2 - N_EXPERT_GROUPS)),
              (INFO_RANK, as_int(r1)), (INFO_RANK + 1, as_int(r2)), (INFO_GATE, as_bits(g0)),
              (INFO_GATE + 1, as_bits(g1)))
    for col, val in fields:
        info = jnp.where(lane == col, val, info)
    info_ref[...] = info

    @pl.when(i == pl.num_programs(0) - 1)
    def _():
        cnt_ref[...] = carry[...]


def _route(logits):
    t = logits.shape[0]
    tr = min(t, 512)
    assert t % tr == 0
    tri = jnp.asarray(np.tril(np.ones((tr, tr), np.float32), -1), BF16)
    info, cnt = pl.pallas_call(
        _route_kernel,
        grid=(t // tr,),
        in_specs=[pl.BlockSpec((tr, ROUTER_COLS), lambda i: (i, 0)), pl.BlockSpec((tr, tr), lambda i: (0, 0))],
        out_specs=[pl.BlockSpec((tr, ROUTER_COLS), lambda i: (i, 0)),
                   pl.BlockSpec((SUBLANES, ROUTER_COLS), lambda i: (0, 0))],
        out_shape=[jax.ShapeDtypeStruct((t, ROUTER_COLS), jnp.int32),
                   jax.ShapeDtypeStruct((SUBLANES, ROUTER_COLS), F32)],
        scratch_shapes=[pltpu.VMEM((SUBLANES, ROUTER_COLS), F32)],
        compiler_params=_cparams(("arbitrary",)),
    )(logits, tri)
    counts = cnt[0, N_EXPERT_GROUPS:N_EXPERT_GROUPS + N_EXPERTS].astype(jnp.int32)
    return info, counts


def _moe(h_packed, logits, w_gate, w_up, w_down):
    t = logits.shape[0]
    info, counts = _route(logits)
    padded = (counts + MOE_TILE - 1) // MOE_TILE * MOE_TILE
    pend = jnp.cumsum(padded)
    pstart = pend - padded
    nblk = -(-2 * t // MOE_TILE) + N_EXPERTS
    blk_start = jnp.arange(nblk, dtype=jnp.int32) * MOE_TILE
    blk_e = jnp.minimum(jnp.sum(blk_start[:, None] >= pend[None, :], axis=1), N_EXPERTS - 1).astype(jnp.int32)
    nvalid = (pend[-1] // MOE_TILE).astype(jnp.int32).reshape(1)
    eid = info[:, INFO_EXPERT:INFO_EXPERT + 2]
    slot = jnp.take(pstart, eid).astype(jnp.int32) + info[:, INFO_RANK:INFO_RANK + 2]
    rows = slot[:, :, None] * ROWS_PER_TOKEN + jnp.arange(ROWS_PER_TOKEN, dtype=jnp.int32)
    idx0 = rows[:, 0].reshape(-1)
    idx1 = rows[:, 1].reshape(-1)
    xs = _sc_scatter_rows(h_packed, idx0, idx1, nblk * MOE_TILE * ROWS_PER_TOKEN)
    ys = _experts(xs, blk_e, nvalid, w_gate, w_up, w_down)
    return _sc_gather_rows(ys, idx0), _sc_gather_rows(ys, idx1), info


def _layer(x, mods_mix, mods_ffn, attn_fn, s5_fn, wts, tm):
    nb, s, d = x.shape
    q, k, v, u = _inproj(x, mods_mix[0], mods_mix[1], wts['g_mix'], wts['w_in'], tm)
    o_attn = attn_fn(q, k, v)
    o_ssm, h_re, h_im = s5_fn(u)
    x1, h2, logits = _outproj(x, o_attn, o_ssm, mods_mix[2], mods_ffn[0], mods_ffn[1], wts['g_attn_out'],
                              wts['g_ffn'], wts['w_out'], wts['w_router'], wts['b_router'], tm)
    y0, y1, info = _moe(h2, logits.reshape(nb * s, ROUTER_COLS), wts['w_gate'], wts['w_up'], wts['w_down'])
    y = _combine(x1, y0, y1, info.reshape(nb, s, ROUTER_COLS), mods_ffn[2], wts['g_final'], tm)
    return y, k, v, h_re, h_im


def kernel(x_prompt, x_sample, cache_k_win, cache_v_win, state_ssm_re, state_ssm_im, c_prompt, c_sample, g_mix, w_ada_mix, b_ada_mix, w_in, w_out, g_attn_out, g_ssm_out, ssm_lambda_re, ssm_lambda_im, ssm_log_dt, ssm_b_re, ssm_b_im, ssm_c_re, ssm_c_im, ssm_d, w_glu, b_glu, g_ffn, w_ada_ffn, b_ada_ffn, w_router_coarse, b_router_coarse, w_router_fine, b_router_fine, w_expert_gate, w_expert_up, w_expert_down, g_final):
    depth = g_mix.shape[0]
    assert depth == 1, "single-layer step"
    l = 0
    bp, sp, d = x_prompt.shape
    bs, ts, _ = x_sample.shape
    keep = min(max(w for w, _ in DILATED_PATTERNS), sp)

    c_all = jnp.concatenate([c_prompt, c_sample], axis=0).astype(F32)
    m_mix = _adaln(c_all, w_ada_mix[l], b_ada_mix[l])
    m_ffn = _adaln(c_all, w_ada_ffn[l], b_ada_ffn[l])

    def mods(m, lo, hi, per_token):
        parts = jnp.split(m[lo:hi], 3, axis=-1)
        if per_token:
            return tuple(jnp.repeat(p, ts, axis=0)[None] for p in parts)
        return tuple(p[:, None, :] for p in parts)

    pad = ROUTER_COLS - N_EXPERT_GROUPS - N_EXPERTS
    w_router = jnp.concatenate([w_router_coarse[l], w_router_fine[l], jnp.zeros((d, pad), F32)], axis=1)
    b_router = jnp.concatenate([b_router_coarse[l], b_router_fine[l], jnp.zeros((pad,), F32)])
    wts = {
        'g_mix': g_mix[l], 'w_in': w_in[l].astype(BF16), 'w_out': w_out[l].astype(BF16),
        'g_attn_out': g_attn_out[l], 'g_ffn': g_ffn[l], 'w_router': w_router, 'b_router': b_router,
        'w_gate': w_expert_gate[l].astype(BF16), 'w_up': w_expert_up[l].astype(BF16),
        'w_down': w_expert_down[l].astype(BF16), 'g_final': g_final,
    }
    a_re, a_im, b_mat, c_mat = _s5_params(ssm_lambda_re[l], ssm_lambda_im[l], ssm_log_dt[l], ssm_b_re[l],
                                          ssm_b_im[l], ssm_c_re[l], ssm_c_im[l])
    s5_w = (a_re, a_im, b_mat, c_mat, ssm_d[l].reshape(-1), w_glu[l].astype(BF16), b_glu[l], g_ssm_out[l])

    yp, kp, vp, hrp, hip = _layer(
        x_prompt, mods(m_mix, 0, bp, False), mods(m_ffn, 0, bp, False), _attn_prompt,
        lambda u: _s5_prompt(u, *s5_w), wts, tm=512)

    n_state = SSM_GROUPS * SSM_STATE
    ck = cache_k_win[l]
    cv = cache_v_win[l]

    def attn_s(q, k, v):
        r = lambda a: a.reshape(bs, ts, D_ATTN)
        return _attn_sample(r(q), r(k), r(v), ck, cv).reshape(1, bs * ts, D_ATTN)

    def s5_s(u):
        u_tm = u.reshape(bs, ts, D_SSM).transpose(1, 0, 2)
        o, hr, hi = _s5_sample(u_tm, state_ssm_re[l].reshape(bs, n_state), state_ssm_im[l].reshape(bs, n_state),
                               *s5_w)
        return o.transpose(1, 0, 2).reshape(1, bs * ts, D_SSM), hr, hi

    ys, ks, vs, hrs, his = _layer(
        x_sample.reshape(1, bs * ts, d), mods(m_mix, bp, bp + bs, True), mods(m_ffn, bp, bp + bs, True),
        attn_s, s5_s, wts, tm=bs * ts)

    heads = lambda a, b, s: a.reshape(1, b, s, N_HEADS, HEAD_DIM)
    state = lambda a, b: a.reshape(1, b, SSM_GROUPS, SSM_STATE)
    return (yp, ys.reshape(bs, ts, d),
            heads(kp[:, sp - keep:], bp, keep), heads(vp[:, sp - keep:], bp, keep), state(hrp, bp), state(hip, bp),
            heads(ks, bs, ts), heads(vs, bs, ts), state(hrs, bs), state(his, bs))
```

```python
import functools
import math

import numpy as np
import jax
import jax.numpy as jnp
from jax import lax
from jax.experimental import pallas as pl
from jax.experimental.pallas import tpu as pltpu
from jax.experimental.pallas import tpu_sc as plsc

F32 = jnp.float32
BF16 = jnp.bfloat16
HIGHEST = lax.Precision.HIGHEST

D_MODEL = 1024
D_ATTN = 512
D_SSM = 512
HEAD_DIM = 64
N_HEADS = 8
DILATED_PATTERNS = ((128, 1), (512, 4), (2048, 16))
SSM_GROUP_CH = 16
SSM_GROUPS = 32
SSM_STATE = 64
N_EXPERT_GROUPS = 4
EXPERTS_PER_GROUP = 8
N_EXPERTS = 32
D_EXPERT = 512
D_IN_PROJ = 3 * D_ATTN + D_SSM
EPS = 1e-6

LANES = 128
SUBLANES = 8
VMEM_LIMIT = 48 * 1024 * 1024

ATTN_BLOCK = 128
ATTN_UNROLL = 8
NEG_BIG = -1e30
SSM_TT = 128
SSM_PITCH = SSM_TT + 8
ROUTER_COLS = 128
MOE_TILE = 512
SC_WINDOW = 128
ROWS_PER_TOKEN = D_MODEL // 2 // LANES
INFO_EXPERT, INFO_RANK, INFO_GATE = 0, 2, 4


def _cparams(sem, vmem=VMEM_LIMIT):
    return pltpu.CompilerParams(dimension_semantics=sem, vmem_limit_bytes=vmem)


def _adaln_kernel(c_ref, w_ref, b_ref, o_ref):
    c = c_ref[...]
    s = c * jax.nn.sigmoid(c)
    o_ref[...] = jnp.dot(s, w_ref[...], precision=HIGHEST, preferred_element_type=F32) + b_ref[...]


def _adaln(c, w, b):
    r, d = c.shape
    n = w.shape[1]
    tn = 768
    return pl.pallas_call(
        _adaln_kernel,
        grid=(n // tn,),
        in_specs=[pl.BlockSpec((r, d), lambda j: (0, 0)),
                  pl.BlockSpec((d, tn), lambda j: (0, j)),
                  pl.BlockSpec((1, tn), lambda j: (0, j))],
        out_specs=pl.BlockSpec((r, tn), lambda j: (0, j)),
        out_shape=jax.ShapeDtypeStruct((r, n), F32),
        compiler_params=_cparams(("arbitrary",)),
    )(c, w, b.reshape(1, n))


def _inproj_kernel(x_ref, sh_ref, sc_ref, g_ref, w_ref, q_ref, k_ref, v_ref, u_ref):
    x = x_ref[...]
    ms = jnp.mean(x * x, axis=-1, keepdims=True)
    h = x * lax.rsqrt(ms + EPS) * g_ref[...]
    h = h * (1.0 + sc_ref[...]) + sh_ref[...]
    p = jnp.dot(h.astype(BF16), w_ref[...], preferred_element_type=F32)
    q_ref[...] = p[:, :D_ATTN] * (HEAD_DIM ** -0.5)
    k_ref[...] = p[:, D_ATTN:2 * D_ATTN]
    v_ref[...] = p[:, 2 * D_ATTN:3 * D_ATTN]
    u_ref[...] = p[:, 3 * D_ATTN:]


def _mod_spec(mod, tm):
    d = mod.shape[-1]
    if mod.shape[1] == 1:
        return pl.BlockSpec((None, 1, d), lambda b, i: (b, 0, 0))
    return pl.BlockSpec((None, tm, d), lambda b, i: (b, i, 0))


def _inproj(x, shift, scale, g, w_bf16, tm):
    nb, s, d = x.shape
    row = lambda n: pl.BlockSpec((None, tm, n), lambda b, i: (b, i, 0))
    out = jax.ShapeDtypeStruct((nb, s, D_ATTN), F32)
    return pl.pallas_call(
        _inproj_kernel,
        grid=(nb, s // tm),
        in_specs=[row(d), _mod_spec(shift, tm), _mod_spec(scale, tm),
                  pl.BlockSpec((1, d), lambda b, i: (0, 0)),
                  pl.BlockSpec((d, D_IN_PROJ), lambda b, i: (0, 0))],
        out_specs=[row(D_ATTN), row(D_ATTN), row(D_ATTN), row(D_SSM)],
        out_shape=[out, out, out, jax.ShapeDtypeStruct((nb, s, D_SSM), F32)],
        compiler_params=_cparams(("parallel", "parallel")),
    )(x, shift, scale, g.reshape(1, d), w_bf16)


def _band_bias():
    qi = np.arange(ATTN_BLOCK)[:, None]
    kj = np.arange(ATTN_BLOCK)[None, :]
    cur = kj <= qi
    prev = kj >= qi
    to_bias = lambda m: np.tile(np.where(m, 0.0, NEG_BIG).astype(np.float32), (2, 1))
    return to_bias(np.concatenate([prev, cur], axis=1)), to_bias(cur)


def _attn_prompt_kernel(q_ref, k_ref, v_ref, bias2_ref, bias1_ref, o_ref, acc_scr, m_scr, l_scr):
    s_len = q_ref.shape[0]
    lane = lax.broadcasted_iota(jnp.int32, (1, LANES), 1)
    head0 = lane < HEAD_DIM

    def rows(ref, start, n, d):
        if d == 1:
            return ref[pl.ds(start, n), :]
        return ref[pl.ds(start, n, stride=d), :]

    def tiles(items, first):
        nk = ATTN_BLOCK if first else 2 * ATTN_BLOCK
        bias = bias1_ref[...] if first else bias2_ref[...]
        one = jnp.ones((), BF16)
        loaded = []
        for _, d, qstart in items:
            kstart = qstart if first else qstart - d * ATTN_BLOCK
            loaded.append((rows(q_ref, qstart, ATTN_BLOCK, d), rows(k_ref, kstart, nk, d),
                           rows(v_ref, kstart, nk, d)))
        results = []
        for qrows, krows, vrows in loaded:
            kb = krows.astype(BF16)
            vb = vrows.astype(BF16)
            q2 = jnp.concatenate([jnp.where(head0, qrows, 0.0), jnp.where(head0, 0.0, qrows)], axis=0).astype(BF16)
            s = lax.dot_general(q2, kb, (((1,), (1,)), ((), ())), preferred_element_type=F32) + bias
            m = jnp.max(s, axis=-1, keepdims=True)
            p = jnp.exp(s - m).astype(BF16)
            pv0 = jnp.dot(p[:ATTN_BLOCK], jnp.where(head0, vb, one), preferred_element_type=F32)
            pv1 = jnp.dot(p[ATTN_BLOCK:], jnp.where(head0, one, vb), preferred_element_type=F32)
            results.append((jnp.where(head0, pv0, pv1), jnp.where(head0, m[:ATTN_BLOCK], m[ATTN_BLOCK:]),
                            jnp.where(head0, pv1, pv0)))
        for (br, d, qstart), (acc, m, l) in zip(items, results):
            dst = pl.ds(qstart, ATTN_BLOCK) if d == 1 else pl.ds(qstart, ATTN_BLOCK, stride=d)
            acc_scr[br, dst, :] = acc
            m_scr[br, dst, :] = m
            l_scr[br, dst, :] = l

    def largest_divisor(n):
        return max(f for f in range(1, ATTN_UNROLL + 1) if n % f == 0)

    grouped = []
    for br, (window, d) in enumerate(DILATED_PATTERNS):
        assert window // d == ATTN_BLOCK
        if d <= ATTN_UNROLL:
            grouped += [(br, d, r) for r in range(d)]
        else:
            un = largest_divisor(d)

            def first_body(i, carry, br=br, d=d, un=un):
                tiles([(br, d, i * un + j) for j in range(un)], True)
                return carry

            lax.fori_loop(0, d // un, first_body, 0)
    for i in range(0, len(grouped), ATTN_UNROLL):
        tiles(grouped[i:i + ATTN_UNROLL], True)

    for br, (window, d) in enumerate(DILATED_PATTERNS):
        nblk = s_len // (d * ATTN_BLOCK)
        n_rest = d * (nblk - 1)
        if n_rest:
            un = largest_divisor(n_rest)

            def rest_body(i, carry, br=br, d=d, nblk=nblk, un=un):
                items = []
                for j in range(un):
                    n = i * un + j
                    qstart = n // (nblk - 1) + d * ATTN_BLOCK * (1 + n % (nblk - 1))
                    items.append((br, d, pl.multiple_of(qstart, ATTN_BLOCK) if d == 1 else qstart))
                tiles(items, False)
                return carry

            lax.fori_loop(0, n_rest // un, rest_body, 0)

    ch = 256

    def merge_body(i, carry):
        sl = pl.ds(pl.multiple_of(i * ch, ch), ch)
        ms = [m_scr[br, sl, :] for br in range(3)]
        mx = jnp.maximum(jnp.maximum(ms[0], ms[1]), ms[2])
        num = jnp.zeros((ch, LANES), F32)
        den = jnp.zeros((ch, LANES), F32)
        for br in range(3):
            e = jnp.exp(ms[br] - mx)
            num = num + e * acc_scr[br, sl, :]
            den = den + e * pltpu.roll(l_scr[br, sl, :], HEAD_DIM, 1)
        o_ref[sl, :] = num / den
        return carry

    lax.fori_loop(0, s_len // ch, merge_body, 0)


def _attn_prompt(q, k, v):
    nb, s, _ = q.shape
    assert s % (16 * ATTN_BLOCK) == 0
    bias2, bias1 = _band_bias()
    blk = pl.BlockSpec((None, s, LANES), lambda b, p: (b, 0, p))
    const = lambda a: pl.BlockSpec(a.shape, lambda b, p: (0, 0))
    return pl.pallas_call(
        _attn_prompt_kernel,
        grid=(nb, D_ATTN // LANES),
        in_specs=[blk, blk, blk, const(bias2), const(bias1)],
        out_specs=blk,
        out_shape=jax.ShapeDtypeStruct((nb, s, D_ATTN), F32),
        scratch_shapes=[pltpu.VMEM((3, s, LANES), F32)] * 3,
        compiler_params=_cparams(("parallel", "parallel")),
    )(q, k, v, jnp.asarray(bias2), jnp.asarray(bias1))


def _sample_bias(t_new, w_buf):
    n_new = -(-t_new // LANES) * LANES
    t = np.arange(t_new)[:, None]
    idx = np.arange(w_buf + n_new)[None, :]
    dist = w_buf + t - idx
    mult = np.zeros(dist.shape, np.int64)
    for window, d in DILATED_PATTERNS:
        mult += ((dist >= 0) & (dist % d == 0) & (dist <= window)).astype(np.int64)
    bias = np.where(mult > 0, np.log(np.maximum(mult, 1)), NEG_BIG).astype(np.float32)
    bias = np.repeat(bias, N_HEADS, axis=0)
    return bias[:, :w_buf], bias[:, w_buf:]


def _attn_sample_kernel(q_ref, kn_ref, vn_ref, kt_ref, vt_ref, bias_ref, biasn_ref, o_ref):
    t_new = q_ref.shape[0]
    n_new = biasn_ref.shape[1]
    row_head = lax.broadcasted_iota(jnp.int32, (t_new * N_HEADS, D_ATTN), 0) & (N_HEADS - 1)
    col_head = lax.broadcasted_iota(jnp.int32, (t_new * N_HEADS, D_ATTN), 1) >> (HEAD_DIM.bit_length() - 1)
    own = row_head == col_head
    q = q_ref[...]
    q_rows = jnp.concatenate([jnp.broadcast_to(q[t:t + 1], (N_HEADS, D_ATTN)) for t in range(t_new)], axis=0)
    q_bd = jnp.where(own, q_rows, 0.0).astype(BF16)
    pad = jnp.zeros((n_new - t_new, D_ATTN), F32)
    kn = jnp.concatenate([kn_ref[...], pad], axis=0).astype(BF16)
    vn = jnp.concatenate([vn_ref[...], pad], axis=0).astype(BF16)
    nt = (((1,), (1,)), ((), ()))
    s_c = jnp.dot(q_bd, kt_ref[...].astype(BF16), preferred_element_type=F32) + bias_ref[...]
    s_n = lax.dot_general(q_bd, kn, nt, preferred_element_type=F32) + biasn_ref[...]
    m = jnp.maximum(jnp.max(s_c, axis=-1, keepdims=True), jnp.max(s_n, axis=-1, keepdims=True))
    p_c = jnp.exp(s_c - m)
    p_n = jnp.exp(s_n - m)
    l = jnp.sum(p_c, axis=-1, keepdims=True) + jnp.sum(p_n, axis=-1, keepdims=True)
    o = lax.dot_general(p_c.astype(BF16), vt_ref[...].astype(BF16), nt, preferred_element_type=F32)
    o = (o + jnp.dot(p_n.astype(BF16), vn, preferred_element_type=F32)) / l
    o = jnp.where(own, o, 0.0)
    o_ref[...] = jnp.concatenate(
        [jnp.sum(o[t * N_HEADS:(t + 1) * N_HEADS], axis=0, keepdims=True) for t in range(t_new)], axis=0)


def _attn_sample(q, k_new, v_new, cache_k, cache_v):
    nb, t_new, _ = q.shape
    w_buf = cache_k.shape[1]
    bias, bias_new = _sample_bias(t_new, w_buf)
    as_stored = lambda a: a.transpose(0, 2, 3, 1).reshape(nb, D_ATTN, w_buf)
    new = pl.BlockSpec((None, t_new, D_ATTN), lambda b: (b, 0, 0))
    buf = pl.BlockSpec((None, D_ATTN, w_buf), lambda b: (b, 0, 0))
    const = lambda a: pl.BlockSpec(a.shape, lambda b: (0, 0))
    return pl.pallas_call(
        _attn_sample_kernel,
        grid=(nb,),
        in_specs=[new, new, new, buf, buf, const(bias), const(bias_new)],
        out_specs=new,
        out_shape=jax.ShapeDtypeStruct((nb, t_new, D_ATTN), F32),
        compiler_params=_cparams(("parallel",)),
    )(q, k_new, v_new, as_stored(cache_k), as_stored(cache_v), jnp.asarray(bias), jnp.asarray(bias_new))


def _s5_params(lam_re, lam_im, log_dt, b_re, b_im, c_re, c_im):
    f32 = F32
    dt = jnp.exp(log_dt.astype(f32))[:, None]
    lr, li = lam_re.astype(f32), lam_im.astype(f32)
    ea = jnp.exp(lr * dt)
    a_re, a_im = ea * jnp.cos(li * dt), ea * jnp.sin(li * dt)
    den = lr * lr + li * li
    co_re = ((a_re - 1.0) * lr + a_im * li) / den
    co_im = (a_im * lr - (a_re - 1.0) * li) / den
    bb_re = co_re[..., None] * b_re - co_im[..., None] * b_im
    bb_im = co_re[..., None] * b_im + co_im[..., None] * b_re
    eye = jnp.eye(8, dtype=f32)

    def b_blocks(bb):
        t = bb.reshape(4, 8, SSM_STATE, SSM_GROUP_CH)
        return jnp.einsum('ab,kapc->kacbp', eye, t).reshape(4, 8 * SSM_GROUP_CH, 8 * SSM_STATE)

    def c_blocks(cc):
        t = cc.reshape(4, 8, SSM_GROUP_CH, SSM_STATE)
        return jnp.einsum('ab,kacp->kbpac', eye, t).reshape(4, 8 * SSM_STATE, 8 * SSM_GROUP_CH)

    b_mat = jnp.concatenate([b_blocks(bb_re), b_blocks(bb_im)], axis=2).astype(BF16)
    c_mat = jnp.concatenate([c_blocks(c_re.astype(f32)), -c_blocks(c_im.astype(f32))], axis=1).astype(BF16)
    return a_re.reshape(-1), a_im.reshape(-1), b_mat, c_mat


def _gelu_tanh(y):
    return 0.5 * y * (1.0 + jnp.tanh(math.sqrt(2.0 / math.pi) * (y + 0.044715 * (y * y * y))))


def _s5_epilogue(y, u, d_ref, wglu_ref, bglu_ref, g_ref):
    y = y + d_ref[...] * u
    z = _gelu_tanh(y)
    gate = jnp.dot(z.astype(BF16), wglu_ref[...], preferred_element_type=F32) + bglu_ref[...]
    out = z * jax.nn.sigmoid(gate)
    ms = jnp.mean(out * out, axis=-1, keepdims=True)
    return out * lax.rsqrt(ms + EPS) * g_ref[...]


def _s5_prompt_kernel(u_ref, are_ref, aim_ref, b_ref, c_ref, d_ref, wglu_ref, bglu_ref, g_ref,
                      o_ref, hre_ref, him_ref, scr, hst):
    nseq, tt, _ = u_ref.shape
    n_slab = D_SSM * SSM_STATE // SSM_GROUP_CH // LANES
    ti = pl.program_id(1)

    @pl.when(ti == 0)
    def _():
        hst[...] = jnp.zeros_like(hst)

    u2 = u_ref[...].reshape(nseq * tt, D_SSM)
    ub = u2.astype(BF16)
    for kc in range(4):
        bu = jnp.dot(ub[:, kc * LANES:(kc + 1) * LANES], b_ref[kc], preferred_element_type=F32)
        for part in range(2):
            for j in range(4):
                col = part * 512 + j * LANES
                for b in range(nseq):
                    scr[part * n_slab + 4 * kc + j, b * SSM_PITCH:b * SSM_PITCH + tt, :] = (
                        bu[b * tt:(b + 1) * tt, col:col + LANES])

    grp = 4
    for sg in range(n_slab // grp):
        slabs = [sg * grp + i for i in range(grp)]
        ar = [are_ref[s] for s in slabs]
        ai = [aim_ref[s] for s in slabs]

        def step(t, carry, slabs=slabs, ar=ar, ai=ai):
            hr, hi = carry
            nhr, nhi = [], []
            for i, s in enumerate(slabs):
                sel = pl.ds(t, nseq, stride=SSM_PITCH)
                br = scr[s, sel, :]
                bi = scr[n_slab + s, sel, :]
                r = ar[i] * hr[i] - ai[i] * hi[i] + br
                im = ar[i] * hi[i] + ai[i] * hr[i] + bi
                scr[s, sel, :] = r
                scr[n_slab + s, sel, :] = im
                nhr.append(r)
                nhi.append(im)
            return tuple(nhr), tuple(nhi)

        init = (tuple(hst[s] for s in slabs), tuple(hst[n_slab + s] for s in slabs))
        hr, hi = lax.fori_loop(0, tt, step, init, unroll=4)
        for i, s in enumerate(slabs):
            hst[s] = hr[i]
            hst[n_slab + s] = hi[i]

    ys = []
    for kc in range(4):
        slabs = [4 * kc + j for j in range(4)] + [n_slab + 4 * kc + j for j in range(4)]
        lhs = jnp.concatenate(
            [jnp.concatenate([scr[s, b * SSM_PITCH:b * SSM_PITCH + tt, :].astype(BF16) for s in slabs], axis=1)
             for b in range(nseq)], axis=0)
        ys.append(jnp.dot(lhs, c_ref[kc], preferred_element_type=F32))
    y = jnp.concatenate(ys, axis=1)
    o_ref[...] = _s5_epilogue(y, u2, d_ref, wglu_ref, bglu_ref, g_ref).reshape(nseq, tt, D_SSM)

    @pl.when(ti == pl.num_programs(1) - 1)
    def _():
        for s in range(n_slab):
            hre_ref[:, s * LANES:(s + 1) * LANES] = hst[s]
            him_ref[:, s * LANES:(s + 1) * LANES] = hst[n_slab + s]


def _s5_prompt(u, a_re, a_im, b_mat, c_mat, d, w_glu_bf16, b_glu, g_out):
    nb, s, _ = u.shape
    nseq = SUBLANES
    assert nb % nseq == 0 and s % SSM_TT == 0
    n_state = a_re.shape[0]
    n_slab = n_state // LANES
    bcast = lambda a: jnp.broadcast_to(a.reshape(n_slab, 1, LANES), (n_slab, nseq, LANES))
    const = lambda a: pl.BlockSpec(a.shape, lambda b, i: (0,) * a.ndim)
    args = (u, bcast(a_re), bcast(a_im), b_mat, c_mat, d.reshape(1, D_SSM), w_glu_bf16,
            b_glu.reshape(1, D_SSM), g_out.reshape(1, D_SSM))
    st = jax.ShapeDtypeStruct((nb, n_state), F32)
    return pl.pallas_call(
        _s5_prompt_kernel,
        grid=(nb // nseq, s // SSM_TT),
        in_specs=[pl.BlockSpec((nseq, SSM_TT, D_SSM), lambda b, i: (b, i, 0))] + [const(a) for a in args[1:]],
        out_specs=[pl.BlockSpec((nseq, SSM_TT, D_SSM), lambda b, i: (b, i, 0)),
                   pl.BlockSpec((nseq, n_state), lambda b, i: (b, 0)),
                   pl.BlockSpec((nseq, n_state), lambda b, i: (b, 0))],
        out_shape=[jax.ShapeDtypeStruct((nb, s, D_SSM), F32), st, st],
        scratch_shapes=[pltpu.VMEM((2 * n_slab, nseq * SSM_PITCH, LANES), F32),
                        pltpu.VMEM((2 * n_slab, nseq, LANES), F32)],
        compiler_params=_cparams(("parallel", "arbitrary")),
    )(*args)


def _s5_sample_kernel(u_ref, h0re_ref, h0im_ref, are_ref, aim_ref, b_ref, c_ref, d_ref, wglu_ref, bglu_ref, g_ref,
                      o_ref, hre_ref, him_ref):
    t_new = u_ref.shape[0]
    hre = h0re_ref[...]
    him = h0im_ref[...]
    are, aim = are_ref[...], aim_ref[...]
    half = 4 * LANES
    for t in range(t_new):
        u = u_ref[t]
        ub = u.astype(BF16)
        bus = [jnp.dot(ub[:, kc * LANES:(kc + 1) * LANES], b_ref[kc], preferred_element_type=F32) for kc in range(4)]
        bre = jnp.concatenate([bu[:, :half] for bu in bus], axis=1)
        bim = jnp.concatenate([bu[:, half:] for bu in bus], axis=1)
        hre, him = are * hre - aim * him + bre, are * him + aim * hre + bim
        ys = []
        for kc in range(4):
            lhs = jnp.concatenate([hre[:, kc * half:(kc + 1) * half], him[:, kc * half:(kc + 1) * half]], axis=1)
            ys.append(jnp.dot(lhs.astype(BF16), c_ref[kc], preferred_element_type=F32))
        y = jnp.concatenate(ys, axis=1)
        o_ref[t] = _s5_epilogue(y, u, d_ref, wglu_ref, bglu_ref, g_ref)
    hre_ref[...] = hre
    him_ref[...] = him


def _s5_sample(u_tm, h0_re, h0_im, a_re, a_im, b_mat, c_mat, d, w_glu_bf16, b_glu, g_out):
    t_new, nb, _ = u_tm.shape
    n_state = a_re.shape[0]
    tb = 64
    assert nb % tb == 0
    const = lambda a: pl.BlockSpec(a.shape, lambda b: (0,) * a.ndim)
    args = (u_tm, h0_re, h0_im, a_re.reshape(1, n_state), a_im.reshape(1, n_state), b_mat, c_mat,
            d.reshape(1, D_SSM), w_glu_bf16, b_glu.reshape(1, D_SSM), g_out.reshape(1, D_SSM))
    st_spec = pl.BlockSpec((tb, n_state), lambda b: (b, 0))
    st = jax.ShapeDtypeStruct((nb, n_state), F32)
    return pl.pallas_call(
        _s5_sample_kernel,
        grid=(nb // tb,),
        in_specs=[pl.BlockSpec((t_new, tb, D_SSM), lambda b: (0, b, 0)), st_spec, st_spec]
                 + [const(a) for a in args[3:]],
        out_specs=[pl.BlockSpec((t_new, tb, D_SSM), lambda b: (0, b, 0)), st_spec, st_spec],
        out_shape=[jax.ShapeDtypeStruct((t_new, nb, D_SSM), F32), st, st],
        compiler_params=_cparams(("parallel",)),
    )(*args)


def _store_packed(ref, val):
    rows, d = val.shape
    half = d // 2
    bits = pltpu.bitcast(val.astype(BF16).astype(F32), jnp.uint32)
    words = bits[:, :half] | lax.shift_right_logical(bits[:, half:], jnp.uint32(16))
    for c in range(ROWS_PER_TOKEN):
        ref[pl.ds(c, rows, stride=ROWS_PER_TOKEN), :] = words[:, c * LANES:(c + 1) * LANES]


def _load_packed(ref, rows):
    hi, lo = [], []
    for c in range(ROWS_PER_TOKEN):
        w = ref[pl.ds(c, rows, stride=ROWS_PER_TOKEN), :]
        hi.append(pltpu.bitcast(w & jnp.uint32(0xFFFF0000), F32))
        lo.append(pltpu.bitcast(lax.shift_left(w, jnp.uint32(16)), F32))
    return jnp.concatenate(hi + lo, axis=1)


def _outproj_kernel(x_ref, oa_ref, os_ref, gate_ref, sh_ref, sc_ref, ga_ref, gf_ref, w_ref, wr_hi_ref, wr_lo_ref,
                    br_ref, x1_ref, h_ref, lg_ref):
    oa = oa_ref[...]
    ms = jnp.mean(oa * oa, axis=-1, keepdims=True)
    na = oa * lax.rsqrt(ms + EPS) * ga_ref[...]
    merged = jnp.concatenate([na, os_ref[...]], axis=-1).astype(BF16)
    x1 = x_ref[...] + gate_ref[...] * jnp.dot(merged, w_ref[...], preferred_element_type=F32)
    x1_ref[...] = x1
    ms = jnp.mean(x1 * x1, axis=-1, keepdims=True)
    h = x1 * lax.rsqrt(ms + EPS) * gf_ref[...]
    h = h * (1.0 + sc_ref[...]) + sh_ref[...]
    h_hi = h.astype(BF16)
    _store_packed(h_ref, h)
    h_lo = (h - h_hi.astype(F32)).astype(BF16)
    lg = jnp.dot(h_hi, wr_hi_ref[...], preferred_element_type=F32)
    lg = lg + jnp.dot(h_hi, wr_lo_ref[...], preferred_element_type=F32)
    lg = lg + jnp.dot(h_lo, wr_hi_ref[...], preferred_element_type=F32)
    lg_ref[...] = lg + br_ref[...]


def _outproj(x, o_attn, o_ssm, gate, shift, scale, g_attn, g_ffn, w_out_bf16, w_router, b_router, tm):
    nb, s, d = x.shape
    row = lambda n: pl.BlockSpec((None, tm, n), lambda b, i: (b, i, 0))
    const = lambda a: pl.BlockSpec(a.shape, lambda b, i: (0,) * a.ndim)
    wr_hi = w_router.astype(BF16)
    wr_lo = (w_router - wr_hi.astype(F32)).astype(BF16)
    consts = (g_attn.reshape(1, D_ATTN), g_ffn.reshape(1, d), w_out_bf16, wr_hi, wr_lo,
              b_router.reshape(1, ROUTER_COLS))
    return pl.pallas_call(
        _outproj_kernel,
        grid=(nb, s // tm),
        in_specs=[row(d), row(D_ATTN), row(D_SSM), _mod_spec(gate, tm), _mod_spec(shift, tm), _mod_spec(scale, tm)]
                 + [const(a) for a in consts],
        out_specs=[row(d), pl.BlockSpec((tm * ROWS_PER_TOKEN, LANES), lambda b, i: (b * (s // tm) + i, 0)),
                   row(ROUTER_COLS)],
        out_shape=[jax.ShapeDtypeStruct((nb, s, d), F32),
                   jax.ShapeDtypeStruct((nb * s * ROWS_PER_TOKEN, LANES), jnp.uint32),
                   jax.ShapeDtypeStruct((nb, s, ROUTER_COLS), F32)],
        compiler_params=_cparams(("parallel", "parallel")),
    )(x, o_attn, o_ssm, gate, shift, scale, *consts)


def _expert_kernel(blk_e_ref, nvalid_ref, x_ref, wg_ref, wu_ref, wd_ref, y_ref):
    i = pl.program_id(0)

    @pl.when(i < nvalid_ref[0])
    def _():
        x = _load_packed(x_ref, MOE_TILE).astype(BF16)
        a = jnp.dot(x, wg_ref[...], preferred_element_type=F32)
        b = jnp.dot(x, wu_ref[...], preferred_element_type=F32)
        hid = (a * jax.nn.sigmoid(a) * b).astype(BF16)
        _store_packed(y_ref, jnp.dot(hid, wd_ref[...], preferred_element_type=F32))

    @pl.when(i >= nvalid_ref[0])
    def _():
        y_ref[...] = jnp.zeros_like(y_ref)


def _experts(xs, blk_e, nvalid, w_gate, w_up, w_down):
    nblk = xs.shape[0] // (MOE_TILE * ROWS_PER_TOKEN)
    d = w_gate.shape[1]
    wspec = lambda shp: pl.BlockSpec((None,) + shp, lambda i, be, nv: (be[i], 0, 0))
    tile = pl.BlockSpec((MOE_TILE * ROWS_PER_TOKEN, LANES), lambda i, be, nv: (i, 0))
    return pl.pallas_call(
        _expert_kernel,
        grid_spec=pltpu.PrefetchScalarGridSpec(
            num_scalar_prefetch=2,
            grid=(nblk,),
            in_specs=[tile, wspec((d, D_EXPERT)), wspec((d, D_EXPERT)), wspec((D_EXPERT, d))],
            out_specs=tile,
        ),
        out_shape=jax.ShapeDtypeStruct(xs.shape, jnp.uint32),
        compiler_params=_cparams(("arbitrary",)),
    )(blk_e, nvalid, xs, w_gate, w_up, w_down)


def _combine_kernel(x1_ref, y0_ref, y1_ref, info_ref, gate_ref, gfin_ref, o_ref):
    tm = x1_ref.shape[0]
    info = pltpu.bitcast(info_ref[...], F32)
    g0 = info[:, INFO_GATE:INFO_GATE + 1]
    g1 = info[:, INFO_GATE + 1:INFO_GATE + 2]
    moe = g0 * _load_packed(y0_ref, tm) + g1 * _load_packed(y1_ref, tm)
    x2 = x1_ref[...] + gate_ref[...] * moe
    ms = jnp.mean(x2 * x2, axis=-1, keepdims=True)
    o_ref[...] = x2 * lax.rsqrt(ms + EPS) * gfin_ref[...]


def _combine(x1, y0, y1, info, gate, g_final, tm):
    nb, s, d = x1.shape
    row = lambda n: pl.BlockSpec((None, tm, n), lambda b, i: (b, i, 0))
    packed = pl.BlockSpec((tm * ROWS_PER_TOKEN, LANES), lambda b, i: (b * (s // tm) + i, 0))
    return pl.pallas_call(
        _combine_kernel,
        grid=(nb, s // tm),
        in_specs=[row(d), packed, packed, row(ROUTER_COLS), _mod_spec(gate, tm),
                  pl.BlockSpec((1, d), lambda b, i: (0, 0))],
        out_specs=row(d),
        out_shape=jax.ShapeDtypeStruct((nb, s, d), F32),
        compiler_params=_cparams(("parallel", "parallel")),
    )(x1, y0, y1, info, gate, g_final.reshape(1, d))


def _sc_window(n_rows):
    assert n_rows % SC_WINDOW == 0
    return SC_WINDOW


def _sc_gather_rows(table, idx):
    n = idx.shape[0]
    w = _sc_window(n)
    mesh = plsc.VectorSubcoreMesh(core_axis_name="core", subcore_axis_name="subcore")

    @functools.partial(pl.kernel, out_type=jax.ShapeDtypeStruct((n, LANES), table.dtype), mesh=mesh)
    def gather_kernel(x_hbm, i_hbm, o_hbm):
        def body(i_vmem, o_vmem):
            pltpu.sync_copy(x_hbm.at[i_vmem.at[0]], o_vmem)

        pltpu.emit_pipeline(
            body, grid=(n // w,),
            in_specs=[pl.BlockSpec((1, w), lambda i: (0, i))],
            out_specs=[pl.BlockSpec((w, LANES), lambda i: (i, 0))],
            core_axis_name=("core", "subcore"), dimension_semantics=(pltpu.PARALLEL,),
        )(i_hbm, o_hbm)

    return gather_kernel(table, idx.reshape(1, n))


def _sc_scatter_rows(rows, idx0, idx1, n_out):
    n = rows.shape[0]
    w = _sc_window(n)
    mesh = plsc.VectorSubcoreMesh(core_axis_name="core", subcore_axis_name="subcore")

    @functools.partial(pl.kernel, out_type=jax.ShapeDtypeStruct((n_out, LANES), rows.dtype), mesh=mesh)
    def scatter_kernel(x_hbm, i0_hbm, i1_hbm, o_hbm):
        def body(x_vmem, i0_vmem, i1_vmem):
            pltpu.sync_copy(x_vmem, o_hbm.at[i0_vmem.at[0]])
            pltpu.sync_copy(x_vmem, o_hbm.at[i1_vmem.at[0]])

        pltpu.emit_pipeline(
            body, grid=(n // w,),
            in_specs=[pl.BlockSpec((w, LANES), lambda i: (i, 0)),
                      pl.BlockSpec((1, w), lambda i: (0, i)),
                      pl.BlockSpec((1, w), lambda i: (0, i))],
            out_specs=[],
            core_axis_name=("core", "subcore"), dimension_semantics=(pltpu.PARALLEL,),
        )(x_hbm, i0_hbm, i1_hbm)

    return scatter_kernel(rows, idx0.reshape(1, n), idx1.reshape(1, n))


def _route_kernel(lg_ref, tri_ref, info_ref, cnt_ref, carry):
    i = pl.program_id(0)

    @pl.when(i == 0)
    def _():
        carry[...] = jnp.zeros_like(carry)

    lg = lg_ref[...]
    lane = lax.broadcasted_iota(jnp.int32, lg.shape, 1)
    lane_f = lane.astype(F32)
    none = float(ROUTER_COLS)
    ninf = float("-inf")
    first = lambda cond: jnp.min(jnp.where(cond, lane_f, none), axis=-1, keepdims=True)

    is_c = lane < N_EXPERT_GROUPS
    lc = jnp.where(is_c, lg, ninf)
    mc = jnp.max(lc, axis=-1, keepdims=True)
    p_grp = 1.0 / jnp.sum(jnp.exp(lc - mc), axis=-1, keepdims=True)
    grp = first(lc == mc)
    fine = lane - N_EXPERT_GROUPS
    fine_grp = lax.shift_right_arithmetic(fine, jnp.int32(EXPERTS_PER_GROUP.bit_length() - 1))
    in_grp = (fine >= 0) & (fine < N_EXPERTS) & (fine_grp.astype(F32) == grp)
    lf = jnp.where(in_grp, lg, ninf)
    v1 = jnp.max(lf, axis=-1, keepdims=True)
    i1 = first(lf == v1)
    lf2 = jnp.where(lane_f == i1, ninf, lf)
    v2 = jnp.max(lf2, axis=-1, keepdims=True)
    i2 = first(lf2 == v2)
    b = jnp.exp(v2 - v1)
    g0 = p_grp / (1.0 + b)
    g1 = p_grp * b / (1.0 + b)

    hit1 = lane_f == i1
    hit2 = lane_f == i2
    onehot = jnp.where(hit1 | hit2, 1.0, 0.0)
    before = jnp.dot(tri_ref[...], onehot.astype(BF16), preferred_element_type=F32) + carry[0:1, :]
    r1 = jnp.sum(jnp.where(hit1, before, 0.0), axis=-1, keepdims=True)
    r2 = jnp.sum(jnp.where(hit2, before, 0.0), axis=-1, keepdims=True)
    carry[...] = carry[...] + jnp.sum(onehot, axis=0, keepdims=True)

    as_int = lambda v: jnp.broadcast_to(v, lg.shape).astype(jnp.int32)
    as_bits = lambda v: pltpu.bitcast(jnp.broadcast_to(v, lg.shape), jnp.int32)
    info = jnp.zeros(lg.shape, jnp.int32)
    fields = ((INFO_EXPERT, as_int(i1 - N_EXPERT_GROUPS)), (INFO_EXPERT + 1, as_int(i2 - N_EXPERT_GROUPS)),
              (INFO_RANK, as_int(r1)), (INFO_RANK + 1, as_int(r2)), (INFO_GATE, as_bits(g0)),
              (INFO_GATE + 1, as_bits(g1)))
    for col, val in fields:
        info = jnp.where(lane == col, val, info)
    info_ref[...] = info

    @pl.when(i == pl.num_programs(0) - 1)
    def _():
        cnt_ref[...] = carry[...]


def _route(logits):
    t = logits.shape[0]
    tr = min(t, 512)
    assert t % tr == 0
    tri = jnp.asarray(np.tril(np.ones((tr, tr), np.float32), -1), BF16)
    info, cnt = pl.pallas_call(
        _route_kernel,
        grid=(t // tr,),
        in_specs=[pl.BlockSpec((tr, ROUTER_COLS), lambda i: (i, 0)), pl.BlockSpec((tr, tr), lambda i: (0, 0))],
        out_specs=[pl.BlockSpec((tr, ROUTER_COLS), lambda i: (i, 0)),
                   pl.BlockSpec((SUBLANES, ROUTER_COLS), lambda i: (0, 0))],
        out_shape=[jax.ShapeDtypeStruct((t, ROUTER_COLS), jnp.int32),
                   jax.ShapeDtypeStruct((SUBLANES, ROUTER_COLS), F32)],
        scratch_shapes=[pltpu.VMEM((SUBLANES, ROUTER_COLS), F32)],
        compiler_params=_cparams(("arbitrary",)),
    )(logits, tri)
    counts = cnt[0, N_EXPERT_GROUPS:N_EXPERT_GROUPS + N_EXPERTS].astype(jnp.int32)
    return info, counts


def _moe(h_packed, logits, w_gate, w_up, w_down):
    t = logits.shape[0]
    info, counts = _route(logits)
    padded = (counts + MOE_TILE - 1) // MOE_TILE * MOE_TILE
    pend = jnp.cumsum(padded)
    pstart = pend - padded
    nblk = -(-2 * t // MOE_TILE) + N_EXPERTS
    blk_start = jnp.arange(nblk, dtype=jnp.int32) * MOE_TILE
    blk_e = jnp.minimum(jnp.sum(blk_start[:, None] >= pend[None, :], axis=1), N_EXPERTS - 1).astype(jnp.int32)
    nvalid = (pend[-1] // MOE_TILE).astype(jnp.int32).reshape(1)
    eid = info[:, INFO_EXPERT:INFO_EXPERT + 2]
    slot = jnp.take(pstart, eid).astype(jnp.int32) + info[:, INFO_RANK:INFO_RANK + 2]
    rows = slot[:, :, None] * ROWS_PER_TOKEN + jnp.arange(ROWS_PER_TOKEN, dtype=jnp.int32)
    idx0 = rows[:, 0].reshape(-1)
    idx1 = rows[:, 1].reshape(-1)
    xs = _sc_scatter_rows(h_packed, idx0, idx1, nblk * MOE_TILE * ROWS_PER_TOKEN)
    ys = _experts(xs, blk_e, nvalid, w_gate, w_up, w_down)
    return _sc_gather_rows(ys, idx0), _sc_gather_rows(ys, idx1), info


def _layer(x, mods_mix, mods_ffn, attn_fn, s5_fn, wts, tm):
    nb, s, d = x.shape
    q, k, v, u = _inproj(x, mods_mix[0], mods_mix[1], wts['g_mix'], wts['w_in'], tm)
    o_attn = attn_fn(q, k, v)
    o_ssm, h_re, h_im = s5_fn(u)
    x1, h2, logits = _outproj(x, o_attn, o_ssm, mods_mix[2], mods_ffn[0], mods_ffn[1], wts['g_attn_out'],
                              wts['g_ffn'], wts['w_out'], wts['w_router'], wts['b_router'], tm)
    y0, y1, info = _moe(h2, logits.reshape(nb * s, ROUTER_COLS), wts['w_gate'], wts['w_up'], wts['w_down'])
    y = _combine(x1, y0, y1, info.reshape(nb, s, ROUTER_COLS), mods_ffn[2], wts['g_final'], tm)
    return y, k, v, h_re, h_im


def kernel(x_prompt, x_sample, cache_k_win, cache_v_win, state_ssm_re, state_ssm_im, c_prompt, c_sample, g_mix, w_ada_mix, b_ada_mix, w_in, w_out, g_attn_out, g_ssm_out, ssm_lambda_re, ssm_lambda_im, ssm_log_dt, ssm_b_re, ssm_b_im, ssm_c_re, ssm_c_im, ssm_d, w_glu, b_glu, g_ffn, w_ada_ffn, b_ada_ffn, w_router_coarse, b_router_coarse, w_router_fine, b_router_fine, w_expert_gate, w_expert_up, w_expert_down, g_final):
    depth = g_mix.shape[0]
    assert depth == 1, "single-layer step"
    l = 0
    bp, sp, d = x_prompt.shape
    bs, ts, _ = x_sample.shape
    keep = min(max(w for w, _ in DILATED_PATTERNS), sp)

    c_all = jnp.concatenate([c_prompt, c_sample], axis=0).astype(F32)
    m_mix = _adaln(c_all, w_ada_mix[l], b_ada_mix[l])
    m_ffn = _adaln(c_all, w_ada_ffn[l], b_ada_ffn[l])

    def mods(m, lo, hi, per_token):
        parts = jnp.split(m[lo:hi], 3, axis=-1)
        if per_token:
            return tuple(jnp.repeat(p, ts, axis=0)[None] for p in parts)
        return tuple(p[:, None, :] for p in parts)

    pad = ROUTER_COLS - N_EXPERT_GROUPS - N_EXPERTS
    w_router = jnp.concatenate([w_router_coarse[l], w_router_fine[l], jnp.zeros((d, pad), F32)], axis=1)
    b_router = jnp.concatenate([b_router_coarse[l], b_router_fine[l], jnp.zeros((pad,), F32)])
    wts = {
        'g_mix': g_mix[l], 'w_in': w_in[l].astype(BF16), 'w_out': w_out[l].astype(BF16),
        'g_attn_out': g_attn_out[l], 'g_ffn': g_ffn[l], 'w_router': w_router, 'b_router': b_router,
        'w_gate': w_expert_gate[l].astype(BF16), 'w_up': w_expert_up[l].astype(BF16),
        'w_down': w_expert_down[l].astype(BF16), 'g_final': g_final,
    }
    a_re, a_im, b_mat, c_mat = _s5_params(ssm_lambda_re[l], ssm_lambda_im[l], ssm_log_dt[l], ssm_b_re[l],
                                          ssm_b_im[l], ssm_c_re[l], ssm_c_im[l])
    s5_w = (a_re, a_im, b_mat, c_mat, ssm_d[l].reshape(-1), w_glu[l].astype(BF16), b_glu[l], g_ssm_out[l])

    yp, kp, vp, hrp, hip = _layer(
        x_prompt, mods(m_mix, 0, bp, False), mods(m_ffn, 0, bp, False), _attn_prompt,
        lambda u: _s5_prompt(u, *s5_w), wts, tm=512)

    n_state = SSM_GROUPS * SSM_STATE
    ck = cache_k_win[l]
    cv = cache_v_win[l]

    def attn_s(q, k, v):
        r = lambda a: a.reshape(bs, ts, D_ATTN)
        return _attn_sample(r(q), r(k), r(v), ck, cv).reshape(1, bs * ts, D_ATTN)

    def s5_s(u):
        u_tm = u.reshape(bs, ts, D_SSM).transpose(1, 0, 2)
        o, hr, hi = _s5_sample(u_tm, state_ssm_re[l].reshape(bs, n_state), state_ssm_im[l].reshape(bs, n_state),
                               *s5_w)
        return o.transpose(1, 0, 2).reshape(1, bs * ts, D_SSM), hr, hi

    ys, ks, vs, hrs, his = _layer(
        x_sample.reshape(1, bs * ts, d), mods(m_mix, bp, bp + bs, True), mods(m_ffn, bp, bp + bs, True),
        attn_s, s5_s, wts, tm=bs * ts)

    heads = lambda a, b, s: a.reshape(1, b, s, N_HEADS, HEAD_DIM)
    state = lambda a, b: a.reshape(1, b, SSM_GROUPS, SSM_STATE)
    return (yp, ys.reshape(bs, ts, d),
            heads(kp[:, sp - keep:], bp, keep), heads(vp[:, sp - keep:], bp, keep), state(hrp, bp), state(hip, bp),
            heads(ks, bs, ts), heads(vs, bs, ts), state(hrs, bs), state(his, bs))
```

```python
import functools
import math

import numpy as np
import jax
import jax.numpy as jnp
from jax import lax
from jax.experimental import pallas as pl
from jax.experimental.pallas import tpu as pltpu
from jax.experimental.pallas import tpu_sc as plsc

F32 = jnp.float32
BF16 = jnp.bfloat16
HIGHEST = lax.Precision.HIGHEST

D_MODEL = 1024
D_ATTN = 512
D_SSM = 512
HEAD_DIM = 64
N_HEADS = 8
DILATED_PATTERNS = ((128, 1), (512, 4), (2048, 16))
SSM_GROUP_CH = 16
SSM_GROUPS = 32
SSM_STATE = 64
N_EXPERT_GROUPS = 4
EXPERTS_PER_GROUP = 8
N_EXPERTS = 32
D_EXPERT = 512
D_IN_PROJ = 3 * D_ATTN + D_SSM
EPS = 1e-6

LANES = 128
SUBLANES = 8
VMEM_LIMIT = 48 * 1024 * 1024

ATTN_BLOCK = 128
ATTN_UNROLL = 8
NEG_BIG = -1e30
SSM_TT = 128
SSM_PITCH = SSM_TT + 8
ROUTER_COLS = 128
MOE_TILE = 512
SC_WINDOW = 128
PACK_CHUNKS = D_MODEL // 2 // LANES
INFO_EXPERT, INFO_RANK, INFO_GATE = 0, 2, 4


def _cparams(sem, vmem=VMEM_LIMIT):
    return pltpu.CompilerParams(dimension_semantics=sem, vmem_limit_bytes=vmem)


def _adaln_kernel(c_ref, w_ref, b_ref, o_ref):
    c = c_ref[...]
    s = c * jax.nn.sigmoid(c)
    o_ref[...] = jnp.dot(s, w_ref[...], precision=HIGHEST, preferred_element_type=F32) + b_ref[...]


def _adaln(c, w, b):
    r, d = c.shape
    n = w.shape[1]
    tn = 768
    return pl.pallas_call(
        _adaln_kernel,
        grid=(n // tn,),
        in_specs=[pl.BlockSpec((r, d), lambda j: (0, 0)),
                  pl.BlockSpec((d, tn), lambda j: (0, j)),
                  pl.BlockSpec((1, tn), lambda j: (0, j))],
        out_specs=pl.BlockSpec((r, tn), lambda j: (0, j)),
        out_shape=jax.ShapeDtypeStruct((r, n), F32),
        compiler_params=_cparams(("arbitrary",)),
    )(c, w, b.reshape(1, n))


def _inproj_kernel(x_ref, sh_ref, sc_ref, g_ref, w_ref, q_ref, k_ref, v_ref, u_ref):
    x = x_ref[...]
    ms = jnp.mean(x * x, axis=-1, keepdims=True)
    h = x * lax.rsqrt(ms + EPS) * g_ref[...]
    h = h * (1.0 + sc_ref[...]) + sh_ref[...]
    p = jnp.dot(h.astype(BF16), w_ref[...], preferred_element_type=F32)
    q_ref[...] = p[:, :D_ATTN] * (HEAD_DIM ** -0.5)
    k_ref[...] = p[:, D_ATTN:2 * D_ATTN]
    v_ref[...] = p[:, 2 * D_ATTN:3 * D_ATTN]
    u_ref[...] = p[:, 3 * D_ATTN:]


def _mod_spec(mod, tm):
    d = mod.shape[-1]
    if mod.shape[1] == 1:
        return pl.BlockSpec((None, 1, d), lambda b, i: (b, 0, 0))
    return pl.BlockSpec((None, tm, d), lambda b, i: (b, i, 0))


def _inproj(x, shift, scale, g, w_bf16, tm):
    nb, s, d = x.shape
    row = lambda n: pl.BlockSpec((None, tm, n), lambda b, i: (b, i, 0))
    out = jax.ShapeDtypeStruct((nb, s, D_ATTN), F32)
    return pl.pallas_call(
        _inproj_kernel,
        grid=(nb, s // tm),
        in_specs=[row(d), _mod_spec(shift, tm), _mod_spec(scale, tm),
                  pl.BlockSpec((1, d), lambda b, i: (0, 0)),
                  pl.BlockSpec((d, D_IN_PROJ), lambda b, i: (0, 0))],
        out_specs=[row(D_ATTN), row(D_ATTN), row(D_ATTN), row(D_SSM)],
        out_shape=[out, out, out, jax.ShapeDtypeStruct((nb, s, D_SSM), F32)],
        compiler_params=_cparams(("parallel", "parallel")),
    )(x, shift, scale, g.reshape(1, d), w_bf16)


def _band_bias():
    qi = np.arange(ATTN_BLOCK)[:, None]
    kj = np.arange(ATTN_BLOCK)[None, :]
    cur = kj <= qi
    prev = kj >= qi
    to_bias = lambda m: np.tile(np.where(m, 0.0, NEG_BIG).astype(np.float32), (2, 1))
    return to_bias(np.concatenate([prev, cur], axis=1)), to_bias(cur)


def _attn_prompt_kernel(q_ref, k_ref, v_ref, bias2_ref, bias1_ref, o_ref, acc_scr, m_scr, l_scr):
    s_len = q_ref.shape[0]
    lane = lax.broadcasted_iota(jnp.int32, (1, LANES), 1)
    head0 = lane < HEAD_DIM

    def rows(ref, start, n, d):
        if d == 1:
            return ref[pl.ds(start, n), :]
        return ref[pl.ds(start, n, stride=d), :]

    def tiles(items, first):
        nk = ATTN_BLOCK if first else 2 * ATTN_BLOCK
        bias = bias1_ref[...] if first else bias2_ref[...]
        one = jnp.ones((), BF16)
        loaded = []
        for _, d, qstart in items:
            kstart = qstart if first else qstart - d * ATTN_BLOCK
            loaded.append((rows(q_ref, qstart, ATTN_BLOCK, d), rows(k_ref, kstart, nk, d),
                           rows(v_ref, kstart, nk, d)))
        results = []
        for qrows, krows, vrows in loaded:
            kb = krows.astype(BF16)
            vb = vrows.astype(BF16)
            q2 = jnp.concatenate([jnp.where(head0, qrows, 0.0), jnp.where(head0, 0.0, qrows)], axis=0).astype(BF16)
            s = lax.dot_general(q2, kb, (((1,), (1,)), ((), ())), preferred_element_type=F32) + bias
            m = jnp.max(s, axis=-1, keepdims=True)
            p = jnp.exp(s - m).astype(BF16)
            pv0 = jnp.dot(p[:ATTN_BLOCK], jnp.where(head0, vb, one), preferred_element_type=F32)
            pv1 = jnp.dot(p[ATTN_BLOCK:], jnp.where(head0, one, vb), preferred_element_type=F32)
            results.append((jnp.where(head0, pv0, pv1), jnp.where(head0, m[:ATTN_BLOCK], m[ATTN_BLOCK:]),
                            jnp.where(head0, pv1, pv0)))
        for (br, d, qstart), (acc, m, l) in zip(items, results):
            dst = pl.ds(qstart, ATTN_BLOCK) if d == 1 else pl.ds(qstart, ATTN_BLOCK, stride=d)
            acc_scr[br, dst, :] = acc
            m_scr[br, dst, :] = m
            l_scr[br, dst, :] = l

    def largest_divisor(n):
        return max(f for f in range(1, ATTN_UNROLL + 1) if n % f == 0)

    grouped = []
    for br, (window, d) in enumerate(DILATED_PATTERNS):
        assert window // d == ATTN_BLOCK
        if d <= ATTN_UNROLL:
            grouped += [(br, d, r) for r in range(d)]
        else:
            un = largest_divisor(d)

            def first_body(i, carry, br=br, d=d, un=un):
                tiles([(br, d, i * un + j) for j in range(un)], True)
                return carry

            lax.fori_loop(0, d // un, first_body, 0)
    for i in range(0, len(grouped), ATTN_UNROLL):
        tiles(grouped[i:i + ATTN_UNROLL], True)

    for br, (window, d) in enumerate(DILATED_PATTERNS):
        nblk = s_len // (d * ATTN_BLOCK)
        n_rest = d * (nblk - 1)
        if n_rest:
            un = largest_divisor(n_rest)

            def rest_body(i, carry, br=br, d=d, nblk=nblk, un=un):
                items = []
                for j in range(un):
                    n = i * un + j
                    qstart = n // (nblk - 1) + d * ATTN_BLOCK * (1 + n % (nblk - 1))
                    items.append((br, d, pl.multiple_of(qstart, ATTN_BLOCK) if d == 1 else qstart))
                tiles(items, False)
                return carry

            lax.fori_loop(0, n_rest // un, rest_body, 0)

    ch = 256

    def merge_body(i, carry):
        sl = pl.ds(pl.multiple_of(i * ch, ch), ch)
        ms = [m_scr[br, sl, :] for br in range(3)]
        mx = jnp.maximum(jnp.maximum(ms[0], ms[1]), ms[2])
        num = jnp.zeros((ch, LANES), F32)
        den = jnp.zeros((ch, LANES), F32)
        for br in range(3):
            e = jnp.exp(ms[br] - mx)
            num = num + e * acc_scr[br, sl, :]
            den = den + e * pltpu.roll(l_scr[br, sl, :], HEAD_DIM, 1)
        o_ref[sl, :] = num / den
        return carry

    lax.fori_loop(0, s_len // ch, merge_body, 0)


def _attn_prompt(q, k, v):
    nb, s, _ = q.shape
    assert s % (16 * ATTN_BLOCK) == 0
    bias2, bias1 = _band_bias()
    blk = pl.BlockSpec((None, s, LANES), lambda b, p: (b, 0, p))
    const = lambda a: pl.BlockSpec(a.shape, lambda b, p: (0, 0))
    return pl.pallas_call(
        _attn_prompt_kernel,
        grid=(nb, D_ATTN // LANES),
        in_specs=[blk, blk, blk, const(bias2), const(bias1)],
        out_specs=blk,
        out_shape=jax.ShapeDtypeStruct((nb, s, D_ATTN), F32),
        scratch_shapes=[pltpu.VMEM((3, s, LANES), F32)] * 3,
        compiler_params=_cparams(("parallel", "parallel")),
    )(q, k, v, jnp.asarray(bias2), jnp.asarray(bias1))


def _sample_bias(t_new, w_buf):
    n_new = -(-t_new // LANES) * LANES
    t = np.arange(t_new)[:, None]
    idx = np.arange(w_buf + n_new)[None, :]
    dist = w_buf + t - idx
    mult = np.zeros(dist.shape, np.int64)
    for window, d in DILATED_PATTERNS:
        mult += ((dist >= 0) & (dist % d == 0) & (dist <= window)).astype(np.int64)
    bias = np.where(mult > 0, np.log(np.maximum(mult, 1)), NEG_BIG).astype(np.float32)
    bias = np.repeat(bias, N_HEADS, axis=0)
    return bias[:, :w_buf], bias[:, w_buf:]


def _attn_sample_kernel(q_ref, kn_ref, vn_ref, kt_ref, vt_ref, bias_ref, biasn_ref, o_ref):
    t_new = q_ref.shape[0]
    n_new = biasn_ref.shape[1]
    row_head = lax.broadcasted_iota(jnp.int32, (t_new * N_HEADS, D_ATTN), 0) & (N_HEADS - 1)
    col_head = lax.broadcasted_iota(jnp.int32, (t_new * N_HEADS, D_ATTN), 1) >> (HEAD_DIM.bit_length() - 1)
    own = row_head == col_head
    q = q_ref[...]
    q_rows = jnp.concatenate([jnp.broadcast_to(q[t:t + 1], (N_HEADS, D_ATTN)) for t in range(t_new)], axis=0)
    q_bd = jnp.where(own, q_rows, 0.0).astype(BF16)
    pad = jnp.zeros((n_new - t_new, D_ATTN), F32)
    kn = jnp.concatenate([kn_ref[...], pad], axis=0).astype(BF16)
    vn = jnp.concatenate([vn_ref[...], pad], axis=0).astype(BF16)
    nt = (((1,), (1,)), ((), ()))
    s_c = jnp.dot(q_bd, kt_ref[...].astype(BF16), preferred_element_type=F32) + bias_ref[...]
    s_n = lax.dot_general(q_bd, kn, nt, preferred_element_type=F32) + biasn_ref[...]
    m = jnp.maximum(jnp.max(s_c, axis=-1, keepdims=True), jnp.max(s_n, axis=-1, keepdims=True))
    p_c = jnp.exp(s_c - m)
    p_n = jnp.exp(s_n - m)
    l = jnp.sum(p_c, axis=-1, keepdims=True) + jnp.sum(p_n, axis=-1, keepdims=True)
    o = lax.dot_general(p_c.astype(BF16), vt_ref[...].astype(BF16), nt, preferred_element_type=F32)
    o = (o + jnp.dot(p_n.astype(BF16), vn, preferred_element_type=F32)) / l
    o = jnp.where(own, o, 0.0)
    o_ref[...] = jnp.concatenate(
        [jnp.sum(o[t * N_HEADS:(t + 1) * N_HEADS], axis=0, keepdims=True) for t in range(t_new)], axis=0)


def _attn_sample(q, k_new, v_new, cache_k, cache_v):
    nb, t_new, _ = q.shape
    w_buf = cache_k.shape[1]
    bias, bias_new = _sample_bias(t_new, w_buf)
    as_stored = lambda a: a.transpose(0, 2, 3, 1).reshape(nb, D_ATTN, w_buf)
    new = pl.BlockSpec((None, t_new, D_ATTN), lambda b: (b, 0, 0))
    buf = pl.BlockSpec((None, D_ATTN, w_buf), lambda b: (b, 0, 0))
    const = lambda a: pl.BlockSpec(a.shape, lambda b: (0, 0))
    return pl.pallas_call(
        _attn_sample_kernel,
        grid=(nb,),
        in_specs=[new, new, new, buf, buf, const(bias), const(bias_new)],
        out_specs=new,
        out_shape=jax.ShapeDtypeStruct((nb, t_new, D_ATTN), F32),
        compiler_params=_cparams(("parallel",)),
    )(q, k_new, v_new, as_stored(cache_k), as_stored(cache_v), jnp.asarray(bias), jnp.asarray(bias_new))


def _s5_params(lam_re, lam_im, log_dt, b_re, b_im, c_re, c_im):
    f32 = F32
    dt = jnp.exp(log_dt.astype(f32))[:, None]
    lr, li = lam_re.astype(f32), lam_im.astype(f32)
    ea = jnp.exp(lr * dt)
    a_re, a_im = ea * jnp.cos(li * dt), ea * jnp.sin(li * dt)
    den = lr * lr + li * li
    co_re = ((a_re - 1.0) * lr + a_im * li) / den
    co_im = (a_im * lr - (a_re - 1.0) * li) / den
    bb_re = co_re[..., None] * b_re - co_im[..., None] * b_im
    bb_im = co_re[..., None] * b_im + co_im[..., None] * b_re
    eye = jnp.eye(8, dtype=f32)

    def b_blocks(bb):
        t = bb.reshape(4, 8, SSM_STATE, SSM_GROUP_CH)
        return jnp.einsum('ab,kapc->kacbp', eye, t).reshape(4, 8 * SSM_GROUP_CH, 8 * SSM_STATE)

    def c_blocks(cc):
        t = cc.reshape(4, 8, SSM_GROUP_CH, SSM_STATE)
        return jnp.einsum('ab,kacp->kbpac', eye, t).reshape(4, 8 * SSM_STATE, 8 * SSM_GROUP_CH)

    b_mat = jnp.concatenate([b_blocks(bb_re), b_blocks(bb_im)], axis=2).astype(BF16)
    c_mat = jnp.concatenate([c_blocks(c_re.astype(f32)), -c_blocks(c_im.astype(f32))], axis=1).astype(BF16)
    return a_re.reshape(-1), a_im.reshape(-1), b_mat, c_mat


def _gelu_tanh(y):
    return 0.5 * y * (1.0 + jnp.tanh(math.sqrt(2.0 / math.pi) * (y + 0.044715 * (y * y * y))))


def _s5_epilogue(y, u, d_ref, wglu_ref, bglu_ref, g_ref):
    y = y + d_ref[...] * u
    z = _gelu_tanh(y)
    gate = jnp.dot(z.astype(BF16), wglu_ref[...], preferred_element_type=F32) + bglu_ref[...]
    out = z * jax.nn.sigmoid(gate)
    ms = jnp.mean(out * out, axis=-1, keepdims=True)
    return out * lax.rsqrt(ms + EPS) * g_ref[...]


def _s5_prompt_kernel(u_ref, are_ref, aim_ref, b_ref, c_ref, d_ref, wglu_ref, bglu_ref, g_ref,
                      o_ref, hre_ref, him_ref, scr, hst):
    nseq, tt, _ = u_ref.shape
    n_slab = D_SSM * SSM_STATE // SSM_GROUP_CH // LANES
    ti = pl.program_id(1)

    @pl.when(ti == 0)
    def _():
        hst[...] = jnp.zeros_like(hst)

    u2 = u_ref[...].reshape(nseq * tt, D_SSM)
    ub = u2.astype(BF16)
    for kc in range(4):
        bu = jnp.dot(ub[:, kc * LANES:(kc + 1) * LANES], b_ref[kc], preferred_element_type=F32)
        for part in range(2):
            for j in range(4):
                col = part * 512 + j * LANES
                for b in range(nseq):
                    scr[part * n_slab + 4 * kc + j, b * SSM_PITCH:b * SSM_PITCH + tt, :] = (
                        bu[b * tt:(b + 1) * tt, col:col + LANES])

    grp = 4
    for sg in range(n_slab // grp):
        slabs = [sg * grp + i for i in range(grp)]
        ar = [are_ref[s] for s in slabs]
        ai = [aim_ref[s] for s in slabs]

        def step(t, carry, slabs=slabs, ar=ar, ai=ai):
            hr, hi = carry
            nhr, nhi = [], []
            for i, s in enumerate(slabs):
                sel = pl.ds(t, nseq, stride=SSM_PITCH)
                br = scr[s, sel, :]
                bi = scr[n_slab + s, sel, :]
                r = ar[i] * hr[i] - ai[i] * hi[i] + br
                im = ar[i] * hi[i] + ai[i] * hr[i] + bi
                scr[s, sel, :] = r
                scr[n_slab + s, sel, :] = im
                nhr.append(r)
                nhi.append(im)
            return tuple(nhr), tuple(nhi)

        init = (tuple(hst[s] for s in slabs), tuple(hst[n_slab + s] for s in slabs))
        hr, hi = lax.fori_loop(0, tt, step, init, unroll=4)
        for i, s in enumerate(slabs):
            hst[s] = hr[i]
            hst[n_slab + s] = hi[i]

    ys = []
    for kc in range(4):
        slabs = [4 * kc + j for j in range(4)] + [n_slab + 4 * kc + j for j in range(4)]
        lhs = jnp.concatenate(
            [jnp.concatenate([scr[s, b * SSM_PITCH:b * SSM_PITCH + tt, :].astype(BF16) for s in slabs], axis=1)
             for b in range(nseq)], axis=0)
        ys.append(jnp.dot(lhs, c_ref[kc], preferred_element_type=F32))
    y = jnp.concatenate(ys, axis=1)
    o_ref[...] = _s5_epilogue(y, u2, d_ref, wglu_ref, bglu_ref, g_ref).reshape(nseq, tt, D_SSM)

    @pl.when(ti == pl.num_programs(1) - 1)
    def _():
        for s in range(n_slab):
            hre_ref[:, s * LANES:(s + 1) * LANES] = hst[s]
            him_ref[:, s * LANES:(s + 1) * LANES] = hst[n_slab + s]


def _s5_prompt(u, a_re, a_im, b_mat, c_mat, d, w_glu_bf16, b_glu, g_out):
    nb, s, _ = u.shape
    nseq = SUBLANES
    assert nb % nseq == 0 and s % SSM_TT == 0
    n_state = a_re.shape[0]
    n_slab = n_state // LANES
    bcast = lambda a: jnp.broadcast_to(a.reshape(n_slab, 1, LANES), (n_slab, nseq, LANES))
    const = lambda a: pl.BlockSpec(a.shape, lambda b, i: (0,) * a.ndim)
    args = (u, bcast(a_re), bcast(a_im), b_mat, c_mat, d.reshape(1, D_SSM), w_glu_bf16,
            b_glu.reshape(1, D_SSM), g_out.reshape(1, D_SSM))
    st = jax.ShapeDtypeStruct((nb, n_state), F32)
    return pl.pallas_call(
        _s5_prompt_kernel,
        grid=(nb // nseq, s // SSM_TT),
        in_specs=[pl.BlockSpec((nseq, SSM_TT, D_SSM), lambda b, i: (b, i, 0))] + [const(a) for a in args[1:]],
        out_specs=[pl.BlockSpec((nseq, SSM_TT, D_SSM), lambda b, i: (b, i, 0)),
                   pl.BlockSpec((nseq, n_state), lambda b, i: (b, 0)),
                   pl.BlockSpec((nseq, n_state), lambda b, i: (b, 0))],
        out_shape=[jax.ShapeDtypeStruct((nb, s, D_SSM), F32), st, st],
        scratch_shapes=[pltpu.VMEM((2 * n_slab, nseq * SSM_PITCH, LANES), F32),
                        pltpu.VMEM((2 * n_slab, nseq, LANES), F32)],
        compiler_params=_cparams(("parallel", "arbitrary")),
    )(*args)


def _s5_sample_kernel(u_ref, h0re_ref, h0im_ref, are_ref, aim_ref, b_ref, c_ref, d_ref, wglu_ref, bglu_ref, g_ref,
                      o_ref, hre_ref, him_ref):
    t_new = u_ref.shape[0]
    hre = h0re_ref[...]
    him = h0im_ref[...]
    are, aim = are_ref[...], aim_ref[...]
    half = 4 * LANES
    for t in range(t_new):
        u = u_ref[t]
        ub = u.astype(BF16)
        bus = [jnp.dot(ub[:, kc * LANES:(kc + 1) * LANES], b_ref[kc], preferred_element_type=F32) for kc in range(4)]
        bre = jnp.concatenate([bu[:, :half] for bu in bus], axis=1)
        bim = jnp.concatenate([bu[:, half:] for bu in bus], axis=1)
        hre, him = are * hre - aim * him + bre, are * him + aim * hre + bim
        ys = []
        for kc in range(4):
            lhs = jnp.concatenate([hre[:, kc * half:(kc + 1) * half], him[:, kc * half:(kc + 1) * half]], axis=1)
            ys.append(jnp.dot(lhs.astype(BF16), c_ref[kc], preferred_element_type=F32))
        y = jnp.concatenate(ys, axis=1)
        o_ref[t] = _s5_epilogue(y, u, d_ref, wglu_ref, bglu_ref, g_ref)
    hre_ref[...] = hre
    him_ref[...] = him


def _s5_sample(u_tm, h0_re, h0_im, a_re, a_im, b_mat, c_mat, d, w_glu_bf16, b_glu, g_out):
    t_new, nb, _ = u_tm.shape
    n_state = a_re.shape[0]
    tb = 64
    assert nb % tb == 0
    const = lambda a: pl.BlockSpec(a.shape, lambda b: (0,) * a.ndim)
    args = (u_tm, h0_re, h0_im, a_re.reshape(1, n_state), a_im.reshape(1, n_state), b_mat, c_mat,
            d.reshape(1, D_SSM), w_glu_bf16, b_glu.reshape(1, D_SSM), g_out.reshape(1, D_SSM))
    st_spec = pl.BlockSpec((tb, n_state), lambda b: (b, 0))
    st = jax.ShapeDtypeStruct((nb, n_state), F32)
    return pl.pallas_call(
        _s5_sample_kernel,
        grid=(nb // tb,),
        in_specs=[pl.BlockSpec((t_new, tb, D_SSM), lambda b: (0, b, 0)), st_spec, st_spec]
                 + [const(a) for a in args[3:]],
        out_specs=[pl.BlockSpec((t_new, tb, D_SSM), lambda b: (0, b, 0)), st_spec, st_spec],
        out_shape=[jax.ShapeDtypeStruct((t_new, nb, D_SSM), F32), st, st],
        compiler_params=_cparams(("parallel",)),
    )(*args)


def _store_packed(ref, val):
    half = val.shape[1] // 2
    bits = pltpu.bitcast(val.astype(BF16).astype(F32), jnp.uint32)
    words = bits[:, :half] | lax.shift_right_logical(bits[:, half:], jnp.uint32(16))
    for c in range(PACK_CHUNKS):
        ref[c] = words[:, c * LANES:(c + 1) * LANES]


def _load_packed(ref):
    hi, lo = [], []
    for c in range(PACK_CHUNKS):
        w = ref[c]
        hi.append(pltpu.bitcast(w & jnp.uint32(0xFFFF0000), F32))
        lo.append(pltpu.bitcast(lax.shift_left(w, jnp.uint32(16)), F32))
    return jnp.concatenate(hi + lo, axis=1)


def _outproj_kernel(x_ref, oa_ref, os_ref, gate_ref, sh_ref, sc_ref, ga_ref, gf_ref, w_ref, wr_hi_ref, wr_lo_ref,
                    br_ref, x1_ref, h_ref, lg_ref):
    oa = oa_ref[...]
    ms = jnp.mean(oa * oa, axis=-1, keepdims=True)
    na = oa * lax.rsqrt(ms + EPS) * ga_ref[...]
    merged = jnp.concatenate([na, os_ref[...]], axis=-1).astype(BF16)
    x1 = x_ref[...] + gate_ref[...] * jnp.dot(merged, w_ref[...], preferred_element_type=F32)
    x1_ref[...] = x1
    ms = jnp.mean(x1 * x1, axis=-1, keepdims=True)
    h = x1 * lax.rsqrt(ms + EPS) * gf_ref[...]
    h = h * (1.0 + sc_ref[...]) + sh_ref[...]
    h_hi = h.astype(BF16)
    _store_packed(h_ref, h)
    h_lo = (h - h_hi.astype(F32)).astype(BF16)
    lg = jnp.dot(h_hi, wr_hi_ref[...], preferred_element_type=F32)
    lg = lg + jnp.dot(h_hi, wr_lo_ref[...], preferred_element_type=F32)
    lg = lg + jnp.dot(h_lo, wr_hi_ref[...], preferred_element_type=F32)
    lg_ref[...] = lg + br_ref[...]


def _outproj(x, o_attn, o_ssm, gate, shift, scale, g_attn, g_ffn, w_out_bf16, w_router, b_router, tm):
    nb, s, d = x.shape
    row = lambda n: pl.BlockSpec((None, tm, n), lambda b, i: (b, i, 0))
    const = lambda a: pl.BlockSpec(a.shape, lambda b, i: (0,) * a.ndim)
    wr_hi = w_router.astype(BF16)
    wr_lo = (w_router - wr_hi.astype(F32)).astype(BF16)
    consts = (g_attn.reshape(1, D_ATTN), g_ffn.reshape(1, d), w_out_bf16, wr_hi, wr_lo,
              b_router.reshape(1, ROUTER_COLS))
    return pl.pallas_call(
        _outproj_kernel,
        grid=(nb, s // tm),
        in_specs=[row(d), row(D_ATTN), row(D_SSM), _mod_spec(gate, tm), _mod_spec(shift, tm), _mod_spec(scale, tm)]
                 + [const(a) for a in consts],
        out_specs=[row(d), pl.BlockSpec((PACK_CHUNKS, tm, LANES), lambda b, i: (0, b * (s // tm) + i, 0)),
                   row(ROUTER_COLS)],
        out_shape=[jax.ShapeDtypeStruct((nb, s, d), F32),
                   jax.ShapeDtypeStruct((PACK_CHUNKS, nb * s, LANES), jnp.uint32),
                   jax.ShapeDtypeStruct((nb, s, ROUTER_COLS), F32)],
        compiler_params=_cparams(("parallel", "parallel")),
    )(x, o_attn, o_ssm, gate, shift, scale, *consts)


def _expert_kernel(blk_e_ref, nvalid_ref, x_ref, wg_ref, wu_ref, wd_ref, y_ref, wg_s, wu_s, wd_s):
    i = pl.program_id(0)

    @pl.when((i == 0) | (blk_e_ref[i] != blk_e_ref[jnp.maximum(i - 1, 0)]))
    def _():
        wg_s[...] = wg_ref[...].astype(BF16)
        wu_s[...] = wu_ref[...].astype(BF16)
        wd_s[...] = wd_ref[...].astype(BF16)

    @pl.when(i < nvalid_ref[0])
    def _():
        x = _load_packed(x_ref).astype(BF16)
        a = jnp.dot(x, wg_s[...], preferred_element_type=F32)
        b = jnp.dot(x, wu_s[...], preferred_element_type=F32)
        hid = (a * jax.nn.sigmoid(a) * b).astype(BF16)
        _store_packed(y_ref, jnp.dot(hid, wd_s[...], preferred_element_type=F32))

    @pl.when(i >= nvalid_ref[0])
    def _():
        y_ref[...] = jnp.zeros_like(y_ref)


def _experts(xs, blk_e, nvalid, w_gate, w_up, w_down):
    nblk = xs.shape[1] // MOE_TILE
    d = w_gate.shape[1]
    wspec = lambda shp: pl.BlockSpec((None,) + shp, lambda i, be, nv: (be[i], 0, 0))
    tile = pl.BlockSpec((PACK_CHUNKS, MOE_TILE, LANES), lambda i, be, nv: (0, i, 0))
    return pl.pallas_call(
        _expert_kernel,
        grid_spec=pltpu.PrefetchScalarGridSpec(
            num_scalar_prefetch=2,
            grid=(nblk,),
            in_specs=[tile, wspec((d, D_EXPERT)), wspec((d, D_EXPERT)), wspec((D_EXPERT, d))],
            out_specs=tile,
            scratch_shapes=[pltpu.VMEM((d, D_EXPERT), BF16), pltpu.VMEM((d, D_EXPERT), BF16),
                            pltpu.VMEM((D_EXPERT, d), BF16)],
        ),
        out_shape=jax.ShapeDtypeStruct(xs.shape, jnp.uint32),
        compiler_params=_cparams(("arbitrary",)),
    )(blk_e, nvalid, xs, w_gate, w_up, w_down)


def _combine_kernel(x1_ref, y0_ref, y1_ref, info_ref, gate_ref, gfin_ref, o_ref):
    info = pltpu.bitcast(info_ref[...], F32)
    g0 = info[:, INFO_GATE:INFO_GATE + 1]
    g1 = info[:, INFO_GATE + 1:INFO_GATE + 2]
    moe = g0 * _load_packed(y0_ref) + g1 * _load_packed(y1_ref)
    x2 = x1_ref[...] + gate_ref[...] * moe
    ms = jnp.mean(x2 * x2, axis=-1, keepdims=True)
    o_ref[...] = x2 * lax.rsqrt(ms + EPS) * gfin_ref[...]


def _combine(x1, y01, info, gate, g_final, tm):
    nb, s, d = x1.shape
    row = lambda n: pl.BlockSpec((None, tm, n), lambda b, i: (b, i, 0))
    packed = lambda k: pl.BlockSpec((None, PACK_CHUNKS, tm, LANES), lambda b, i: (k, 0, b * (s // tm) + i, 0))
    return pl.pallas_call(
        _combine_kernel,
        grid=(nb, s // tm),
        in_specs=[row(d), packed(0), packed(1), row(ROUTER_COLS), _mod_spec(gate, tm),
                  pl.BlockSpec((1, d), lambda b, i: (0, 0))],
        out_specs=row(d),
        out_shape=jax.ShapeDtypeStruct((nb, s, d), F32),
        compiler_params=_cparams(("parallel", "parallel")),
    )(x1, y01, y01, info, gate, g_final.reshape(1, d))


def _sc_window(n_rows):
    assert n_rows % SC_WINDOW == 0
    return SC_WINDOW


def _sc_gather_rows(table, idx):
    n = idx.shape[0]
    w = _sc_window(n)
    mesh = plsc.VectorSubcoreMesh(core_axis_name="core", subcore_axis_name="subcore")

    @functools.partial(pl.kernel, out_type=jax.ShapeDtypeStruct((n, LANES), table.dtype), mesh=mesh)
    def gather_kernel(x_hbm, i_hbm, o_hbm):
        def body(i_vmem, o_vmem):
            pltpu.sync_copy(x_hbm.at[i_vmem.at[0]], o_vmem)

        pltpu.emit_pipeline(
            body, grid=(n // w,),
            in_specs=[pl.BlockSpec((1, w), lambda i: (0, i))],
            out_specs=[pl.BlockSpec((w, LANES), lambda i: (i, 0))],
            core_axis_name=("core", "subcore"), dimension_semantics=(pltpu.PARALLEL,),
        )(i_hbm, o_hbm)

    return gather_kernel(table, idx.reshape(1, n))


def _sc_scatter_rows(rows, idx0, idx1, n_out):
    n = rows.shape[0]
    w = _sc_window(n)
    mesh = plsc.VectorSubcoreMesh(core_axis_name="core", subcore_axis_name="subcore")

    @functools.partial(pl.kernel, out_type=jax.ShapeDtypeStruct((n_out, LANES), rows.dtype), mesh=mesh)
    def scatter_kernel(x_hbm, i0_hbm, i1_hbm, o_hbm):
        def body(x_vmem, i0_vmem, i1_vmem):
            pltpu.sync_copy(x_vmem, o_hbm.at[i0_vmem.at[0]])
            pltpu.sync_copy(x_vmem, o_hbm.at[i1_vmem.at[0]])

        pltpu.emit_pipeline(
            body, grid=(n // w,),
            in_specs=[pl.BlockSpec((w, LANES), lambda i: (i, 0)),
                      pl.BlockSpec((1, w), lambda i: (0, i)),
                      pl.BlockSpec((1, w), lambda i: (0, i))],
            out_specs=[],
            core_axis_name=("core", "subcore"), dimension_semantics=(pltpu.PARALLEL,),
        )(x_hbm, i0_hbm, i1_hbm)

    return scatter_kernel(rows, idx0.reshape(1, n), idx1.reshape(1, n))


def _route_kernel(lg_ref, tri_ref, info_ref, rec_ref, cnt_ref, carry):
    i = pl.program_id(0)

    @pl.when(i == 0)
    def _():
        carry[...] = jnp.zeros_like(carry)

    lg = lg_ref[...]
    lane = lax.broadcasted_iota(jnp.int32, lg.shape, 1)
    lane_f = lane.astype(F32)
    none = float(ROUTER_COLS)
    ninf = float("-inf")
    first = lambda cond: jnp.min(jnp.where(cond, lane_f, none), axis=-1, keepdims=True)

    is_c = lane < N_EXPERT_GROUPS
    lc = jnp.where(is_c, lg, ninf)
    mc = jnp.max(lc, axis=-1, keepdims=True)
    p_grp = 1.0 / jnp.sum(jnp.exp(lc - mc), axis=-1, keepdims=True)
    grp = first(lc == mc)
    fine = lane - N_EXPERT_GROUPS
    fine_grp = lax.shift_right_arithmetic(fine, jnp.int32(EXPERTS_PER_GROUP.bit_length() - 1))
    in_grp = (fine >= 0) & (fine < N_EXPERTS) & (fine_grp.astype(F32) == grp)
    lf = jnp.where(in_grp, lg, ninf)
    v1 = jnp.max(lf, axis=-1, keepdims=True)
    i1 = first(lf == v1)
    lf2 = jnp.where(lane_f == i1, ninf, lf)
    v2 = jnp.max(lf2, axis=-1, keepdims=True)
    i2 = first(lf2 == v2)
    b = jnp.exp(v2 - v1)
    g0 = p_grp / (1.0 + b)
    g1 = p_grp * b / (1.0 + b)

    hit1 = lane_f == i1
    hit2 = lane_f == i2
    onehot = jnp.where(hit1 | hit2, 1.0, 0.0)
    before = jnp.dot(tri_ref[...], onehot.astype(BF16), preferred_element_type=F32) + carry[0:1, :]
    r1 = jnp.sum(jnp.where(hit1, before, 0.0), axis=-1, keepdims=True)
    r2 = jnp.sum(jnp.where(hit2, before, 0.0), axis=-1, keepdims=True)
    carry[...] = carry[...] + jnp.sum(onehot, axis=0, keepdims=True)

    as_int = lambda v: jnp.broadcast_to(v, lg.shape).astype(jnp.int32)
    as_bits = lambda v: pltpu.bitcast(jnp.broadcast_to(v, lg.shape), jnp.int32)
    info = jnp.zeros(lg.shape, jnp.int32)
    fields = ((INFO_EXPERT, as_int(i1 - N_EXPERT_GROUPS)), (INFO_EXPERT + 1, as_int(i2 - N_EXPERT_GROUPS)),
              (INFO_RANK, as_int(r1)), (INFO_RANK + 1, as_int(r2)), (INFO_GATE, as_bits(g0)),
              (INFO_GATE + 1, as_bits(g1)))
    for col, val in fields:
        info = jnp.where(lane == col, val, info)
    info_ref[...] = info
    rec_ref[...] = pltpu.bitcast(pltpu.bitcast(info, F32).T[:SUBLANES], jnp.int32)

    @pl.when(i == pl.num_programs(0) - 1)
    def _():
        cnt_ref[...] = carry[...]


def _route(logits):
    t = logits.shape[0]
    tr = min(t, 512)
    assert t % tr == 0
    tri = jnp.asarray(np.tril(np.ones((tr, tr), np.float32), -1), BF16)
    info, rec, cnt = pl.pallas_call(
        _route_kernel,
        grid=(t // tr,),
        in_specs=[pl.BlockSpec((tr, ROUTER_COLS), lambda i: (i, 0)), pl.BlockSpec((tr, tr), lambda i: (0, 0))],
        out_specs=[pl.BlockSpec((tr, ROUTER_COLS), lambda i: (i, 0)),
                   pl.BlockSpec((SUBLANES, tr), lambda i: (0, i)),
                   pl.BlockSpec((SUBLANES, ROUTER_COLS), lambda i: (0, 0))],
        out_shape=[jax.ShapeDtypeStruct((t, ROUTER_COLS), jnp.int32),
                   jax.ShapeDtypeStruct((SUBLANES, t), jnp.int32),
                   jax.ShapeDtypeStruct((SUBLANES, ROUTER_COLS), F32)],
        scratch_shapes=[pltpu.VMEM((SUBLANES, ROUTER_COLS), F32)],
        compiler_params=_cparams(("arbitrary",)),
    )(logits, tri)
    counts = cnt[0, N_EXPERT_GROUPS:N_EXPERT_GROUPS + N_EXPERTS].astype(jnp.int32)
    return info, rec, counts


def _moe(h_packed, logits, w_gate, w_up, w_down):
    t = logits.shape[0]
    info, rec, counts = _route(logits)
    padded = (counts + MOE_TILE - 1) // MOE_TILE * MOE_TILE
    pend = jnp.cumsum(padded)
    pstart = pend - padded
    nblk = -(-2 * t // MOE_TILE) + N_EXPERTS
    n_slots = nblk * MOE_TILE
    blk_start = jnp.arange(nblk, dtype=jnp.int32) * MOE_TILE
    blk_e = jnp.minimum(jnp.sum(blk_start[:, None] >= pend[None, :], axis=1), N_EXPERTS - 1).astype(jnp.int32)
    nvalid = (pend[-1] // MOE_TILE).astype(jnp.int32).reshape(1)
    chunk_base = (jnp.arange(PACK_CHUNKS, dtype=jnp.int32) * n_slots)[:, None]
    rows = lambda k: (chunk_base + (jnp.take(pstart, rec[INFO_EXPERT + k]).astype(jnp.int32)
                                    + rec[INFO_RANK + k])[None, :]).reshape(-1)
    idx0, idx1 = rows(0), rows(1)
    xs = _sc_scatter_rows(h_packed.reshape(PACK_CHUNKS * t, LANES), idx0, idx1, PACK_CHUNKS * n_slots)
    ys = _experts(xs.reshape(PACK_CHUNKS, n_slots, LANES), blk_e, nvalid, w_gate, w_up, w_down)
    y01 = _sc_gather_rows(ys.reshape(PACK_CHUNKS * n_slots, LANES), jnp.concatenate([idx0, idx1]))
    return y01.reshape(2, PACK_CHUNKS, t, LANES), info


def _layer(x, mods_mix, mods_ffn, attn_fn, s5_fn, wts, tm):
    nb, s, d = x.shape
    q, k, v, u = _inproj(x, mods_mix[0], mods_mix[1], wts['g_mix'], wts['w_in'], tm)
    o_attn = attn_fn(q, k, v)
    o_ssm, h_re, h_im = s5_fn(u)
    x1, h2, logits = _outproj(x, o_attn, o_ssm, mods_mix[2], mods_ffn[0], mods_ffn[1], wts['g_attn_out'],
                              wts['g_ffn'], wts['w_out'], wts['w_router'], wts['b_router'], tm)
    y01, info = _moe(h2, logits.reshape(nb * s, ROUTER_COLS), wts['w_gate'], wts['w_up'], wts['w_down'])
    y = _combine(x1, y01, info.reshape(nb, s, ROUTER_COLS), mods_ffn[2], wts['g_final'], tm)
    return y, k, v, h_re, h_im


def kernel(x_prompt, x_sample, cache_k_win, cache_v_win, state_ssm_re, state_ssm_im, c_prompt, c_sample, g_mix, w_ada_mix, b_ada_mix, w_in, w_out, g_attn_out, g_ssm_out, ssm_lambda_re, ssm_lambda_im, ssm_log_dt, ssm_b_re, ssm_b_im, ssm_c_re, ssm_c_im, ssm_d, w_glu, b_glu, g_ffn, w_ada_ffn, b_ada_ffn, w_router_coarse, b_router_coarse, w_router_fine, b_router_fine, w_expert_gate, w_expert_up, w_expert_down, g_final):
    depth = g_mix.shape[0]
    assert depth == 1, "single-layer step"
    l = 0
    bp, sp, d = x_prompt.shape
    bs, ts, _ = x_sample.shape
    keep = min(max(w for w, _ in DILATED_PATTERNS), sp)

    c_all = jnp.concatenate([c_prompt, c_sample], axis=0).astype(F32)
    m_mix = _adaln(c_all, w_ada_mix[l], b_ada_mix[l])
    m_ffn = _adaln(c_all, w_ada_ffn[l], b_ada_ffn[l])

    def mods(m, lo, hi, per_token):
        parts = jnp.split(m[lo:hi], 3, axis=-1)
        if per_token:
            return tuple(jnp.repeat(p, ts, axis=0)[None] for p in parts)
        return tuple(p[:, None, :] for p in parts)

    pad = ROUTER_COLS - N_EXPERT_GROUPS - N_EXPERTS
    w_router = jnp.concatenate([w_router_coarse[l], w_router_fine[l], jnp.zeros((d, pad), F32)], axis=1)
    b_router = jnp.concatenate([b_router_coarse[l], b_router_fine[l], jnp.zeros((pad,), F32)])
    wts = {
        'g_mix': g_mix[l], 'w_in': w_in[l].astype(BF16), 'w_out': w_out[l].astype(BF16),
        'g_attn_out': g_attn_out[l], 'g_ffn': g_ffn[l], 'w_router': w_router, 'b_router': b_router,
        'w_gate': w_expert_gate[l], 'w_up': w_expert_up[l], 'w_down': w_expert_down[l], 'g_final': g_final,
    }
    a_re, a_im, b_mat, c_mat = _s5_params(ssm_lambda_re[l], ssm_lambda_im[l], ssm_log_dt[l], ssm_b_re[l],
                                          ssm_b_im[l], ssm_c_re[l], ssm_c_im[l])
    s5_w = (a_re, a_im, b_mat, c_mat, ssm_d[l].reshape(-1), w_glu[l].astype(BF16), b_glu[l], g_ssm_out[l])

    yp, kp, vp, hrp, hip = _layer(
        x_prompt, mods(m_mix, 0, bp, False), mods(m_ffn, 0, bp, False), _attn_prompt,
        lambda u: _s5_prompt(u, *s5_w), wts, tm=512)

    n_state = SSM_GROUPS * SSM_STATE
    ck = cache_k_win[l]
    cv = cache_v_win[l]

    def attn_s(q, k, v):
        r = lambda a: a.reshape(bs, ts, D_ATTN)
        return _attn_sample(r(q), r(k), r(v), ck, cv).reshape(1, bs * ts, D_ATTN)

    def s5_s(u):
        u_tm = u.reshape(bs, ts, D_SSM).transpose(1, 0, 2)
        o, hr, hi = _s5_sample(u_tm, state_ssm_re[l].reshape(bs, n_state), state_ssm_im[l].reshape(bs, n_state),
                               *s5_w)
        return o.transpose(1, 0, 2).reshape(1, bs * ts, D_SSM), hr, hi

    ys, ks, vs, hrs, his = _layer(
        x_sample.reshape(1, bs * ts, d), mods(m_mix, bp, bp + bs, True), mods(m_ffn, bp, bp + bs, True),
        attn_s, s5_s, wts, tm=bs * ts)

    heads = lambda a, b, s: a.reshape(1, b, s, N_HEADS, HEAD_DIM)
    state = lambda a, b: a.reshape(1, b, SSM_GROUPS, SSM_STATE)
    return (yp, ys.reshape(bs, ts, d),
            heads(kp[:, sp - keep:], bp, keep), heads(vp[:, sp - keep:], bp, keep), state(hrp, bp), state(hip, bp),
            heads(ks, bs, ts), heads(vs, bs, ts), state(hrs, bs), state(his, bs))
```

```python
import functools
import math

import numpy as np
import jax
import jax.numpy as jnp
from jax import lax
from jax.experimental import pallas as pl
from jax.experimental.pallas import tpu as pltpu
from jax.experimental.pallas import tpu_sc as plsc

F32 = jnp.float32
BF16 = jnp.bfloat16
HIGHEST = lax.Precision.HIGHEST

D_MODEL = 1024
D_ATTN = 512
D_SSM = 512
HEAD_DIM = 64
N_HEADS = 8
DILATED_PATTERNS = ((128, 1), (512, 4), (2048, 16))
SSM_GROUP_CH = 16
SSM_GROUPS = 32
SSM_STATE = 64
N_EXPERT_GROUPS = 4
EXPERTS_PER_GROUP = 8
N_EXPERTS = 32
D_EXPERT = 512
D_IN_PROJ = 3 * D_ATTN + D_SSM
EPS = 1e-6

LANES = 128
SUBLANES = 8
VMEM_LIMIT = 48 * 1024 * 1024

ATTN_BLOCK = 128
ATTN_UNROLL = 8
NEG_BIG = -1e30
SSM_TT = 128
SSM_PITCH = SSM_TT + 8
ROUTER_COLS = 128
MOE_TILE = 512
PROJ_ROWS = 512
COMBINE_ROWS = 1024
SC_WINDOW = 128
PACK_CHUNKS = D_MODEL // 2 // LANES
INFO_EXPERT, INFO_RANK, INFO_GATE = 0, 2, 4


def _cparams(sem, vmem=VMEM_LIMIT):
    return pltpu.CompilerParams(dimension_semantics=sem, vmem_limit_bytes=vmem)


def _adaln_kernel(c_ref, w_ref, b_ref, o_ref):
    c = c_ref[...]
    s = c * jax.nn.sigmoid(c)
    o_ref[...] = jnp.dot(s, w_ref[...], precision=HIGHEST, preferred_element_type=F32) + b_ref[...]


def _adaln(c, w, b):
    r, d = c.shape
    n = w.shape[1]
    tn = 768
    return pl.pallas_call(
        _adaln_kernel,
        grid=(n // tn,),
        in_specs=[pl.BlockSpec((r, d), lambda j: (0, 0)),
                  pl.BlockSpec((d, tn), lambda j: (0, j)),
                  pl.BlockSpec((1, tn), lambda j: (0, j))],
        out_specs=pl.BlockSpec((r, tn), lambda j: (0, j)),
        out_shape=jax.ShapeDtypeStruct((r, n), F32),
        compiler_params=_cparams(("arbitrary",)),
    )(c, w, b.reshape(1, n))


def _inproj_kernel(x_ref, sh_ref, sc_ref, g_ref, w_ref, q_ref, k_ref, v_ref, u_ref):
    x = x_ref[...]
    ms = jnp.mean(x * x, axis=-1, keepdims=True)
    h = x * lax.rsqrt(ms + EPS) * g_ref[...]
    h = h * (1.0 + sc_ref[...]) + sh_ref[...]
    p = jnp.dot(h.astype(BF16), w_ref[...], preferred_element_type=F32)
    q_ref[...] = p[:, :D_ATTN] * (HEAD_DIM ** -0.5)
    k_ref[...] = p[:, D_ATTN:2 * D_ATTN]
    v_ref[...] = p[:, 2 * D_ATTN:3 * D_ATTN]
    u_ref[...] = p[:, 3 * D_ATTN:]


def _mod_spec(mod, tm):
    d = mod.shape[-1]
    if mod.shape[1] == 1:
        return pl.BlockSpec((None, 1, d), lambda b, i: (b, 0, 0))
    return pl.BlockSpec((None, tm, d), lambda b, i: (b, i, 0))


def _inproj(x, shift, scale, g, w_bf16, tm):
    nb, s, d = x.shape
    row = lambda n: pl.BlockSpec((None, tm, n), lambda b, i: (b, i, 0))
    out = jax.ShapeDtypeStruct((nb, s, D_ATTN), F32)
    return pl.pallas_call(
        _inproj_kernel,
        grid=(nb, s // tm),
        in_specs=[row(d), _mod_spec(shift, tm), _mod_spec(scale, tm),
                  pl.BlockSpec((1, d), lambda b, i: (0, 0)),
                  pl.BlockSpec((d, D_IN_PROJ), lambda b, i: (0, 0))],
        out_specs=[row(D_ATTN), row(D_ATTN), row(D_ATTN), row(D_SSM)],
        out_shape=[out, out, out, jax.ShapeDtypeStruct((nb, s, D_SSM), F32)],
        compiler_params=_cparams(("parallel", "parallel")),
    )(x, shift, scale, g.reshape(1, d), w_bf16)


def _band_bias():
    qi = np.arange(ATTN_BLOCK)[:, None]
    kj = np.arange(ATTN_BLOCK)[None, :]
    cur = kj <= qi
    prev = kj >= qi
    to_bias = lambda m: np.tile(np.where(m, 0.0, NEG_BIG).astype(np.float32), (2, 1))
    return to_bias(np.concatenate([prev, cur], axis=1)), to_bias(cur)


def _attn_prompt_kernel(q_ref, k_ref, v_ref, bias2_ref, bias1_ref, o_ref, acc_scr, m_scr, l_scr):
    s_len = q_ref.shape[0]
    lane = lax.broadcasted_iota(jnp.int32, (1, LANES), 1)
    head0 = lane < HEAD_DIM

    def rows(ref, start, n, d):
        if d == 1:
            return ref[pl.ds(start, n), :]
        return ref[pl.ds(start, n, stride=d), :]

    def tiles(items, first):
        nk = ATTN_BLOCK if first else 2 * ATTN_BLOCK
        bias = bias1_ref[...] if first else bias2_ref[...]
        one = jnp.ones((), BF16)
        loaded = []
        for _, d, qstart in items:
            kstart = qstart if first else qstart - d * ATTN_BLOCK
            loaded.append((rows(q_ref, qstart, ATTN_BLOCK, d), rows(k_ref, kstart, nk, d),
                           rows(v_ref, kstart, nk, d)))
        results = []
        for qrows, krows, vrows in loaded:
            kb = krows.astype(BF16)
            vb = vrows.astype(BF16)
            q2 = jnp.concatenate([jnp.where(head0, qrows, 0.0), jnp.where(head0, 0.0, qrows)], axis=0).astype(BF16)
            s = lax.dot_general(q2, kb, (((1,), (1,)), ((), ())), preferred_element_type=F32) + bias
            m = jnp.max(s, axis=-1, keepdims=True)
            p = jnp.exp(s - m).astype(BF16)
            pv0 = jnp.dot(p[:ATTN_BLOCK], jnp.where(head0, vb, one), preferred_element_type=F32)
            pv1 = jnp.dot(p[ATTN_BLOCK:], jnp.where(head0, one, vb), preferred_element_type=F32)
            results.append((jnp.where(head0, pv0, pv1), jnp.where(head0, m[:ATTN_BLOCK], m[ATTN_BLOCK:]),
                            jnp.where(head0, pv1, pv0)))
        for (br, d, qstart), (acc, m, l) in zip(items, results):
            dst = pl.ds(qstart, ATTN_BLOCK) if d == 1 else pl.ds(qstart, ATTN_BLOCK, stride=d)
            acc_scr[br, dst, :] = acc
            m_scr[br, dst, :] = m
            l_scr[br, dst, :] = l

    def largest_divisor(n):
        return max(f for f in range(1, ATTN_UNROLL + 1) if n % f == 0)

    grouped = []
    for br, (window, d) in enumerate(DILATED_PATTERNS):
        assert window // d == ATTN_BLOCK
        if d <= ATTN_UNROLL:
            grouped += [(br, d, r) for r in range(d)]
        else:
            un = largest_divisor(d)

            def first_body(i, carry, br=br, d=d, un=un):
                tiles([(br, d, i * un + j) for j in range(un)], True)
                return carry

            lax.fori_loop(0, d // un, first_body, 0)
    for i in range(0, len(grouped), ATTN_UNROLL):
        tiles(grouped[i:i + ATTN_UNROLL], True)

    for br, (window, d) in enumerate(DILATED_PATTERNS):
        nblk = s_len // (d * ATTN_BLOCK)
        n_rest = d * (nblk - 1)
        if n_rest:
            un = largest_divisor(n_rest)

            def rest_body(i, carry, br=br, d=d, nblk=nblk, un=un):
                items = []
                for j in range(un):
                    n = i * un + j
                    qstart = n // (nblk - 1) + d * ATTN_BLOCK * (1 + n % (nblk - 1))
                    items.append((br, d, pl.multiple_of(qstart, ATTN_BLOCK) if d == 1 else qstart))
                tiles(items, False)
                return carry

            lax.fori_loop(0, n_rest // un, rest_body, 0)

    ch = 256

    def merge_body(i, carry):
        sl = pl.ds(pl.multiple_of(i * ch, ch), ch)
        ms = [m_scr[br, sl, :] for br in range(3)]
        mx = jnp.maximum(jnp.maximum(ms[0], ms[1]), ms[2])
        num = jnp.zeros((ch, LANES), F32)
        den = jnp.zeros((ch, LANES), F32)
        for br in range(3):
            e = jnp.exp(ms[br] - mx)
            num = num + e * acc_scr[br, sl, :]
            den = den + e * pltpu.roll(l_scr[br, sl, :], HEAD_DIM, 1)
        o_ref[sl, :] = num / den
        return carry

    lax.fori_loop(0, s_len // ch, merge_body, 0)


def _attn_prompt(q, k, v):
    nb, s, _ = q.shape
    assert s % (16 * ATTN_BLOCK) == 0
    bias2, bias1 = _band_bias()
    blk = pl.BlockSpec((None, s, LANES), lambda b, p: (b, 0, p))
    const = lambda a: pl.BlockSpec(a.shape, lambda b, p: (0, 0))
    return pl.pallas_call(
        _attn_prompt_kernel,
        grid=(nb, D_ATTN // LANES),
        in_specs=[blk, blk, blk, const(bias2), const(bias1)],
        out_specs=blk,
        out_shape=jax.ShapeDtypeStruct((nb, s, D_ATTN), F32),
        scratch_shapes=[pltpu.VMEM((3, s, LANES), F32)] * 3,
        compiler_params=_cparams(("parallel", "parallel")),
    )(q, k, v, jnp.asarray(bias2), jnp.asarray(bias1))


def _sample_bias(t_new, w_buf):
    n_new = -(-t_new // LANES) * LANES
    t = np.arange(t_new)[:, None]
    idx = np.arange(w_buf + n_new)[None, :]
    dist = w_buf + t - idx
    mult = np.zeros(dist.shape, np.int64)
    for window, d in DILATED_PATTERNS:
        mult += ((dist >= 0) & (dist % d == 0) & (dist <= window)).astype(np.int64)
    bias = np.where(mult > 0, np.log(np.maximum(mult, 1)), NEG_BIG).astype(np.float32)
    bias = np.repeat(bias, N_HEADS, axis=0)
    return bias[:, :w_buf], bias[:, w_buf:]


def _attn_sample_kernel(q_ref, kn_ref, vn_ref, kt_ref, vt_ref, bias_ref, biasn_ref, o_ref):
    t_new = q_ref.shape[0]
    n_new = biasn_ref.shape[1]
    row_head = lax.broadcasted_iota(jnp.int32, (t_new * N_HEADS, D_ATTN), 0) & (N_HEADS - 1)
    col_head = lax.broadcasted_iota(jnp.int32, (t_new * N_HEADS, D_ATTN), 1) >> (HEAD_DIM.bit_length() - 1)
    own = row_head == col_head
    q = q_ref[...]
    q_rows = jnp.concatenate([jnp.broadcast_to(q[t:t + 1], (N_HEADS, D_ATTN)) for t in range(t_new)], axis=0)
    q_bd = jnp.where(own, q_rows, 0.0).astype(BF16)
    pad = jnp.zeros((n_new - t_new, D_ATTN), F32)
    kn = jnp.concatenate([kn_ref[...], pad], axis=0).astype(BF16)
    vn = jnp.concatenate([vn_ref[...], pad], axis=0).astype(BF16)
    nt = (((1,), (1,)), ((), ()))
    s_c = jnp.dot(q_bd, kt_ref[...].astype(BF16), preferred_element_type=F32) + bias_ref[...]
    s_n = lax.dot_general(q_bd, kn, nt, preferred_element_type=F32) + biasn_ref[...]
    m = jnp.maximum(jnp.max(s_c, axis=-1, keepdims=True), jnp.max(s_n, axis=-1, keepdims=True))
    p_c = jnp.exp(s_c - m)
    p_n = jnp.exp(s_n - m)
    l = jnp.sum(p_c, axis=-1, keepdims=True) + jnp.sum(p_n, axis=-1, keepdims=True)
    o = lax.dot_general(p_c.astype(BF16), vt_ref[...].astype(BF16), nt, preferred_element_type=F32)
    o = (o + jnp.dot(p_n.astype(BF16), vn, preferred_element_type=F32)) / l
    o = jnp.where(own, o, 0.0)
    o_ref[...] = jnp.concatenate(
        [jnp.sum(o[t * N_HEADS:(t + 1) * N_HEADS], axis=0, keepdims=True) for t in range(t_new)], axis=0)


def _attn_sample(q, k_new, v_new, cache_k, cache_v):
    nb, t_new, _ = q.shape
    w_buf = cache_k.shape[1]
    bias, bias_new = _sample_bias(t_new, w_buf)
    as_stored = lambda a: a.transpose(0, 2, 3, 1).reshape(nb, D_ATTN, w_buf)
    new = pl.BlockSpec((None, t_new, D_ATTN), lambda b: (b, 0, 0))
    buf = pl.BlockSpec((None, D_ATTN, w_buf), lambda b: (b, 0, 0))
    const = lambda a: pl.BlockSpec(a.shape, lambda b: (0, 0))
    return pl.pallas_call(
        _attn_sample_kernel,
        grid=(nb,),
        in_specs=[new, new, new, buf, buf, const(bias), const(bias_new)],
        out_specs=new,
        out_shape=jax.ShapeDtypeStruct((nb, t_new, D_ATTN), F32),
        compiler_params=_cparams(("parallel",)),
    )(q, k_new, v_new, as_stored(cache_k), as_stored(cache_v), jnp.asarray(bias), jnp.asarray(bias_new))


def _s5_params(lam_re, lam_im, log_dt, b_re, b_im, c_re, c_im):
    f32 = F32
    dt = jnp.exp(log_dt.astype(f32))[:, None]
    lr, li = lam_re.astype(f32), lam_im.astype(f32)
    ea = jnp.exp(lr * dt)
    a_re, a_im = ea * jnp.cos(li * dt), ea * jnp.sin(li * dt)
    den = lr * lr + li * li
    co_re = ((a_re - 1.0) * lr + a_im * li) / den
    co_im = (a_im * lr - (a_re - 1.0) * li) / den
    bb_re = co_re[..., None] * b_re - co_im[..., None] * b_im
    bb_im = co_re[..., None] * b_im + co_im[..., None] * b_re
    eye = jnp.eye(8, dtype=f32)

    def b_blocks(bb):
        t = bb.reshape(4, 8, SSM_STATE, SSM_GROUP_CH)
        return jnp.einsum('ab,kapc->kacbp', eye, t).reshape(4, 8 * SSM_GROUP_CH, 8 * SSM_STATE)

    def c_blocks(cc):
        t = cc.reshape(4, 8, SSM_GROUP_CH, SSM_STATE)
        return jnp.einsum('ab,kacp->kbpac', eye, t).reshape(4, 8 * SSM_STATE, 8 * SSM_GROUP_CH)

    b_mat = jnp.concatenate([b_blocks(bb_re), b_blocks(bb_im)], axis=2).astype(BF16)
    c_mat = jnp.concatenate([c_blocks(c_re.astype(f32)), -c_blocks(c_im.astype(f32))], axis=1).astype(BF16)
    return a_re.reshape(-1), a_im.reshape(-1), b_mat, c_mat


def _gelu_tanh(y):
    return 0.5 * y * (1.0 + jnp.tanh(math.sqrt(2.0 / math.pi) * (y + 0.044715 * (y * y * y))))


def _s5_epilogue(y, u, d_ref, wglu_ref, bglu_ref, g_ref):
    y = y + d_ref[...] * u
    z = _gelu_tanh(y)
    gate = jnp.dot(z.astype(BF16), wglu_ref[...], preferred_element_type=F32) + bglu_ref[...]
    out = z * jax.nn.sigmoid(gate)
    ms = jnp.mean(out * out, axis=-1, keepdims=True)
    return out * lax.rsqrt(ms + EPS) * g_ref[...]


def _s5_prompt_kernel(u_ref, are_ref, aim_ref, b_ref, c_ref, d_ref, wglu_ref, bglu_ref, g_ref,
                      o_ref, hre_ref, him_ref, scr0, scr1, scr2, scr3, y_scr, hst):
    scrs = (scr0, scr1, scr2, scr3)
    nseq, tt, _ = u_ref.shape
    n_slab = D_SSM * SSM_STATE // SSM_GROUP_CH // LANES
    grp = n_slab // len(scrs)
    steps = tt // nseq
    ti = pl.program_id(1)

    @pl.when(ti == 0)
    def _():
        hst[...] = jnp.zeros_like(hst)

    def project_in(g, b):
        ub = u_ref[b, :, g * LANES:(g + 1) * LANES].astype(BF16)
        bu = jnp.dot(ub, b_ref[g], preferred_element_type=F32)
        rows = pl.ds(pl.multiple_of(b * SSM_PITCH, SUBLANES), tt)
        for p in range(2 * grp):
            scrs[g][p, rows, :] = bu[:, p * LANES:(p + 1) * LANES]

    def project_out(g, b):
        rows = pl.ds(pl.multiple_of(b * SSM_PITCH, SUBLANES), tt)
        lhs = jnp.concatenate([scrs[g][p, rows, :].astype(BF16) for p in range(2 * grp)], axis=1)
        y_scr[pl.ds(pl.multiple_of(b * tt, tt), tt), g * LANES:(g + 1) * LANES] = jnp.dot(
            lhs, c_ref[g], preferred_element_type=F32)

    def in_body(b, carry):
        project_in(0, b)
        return carry

    lax.fori_loop(0, nseq, in_body, 0)

    for g in range(len(scrs)):
        ar = [are_ref[g * grp + i] for i in range(grp)]
        ai = [aim_ref[g * grp + i] for i in range(grp)]

        def body(it, carry, g=g, ar=ar, ai=ai):
            hr, hi = list(carry[0]), list(carry[1])
            for dt in range(steps):
                sel = pl.ds(it * steps + dt, nseq, stride=SSM_PITCH)
                for i in range(grp):
                    r = ar[i] * hr[i] - ai[i] * hi[i] + scrs[g][i, sel, :]
                    im = ar[i] * hi[i] + ai[i] * hr[i] + scrs[g][grp + i, sel, :]
                    scrs[g][i, sel, :] = r
                    scrs[g][grp + i, sel, :] = im
                    hr[i], hi[i] = r, im
            if g + 1 < len(scrs):
                project_in(g + 1, it)
            if g > 0:
                project_out(g - 1, it)
            return tuple(hr), tuple(hi)

        init = (tuple(hst[g * grp + i] for i in range(grp)), tuple(hst[n_slab + g * grp + i] for i in range(grp)))
        hr, hi = lax.fori_loop(0, nseq, body, init)
        for i in range(grp):
            hst[g * grp + i] = hr[i]
            hst[n_slab + g * grp + i] = hi[i]

    def out_body(b, carry):
        project_out(len(scrs) - 1, b)
        return carry

    lax.fori_loop(0, nseq, out_body, 0)

    u2 = u_ref[...].reshape(nseq * tt, D_SSM)
    o_ref[...] = _s5_epilogue(y_scr[...], u2, d_ref, wglu_ref, bglu_ref, g_ref).reshape(nseq, tt, D_SSM)

    @pl.when(ti == pl.num_programs(1) - 1)
    def _():
        for s in range(n_slab):
            hre_ref[:, s * LANES:(s + 1) * LANES] = hst[s]
            him_ref[:, s * LANES:(s + 1) * LANES] = hst[n_slab + s]


def _s5_prompt(u, a_re, a_im, b_mat, c_mat, d, w_glu_bf16, b_glu, g_out):
    nb, s, _ = u.shape
    nseq = SUBLANES
    assert nb % nseq == 0 and s % SSM_TT == 0 and SSM_TT % nseq == 0
    n_state = a_re.shape[0]
    n_slab = n_state // LANES
    n_groups = D_SSM // LANES
    group_scr = pltpu.VMEM((2 * n_slab // n_groups, nseq * SSM_PITCH, LANES), F32)
    bcast = lambda a: jnp.broadcast_to(a.reshape(n_slab, 1, LANES), (n_slab, nseq, LANES))
    const = lambda a: pl.BlockSpec(a.shape, lambda b, i: (0,) * a.ndim)
    args = (u, bcast(a_re), bcast(a_im), b_mat, c_mat, d.reshape(1, D_SSM), w_glu_bf16,
            b_glu.reshape(1, D_SSM), g_out.reshape(1, D_SSM))
    st = jax.ShapeDtypeStruct((nb, n_state), F32)
    return pl.pallas_call(
        _s5_prompt_kernel,
        grid=(nb // nseq, s // SSM_TT),
        in_specs=[pl.BlockSpec((nseq, SSM_TT, D_SSM), lambda b, i: (b, i, 0))] + [const(a) for a in args[1:]],
        out_specs=[pl.BlockSpec((nseq, SSM_TT, D_SSM), lambda b, i: (b, i, 0)),
                   pl.BlockSpec((nseq, n_state), lambda b, i: (b, 0)),
                   pl.BlockSpec((nseq, n_state), lambda b, i: (b, 0))],
        out_shape=[jax.ShapeDtypeStruct((nb, s, D_SSM), F32), st, st],
        scratch_shapes=[group_scr] * n_groups + [pltpu.VMEM((nseq * SSM_TT, D_SSM), F32),
                                                 pltpu.VMEM((2 * n_slab, nseq, LANES), F32)],
        compiler_params=_cparams(("parallel", "arbitrary")),
    )(*args)


def _s5_sample_kernel(u_ref, h0re_ref, h0im_ref, are_ref, aim_ref, b_ref, c_ref, d_ref, wglu_ref, bglu_ref, g_ref,
                      o_ref, hre_ref, him_ref):
    t_new = u_ref.shape[0]
    hre = h0re_ref[...]
    him = h0im_ref[...]
    are, aim = are_ref[...], aim_ref[...]
    half = 4 * LANES
    for t in range(t_new):
        u = u_ref[t]
        ub = u.astype(BF16)
        bus = [jnp.dot(ub[:, kc * LANES:(kc + 1) * LANES], b_ref[kc], preferred_element_type=F32) for kc in range(4)]
        bre = jnp.concatenate([bu[:, :half] for bu in bus], axis=1)
        bim = jnp.concatenate([bu[:, half:] for bu in bus], axis=1)
        hre, him = are * hre - aim * him + bre, are * him + aim * hre + bim
        ys = []
        for kc in range(4):
            lhs = jnp.concatenate([hre[:, kc * half:(kc + 1) * half], him[:, kc * half:(kc + 1) * half]], axis=1)
            ys.append(jnp.dot(lhs.astype(BF16), c_ref[kc], preferred_element_type=F32))
        y = jnp.concatenate(ys, axis=1)
        o_ref[t] = _s5_epilogue(y, u, d_ref, wglu_ref, bglu_ref, g_ref)
    hre_ref[...] = hre
    him_ref[...] = him


def _s5_sample(u_tm, h0_re, h0_im, a_re, a_im, b_mat, c_mat, d, w_glu_bf16, b_glu, g_out):
    t_new, nb, _ = u_tm.shape
    n_state = a_re.shape[0]
    tb = 64
    assert nb % tb == 0
    const = lambda a: pl.BlockSpec(a.shape, lambda b: (0,) * a.ndim)
    args = (u_tm, h0_re, h0_im, a_re.reshape(1, n_state), a_im.reshape(1, n_state), b_mat, c_mat,
            d.reshape(1, D_SSM), w_glu_bf16, b_glu.reshape(1, D_SSM), g_out.reshape(1, D_SSM))
    st_spec = pl.BlockSpec((tb, n_state), lambda b: (b, 0))
    st = jax.ShapeDtypeStruct((nb, n_state), F32)
    return pl.pallas_call(
        _s5_sample_kernel,
        grid=(nb // tb,),
        in_specs=[pl.BlockSpec((t_new, tb, D_SSM), lambda b: (0, b, 0)), st_spec, st_spec]
                 + [const(a) for a in args[3:]],
        out_specs=[pl.BlockSpec((t_new, tb, D_SSM), lambda b: (0, b, 0)), st_spec, st_spec],
        out_shape=[jax.ShapeDtypeStruct((t_new, nb, D_SSM), F32), st, st],
        compiler_params=_cparams(("parallel",)),
    )(*args)


def _store_packed(ref, val):
    half = val.shape[1] // 2
    bits = pltpu.bitcast(val.astype(BF16).astype(F32), jnp.uint32)
    words = bits[:, :half] | lax.shift_right_logical(bits[:, half:], jnp.uint32(16))
    for c in range(PACK_CHUNKS):
        ref[c] = words[:, c * LANES:(c + 1) * LANES]


def _load_packed(ref):
    hi, lo = [], []
    for c in range(PACK_CHUNKS):
        w = ref[c]
        hi.append(pltpu.bitcast(w & jnp.uint32(0xFFFF0000), F32))
        lo.append(pltpu.bitcast(lax.shift_left(w, jnp.uint32(16)), F32))
    return jnp.concatenate(hi + lo, axis=1)


def _outproj_kernel(x_ref, oa_ref, os_ref, gate_ref, sh_ref, sc_ref, ga_ref, gf_ref, w_ref, wr_hi_ref, wr_lo_ref,
                    br_ref, x1_ref, h_ref, lg_ref):
    oa = oa_ref[...]
    ms = jnp.mean(oa * oa, axis=-1, keepdims=True)
    na = oa * lax.rsqrt(ms + EPS) * ga_ref[...]
    merged = jnp.concatenate([na, os_ref[...]], axis=-1).astype(BF16)
    x1 = x_ref[...] + gate_ref[...] * jnp.dot(merged, w_ref[...], preferred_element_type=F32)
    x1_ref[...] = x1
    ms = jnp.mean(x1 * x1, axis=-1, keepdims=True)
    h = x1 * lax.rsqrt(ms + EPS) * gf_ref[...]
    h = h * (1.0 + sc_ref[...]) + sh_ref[...]
    h_hi = h.astype(BF16)
    _store_packed(h_ref, h)
    h_lo = (h - h_hi.astype(F32)).astype(BF16)
    lg = jnp.dot(h_hi, wr_hi_ref[...], preferred_element_type=F32)
    lg = lg + jnp.dot(h_hi, wr_lo_ref[...], preferred_element_type=F32)
    lg = lg + jnp.dot(h_lo, wr_hi_ref[...], preferred_element_type=F32)
    lg_ref[...] = lg + br_ref[...]


def _outproj(x, o_attn, o_ssm, gate, shift, scale, g_attn, g_ffn, w_out_bf16, w_router, b_router, tm):
    nb, s, d = x.shape
    row = lambda n: pl.BlockSpec((None, tm, n), lambda b, i: (b, i, 0))
    const = lambda a: pl.BlockSpec(a.shape, lambda b, i: (0,) * a.ndim)
    wr_hi = w_router.astype(BF16)
    wr_lo = (w_router - wr_hi.astype(F32)).astype(BF16)
    consts = (g_attn.reshape(1, D_ATTN), g_ffn.reshape(1, d), w_out_bf16, wr_hi, wr_lo,
              b_router.reshape(1, ROUTER_COLS))
    return pl.pallas_call(
        _outproj_kernel,
        grid=(nb, s // tm),
        in_specs=[row(d), row(D_ATTN), row(D_SSM), _mod_spec(gate, tm), _mod_spec(shift, tm), _mod_spec(scale, tm)]
                 + [const(a) for a in consts],
        out_specs=[row(d), pl.BlockSpec((PACK_CHUNKS, tm, LANES), lambda b, i: (0, b * (s // tm) + i, 0)),
                   row(ROUTER_COLS)],
        out_shape=[jax.ShapeDtypeStruct((nb, s, d), F32),
                   jax.ShapeDtypeStruct((PACK_CHUNKS, nb * s, LANES), jnp.uint32),
                   jax.ShapeDtypeStruct((nb, s, ROUTER_COLS), F32)],
        compiler_params=_cparams(("parallel", "parallel")),
    )(x, o_attn, o_ssm, gate, shift, scale, *consts)


def _expert_kernel(blk_e_ref, nvalid_ref, x_ref, wg_ref, wu_ref, wd_ref, y_ref, wg_s, wu_s, wd_s):
    i = pl.program_id(0)

    @pl.when((i == 0) | (blk_e_ref[i] != blk_e_ref[jnp.maximum(i - 1, 0)]))
    def _():
        wg_s[...] = wg_ref[...].astype(BF16)
        wu_s[...] = wu_ref[...].astype(BF16)
        wd_s[...] = wd_ref[...].astype(BF16)

    @pl.when(i < nvalid_ref[0])
    def _():
        x = _load_packed(x_ref).astype(BF16)
        a = jnp.dot(x, wg_s[...], preferred_element_type=F32)
        b = jnp.dot(x, wu_s[...], preferred_element_type=F32)
        hid = (a * jax.nn.sigmoid(a) * b).astype(BF16)
        _store_packed(y_ref, jnp.dot(hid, wd_s[...], preferred_element_type=F32))

    @pl.when(i >= nvalid_ref[0])
    def _():
        y_ref[...] = jnp.zeros_like(y_ref)


def _experts(xs, blk_e, nvalid, w_gate, w_up, w_down):
    nblk = xs.shape[1] // MOE_TILE
    d = w_gate.shape[1]
    wspec = lambda shp: pl.BlockSpec((None,) + shp, lambda i, be, nv: (be[i], 0, 0))
    tile = pl.BlockSpec((PACK_CHUNKS, MOE_TILE, LANES), lambda i, be, nv: (0, i, 0))
    return pl.pallas_call(
        _expert_kernel,
        grid_spec=pltpu.PrefetchScalarGridSpec(
            num_scalar_prefetch=2,
            grid=(nblk,),
            in_specs=[tile, wspec((d, D_EXPERT)), wspec((d, D_EXPERT)), wspec((D_EXPERT, d))],
            out_specs=tile,
            scratch_shapes=[pltpu.VMEM((d, D_EXPERT), BF16), pltpu.VMEM((d, D_EXPERT), BF16),
                            pltpu.VMEM((D_EXPERT, d), BF16)],
        ),
        out_shape=jax.ShapeDtypeStruct(xs.shape, jnp.uint32),
        compiler_params=_cparams(("arbitrary",)),
    )(blk_e, nvalid, xs, w_gate, w_up, w_down)


def _combine_kernel(x1_ref, y0_ref, y1_ref, info_ref, gate_ref, gfin_ref, o_ref):
    info = pltpu.bitcast(info_ref[...], F32)
    g0 = info[:, INFO_GATE:INFO_GATE + 1]
    g1 = info[:, INFO_GATE + 1:INFO_GATE + 2]
    moe = g0 * _load_packed(y0_ref) + g1 * _load_packed(y1_ref)
    x2 = x1_ref[...] + gate_ref[...] * moe
    ms = jnp.mean(x2 * x2, axis=-1, keepdims=True)
    o_ref[...] = x2 * lax.rsqrt(ms + EPS) * gfin_ref[...]


def _combine(x1, y01, info, gate, g_final, tm):
    nb, s, d = x1.shape
    row = lambda n: pl.BlockSpec((None, tm, n), lambda b, i: (b, i, 0))
    packed = lambda k: pl.BlockSpec((None, PACK_CHUNKS, tm, LANES), lambda b, i: (k, 0, b * (s // tm) + i, 0))
    return pl.pallas_call(
        _combine_kernel,
        grid=(nb, s // tm),
        in_specs=[row(d), packed(0), packed(1), row(ROUTER_COLS), _mod_spec(gate, tm),
                  pl.BlockSpec((1, d), lambda b, i: (0, 0))],
        out_specs=row(d),
        out_shape=jax.ShapeDtypeStruct((nb, s, d), F32),
        compiler_params=_cparams(("parallel", "parallel")),
    )(x1, y01, y01, info, gate, g_final.reshape(1, d))


def _sc_window(n_rows):
    assert n_rows % SC_WINDOW == 0
    return SC_WINDOW


def _sc_gather_rows(table, idx):
    n = idx.shape[0]
    w = _sc_window(n)
    mesh = plsc.VectorSubcoreMesh(core_axis_name="core", subcore_axis_name="subcore")

    @functools.partial(pl.kernel, out_type=jax.ShapeDtypeStruct((n, LANES), table.dtype), mesh=mesh)
    def gather_kernel(x_hbm, i_hbm, o_hbm):
        def body(i_vmem, o_vmem):
            pltpu.sync_copy(x_hbm.at[i_vmem.at[0]], o_vmem)

        pltpu.emit_pipeline(
            body, grid=(n // w,),
            in_specs=[pl.BlockSpec((1, w), lambda i: (0, i))],
            out_specs=[pl.BlockSpec((w, LANES), lambda i: (i, 0))],
            core_axis_name=("core", "subcore"), dimension_semantics=(pltpu.PARALLEL,),
        )(i_hbm, o_hbm)

    return gather_kernel(table, idx.reshape(1, n))


def _sc_scatter_rows(rows, idx0, idx1, n_out):
    n = rows.shape[0]
    w = _sc_window(n)
    mesh = plsc.VectorSubcoreMesh(core_axis_name="core", subcore_axis_name="subcore")

    @functools.partial(pl.kernel, out_type=jax.ShapeDtypeStruct((n_out, LANES), rows.dtype), mesh=mesh)
    def scatter_kernel(x_hbm, i0_hbm, i1_hbm, o_hbm):
        def body(x_vmem, i0_vmem, i1_vmem):
            pltpu.sync_copy(x_vmem, o_hbm.at[i0_vmem.at[0]])
            pltpu.sync_copy(x_vmem, o_hbm.at[i1_vmem.at[0]])

        pltpu.emit_pipeline(
            body, grid=(n // w,),
            in_specs=[pl.BlockSpec((w, LANES), lambda i: (i, 0)),
                      pl.BlockSpec((1, w), lambda i: (0, i)),
                      pl.BlockSpec((1, w), lambda i: (0, i))],
            out_specs=[],
            core_axis_name=("core", "subcore"), dimension_semantics=(pltpu.PARALLEL,),
        )(x_hbm, i0_hbm, i1_hbm)

    return scatter_kernel(rows, idx0.reshape(1, n), idx1.reshape(1, n))


def _route_kernel(lg_ref, tri_ref, info_ref, rec_ref, cnt_ref, carry):
    i = pl.program_id(0)

    @pl.when(i == 0)
    def _():
        carry[...] = jnp.zeros_like(carry)

    lg = lg_ref[...]
    lane = lax.broadcasted_iota(jnp.int32, lg.shape, 1)
    lane_f = lane.astype(F32)
    none = float(ROUTER_COLS)
    ninf = float("-inf")
    first = lambda cond: jnp.min(jnp.where(cond, lane_f, none), axis=-1, keepdims=True)

    is_c = lane < N_EXPERT_GROUPS
    lc = jnp.where(is_c, lg, ninf)
    mc = jnp.max(lc, axis=-1, keepdims=True)
    p_grp = 1.0 / jnp.sum(jnp.exp(lc - mc), axis=-1, keepdims=True)
    grp = first(lc == mc)
    fine = lane - N_EXPERT_GROUPS
    fine_grp = lax.shift_right_arithmetic(fine, jnp.int32(EXPERTS_PER_GROUP.bit_length() - 1))
    in_grp = (fine >= 0) & (fine < N_EXPERTS) & (fine_grp.astype(F32) == grp)
    lf = jnp.where(in_grp, lg, ninf)
    v1 = jnp.max(lf, axis=-1, keepdims=True)
    i1 = first(lf == v1)
    lf2 = jnp.where(lane_f == i1, ninf, lf)
    v2 = jnp.max(lf2, axis=-1, keepdims=True)
    i2 = first(lf2 == v2)
    b = jnp.exp(v2 - v1)
    g0 = p_grp / (1.0 + b)
    g1 = p_grp * b / (1.0 + b)

    hit1 = lane_f == i1
    hit2 = lane_f == i2
    onehot = jnp.where(hit1 | hit2, 1.0, 0.0)
    before = jnp.dot(tri_ref[...], onehot.astype(BF16), preferred_element_type=F32) + carry[0:1, :]
    r1 = jnp.sum(jnp.where(hit1, before, 0.0), axis=-1, keepdims=True)
    r2 = jnp.sum(jnp.where(hit2, before, 0.0), axis=-1, keepdims=True)
    carry[...] = carry[...] + jnp.sum(onehot, axis=0, keepdims=True)

    as_int = lambda v: jnp.broadcast_to(v, lg.shape).astype(jnp.int32)
    as_bits = lambda v: pltpu.bitcast(jnp.broadcast_to(v, lg.shape), jnp.int32)
    info = jnp.zeros(lg.shape, jnp.int32)
    fields = ((INFO_EXPERT, as_int(i1 - N_EXPERT_GROUPS)), (INFO_EXPERT + 1, as_int(i2 - N_EXPERT_GROUPS)),
              (INFO_RANK, as_int(r1)), (INFO_RANK + 1, as_int(r2)), (INFO_GATE, as_bits(g0)),
              (INFO_GATE + 1, as_bits(g1)))
    for col, val in fields:
        info = jnp.where(lane == col, val, info)
    info_ref[...] = info
    rec_ref[...] = pltpu.bitcast(pltpu.bitcast(info, F32).T[:SUBLANES], jnp.int32)

    @pl.when(i == pl.num_programs(0) - 1)
    def _():
        cnt_ref[...] = carry[...]


def _route(logits):
    t = logits.shape[0]
    tr = min(t, 512)
    assert t % tr == 0
    tri = jnp.asarray(np.tril(np.ones((tr, tr), np.float32), -1), BF16)
    info, rec, cnt = pl.pallas_call(
        _route_kernel,
        grid=(t // tr,),
        in_specs=[pl.BlockSpec((tr, ROUTER_COLS), lambda i: (i, 0)), pl.BlockSpec((tr, tr), lambda i: (0, 0))],
        out_specs=[pl.BlockSpec((tr, ROUTER_COLS), lambda i: (i, 0)),
                   pl.BlockSpec((SUBLANES, tr), lambda i: (0, i)),
                   pl.BlockSpec((SUBLANES, ROUTER_COLS), lambda i: (0, 0))],
        out_shape=[jax.ShapeDtypeStruct((t, ROUTER_COLS), jnp.int32),
                   jax.ShapeDtypeStruct((SUBLANES, t), jnp.int32),
                   jax.ShapeDtypeStruct((SUBLANES, ROUTER_COLS), F32)],
        scratch_shapes=[pltpu.VMEM((SUBLANES, ROUTER_COLS), F32)],
        compiler_params=_cparams(("arbitrary",)),
    )(logits, tri)
    counts = cnt[0, N_EXPERT_GROUPS:N_EXPERT_GROUPS + N_EXPERTS].astype(jnp.int32)
    return info, rec, counts


def _moe(h_packed, logits, w_gate, w_up, w_down):
    t = logits.shape[0]
    info, rec, counts = _route(logits)
    padded = (counts + MOE_TILE - 1) // MOE_TILE * MOE_TILE
    pend = jnp.cumsum(padded)
    pstart = pend - padded
    nblk = -(-2 * t // MOE_TILE) + N_EXPERTS
    n_slots = nblk * MOE_TILE
    blk_start = jnp.arange(nblk, dtype=jnp.int32) * MOE_TILE
    blk_e = jnp.minimum(jnp.sum(blk_start[:, None] >= pend[None, :], axis=1), N_EXPERTS - 1).astype(jnp.int32)
    nvalid = (pend[-1] // MOE_TILE).astype(jnp.int32).reshape(1)
    chunk_base = (jnp.arange(PACK_CHUNKS, dtype=jnp.int32) * n_slots)[:, None]
    rows = lambda k: (chunk_base + (jnp.take(pstart, rec[INFO_EXPERT + k]).astype(jnp.int32)
                                    + rec[INFO_RANK + k])[None, :]).reshape(-1)
    idx0, idx1 = rows(0), rows(1)
    xs = _sc_scatter_rows(h_packed.reshape(PACK_CHUNKS * t, LANES), idx0, idx1, PACK_CHUNKS * n_slots)
    yield
    ys = _experts(xs.reshape(PACK_CHUNKS, n_slots, LANES), blk_e, nvalid, w_gate, w_up, w_down)
    y01 = _sc_gather_rows(ys.reshape(PACK_CHUNKS * n_slots, LANES), jnp.concatenate([idx0, idx1]))
    yield
    yield y01.reshape(2, PACK_CHUNKS, t, LANES), info


def _layer(x, mods_mix, mods_ffn, attn_fn, s5_fn, wts):
    nb, s, d = x.shape
    tm = min(s, PROJ_ROWS)
    tm_combine = min(s, COMBINE_ROWS)
    assert s % tm == 0 and s % tm_combine == 0
    q, k, v, u = _inproj(x, mods_mix[0], mods_mix[1], wts['g_mix'], wts['w_in'], tm)
    o_attn = attn_fn(q, k, v)
    o_ssm, h_re, h_im = s5_fn(u)
    x1, h2, logits = _outproj(x, o_attn, o_ssm, mods_mix[2], mods_ffn[0], mods_ffn[1], wts['g_attn_out'],
                              wts['g_ffn'], wts['w_out'], wts['w_router'], wts['b_router'], tm)
    moe = _moe(h2, logits.reshape(nb * s, ROUTER_COLS), wts['w_gate'], wts['w_up'], wts['w_down'])
    yield next(moe)
    yield next(moe)
    y01, info = next(moe)
    y = _combine(x1, y01, info.reshape(nb, s, ROUTER_COLS), mods_ffn[2], wts['g_final'], tm_combine)
    yield y, k, v, h_re, h_im


def kernel(x_prompt, x_sample, cache_k_win, cache_v_win, state_ssm_re, state_ssm_im, c_prompt, c_sample, g_mix, w_ada_mix, b_ada_mix, w_in, w_out, g_attn_out, g_ssm_out, ssm_lambda_re, ssm_lambda_im, ssm_log_dt, ssm_b_re, ssm_b_im, ssm_c_re, ssm_c_im, ssm_d, w_glu, b_glu, g_ffn, w_ada_ffn, b_ada_ffn, w_router_coarse, b_router_coarse, w_router_fine, b_router_fine, w_expert_gate, w_expert_up, w_expert_down, g_final):
    depth = g_mix.shape[0]
    assert depth == 1, "single-layer step"
    l = 0
    bp, sp, d = x_prompt.shape
    bs, ts, _ = x_sample.shape
    keep = min(max(w for w, _ in DILATED_PATTERNS), sp)

    c_all = jnp.concatenate([c_prompt, c_sample], axis=0).astype(F32)
    m_mix = _adaln(c_all, w_ada_mix[l], b_ada_mix[l])
    m_ffn = _adaln(c_all, w_ada_ffn[l], b_ada_ffn[l])

    def mods(m, lo, hi, per_token):
        parts = jnp.split(m[lo:hi], 3, axis=-1)
        if per_token:
            return tuple(jnp.repeat(p, ts, axis=0)[None] for p in parts)
        return tuple(p[:, None, :] for p in parts)

    pad = ROUTER_COLS - N_EXPERT_GROUPS - N_EXPERTS
    w_router = jnp.concatenate([w_router_coarse[l], w_router_fine[l], jnp.zeros((d, pad), F32)], axis=1)
    b_router = jnp.concatenate([b_router_coarse[l], b_router_fine[l], jnp.zeros((pad,), F32)])
    wts = {
        'g_mix': g_mix[l], 'w_in': w_in[l].astype(BF16), 'w_out': w_out[l].astype(BF16),
        'g_attn_out': g_attn_out[l], 'g_ffn': g_ffn[l], 'w_router': w_router, 'b_router': b_router,
        'w_gate': w_expert_gate[l], 'w_up': w_expert_up[l], 'w_down': w_expert_down[l], 'g_final': g_final,
    }
    a_re, a_im, b_mat, c_mat = _s5_params(ssm_lambda_re[l], ssm_lambda_im[l], ssm_log_dt[l], ssm_b_re[l],
                                          ssm_b_im[l], ssm_c_re[l], ssm_c_im[l])
    s5_w = (a_re, a_im, b_mat, c_mat, ssm_d[l].reshape(-1), w_glu[l].astype(BF16), b_glu[l], g_ssm_out[l])

    prompt = _layer(
        x_prompt, mods(m_mix, 0, bp, False), mods(m_ffn, 0, bp, False), _attn_prompt,
        lambda u: _s5_prompt(u, *s5_w), wts)

    n_state = SSM_GROUPS * SSM_STATE
    ck = cache_k_win[l]
    cv = cache_v_win[l]

    def attn_s(q, k, v):
        r = lambda a: a.reshape(bs, ts, D_ATTN)
        return _attn_sample(r(q), r(k), r(v), ck, cv).reshape(1, bs * ts, D_ATTN)

    def s5_s(u):
        u_tm = u.reshape(bs, ts, D_SSM).transpose(1, 0, 2)
        o, hr, hi = _s5_sample(u_tm, state_ssm_re[l].reshape(bs, n_state), state_ssm_im[l].reshape(bs, n_state),
                               *s5_w)
        return o.transpose(1, 0, 2).reshape(1, bs * ts, D_SSM), hr, hi

    sample = _layer(
        x_sample.reshape(1, bs * ts, d), mods(m_mix, bp, bp + bs, True), mods(m_ffn, bp, bp + bs, True),
        attn_s, s5_s, wts)

    next(prompt), next(sample)
    next(prompt), next(sample)
    yp, kp, vp, hrp, hip = next(prompt)
    ys, ks, vs, hrs, his = next(sample)

    heads = lambda a, b, s: a.reshape(1, b, s, N_HEADS, HEAD_DIM)
    state = lambda a, b: a.reshape(1, b, SSM_GROUPS, SSM_STATE)
    return (yp, ys.reshape(bs, ts, d),
            heads(kp[:, sp - keep:], bp, keep), heads(vp[:, sp - keep:], bp, keep), state(hrp, bp), state(hip, bp),
            heads(ks, bs, ts), heads(vs, bs, ts), state(hrs, bs), state(his, bs))
```

```python
import functools
import math

import numpy as np
import jax
import jax.numpy as jnp
from jax import lax
from jax.experimental import pallas as pl
from jax.experimental.pallas import tpu as pltpu
from jax.experimental.pallas import tpu_sc as plsc

F32 = jnp.float32
BF16 = jnp.bfloat16
HIGHEST = lax.Precision.HIGHEST

D_MODEL = 1024
D_ATTN = 512
D_SSM = 512
HEAD_DIM = 64
N_HEADS = 8
DILATED_PATTERNS = ((128, 1), (512, 4), (2048, 16))
SSM_GROUP_CH = 16
SSM_GROUPS = 32
SSM_STATE = 64
N_EXPERT_GROUPS = 4
EXPERTS_PER_GROUP = 8
N_EXPERTS = 32
D_EXPERT = 512
D_IN_PROJ = 3 * D_ATTN + D_SSM
EPS = 1e-6

LANES = 128
SUBLANES = 8
VMEM_LIMIT = 48 * 1024 * 1024

ATTN_BLOCK = 128
ATTN_UNROLL = 8
NEG_BIG = -1e30
SSM_TT = 128
SSM_PITCH = SSM_TT + 8
ROUTER_COLS = 128
MOE_TILE = 512
ROW_TILE = 512
SC_WINDOW = 128
PACK_CHUNKS = D_MODEL // 2 // LANES
INFO_EXPERT, INFO_RANK, INFO_GATE = 0, 2, 4


def _cparams(sem, vmem=VMEM_LIMIT):
    return pltpu.CompilerParams(dimension_semantics=sem, vmem_limit_bytes=vmem)


def _adaln_kernel(c_ref, w_ref, b_ref, o_ref):
    c = c_ref[...]
    s = c * jax.nn.sigmoid(c)
    o_ref[...] = jnp.dot(s, w_ref[...], precision=HIGHEST, preferred_element_type=F32) + b_ref[...]


def _adaln(c, w, b):
    r, d = c.shape
    n = w.shape[1]
    tn = 768
    return pl.pallas_call(
        _adaln_kernel,
        grid=(n // tn,),
        in_specs=[pl.BlockSpec((r, d), lambda j: (0, 0)),
                  pl.BlockSpec((d, tn), lambda j: (0, j)),
                  pl.BlockSpec((1, tn), lambda j: (0, j))],
        out_specs=pl.BlockSpec((r, tn), lambda j: (0, j)),
        out_shape=jax.ShapeDtypeStruct((r, n), F32),
        compiler_params=_cparams(("arbitrary",)),
    )(c, w, b.reshape(1, n))


def _inproj_kernel(x_ref, sh_ref, sc_ref, g_ref, w_ref, q_ref, k_ref, v_ref, u_ref):
    x = x_ref[...]
    ms = jnp.mean(x * x, axis=-1, keepdims=True)
    h = x * lax.rsqrt(ms + EPS) * g_ref[...]
    h = h * (1.0 + sc_ref[...]) + sh_ref[...]
    p = jnp.dot(h.astype(BF16), w_ref[...], preferred_element_type=F32)
    q_ref[...] = p[:, :D_ATTN] * (HEAD_DIM ** -0.5)
    k_ref[...] = p[:, D_ATTN:2 * D_ATTN]
    v_ref[...] = p[:, 2 * D_ATTN:3 * D_ATTN]
    u_ref[...] = p[:, 3 * D_ATTN:].astype(u_ref.dtype)


def _mod_spec(mod, tm):
    d = mod.shape[-1]
    if mod.shape[1] == 1:
        return pl.BlockSpec((None, 1, d), lambda b, i: (b, 0, 0))
    return pl.BlockSpec((None, tm, d), lambda b, i: (b, i, 0))


def _inproj(x, shift, scale, g, w_bf16, tm):
    nb, s, d = x.shape
    row = lambda n: pl.BlockSpec((None, tm, n), lambda b, i: (b, i, 0))
    out = jax.ShapeDtypeStruct((nb, s, D_ATTN), F32)
    return pl.pallas_call(
        _inproj_kernel,
        grid=(nb, s // tm),
        in_specs=[row(d), _mod_spec(shift, tm), _mod_spec(scale, tm),
                  pl.BlockSpec((1, d), lambda b, i: (0, 0)),
                  pl.BlockSpec((d, D_IN_PROJ), lambda b, i: (0, 0))],
        out_specs=[row(D_ATTN), row(D_ATTN), row(D_ATTN), row(D_SSM)],
        out_shape=[out, out, out, jax.ShapeDtypeStruct((nb, s, D_SSM), BF16)],
        compiler_params=_cparams(("parallel", "parallel")),
    )(x, shift, scale, g.reshape(1, d), w_bf16)


def _band_bias():
    qi = np.arange(ATTN_BLOCK)[:, None]
    kj = np.arange(ATTN_BLOCK)[None, :]
    cur = kj <= qi
    prev = kj >= qi
    to_bias = lambda m: np.tile(np.where(m, 0.0, NEG_BIG).astype(np.float32), (2, 1))
    return to_bias(np.concatenate([prev, cur], axis=1)), to_bias(cur)


def _attn_prompt_kernel(q_ref, k_ref, v_ref, bias2_ref, bias1_ref, o_ref, acc_scr, m_scr, l_scr):
    s_len = q_ref.shape[0]
    lane = lax.broadcasted_iota(jnp.int32, (1, LANES), 1)
    head0 = lane < HEAD_DIM

    def rows(ref, start, n, d):
        if d == 1:
            return ref[pl.ds(start, n), :]
        return ref[pl.ds(start, n, stride=d), :]

    def tiles(items, first):
        nk = ATTN_BLOCK if first else 2 * ATTN_BLOCK
        bias = bias1_ref[...] if first else bias2_ref[...]
        one = jnp.ones((), BF16)
        loaded = []
        for _, d, qstart in items:
            kstart = qstart if first else qstart - d * ATTN_BLOCK
            loaded.append((rows(q_ref, qstart, ATTN_BLOCK, d), rows(k_ref, kstart, nk, d),
                           rows(v_ref, kstart, nk, d)))
        results = []
        for qrows, krows, vrows in loaded:
            kb = krows.astype(BF16)
            vb = vrows.astype(BF16)
            q2 = jnp.concatenate([jnp.where(head0, qrows, 0.0), jnp.where(head0, 0.0, qrows)], axis=0).astype(BF16)
            s = lax.dot_general(q2, kb, (((1,), (1,)), ((), ())), preferred_element_type=F32) + bias
            m = jnp.max(s, axis=-1, keepdims=True)
            p = jnp.exp(s - m).astype(BF16)
            pv0 = jnp.dot(p[:ATTN_BLOCK], jnp.where(head0, vb, one), preferred_element_type=F32)
            pv1 = jnp.dot(p[ATTN_BLOCK:], jnp.where(head0, one, vb), preferred_element_type=F32)
            results.append((jnp.where(head0, pv0, pv1), jnp.where(head0, m[:ATTN_BLOCK], m[ATTN_BLOCK:]),
                            jnp.where(head0, pv1, pv0)))
        for (br, d, qstart), (acc, m, l) in zip(items, results):
            dst = pl.ds(qstart, ATTN_BLOCK) if d == 1 else pl.ds(qstart, ATTN_BLOCK, stride=d)
            acc_scr[br, dst, :] = acc
            m_scr[br, dst, :] = m
            l_scr[br, dst, :] = l

    def largest_divisor(n):
        return max(f for f in range(1, ATTN_UNROLL + 1) if n % f == 0)

    grouped = []
    for br, (window, d) in enumerate(DILATED_PATTERNS):
        assert window // d == ATTN_BLOCK
        if d <= ATTN_UNROLL:
            grouped += [(br, d, r) for r in range(d)]
        else:
            un = largest_divisor(d)

            def first_body(i, carry, br=br, d=d, un=un):
                tiles([(br, d, i * un + j) for j in range(un)], True)
                return carry

            lax.fori_loop(0, d // un, first_body, 0)
    for i in range(0, len(grouped), ATTN_UNROLL):
        tiles(grouped[i:i + ATTN_UNROLL], True)

    for br, (window, d) in enumerate(DILATED_PATTERNS):
        nblk = s_len // (d * ATTN_BLOCK)
        n_rest = d * (nblk - 1)
        if n_rest:
            un = largest_divisor(n_rest)

            def rest_body(i, carry, br=br, d=d, nblk=nblk, un=un):
                items = []
                for j in range(un):
                    n = i * un + j
                    qstart = n // (nblk - 1) + d * ATTN_BLOCK * (1 + n % (nblk - 1))
                    items.append((br, d, pl.multiple_of(qstart, ATTN_BLOCK) if d == 1 else qstart))
                tiles(items, False)
                return carry

            lax.fori_loop(0, n_rest // un, rest_body, 0)

    ch = 256

    def merge_body(i, carry):
        sl = pl.ds(pl.multiple_of(i * ch, ch), ch)
        ms = [m_scr[br, sl, :] for br in range(3)]
        mx = jnp.maximum(jnp.maximum(ms[0], ms[1]), ms[2])
        num = jnp.zeros((ch, LANES), F32)
        den = jnp.zeros((ch, LANES), F32)
        for br in range(3):
            e = jnp.exp(ms[br] - mx)
            num = num + e * acc_scr[br, sl, :]
            den = den + e * pltpu.roll(l_scr[br, sl, :], HEAD_DIM, 1)
        o_ref[sl, :] = (num / den).astype(o_ref.dtype)
        return carry

    lax.fori_loop(0, s_len // ch, merge_body, 0)


def _attn_prompt(q, k, v):
    nb, s, _ = q.shape
    assert s % (16 * ATTN_BLOCK) == 0
    bias2, bias1 = _band_bias()
    blk = pl.BlockSpec((None, s, LANES), lambda b, p: (b, 0, p))
    const = lambda a: pl.BlockSpec(a.shape, lambda b, p: (0, 0))
    return pl.pallas_call(
        _attn_prompt_kernel,
        grid=(nb, D_ATTN // LANES),
        in_specs=[blk, blk, blk, const(bias2), const(bias1)],
        out_specs=blk,
        out_shape=jax.ShapeDtypeStruct((nb, s, D_ATTN), BF16),
        scratch_shapes=[pltpu.VMEM((3, s, LANES), F32)] * 3,
        compiler_params=_cparams(("parallel", "parallel")),
    )(q, k, v, jnp.asarray(bias2), jnp.asarray(bias1))


def _sample_bias(t_new, w_buf):
    n_new = -(-t_new // LANES) * LANES
    t = np.arange(t_new)[:, None]
    idx = np.arange(w_buf + n_new)[None, :]
    dist = w_buf + t - idx
    mult = np.zeros(dist.shape, np.int64)
    for window, d in DILATED_PATTERNS:
        mult += ((dist >= 0) & (dist % d == 0) & (dist <= window)).astype(np.int64)
    bias = np.where(mult > 0, np.log(np.maximum(mult, 1)), NEG_BIG).astype(np.float32)
    bias = np.repeat(bias, N_HEADS, axis=0)
    return bias[:, :w_buf], bias[:, w_buf:]


def _attn_sample_kernel(q_ref, kn_ref, vn_ref, kt_ref, vt_ref, bias_ref, biasn_ref, o_ref):
    t_new = q_ref.shape[0]
    n_new = biasn_ref.shape[1]
    row_head = lax.broadcasted_iota(jnp.int32, (t_new * N_HEADS, D_ATTN), 0) & (N_HEADS - 1)
    col_head = lax.broadcasted_iota(jnp.int32, (t_new * N_HEADS, D_ATTN), 1) >> (HEAD_DIM.bit_length() - 1)
    own = row_head == col_head
    q = q_ref[...]
    q_rows = jnp.concatenate([jnp.broadcast_to(q[t:t + 1], (N_HEADS, D_ATTN)) for t in range(t_new)], axis=0)
    q_bd = jnp.where(own, q_rows, 0.0).astype(BF16)
    pad = jnp.zeros((n_new - t_new, D_ATTN), F32)
    kn = jnp.concatenate([kn_ref[...], pad], axis=0).astype(BF16)
    vn = jnp.concatenate([vn_ref[...], pad], axis=0).astype(BF16)
    nt = (((1,), (1,)), ((), ()))
    s_c = jnp.dot(q_bd, kt_ref[...].astype(BF16), preferred_element_type=F32) + bias_ref[...]
    s_n = lax.dot_general(q_bd, kn, nt, preferred_element_type=F32) + biasn_ref[...]
    m = jnp.maximum(jnp.max(s_c, axis=-1, keepdims=True), jnp.max(s_n, axis=-1, keepdims=True))
    p_c = jnp.exp(s_c - m)
    p_n = jnp.exp(s_n - m)
    l = jnp.sum(p_c, axis=-1, keepdims=True) + jnp.sum(p_n, axis=-1, keepdims=True)
    o = lax.dot_general(p_c.astype(BF16), vt_ref[...].astype(BF16), nt, preferred_element_type=F32)
    o = (o + jnp.dot(p_n.astype(BF16), vn, preferred_element_type=F32)) / l
    o = jnp.where(own, o, 0.0)
    o_ref[...] = jnp.concatenate(
        [jnp.sum(o[t * N_HEADS:(t + 1) * N_HEADS], axis=0, keepdims=True) for t in range(t_new)],
        axis=0).astype(o_ref.dtype)


def _attn_sample(q, k_new, v_new, cache_k, cache_v):
    nb, t_new, _ = q.shape
    w_buf = cache_k.shape[1]
    bias, bias_new = _sample_bias(t_new, w_buf)
    as_stored = lambda a: a.transpose(0, 2, 3, 1).reshape(nb, D_ATTN, w_buf)
    new = pl.BlockSpec((None, t_new, D_ATTN), lambda b: (b, 0, 0))
    buf = pl.BlockSpec((None, D_ATTN, w_buf), lambda b: (b, 0, 0))
    const = lambda a: pl.BlockSpec(a.shape, lambda b: (0, 0))
    return pl.pallas_call(
        _attn_sample_kernel,
        grid=(nb,),
        in_specs=[new, new, new, buf, buf, const(bias), const(bias_new)],
        out_specs=new,
        out_shape=jax.ShapeDtypeStruct((nb, t_new, D_ATTN), BF16),
        compiler_params=_cparams(("parallel",)),
    )(q, k_new, v_new, as_stored(cache_k), as_stored(cache_v), jnp.asarray(bias), jnp.asarray(bias_new))


def _s5_params(lam_re, lam_im, log_dt, b_re, b_im, c_re, c_im):
    f32 = F32
    dt = jnp.exp(log_dt.astype(f32))[:, None]
    lr, li = lam_re.astype(f32), lam_im.astype(f32)
    ea = jnp.exp(lr * dt)
    a_re, a_im = ea * jnp.cos(li * dt), ea * jnp.sin(li * dt)
    den = lr * lr + li * li
    co_re = ((a_re - 1.0) * lr + a_im * li) / den
    co_im = (a_im * lr - (a_re - 1.0) * li) / den
    bb_re = co_re[..., None] * b_re - co_im[..., None] * b_im
    bb_im = co_re[..., None] * b_im + co_im[..., None] * b_re
    eye = jnp.eye(8, dtype=f32)

    def b_blocks(bb):
        t = bb.reshape(4, 8, SSM_STATE, SSM_GROUP_CH)
        return jnp.einsum('ab,kapc->kacbp', eye, t).reshape(4, 8 * SSM_GROUP_CH, 8 * SSM_STATE)

    def c_blocks(cc):
        t = cc.reshape(4, 8, SSM_GROUP_CH, SSM_STATE)
        return jnp.einsum('ab,kacp->kbpac', eye, t).reshape(4, 8 * SSM_STATE, 8 * SSM_GROUP_CH)

    b_mat = jnp.concatenate([b_blocks(bb_re), b_blocks(bb_im)], axis=2).astype(BF16)
    c_mat = jnp.concatenate([c_blocks(c_re.astype(f32)), -c_blocks(c_im.astype(f32))], axis=1).astype(BF16)
    return a_re.reshape(-1), a_im.reshape(-1), b_mat, c_mat


def _gelu_tanh(y):
    return 0.5 * y * (1.0 + jnp.tanh(math.sqrt(2.0 / math.pi) * (y + 0.044715 * (y * y * y))))


def _s5_epilogue(y, u, d_ref, wglu_ref, bglu_ref, g_ref):
    y = y + d_ref[...] * u.astype(F32)
    z = _gelu_tanh(y)
    gate = jnp.dot(z.astype(BF16), wglu_ref[...], preferred_element_type=F32) + bglu_ref[...]
    out = z * jax.nn.sigmoid(gate)
    ms = jnp.mean(out * out, axis=-1, keepdims=True)
    return out * lax.rsqrt(ms + EPS) * g_ref[...]


def _s5_prompt_kernel(u_ref, are_ref, aim_ref, b_ref, c_ref, d_ref, wglu_ref, bglu_ref, g_ref,
                      o_ref, hre_ref, him_ref, scr, hst):
    nseq, tt, _ = u_ref.shape
    n_slab = D_SSM * SSM_STATE // SSM_GROUP_CH // LANES
    ti = pl.program_id(1)

    @pl.when(ti == 0)
    def _():
        hst[...] = jnp.zeros_like(hst)

    ub = u_ref[...].reshape(nseq * tt, D_SSM)
    for kc in range(4):
        bu = jnp.dot(ub[:, kc * LANES:(kc + 1) * LANES], b_ref[kc], preferred_element_type=F32)
        for part in range(2):
            for j in range(4):
                col = part * 512 + j * LANES
                for b in range(nseq):
                    scr[part * n_slab + 4 * kc + j, b * SSM_PITCH:b * SSM_PITCH + tt, :] = (
                        bu[b * tt:(b + 1) * tt, col:col + LANES])

    grp = 4
    for sg in range(n_slab // grp):
        slabs = [sg * grp + i for i in range(grp)]
        ar = [are_ref[s] for s in slabs]
        ai = [aim_ref[s] for s in slabs]

        def step(t, carry, slabs=slabs, ar=ar, ai=ai):
            hr, hi = carry
            nhr, nhi = [], []
            for i, s in enumerate(slabs):
                sel = pl.ds(t, nseq, stride=SSM_PITCH)
                br = scr[s, sel, :]
                bi = scr[n_slab + s, sel, :]
                r = ar[i] * hr[i] - ai[i] * hi[i] + br
                im = ar[i] * hi[i] + ai[i] * hr[i] + bi
                scr[s, sel, :] = r
                scr[n_slab + s, sel, :] = im
                nhr.append(r)
                nhi.append(im)
            return tuple(nhr), tuple(nhi)

        init = (tuple(hst[s] for s in slabs), tuple(hst[n_slab + s] for s in slabs))
        hr, hi = lax.fori_loop(0, tt, step, init, unroll=4)
        for i, s in enumerate(slabs):
            hst[s] = hr[i]
            hst[n_slab + s] = hi[i]

    ys = []
    for kc in range(4):
        slabs = [4 * kc + j for j in range(4)] + [n_slab + 4 * kc + j for j in range(4)]
        lhs = jnp.concatenate(
            [jnp.concatenate([scr[s, b * SSM_PITCH:b * SSM_PITCH + tt, :].astype(BF16) for s in slabs], axis=1)
             for b in range(nseq)], axis=0)
        ys.append(jnp.dot(lhs, c_ref[kc], preferred_element_type=F32))
    y = jnp.concatenate(ys, axis=1)
    out = _s5_epilogue(y, ub, d_ref, wglu_ref, bglu_ref, g_ref)
    o_ref[...] = out.reshape(nseq, tt, D_SSM).astype(o_ref.dtype)

    @pl.when(ti == pl.num_programs(1) - 1)
    def _():
        for s in range(n_slab):
            hre_ref[:, s * LANES:(s + 1) * LANES] = hst[s]
            him_ref[:, s * LANES:(s + 1) * LANES] = hst[n_slab + s]


def _s5_prompt(u, a_re, a_im, b_mat, c_mat, d, w_glu_bf16, b_glu, g_out):
    nb, s, _ = u.shape
    nseq = SUBLANES
    assert nb % nseq == 0 and s % SSM_TT == 0
    n_state = a_re.shape[0]
    n_slab = n_state // LANES
    bcast = lambda a: jnp.broadcast_to(a.reshape(n_slab, 1, LANES), (n_slab, nseq, LANES))
    const = lambda a: pl.BlockSpec(a.shape, lambda b, i: (0,) * a.ndim)
    args = (u, bcast(a_re), bcast(a_im), b_mat, c_mat, d.reshape(1, D_SSM), w_glu_bf16,
            b_glu.reshape(1, D_SSM), g_out.reshape(1, D_SSM))
    st = jax.ShapeDtypeStruct((nb, n_state), F32)
    return pl.pallas_call(
        _s5_prompt_kernel,
        grid=(nb // nseq, s // SSM_TT),
        in_specs=[pl.BlockSpec((nseq, SSM_TT, D_SSM), lambda b, i: (b, i, 0))] + [const(a) for a in args[1:]],
        out_specs=[pl.BlockSpec((nseq, SSM_TT, D_SSM), lambda b, i: (b, i, 0)),
                   pl.BlockSpec((nseq, n_state), lambda b, i: (b, 0)),
                   pl.BlockSpec((nseq, n_state), lambda b, i: (b, 0))],
        out_shape=[jax.ShapeDtypeStruct((nb, s, D_SSM), BF16), st, st],
        scratch_shapes=[pltpu.VMEM((2 * n_slab, nseq * SSM_PITCH, LANES), F32),
                        pltpu.VMEM((2 * n_slab, nseq, LANES), F32)],
        compiler_params=_cparams(("parallel", "arbitrary")),
    )(*args)


def _s5_sample_kernel(u_ref, h0re_ref, h0im_ref, are_ref, aim_ref, b_ref, c_ref, d_ref, wglu_ref, bglu_ref, g_ref,
                      o_ref, hre_ref, him_ref):
    t_new = u_ref.shape[0]
    hre = h0re_ref[...]
    him = h0im_ref[...]
    are, aim = are_ref[...], aim_ref[...]
    half = 4 * LANES
    for t in range(t_new):
        ub = u_ref[t]
        bus = [jnp.dot(ub[:, kc * LANES:(kc + 1) * LANES], b_ref[kc], preferred_element_type=F32) for kc in range(4)]
        bre = jnp.concatenate([bu[:, :half] for bu in bus], axis=1)
        bim = jnp.concatenate([bu[:, half:] for bu in bus], axis=1)
        hre, him = are * hre - aim * him + bre, are * him + aim * hre + bim
        ys = []
        for kc in range(4):
            lhs = jnp.concatenate([hre[:, kc * half:(kc + 1) * half], him[:, kc * half:(kc + 1) * half]], axis=1)
            ys.append(jnp.dot(lhs.astype(BF16), c_ref[kc], preferred_element_type=F32))
        y = jnp.concatenate(ys, axis=1)
        o_ref[t] = _s5_epilogue(y, ub, d_ref, wglu_ref, bglu_ref, g_ref).astype(o_ref.dtype)
    hre_ref[...] = hre
    him_ref[...] = him


def _s5_sample(u_tm, h0_re, h0_im, a_re, a_im, b_mat, c_mat, d, w_glu_bf16, b_glu, g_out):
    t_new, nb, _ = u_tm.shape
    n_state = a_re.shape[0]
    tb = 64
    assert nb % tb == 0
    const = lambda a: pl.BlockSpec(a.shape, lambda b: (0,) * a.ndim)
    args = (u_tm, h0_re, h0_im, a_re.reshape(1, n_state), a_im.reshape(1, n_state), b_mat, c_mat,
            d.reshape(1, D_SSM), w_glu_bf16, b_glu.reshape(1, D_SSM), g_out.reshape(1, D_SSM))
    st_spec = pl.BlockSpec((tb, n_state), lambda b: (b, 0))
    st = jax.ShapeDtypeStruct((nb, n_state), F32)
    return pl.pallas_call(
        _s5_sample_kernel,
        grid=(nb // tb,),
        in_specs=[pl.BlockSpec((t_new, tb, D_SSM), lambda b: (0, b, 0)), st_spec, st_spec]
                 + [const(a) for a in args[3:]],
        out_specs=[pl.BlockSpec((t_new, tb, D_SSM), lambda b: (0, b, 0)), st_spec, st_spec],
        out_shape=[jax.ShapeDtypeStruct((t_new, nb, D_SSM), BF16), st, st],
        compiler_params=_cparams(("parallel",)),
    )(*args)


def _store_packed(ref, val):
    half = val.shape[1] // 2
    bits = pltpu.bitcast(val.astype(BF16).astype(F32), jnp.uint32)
    words = bits[:, :half] | lax.shift_right_logical(bits[:, half:], jnp.uint32(16))
    for c in range(PACK_CHUNKS):
        ref[c] = words[:, c * LANES:(c + 1) * LANES]


def _load_packed(ref):
    hi, lo = [], []
    for c in range(PACK_CHUNKS):
        w = ref[c]
        hi.append(pltpu.bitcast(w & jnp.uint32(0xFFFF0000), F32))
        lo.append(pltpu.bitcast(lax.shift_left(w, jnp.uint32(16)), F32))
    return jnp.concatenate(hi + lo, axis=1)


def _outproj_kernel(x_ref, oa_ref, os_ref, gate_ref, sh_ref, sc_ref, ga_ref, gf_ref, w_ref, wr_hi_ref, wr_lo_ref,
                    br_ref, *rest):
    x1_ref, h_ref, lg_ref = rest[-3:]
    oa = oa_ref[...].astype(F32)
    ms = jnp.mean(oa * oa, axis=-1, keepdims=True)
    na = oa * lax.rsqrt(ms + EPS) * ga_ref[...]
    merged = jnp.concatenate([na.astype(BF16), os_ref[...]], axis=-1)
    x1 = x_ref[...] + gate_ref[...] * jnp.dot(merged, w_ref[...], preferred_element_type=F32)
    x1_ref[...] = x1
    ms = jnp.mean(x1 * x1, axis=-1, keepdims=True)
    h = x1 * lax.rsqrt(ms + EPS) * gf_ref[...]
    h = h * (1.0 + sc_ref[...]) + sh_ref[...]
    h_hi = h.astype(BF16)
    _store_packed(h_ref, h)
    h_lo = (h - h_hi.astype(F32)).astype(BF16)
    lg = jnp.dot(h_hi, wr_hi_ref[...], preferred_element_type=F32)
    lg = lg + jnp.dot(h_hi, wr_lo_ref[...], preferred_element_type=F32)
    lg = lg + jnp.dot(h_lo, wr_hi_ref[...], preferred_element_type=F32)
    lg_ref[...] = lg + br_ref[...]


def _outproj(x, o_attn, o_ssm, gate, shift, scale, g_attn, g_ffn, w_out_bf16, w_router, b_router, tm, t_all, row0,
             shared):
    nb, s, d = x.shape
    assert row0 % tm == 0
    blk = lambda b, i: row0 // tm + b * (s // tm) + i
    row = lambda n: pl.BlockSpec((None, tm, n), lambda b, i: (b, i, 0))
    const = lambda a: pl.BlockSpec(a.shape, lambda b, i: (0,) * a.ndim)
    wr_hi = w_router.astype(BF16)
    wr_lo = (w_router - wr_hi.astype(F32)).astype(BF16)
    consts = (g_attn.reshape(1, D_ATTN), g_ffn.reshape(1, d), w_out_bf16, wr_hi, wr_lo,
              b_router.reshape(1, ROUTER_COLS))
    n_in = 6 + len(consts)
    out = pl.pallas_call(
        _outproj_kernel,
        grid=(nb, s // tm),
        in_specs=[row(d), row(D_ATTN), row(D_SSM), _mod_spec(gate, tm), _mod_spec(shift, tm), _mod_spec(scale, tm)]
                 + [const(a) for a in consts] + [pl.BlockSpec(memory_space=pl.ANY)] * len(shared or ()),
        out_specs=[row(d), pl.BlockSpec((PACK_CHUNKS, tm, LANES), lambda b, i: (0, blk(b, i), 0)),
                   pl.BlockSpec((tm, ROUTER_COLS), lambda b, i: (blk(b, i), 0))],
        out_shape=[jax.ShapeDtypeStruct((nb, s, d), F32),
                   jax.ShapeDtypeStruct((PACK_CHUNKS, t_all, LANES), jnp.uint32),
                   jax.ShapeDtypeStruct((t_all, ROUTER_COLS), F32)],
        input_output_aliases={n_in + j: 1 + j for j in range(len(shared or ()))},
        compiler_params=_cparams(("parallel", "parallel")),
    )(x, o_attn, o_ssm, gate, shift, scale, *consts, *(shared or ()))
    return out[0], tuple(out[1:])


def _expert_kernel(blk_e_ref, nvalid_ref, x_ref, wg_ref, wu_ref, wd_ref, y_ref, wg_s, wu_s, wd_s):
    i = pl.program_id(0)

    @pl.when((i == 0) | (blk_e_ref[i] != blk_e_ref[jnp.maximum(i - 1, 0)]))
    def _():
        wg_s[...] = wg_ref[...].astype(BF16)
        wu_s[...] = wu_ref[...].astype(BF16)
        wd_s[...] = wd_ref[...].astype(BF16)

    @pl.when(i < nvalid_ref[0])
    def _():
        x = _load_packed(x_ref).astype(BF16)
        a = jnp.dot(x, wg_s[...], preferred_element_type=F32)
        b = jnp.dot(x, wu_s[...], preferred_element_type=F32)
        hid = (a * jax.nn.sigmoid(a) * b).astype(BF16)
        _store_packed(y_ref, jnp.dot(hid, wd_s[...], preferred_element_type=F32))

    @pl.when(i >= nvalid_ref[0])
    def _():
        y_ref[...] = jnp.zeros_like(y_ref)


def _experts(xs, blk_e, nvalid, w_gate, w_up, w_down):
    nblk = xs.shape[1] // MOE_TILE
    d = w_gate.shape[1]
    wspec = lambda shp: pl.BlockSpec((None,) + shp, lambda i, be, nv: (be[i], 0, 0))
    tile = pl.BlockSpec((PACK_CHUNKS, MOE_TILE, LANES), lambda i, be, nv: (0, i, 0))
    return pl.pallas_call(
        _expert_kernel,
        grid_spec=pltpu.PrefetchScalarGridSpec(
            num_scalar_prefetch=2,
            grid=(nblk,),
            in_specs=[tile, wspec((d, D_EXPERT)), wspec((d, D_EXPERT)), wspec((D_EXPERT, d))],
            out_specs=tile,
            scratch_shapes=[pltpu.VMEM((d, D_EXPERT), BF16), pltpu.VMEM((d, D_EXPERT), BF16),
                            pltpu.VMEM((D_EXPERT, d), BF16)],
        ),
        out_shape=jax.ShapeDtypeStruct(xs.shape, jnp.uint32),
        compiler_params=_cparams(("arbitrary",)),
    )(blk_e, nvalid, xs, w_gate, w_up, w_down)


def _combine_kernel(x1_ref, y0_ref, y1_ref, info_ref, gate_ref, gfin_ref, o_ref):
    info = pltpu.bitcast(info_ref[...], F32)
    g0 = info[:, INFO_GATE:INFO_GATE + 1]
    g1 = info[:, INFO_GATE + 1:INFO_GATE + 2]
    moe = g0 * _load_packed(y0_ref) + g1 * _load_packed(y1_ref)
    x2 = x1_ref[...] + gate_ref[...] * moe
    ms = jnp.mean(x2 * x2, axis=-1, keepdims=True)
    o_ref[...] = x2 * lax.rsqrt(ms + EPS) * gfin_ref[...]


def _combine(x1, y01, info, gate, g_final, tm, row0):
    nb, s, d = x1.shape
    assert row0 % tm == 0
    blk = lambda b, i: row0 // tm + b * (s // tm) + i
    row = lambda n: pl.BlockSpec((None, tm, n), lambda b, i: (b, i, 0))
    packed = lambda k: pl.BlockSpec((None, PACK_CHUNKS, tm, LANES), lambda b, i: (k, 0, blk(b, i), 0))
    return pl.pallas_call(
        _combine_kernel,
        grid=(nb, s // tm),
        in_specs=[row(d), packed(0), packed(1), pl.BlockSpec((tm, ROUTER_COLS), lambda b, i: (blk(b, i), 0)),
                  _mod_spec(gate, tm), pl.BlockSpec((1, d), lambda b, i: (0, 0))],
        out_specs=row(d),
        out_shape=jax.ShapeDtypeStruct((nb, s, d), F32),
        compiler_params=_cparams(("parallel", "parallel")),
    )(x1, y01, y01, info, gate, g_final.reshape(1, d))


def _sc_window(n_rows):
    assert n_rows % SC_WINDOW == 0
    return SC_WINDOW


def _sc_gather_rows(table, idx):
    n = idx.shape[0]
    w = _sc_window(n)
    mesh = plsc.VectorSubcoreMesh(core_axis_name="core", subcore_axis_name="subcore")

    @functools.partial(pl.kernel, out_type=jax.ShapeDtypeStruct((n, LANES), table.dtype), mesh=mesh)
    def gather_kernel(x_hbm, i_hbm, o_hbm):
        def body(i_vmem, o_vmem):
            pltpu.sync_copy(x_hbm.at[i_vmem.at[0]], o_vmem)

        pltpu.emit_pipeline(
            body, grid=(n // w,),
            in_specs=[pl.BlockSpec((1, w), lambda i: (0, i))],
            out_specs=[pl.BlockSpec((w, LANES), lambda i: (i, 0))],
            core_axis_name=("core", "subcore"), dimension_semantics=(pltpu.PARALLEL,),
        )(i_hbm, o_hbm)

    return gather_kernel(table, idx.reshape(1, n))


def _sc_scatter_rows(rows, idx0, idx1, n_out):
    n = rows.shape[0]
    w = _sc_window(n)
    mesh = plsc.VectorSubcoreMesh(core_axis_name="core", subcore_axis_name="subcore")

    @functools.partial(pl.kernel, out_type=jax.ShapeDtypeStruct((n_out, LANES), rows.dtype), mesh=mesh)
    def scatter_kernel(x_hbm, i0_hbm, i1_hbm, o_hbm):
        def body(x_vmem, i0_vmem, i1_vmem):
            pltpu.sync_copy(x_vmem, o_hbm.at[i0_vmem.at[0]])
            pltpu.sync_copy(x_vmem, o_hbm.at[i1_vmem.at[0]])

        pltpu.emit_pipeline(
            body, grid=(n // w,),
            in_specs=[pl.BlockSpec((w, LANES), lambda i: (i, 0)),
                      pl.BlockSpec((1, w), lambda i: (0, i)),
                      pl.BlockSpec((1, w), lambda i: (0, i))],
            out_specs=[],
            core_axis_name=("core", "subcore"), dimension_semantics=(pltpu.PARALLEL,),
        )(x_hbm, i0_hbm, i1_hbm)

    return scatter_kernel(rows, idx0.reshape(1, n), idx1.reshape(1, n))


def _route_kernel(lg_ref, tri_ref, info_ref, rec_ref, cnt_ref, carry):
    i = pl.program_id(0)

    @pl.when(i == 0)
    def _():
        carry[...] = jnp.zeros_like(carry)

    lg = lg_ref[...]
    lane = lax.broadcasted_iota(jnp.int32, lg.shape, 1)
    lane_f = lane.astype(F32)
    none = float(ROUTER_COLS)
    ninf = float("-inf")
    first = lambda cond: jnp.min(jnp.where(cond, lane_f, none), axis=-1, keepdims=True)

    is_c = lane < N_EXPERT_GROUPS
    lc = jnp.where(is_c, lg, ninf)
    mc = jnp.max(lc, axis=-1, keepdims=True)
    p_grp = 1.0 / jnp.sum(jnp.exp(lc - mc), axis=-1, keepdims=True)
    grp = first(lc == mc)
    fine = lane - N_EXPERT_GROUPS
    fine_grp = lax.shift_right_arithmetic(fine, jnp.int32(EXPERTS_PER_GROUP.bit_length() - 1))
    in_grp = (fine >= 0) & (fine < N_EXPERTS) & (fine_grp.astype(F32) == grp)
    lf = jnp.where(in_grp, lg, ninf)
    v1 = jnp.max(lf, axis=-1, keepdims=True)
    i1 = first(lf == v1)
    lf2 = jnp.where(lane_f == i1, ninf, lf)
    v2 = jnp.max(lf2, axis=-1, keepdims=True)
    i2 = first(lf2 == v2)
    b = jnp.exp(v2 - v1)
    g0 = p_grp / (1.0 + b)
    g1 = p_grp * b / (1.0 + b)

    hit1 = lane_f == i1
    hit2 = lane_f == i2
    onehot = jnp.where(hit1 | hit2, 1.0, 0.0)
    before = jnp.dot(tri_ref[...], onehot.astype(BF16), preferred_element_type=F32) + carry[0:1, :]
    r1 = jnp.sum(jnp.where(hit1, before, 0.0), axis=-1, keepdims=True)
    r2 = jnp.sum(jnp.where(hit2, before, 0.0), axis=-1, keepdims=True)
    carry[...] = carry[...] + jnp.sum(onehot, axis=0, keepdims=True)

    as_int = lambda v: jnp.broadcast_to(v, lg.shape).astype(jnp.int32)
    as_bits = lambda v: pltpu.bitcast(jnp.broadcast_to(v, lg.shape), jnp.int32)
    info = jnp.zeros(lg.shape, jnp.int32)
    fields = ((INFO_EXPERT, as_int(i1 - N_EXPERT_GROUPS)), (INFO_EXPERT + 1, as_int(i2 - N_EXPERT_GROUPS)),
              (INFO_RANK, as_int(r1)), (INFO_RANK + 1, as_int(r2)), (INFO_GATE, as_bits(g0)),
              (INFO_GATE + 1, as_bits(g1)))
    for col, val in fields:
        info = jnp.where(lane == col, val, info)
    info_ref[...] = info
    rec_ref[...] = pltpu.bitcast(pltpu.bitcast(info, F32).T[:SUBLANES], jnp.int32)

    @pl.when(i == pl.num_programs(0) - 1)
    def _():
        cnt_ref[...] = carry[...]


def _route(logits):
    t = logits.shape[0]
    tr = min(t, 512)
    assert t % tr == 0
    tri = jnp.asarray(np.tril(np.ones((tr, tr), np.float32), -1), BF16)
    info, rec, cnt = pl.pallas_call(
        _route_kernel,
        grid=(t // tr,),
        in_specs=[pl.BlockSpec((tr, ROUTER_COLS), lambda i: (i, 0)), pl.BlockSpec((tr, tr), lambda i: (0, 0))],
        out_specs=[pl.BlockSpec((tr, ROUTER_COLS), lambda i: (i, 0)),
                   pl.BlockSpec((SUBLANES, tr), lambda i: (0, i)),
                   pl.BlockSpec((SUBLANES, ROUTER_COLS), lambda i: (0, 0))],
        out_shape=[jax.ShapeDtypeStruct((t, ROUTER_COLS), jnp.int32),
                   jax.ShapeDtypeStruct((SUBLANES, t), jnp.int32),
                   jax.ShapeDtypeStruct((SUBLANES, ROUTER_COLS), F32)],
        scratch_shapes=[pltpu.VMEM((SUBLANES, ROUTER_COLS), F32)],
        compiler_params=_cparams(("arbitrary",)),
    )(logits, tri)
    counts = cnt[0, N_EXPERT_GROUPS:N_EXPERT_GROUPS + N_EXPERTS].astype(jnp.int32)
    return info, rec, counts


def _moe(h_packed, logits, w_gate, w_up, w_down):
    t = logits.shape[0]
    info, rec, counts = _route(logits)
    padded = (counts + MOE_TILE - 1) // MOE_TILE * MOE_TILE
    pend = jnp.cumsum(padded)
    pstart = pend - padded
    nblk = -(-2 * t // MOE_TILE) + N_EXPERTS
    n_slots = nblk * MOE_TILE
    blk_start = jnp.arange(nblk, dtype=jnp.int32) * MOE_TILE
    blk_e = jnp.minimum(jnp.sum(blk_start[:, None] >= pend[None, :], axis=1), N_EXPERTS - 1).astype(jnp.int32)
    nvalid = (pend[-1] // MOE_TILE).astype(jnp.int32).reshape(1)
    chunk_base = (jnp.arange(PACK_CHUNKS, dtype=jnp.int32) * n_slots)[:, None]
    rows = lambda k: (chunk_base + (jnp.take(pstart, rec[INFO_EXPERT + k]).astype(jnp.int32)
                                    + rec[INFO_RANK + k])[None, :]).reshape(-1)
    idx0, idx1 = rows(0), rows(1)
    xs = _sc_scatter_rows(h_packed.reshape(PACK_CHUNKS * t, LANES), idx0, idx1, PACK_CHUNKS * n_slots)
    ys = _experts(xs.reshape(PACK_CHUNKS, n_slots, LANES), blk_e, nvalid, w_gate, w_up, w_down)
    y01 = _sc_gather_rows(ys.reshape(PACK_CHUNKS * n_slots, LANES), jnp.concatenate([idx0, idx1]))
    return y01.reshape(2, PACK_CHUNKS, t, LANES), info


def _mixer(x, mods_mix, mods_ffn, attn_fn, s5_fn, wts, tm, t_all, row0, shared):
    q, k, v, u = _inproj(x, mods_mix[0], mods_mix[1], wts['g_mix'], wts['w_in'], tm)
    o_attn = attn_fn(q, k, v)
    o_ssm, h_re, h_im = s5_fn(u)
    x1, shared = _outproj(x, o_attn, o_ssm, mods_mix[2], mods_ffn[0], mods_ffn[1], wts['g_attn_out'], wts['g_ffn'],
                          wts['w_out'], wts['w_router'], wts['b_router'], tm, t_all, row0, shared)
    return x1, shared, (k, v, h_re, h_im)


def kernel(x_prompt, x_sample, cache_k_win, cache_v_win, state_ssm_re, state_ssm_im, c_prompt, c_sample, g_mix, w_ada_mix, b_ada_mix, w_in, w_out, g_attn_out, g_ssm_out, ssm_lambda_re, ssm_lambda_im, ssm_log_dt, ssm_b_re, ssm_b_im, ssm_c_re, ssm_c_im, ssm_d, w_glu, b_glu, g_ffn, w_ada_ffn, b_ada_ffn, w_router_coarse, b_router_coarse, w_router_fine, b_router_fine, w_expert_gate, w_expert_up, w_expert_down, g_final):
    depth = g_mix.shape[0]
    assert depth == 1, "single-layer step"
    l = 0
    bp, sp, d = x_prompt.shape
    bs, ts, _ = x_sample.shape
    keep = min(max(w for w, _ in DILATED_PATTERNS), sp)

    c_all = jnp.concatenate([c_prompt, c_sample], axis=0).astype(F32)
    m_mix = _adaln(c_all, w_ada_mix[l], b_ada_mix[l])
    m_ffn = _adaln(c_all, w_ada_ffn[l], b_ada_ffn[l])

    def mods(m, lo, hi, per_token):
        parts = jnp.split(m[lo:hi], 3, axis=-1)
        if per_token:
            return tuple(jnp.repeat(p, ts, axis=0)[None] for p in parts)
        return tuple(p[:, None, :] for p in parts)

    pad = ROUTER_COLS - N_EXPERT_GROUPS - N_EXPERTS
    w_router = jnp.concatenate([w_router_coarse[l], w_router_fine[l], jnp.zeros((d, pad), F32)], axis=1)
    b_router = jnp.concatenate([b_router_coarse[l], b_router_fine[l], jnp.zeros((pad,), F32)])
    wts = {
        'g_mix': g_mix[l], 'w_in': w_in[l].astype(BF16), 'w_out': w_out[l].astype(BF16),
        'g_attn_out': g_attn_out[l], 'g_ffn': g_ffn[l], 'w_router': w_router, 'b_router': b_router,
        'w_gate': w_expert_gate[l], 'w_up': w_expert_up[l], 'w_down': w_expert_down[l],
    }
    a_re, a_im, b_mat, c_mat = _s5_params(ssm_lambda_re[l], ssm_lambda_im[l], ssm_log_dt[l], ssm_b_re[l],
                                          ssm_b_im[l], ssm_c_re[l], ssm_c_im[l])
    s5_w = (a_re, a_im, b_mat, c_mat, ssm_d[l].reshape(-1), w_glu[l].astype(BF16), b_glu[l], g_ssm_out[l])

    t_prompt, t_sample = bp * sp, bs * ts
    t_all = t_prompt + t_sample
    tm_p, tm_s = min(sp, ROW_TILE), min(t_sample, ROW_TILE)
    mods_ffn_p, mods_ffn_s = mods(m_ffn, 0, bp, False), mods(m_ffn, bp, bp + bs, True)

    x1p, shared, (kp, vp, hrp, hip) = _mixer(
        x_prompt, mods(m_mix, 0, bp, False), mods_ffn_p, _attn_prompt, lambda u: _s5_prompt(u, *s5_w), wts,
        tm_p, t_all, 0, None)

    n_state = SSM_GROUPS * SSM_STATE
    ck = cache_k_win[l]
    cv = cache_v_win[l]

    def attn_s(q, k, v):
        r = lambda a: a.reshape(bs, ts, D_ATTN)
        return _attn_sample(r(q), r(k), r(v), ck, cv).reshape(1, bs * ts, D_ATTN)

    def s5_s(u):
        u_tm = u.reshape(bs, ts, D_SSM).transpose(1, 0, 2)
        o, hr, hi = _s5_sample(u_tm, state_ssm_re[l].reshape(bs, n_state), state_ssm_im[l].reshape(bs, n_state),
                               *s5_w)
        return o.transpose(1, 0, 2).reshape(1, bs * ts, D_SSM), hr, hi

    x1s, shared, (ks, vs, hrs, his) = _mixer(
        x_sample.reshape(1, t_sample, d), mods(m_mix, bp, bp + bs, True), mods_ffn_s, attn_s, s5_s, wts,
        tm_s, t_all, t_prompt, shared)

    y01, info = _moe(*shared, wts['w_gate'], wts['w_up'], wts['w_down'])
    yp = _combine(x1p, y01, info, mods_ffn_p[2], g_final, tm_p, 0)
    ys = _combine(x1s, y01, info, mods_ffn_s[2], g_final, tm_s, t_prompt)

    heads = lambda a, b, s: a.reshape(1, b, s, N_HEADS, HEAD_DIM)
    state = lambda a, b: a.reshape(1, b, SSM_GROUPS, SSM_STATE)
    return (yp, ys.reshape(bs, ts, d),
            heads(kp[:, sp - keep:], bp, keep), heads(vp[:, sp - keep:], bp, keep), state(hrp, bp), state(hip, bp),
            heads(ks, bs, ts), heads(vs, bs, ts), state(hrs, bs), state(his, bs))
```

```python
import functools
import math

import numpy as np
import jax
import jax.numpy as jnp
from jax import lax
from jax.experimental import pallas as pl
from jax.experimental.pallas import tpu as pltpu
from jax.experimental.pallas import tpu_sc as plsc

F32 = jnp.float32
BF16 = jnp.bfloat16
HIGHEST = lax.Precision.HIGHEST

D_MODEL = 1024
D_ATTN = 512
D_SSM = 512
HEAD_DIM = 64
N_HEADS = 8
DILATED_PATTERNS = ((128, 1), (512, 4), (2048, 16))
SSM_GROUP_CH = 16
SSM_GROUPS = 32
SSM_STATE = 64
N_EXPERT_GROUPS = 4
EXPERTS_PER_GROUP = 8
N_EXPERTS = 32
D_EXPERT = 512
D_IN_PROJ = 3 * D_ATTN + D_SSM
EPS = 1e-6

LANES = 128
SUBLANES = 8
VMEM_LIMIT = 48 * 1024 * 1024

ATTN_BLOCK = 128
ATTN_UNROLL = 8
NEG_BIG = -1e30
SSM_TT = 128
SSM_PITCH = SSM_TT + 8
ROUTER_COLS = 128
MOE_TILE = 512
ROW_TILE = 512
SC_WINDOW = 128
PACK_CHUNKS = D_MODEL // 2 // LANES
INFO_EXPERT, INFO_RANK, INFO_GATE = 0, 2, 4


def _cparams(sem, vmem=VMEM_LIMIT):
    return pltpu.CompilerParams(dimension_semantics=sem, vmem_limit_bytes=vmem)


def _adaln_kernel(c_ref, w_ref, b_ref, o_ref):
    c = c_ref[...]
    s = c * jax.nn.sigmoid(c)
    o_ref[...] = jnp.dot(s, w_ref[...], precision=HIGHEST, preferred_element_type=F32) + b_ref[...]


def _adaln(c, w, b):
    r, d = c.shape
    n = w.shape[1]
    tn = 768
    return pl.pallas_call(
        _adaln_kernel,
        grid=(n // tn,),
        in_specs=[pl.BlockSpec((r, d), lambda j: (0, 0)),
                  pl.BlockSpec((d, tn), lambda j: (0, j)),
                  pl.BlockSpec((1, tn), lambda j: (0, j))],
        out_specs=pl.BlockSpec((r, tn), lambda j: (0, j)),
        out_shape=jax.ShapeDtypeStruct((r, n), F32),
        compiler_params=_cparams(("arbitrary",)),
    )(c, w, b.reshape(1, n))


def _inproj_kernel(x_ref, sh_ref, sc_ref, g_ref, w_ref, q_ref, k_ref, v_ref, u_ref):
    x = x_ref[...]
    ms = jnp.mean(x * x, axis=-1, keepdims=True)
    h = x * lax.rsqrt(ms + EPS) * g_ref[...]
    h = h * (1.0 + sc_ref[...]) + sh_ref[...]
    p = jnp.dot(h.astype(BF16), w_ref[...], preferred_element_type=F32)
    q_ref[...] = p[:, :D_ATTN] * (HEAD_DIM ** -0.5)
    k_ref[...] = p[:, D_ATTN:2 * D_ATTN]
    v_ref[...] = p[:, 2 * D_ATTN:3 * D_ATTN]
    u_ref[...] = p[:, 3 * D_ATTN:].astype(u_ref.dtype)


def _mod_spec(mod, tm):
    d = mod.shape[-1]
    if mod.shape[1] == 1:
        return pl.BlockSpec((None, 1, d), lambda b, i: (b, 0, 0))
    return pl.BlockSpec((None, tm, d), lambda b, i: (b, i, 0))


def _inproj(x, shift, scale, g, w_bf16, tm):
    nb, s, d = x.shape
    row = lambda n: pl.BlockSpec((None, tm, n), lambda b, i: (b, i, 0))
    out = jax.ShapeDtypeStruct((nb, s, D_ATTN), F32)
    return pl.pallas_call(
        _inproj_kernel,
        grid=(nb, s // tm),
        in_specs=[row(d), _mod_spec(shift, tm), _mod_spec(scale, tm),
                  pl.BlockSpec((1, d), lambda b, i: (0, 0)),
                  pl.BlockSpec((d, D_IN_PROJ), lambda b, i: (0, 0))],
        out_specs=[row(D_ATTN), row(D_ATTN), row(D_ATTN), row(D_SSM)],
        out_shape=[out, out, out, jax.ShapeDtypeStruct((nb, s, D_SSM), BF16)],
        compiler_params=_cparams(("parallel", "parallel")),
    )(x, shift, scale, g.reshape(1, d), w_bf16)


def _band_bias():
    qi = np.arange(ATTN_BLOCK)[:, None]
    kj = np.arange(ATTN_BLOCK)[None, :]
    cur = kj <= qi
    prev = kj >= qi
    to_bias = lambda m: np.tile(np.where(m, 0.0, NEG_BIG).astype(np.float32), (2, 1))
    return to_bias(np.concatenate([prev, cur], axis=1)), to_bias(cur)


def _attn_prompt_kernel(q_ref, k_ref, v_ref, bias2_ref, bias1_ref, o_ref, acc_scr, m_scr, l_scr):
    s_len = q_ref.shape[0]
    lane = lax.broadcasted_iota(jnp.int32, (1, LANES), 1)
    head0 = lane < HEAD_DIM

    def rows(ref, start, n, d):
        if d == 1:
            return ref[pl.ds(start, n), :]
        return ref[pl.ds(start, n, stride=d), :]

    def tiles(items, first):
        nk = ATTN_BLOCK if first else 2 * ATTN_BLOCK
        bias = bias1_ref[...] if first else bias2_ref[...]
        one = jnp.ones((), BF16)
        loaded = []
        for _, d, qstart in items:
            kstart = qstart if first else qstart - d * ATTN_BLOCK
            loaded.append((rows(q_ref, qstart, ATTN_BLOCK, d), rows(k_ref, kstart, nk, d),
                           rows(v_ref, kstart, nk, d)))
        results = []
        for qrows, krows, vrows in loaded:
            kb = krows.astype(BF16)
            vb = vrows.astype(BF16)
            q2 = jnp.concatenate([jnp.where(head0, qrows, 0.0), jnp.where(head0, 0.0, qrows)], axis=0).astype(BF16)
            s = lax.dot_general(q2, kb, (((1,), (1,)), ((), ())), preferred_element_type=F32) + bias
            m = jnp.max(s, axis=-1, keepdims=True)
            p = jnp.exp(s - m).astype(BF16)
            pv0 = jnp.dot(p[:ATTN_BLOCK], jnp.where(head0, vb, one), preferred_element_type=F32)
            pv1 = jnp.dot(p[ATTN_BLOCK:], jnp.where(head0, one, vb), preferred_element_type=F32)
            results.append((jnp.where(head0, pv0, pv1), jnp.where(head0, m[:ATTN_BLOCK], m[ATTN_BLOCK:]),
                            jnp.where(head0, pv1, pv0)))
        for (br, d, qstart), (acc, m, l) in zip(items, results):
            dst = pl.ds(qstart, ATTN_BLOCK) if d == 1 else pl.ds(qstart, ATTN_BLOCK, stride=d)
            acc_scr[br, dst, :] = acc
            m_scr[br, dst, :] = m
            l_scr[br, dst, :] = l

    def largest_divisor(n):
        return max(f for f in range(1, ATTN_UNROLL + 1) if n % f == 0)

    grouped = []
    for br, (window, d) in enumerate(DILATED_PATTERNS):
        assert window // d == ATTN_BLOCK
        if d <= ATTN_UNROLL:
            grouped += [(br, d, r) for r in range(d)]
        else:
            un = largest_divisor(d)

            def first_body(i, carry, br=br, d=d, un=un):
                tiles([(br, d, i * un + j) for j in range(un)], True)
                return carry

            lax.fori_loop(0, d // un, first_body, 0)
    for i in range(0, len(grouped), ATTN_UNROLL):
        tiles(grouped[i:i + ATTN_UNROLL], True)

    for br, (window, d) in enumerate(DILATED_PATTERNS):
        nblk = s_len // (d * ATTN_BLOCK)
        n_rest = d * (nblk - 1)
        if n_rest:
            un = largest_divisor(n_rest)

            def rest_body(i, carry, br=br, d=d, nblk=nblk, un=un):
                items = []
                for j in range(un):
                    n = i * un + j
                    qstart = n // (nblk - 1) + d * ATTN_BLOCK * (1 + n % (nblk - 1))
                    items.append((br, d, pl.multiple_of(qstart, ATTN_BLOCK) if d == 1 else qstart))
                tiles(items, False)
                return carry

            lax.fori_loop(0, n_rest // un, rest_body, 0)

    ch = 256

    def merge_body(i, carry):
        sl = pl.ds(pl.multiple_of(i * ch, ch), ch)
        ms = [m_scr[br, sl, :] for br in range(3)]
        mx = jnp.maximum(jnp.maximum(ms[0], ms[1]), ms[2])
        num = jnp.zeros((ch, LANES), F32)
        den = jnp.zeros((ch, LANES), F32)
        for br in range(3):
            e = jnp.exp(ms[br] - mx)
            num = num + e * acc_scr[br, sl, :]
            den = den + e * pltpu.roll(l_scr[br, sl, :], HEAD_DIM, 1)
        o_ref[sl, :] = (num / den).astype(o_ref.dtype)
        return carry

    lax.fori_loop(0, s_len // ch, merge_body, 0)


def _attn_prompt(q, k, v):
    nb, s, _ = q.shape
    assert s % (16 * ATTN_BLOCK) == 0
    bias2, bias1 = _band_bias()
    blk = pl.BlockSpec((None, s, LANES), lambda b, p: (b, 0, p))
    const = lambda a: pl.BlockSpec(a.shape, lambda b, p: (0, 0))
    return pl.pallas_call(
        _attn_prompt_kernel,
        grid=(nb, D_ATTN // LANES),
        in_specs=[blk, blk, blk, const(bias2), const(bias1)],
        out_specs=blk,
        out_shape=jax.ShapeDtypeStruct((nb, s, D_ATTN), BF16),
        scratch_shapes=[pltpu.VMEM((3, s, LANES), F32)] * 3,
        compiler_params=_cparams(("parallel", "parallel")),
    )(q, k, v, jnp.asarray(bias2), jnp.asarray(bias1))


def _sample_bias(t_new, w_buf):
    n_new = -(-t_new // LANES) * LANES
    t = np.arange(t_new)[:, None]
    idx = np.arange(w_buf + n_new)[None, :]
    dist = w_buf + t - idx
    mult = np.zeros(dist.shape, np.int64)
    for window, d in DILATED_PATTERNS:
        mult += ((dist >= 0) & (dist % d == 0) & (dist <= window)).astype(np.int64)
    bias = np.where(mult > 0, np.log(np.maximum(mult, 1)), NEG_BIG).astype(np.float32)
    bias = np.repeat(bias, N_HEADS, axis=0)
    return bias[:, :w_buf], bias[:, w_buf:]


def _attn_sample_kernel(q_ref, kn_ref, vn_ref, kt_ref, vt_ref, bias_ref, biasn_ref, o_ref):
    t_new = q_ref.shape[0]
    n_new = biasn_ref.shape[1]
    row_head = lax.broadcasted_iota(jnp.int32, (t_new * N_HEADS, D_ATTN), 0) & (N_HEADS - 1)
    col_head = lax.broadcasted_iota(jnp.int32, (t_new * N_HEADS, D_ATTN), 1) >> (HEAD_DIM.bit_length() - 1)
    own = row_head == col_head
    q = q_ref[...]
    q_rows = jnp.concatenate([jnp.broadcast_to(q[t:t + 1], (N_HEADS, D_ATTN)) for t in range(t_new)], axis=0)
    q_bd = jnp.where(own, q_rows, 0.0).astype(BF16)
    pad = jnp.zeros((n_new - t_new, D_ATTN), F32)
    kn = jnp.concatenate([kn_ref[...], pad], axis=0).astype(BF16)
    vn = jnp.concatenate([vn_ref[...], pad], axis=0).astype(BF16)
    nt = (((1,), (1,)), ((), ()))
    s_c = jnp.dot(q_bd, kt_ref[...].astype(BF16), preferred_element_type=F32) + bias_ref[...]
    s_n = lax.dot_general(q_bd, kn, nt, preferred_element_type=F32) + biasn_ref[...]
    m = jnp.maximum(jnp.max(s_c, axis=-1, keepdims=True), jnp.max(s_n, axis=-1, keepdims=True))
    p_c = jnp.exp(s_c - m)
    p_n = jnp.exp(s_n - m)
    l = jnp.sum(p_c, axis=-1, keepdims=True) + jnp.sum(p_n, axis=-1, keepdims=True)
    o = lax.dot_general(p_c.astype(BF16), vt_ref[...].astype(BF16), nt, preferred_element_type=F32)
    o = (o + jnp.dot(p_n.astype(BF16), vn, preferred_element_type=F32)) / l
    o = jnp.where(own, o, 0.0)
    o_ref[...] = jnp.concatenate(
        [jnp.sum(o[t * N_HEADS:(t + 1) * N_HEADS], axis=0, keepdims=True) for t in range(t_new)],
        axis=0).astype(o_ref.dtype)


def _attn_sample(q, k_new, v_new, cache_k, cache_v):
    nb, t_new, _ = q.shape
    w_buf = cache_k.shape[1]
    bias, bias_new = _sample_bias(t_new, w_buf)
    as_stored = lambda a: a.transpose(0, 2, 3, 1).reshape(nb, D_ATTN, w_buf)
    new = pl.BlockSpec((None, t_new, D_ATTN), lambda b: (b, 0, 0))
    buf = pl.BlockSpec((None, D_ATTN, w_buf), lambda b: (b, 0, 0))
    const = lambda a: pl.BlockSpec(a.shape, lambda b: (0, 0))
    return pl.pallas_call(
        _attn_sample_kernel,
        grid=(nb,),
        in_specs=[new, new, new, buf, buf, const(bias), const(bias_new)],
        out_specs=new,
        out_shape=jax.ShapeDtypeStruct((nb, t_new, D_ATTN), BF16),
        compiler_params=_cparams(("parallel",)),
    )(q, k_new, v_new, as_stored(cache_k), as_stored(cache_v), jnp.asarray(bias), jnp.asarray(bias_new))


def _s5_params(lam_re, lam_im, log_dt, b_re, b_im, c_re, c_im):
    f32 = F32
    dt = jnp.exp(log_dt.astype(f32))[:, None]
    lr, li = lam_re.astype(f32), lam_im.astype(f32)
    ea = jnp.exp(lr * dt)
    a_re, a_im = ea * jnp.cos(li * dt), ea * jnp.sin(li * dt)
    den = lr * lr + li * li
    co_re = ((a_re - 1.0) * lr + a_im * li) / den
    co_im = (a_im * lr - (a_re - 1.0) * li) / den
    bb_re = co_re[..., None] * b_re - co_im[..., None] * b_im
    bb_im = co_re[..., None] * b_im + co_im[..., None] * b_re
    eye = jnp.eye(8, dtype=f32)

    def b_blocks(bb):
        t = bb.reshape(4, 8, SSM_STATE, SSM_GROUP_CH)
        return jnp.einsum('ab,kapc->kacbp', eye, t).reshape(4, 8 * SSM_GROUP_CH, 8 * SSM_STATE)

    def c_blocks(cc):
        t = cc.reshape(4, 8, SSM_GROUP_CH, SSM_STATE)
        return jnp.einsum('ab,kacp->kbpac', eye, t).reshape(4, 8 * SSM_STATE, 8 * SSM_GROUP_CH)

    b_mat = jnp.concatenate([b_blocks(bb_re), b_blocks(bb_im)], axis=2).astype(BF16)
    c_mat = jnp.concatenate([c_blocks(c_re.astype(f32)), -c_blocks(c_im.astype(f32))], axis=1).astype(BF16)
    return a_re.reshape(-1), a_im.reshape(-1), b_mat, c_mat


def _gelu_tanh(y):
    return 0.5 * y * (1.0 + jnp.tanh(math.sqrt(2.0 / math.pi) * (y + 0.044715 * (y * y * y))))


def _s5_epilogue(y, u, d_ref, wglu_ref, bglu_ref, g_ref):
    y = y + d_ref[...] * u.astype(F32)
    z = _gelu_tanh(y)
    gate = jnp.dot(z.astype(BF16), wglu_ref[...], preferred_element_type=F32) + bglu_ref[...]
    out = z * jax.nn.sigmoid(gate)
    ms = jnp.mean(out * out, axis=-1, keepdims=True)
    return out * lax.rsqrt(ms + EPS) * g_ref[...]


def _s5_prompt_kernel(u_ref, are_ref, aim_ref, b_ref, c_ref, d_ref, wglu_ref, bglu_ref, g_ref,
                      o_ref, hre_ref, him_ref, scr, hst):
    nseq, tt, _ = u_ref.shape
    n_slab = D_SSM * SSM_STATE // SSM_GROUP_CH // LANES
    ti = pl.program_id(1)

    @pl.when(ti == 0)
    def _():
        hst[...] = jnp.zeros_like(hst)

    ub = u_ref[...].reshape(nseq * tt, D_SSM)
    for kc in range(4):
        bu = jnp.dot(ub[:, kc * LANES:(kc + 1) * LANES], b_ref[kc], preferred_element_type=F32)
        for part in range(2):
            for j in range(4):
                col = part * 512 + j * LANES
                for b in range(nseq):
                    scr[part * n_slab + 4 * kc + j, b * SSM_PITCH:b * SSM_PITCH + tt, :] = (
                        bu[b * tt:(b + 1) * tt, col:col + LANES])

    grp = 4
    for sg in range(n_slab // grp):
        slabs = [sg * grp + i for i in range(grp)]
        ar = [are_ref[s] for s in slabs]
        ai = [aim_ref[s] for s in slabs]

        def step(t, carry, slabs=slabs, ar=ar, ai=ai):
            hr, hi = carry
            nhr, nhi = [], []
            for i, s in enumerate(slabs):
                sel = pl.ds(t, nseq, stride=SSM_PITCH)
                br = scr[s, sel, :]
                bi = scr[n_slab + s, sel, :]
                r = ar[i] * hr[i] - ai[i] * hi[i] + br
                im = ar[i] * hi[i] + ai[i] * hr[i] + bi
                scr[s, sel, :] = r
                scr[n_slab + s, sel, :] = im
                nhr.append(r)
                nhi.append(im)
            return tuple(nhr), tuple(nhi)

        init = (tuple(hst[s] for s in slabs), tuple(hst[n_slab + s] for s in slabs))
        hr, hi = lax.fori_loop(0, tt, step, init, unroll=4)
        for i, s in enumerate(slabs):
            hst[s] = hr[i]
            hst[n_slab + s] = hi[i]

    ys = []
    for kc in range(4):
        slabs = [4 * kc + j for j in range(4)] + [n_slab + 4 * kc + j for j in range(4)]
        lhs = jnp.concatenate(
            [jnp.concatenate([scr[s, b * SSM_PITCH:b * SSM_PITCH + tt, :].astype(BF16) for s in slabs], axis=1)
             for b in range(nseq)], axis=0)
        ys.append(jnp.dot(lhs, c_ref[kc], preferred_element_type=F32))
    y = jnp.concatenate(ys, axis=1)
    out = _s5_epilogue(y, ub, d_ref, wglu_ref, bglu_ref, g_ref)
    o_ref[...] = out.reshape(nseq, tt, D_SSM).astype(o_ref.dtype)

    @pl.when(ti == pl.num_programs(1) - 1)
    def _():
        for s in range(n_slab):
            hre_ref[:, s * LANES:(s + 1) * LANES] = hst[s]
            him_ref[:, s * LANES:(s + 1) * LANES] = hst[n_slab + s]


def _s5_prompt(u, a_re, a_im, b_mat, c_mat, d, w_glu_bf16, b_glu, g_out):
    nb, s, _ = u.shape
    nseq = SUBLANES
    assert nb % nseq == 0 and s % SSM_TT == 0
    n_state = a_re.shape[0]
    n_slab = n_state // LANES
    bcast = lambda a: jnp.broadcast_to(a.reshape(n_slab, 1, LANES), (n_slab, nseq, LANES))
    const = lambda a: pl.BlockSpec(a.shape, lambda b, i: (0,) * a.ndim)
    args = (u, bcast(a_re), bcast(a_im), b_mat, c_mat, d.reshape(1, D_SSM), w_glu_bf16,
            b_glu.reshape(1, D_SSM), g_out.reshape(1, D_SSM))
    st = jax.ShapeDtypeStruct((nb, n_state), F32)
    return pl.pallas_call(
        _s5_prompt_kernel,
        grid=(nb // nseq, s // SSM_TT),
        in_specs=[pl.BlockSpec((nseq, SSM_TT, D_SSM), lambda b, i: (b, i, 0))] + [const(a) for a in args[1:]],
        out_specs=[pl.BlockSpec((nseq, SSM_TT, D_SSM), lambda b, i: (b, i, 0)),
                   pl.BlockSpec((nseq, n_state), lambda b, i: (b, 0)),
                   pl.BlockSpec((nseq, n_state), lambda b, i: (b, 0))],
        out_shape=[jax.ShapeDtypeStruct((nb, s, D_SSM), BF16), st, st],
        scratch_shapes=[pltpu.VMEM((2 * n_slab, nseq * SSM_PITCH, LANES), F32),
                        pltpu.VMEM((2 * n_slab, nseq, LANES), F32)],
        compiler_params=_cparams(("parallel", "arbitrary")),
    )(*args)


def _s5_sample_kernel(u_ref, h0re_ref, h0im_ref, are_ref, aim_ref, b_ref, c_ref, d_ref, wglu_ref, bglu_ref, g_ref,
                      o_ref, hre_ref, him_ref):
    t_new = u_ref.shape[0]
    hre = h0re_ref[...]
    him = h0im_ref[...]
    are, aim = are_ref[...], aim_ref[...]
    half = 4 * LANES
    for t in range(t_new):
        ub = u_ref[t]
        bus = [jnp.dot(ub[:, kc * LANES:(kc + 1) * LANES], b_ref[kc], preferred_element_type=F32) for kc in range(4)]
        bre = jnp.concatenate([bu[:, :half] for bu in bus], axis=1)
        bim = jnp.concatenate([bu[:, half:] for bu in bus], axis=1)
        hre, him = are * hre - aim * him + bre, are * him + aim * hre + bim
        ys = []
        for kc in range(4):
            lhs = jnp.concatenate([hre[:, kc * half:(kc + 1) * half], him[:, kc * half:(kc + 1) * half]], axis=1)
            ys.append(jnp.dot(lhs.astype(BF16), c_ref[kc], preferred_element_type=F32))
        y = jnp.concatenate(ys, axis=1)
        o_ref[t] = _s5_epilogue(y, ub, d_ref, wglu_ref, bglu_ref, g_ref).astype(o_ref.dtype)
    hre_ref[...] = hre
    him_ref[...] = him


def _s5_sample(u_tm, h0_re, h0_im, a_re, a_im, b_mat, c_mat, d, w_glu_bf16, b_glu, g_out):
    t_new, nb, _ = u_tm.shape
    n_state = a_re.shape[0]
    tb = 64
    assert nb % tb == 0
    const = lambda a: pl.BlockSpec(a.shape, lambda b: (0,) * a.ndim)
    args = (u_tm, h0_re, h0_im, a_re.reshape(1, n_state), a_im.reshape(1, n_state), b_mat, c_mat,
            d.reshape(1, D_SSM), w_glu_bf16, b_glu.reshape(1, D_SSM), g_out.reshape(1, D_SSM))
    st_spec = pl.BlockSpec((tb, n_state), lambda b: (b, 0))
    st = jax.ShapeDtypeStruct((nb, n_state), F32)
    return pl.pallas_call(
        _s5_sample_kernel,
        grid=(nb // tb,),
        in_specs=[pl.BlockSpec((t_new, tb, D_SSM), lambda b: (0, b, 0)), st_spec, st_spec]
                 + [const(a) for a in args[3:]],
        out_specs=[pl.BlockSpec((t_new, tb, D_SSM), lambda b: (0, b, 0)), st_spec, st_spec],
        out_shape=[jax.ShapeDtypeStruct((t_new, nb, D_SSM), BF16), st, st],
        compiler_params=_cparams(("parallel",)),
    )(*args)


def _store_packed(ref, val):
    half = val.shape[1] // 2
    bits = pltpu.bitcast(val.astype(BF16).astype(F32), jnp.uint32)
    words = bits[:, :half] | lax.shift_right_logical(bits[:, half:], jnp.uint32(16))
    for c in range(PACK_CHUNKS):
        ref[c] = words[:, c * LANES:(c + 1) * LANES]


def _load_packed(ref):
    hi, lo = [], []
    for c in range(PACK_CHUNKS):
        w = ref[c]
        hi.append(pltpu.bitcast(w & jnp.uint32(0xFFFF0000), F32))
        lo.append(pltpu.bitcast(lax.shift_left(w, jnp.uint32(16)), F32))
    return jnp.concatenate(hi + lo, axis=1)


def _route_tile(lg, tri_ref, carry):
    lane = lax.broadcasted_iota(jnp.int32, lg.shape, 1)
    lane_f = lane.astype(F32)
    none = float(ROUTER_COLS)
    ninf = float("-inf")
    first = lambda cond: jnp.min(jnp.where(cond, lane_f, none), axis=-1, keepdims=True)

    is_c = lane < N_EXPERT_GROUPS
    lc = jnp.where(is_c, lg, ninf)
    mc = jnp.max(lc, axis=-1, keepdims=True)
    p_grp = 1.0 / jnp.sum(jnp.exp(lc - mc), axis=-1, keepdims=True)
    grp = first(lc == mc)
    fine = lane - N_EXPERT_GROUPS
    fine_grp = lax.shift_right_arithmetic(fine, jnp.int32(EXPERTS_PER_GROUP.bit_length() - 1))
    in_grp = (fine >= 0) & (fine < N_EXPERTS) & (fine_grp.astype(F32) == grp)
    lf = jnp.where(in_grp, lg, ninf)
    v1 = jnp.max(lf, axis=-1, keepdims=True)
    i1 = first(lf == v1)
    lf2 = jnp.where(lane_f == i1, ninf, lf)
    v2 = jnp.max(lf2, axis=-1, keepdims=True)
    i2 = first(lf2 == v2)
    b = jnp.exp(v2 - v1)
    g0 = p_grp / (1.0 + b)
    g1 = p_grp * b / (1.0 + b)

    hit1 = lane_f == i1
    hit2 = lane_f == i2
    onehot = jnp.where(hit1 | hit2, 1.0, 0.0)
    before = jnp.dot(tri_ref[...], onehot.astype(BF16), preferred_element_type=F32) + carry[0:1, :]
    r1 = jnp.sum(jnp.where(hit1, before, 0.0), axis=-1, keepdims=True)
    r2 = jnp.sum(jnp.where(hit2, before, 0.0), axis=-1, keepdims=True)
    carry[...] = carry[...] + jnp.sum(onehot, axis=0, keepdims=True)

    as_int = lambda v: jnp.broadcast_to(v, lg.shape).astype(jnp.int32)
    as_bits = lambda v: pltpu.bitcast(jnp.broadcast_to(v, lg.shape), jnp.int32)
    info = jnp.zeros(lg.shape, jnp.int32)
    fields = ((INFO_EXPERT, as_int(i1 - N_EXPERT_GROUPS)), (INFO_EXPERT + 1, as_int(i2 - N_EXPERT_GROUPS)),
              (INFO_RANK, as_int(r1)), (INFO_RANK + 1, as_int(r2)), (INFO_GATE, as_bits(g0)),
              (INFO_GATE + 1, as_bits(g1)))
    for col, val in fields:
        info = jnp.where(lane == col, val, info)
    return info


def _outproj_kernel(x_ref, oa_ref, os_ref, gate_ref, sh_ref, sc_ref, ga_ref, gf_ref, w_ref, wr_hi_ref, wr_lo_ref,
                    br_ref, tri_ref, cnt_in_ref, *rest):
    x1_ref, h_ref, info_ref, rec_ref, cnt_ref, carry = rest[-6:]
    step = pl.program_id(0) * pl.num_programs(1) + pl.program_id(1)

    @pl.when(step == 0)
    def _():
        carry[...] = cnt_in_ref[...]

    oa = oa_ref[...].astype(F32)
    ms = jnp.mean(oa * oa, axis=-1, keepdims=True)
    na = oa * lax.rsqrt(ms + EPS) * ga_ref[...]
    merged = jnp.concatenate([na.astype(BF16), os_ref[...]], axis=-1)
    x1 = x_ref[...] + gate_ref[...] * jnp.dot(merged, w_ref[...], preferred_element_type=F32)
    x1_ref[...] = x1
    ms = jnp.mean(x1 * x1, axis=-1, keepdims=True)
    h = x1 * lax.rsqrt(ms + EPS) * gf_ref[...]
    h = h * (1.0 + sc_ref[...]) + sh_ref[...]
    h_hi = h.astype(BF16)
    _store_packed(h_ref, h)
    h_lo = (h - h_hi.astype(F32)).astype(BF16)
    lg = jnp.dot(h_hi, wr_hi_ref[...], preferred_element_type=F32)
    lg = lg + jnp.dot(h_hi, wr_lo_ref[...], preferred_element_type=F32)
    lg = lg + jnp.dot(h_lo, wr_hi_ref[...], preferred_element_type=F32)
    info = _route_tile(lg + br_ref[...], tri_ref, carry)
    info_ref[...] = info
    rec_ref[...] = pltpu.bitcast(pltpu.bitcast(info, F32).T[:SUBLANES], jnp.int32)

    @pl.when(step == pl.num_programs(0) * pl.num_programs(1) - 1)
    def _():
        cnt_ref[...] = carry[...]


def _outproj(x, o_attn, o_ssm, gate, shift, scale, g_attn, g_ffn, w_out_bf16, w_router, b_router, tm, t_all, row0,
             shared, counts):
    nb, s, d = x.shape
    assert row0 % tm == 0
    blk = lambda b, i: row0 // tm + b * (s // tm) + i
    row = lambda n: pl.BlockSpec((None, tm, n), lambda b, i: (b, i, 0))
    const = lambda a: pl.BlockSpec(a.shape, lambda b, i: (0,) * a.ndim)
    wr_hi = w_router.astype(BF16)
    wr_lo = (w_router - wr_hi.astype(F32)).astype(BF16)
    tri = jnp.asarray(np.tril(np.ones((tm, tm), np.float32), -1), BF16)
    consts = (g_attn.reshape(1, D_ATTN), g_ffn.reshape(1, d), w_out_bf16, wr_hi, wr_lo,
              b_router.reshape(1, ROUTER_COLS), tri, counts)
    n_in = 6 + len(consts)
    count_spec = pl.BlockSpec((SUBLANES, ROUTER_COLS), lambda b, i: (0, 0))
    out = pl.pallas_call(
        _outproj_kernel,
        grid=(nb, s // tm),
        in_specs=[row(d), row(D_ATTN), row(D_SSM), _mod_spec(gate, tm), _mod_spec(shift, tm), _mod_spec(scale, tm)]
                 + [const(a) for a in consts] + [pl.BlockSpec(memory_space=pl.ANY)] * len(shared or ()),
        out_specs=[row(d), pl.BlockSpec((PACK_CHUNKS, tm, LANES), lambda b, i: (0, blk(b, i), 0)),
                   pl.BlockSpec((tm, ROUTER_COLS), lambda b, i: (blk(b, i), 0)),
                   pl.BlockSpec((SUBLANES, tm), lambda b, i: (0, blk(b, i))), count_spec],
        out_shape=[jax.ShapeDtypeStruct((nb, s, d), F32),
                   jax.ShapeDtypeStruct((PACK_CHUNKS, t_all, LANES), jnp.uint32),
                   jax.ShapeDtypeStruct((t_all, ROUTER_COLS), jnp.int32),
                   jax.ShapeDtypeStruct((SUBLANES, t_all), jnp.int32),
                   jax.ShapeDtypeStruct((SUBLANES, ROUTER_COLS), F32)],
        scratch_shapes=[pltpu.VMEM((SUBLANES, ROUTER_COLS), F32)],
        input_output_aliases={n_in + j: 1 + j for j in range(len(shared or ()))},
        compiler_params=_cparams(("arbitrary", "arbitrary")),
    )(x, o_attn, o_ssm, gate, shift, scale, *consts, *(shared or ()))
    return out[0], tuple(out[1:4]), out[4]


def _expert_kernel(blk_e_ref, nvalid_ref, x_ref, wg_ref, wu_ref, wd_ref, y_ref, wg_s, wu_s, wd_s):
    i = pl.program_id(0)

    @pl.when((i == 0) | (blk_e_ref[i] != blk_e_ref[jnp.maximum(i - 1, 0)]))
    def _():
        wg_s[...] = wg_ref[...].astype(BF16)
        wu_s[...] = wu_ref[...].astype(BF16)
        wd_s[...] = wd_ref[...].astype(BF16)

    @pl.when(i < nvalid_ref[0])
    def _():
        x = _load_packed(x_ref).astype(BF16)
        a = jnp.dot(x, wg_s[...], preferred_element_type=F32)
        b = jnp.dot(x, wu_s[...], preferred_element_type=F32)
        hid = (a * jax.nn.sigmoid(a) * b).astype(BF16)
        _store_packed(y_ref, jnp.dot(hid, wd_s[...], preferred_element_type=F32))

    @pl.when(i >= nvalid_ref[0])
    def _():
        y_ref[...] = jnp.zeros_like(y_ref)


def _experts(xs, blk_e, nvalid, w_gate, w_up, w_down):
    nblk = xs.shape[1] // MOE_TILE
    d = w_gate.shape[1]
    wspec = lambda shp: pl.BlockSpec((None,) + shp, lambda i, be, nv: (be[i], 0, 0))
    tile = pl.BlockSpec((PACK_CHUNKS, MOE_TILE, LANES), lambda i, be, nv: (0, i, 0))
    return pl.pallas_call(
        _expert_kernel,
        grid_spec=pltpu.PrefetchScalarGridSpec(
            num_scalar_prefetch=2,
            grid=(nblk,),
            in_specs=[tile, wspec((d, D_EXPERT)), wspec((d, D_EXPERT)), wspec((D_EXPERT, d))],
            out_specs=tile,
            scratch_shapes=[pltpu.VMEM((d, D_EXPERT), BF16), pltpu.VMEM((d, D_EXPERT), BF16),
                            pltpu.VMEM((D_EXPERT, d), BF16)],
        ),
        out_shape=jax.ShapeDtypeStruct(xs.shape, jnp.uint32),
        compiler_params=_cparams(("arbitrary",)),
    )(blk_e, nvalid, xs, w_gate, w_up, w_down)


def _combine_kernel(x1_ref, y0_ref, y1_ref, info_ref, gate_ref, gfin_ref, o_ref):
    info = pltpu.bitcast(info_ref[...], F32)
    g0 = info[:, INFO_GATE:INFO_GATE + 1]
    g1 = info[:, INFO_GATE + 1:INFO_GATE + 2]
    moe = g0 * _load_packed(y0_ref) + g1 * _load_packed(y1_ref)
    x2 = x1_ref[...] + gate_ref[...] * moe
    ms = jnp.mean(x2 * x2, axis=-1, keepdims=True)
    o_ref[...] = x2 * lax.rsqrt(ms + EPS) * gfin_ref[...]


def _combine(x1, y01, info, gate, g_final, tm, row0):
    nb, s, d = x1.shape
    assert row0 % tm == 0
    blk = lambda b, i: row0 // tm + b * (s // tm) + i
    row = lambda n: pl.BlockSpec((None, tm, n), lambda b, i: (b, i, 0))
    packed = lambda k: pl.BlockSpec((None, PACK_CHUNKS, tm, LANES), lambda b, i: (k, 0, blk(b, i), 0))
    return pl.pallas_call(
        _combine_kernel,
        grid=(nb, s // tm),
        in_specs=[row(d), packed(0), packed(1), pl.BlockSpec((tm, ROUTER_COLS), lambda b, i: (blk(b, i), 0)),
                  _mod_spec(gate, tm), pl.BlockSpec((1, d), lambda b, i: (0, 0))],
        out_specs=row(d),
        out_shape=jax.ShapeDtypeStruct((nb, s, d), F32),
        compiler_params=_cparams(("parallel", "parallel")),
    )(x1, y01, y01, info, gate, g_final.reshape(1, d))


def _sc_window(n_rows):
    assert n_rows % SC_WINDOW == 0
    return SC_WINDOW


def _sc_gather_rows(table, idx):
    n = idx.shape[0]
    w = _sc_window(n)
    mesh = plsc.VectorSubcoreMesh(core_axis_name="core", subcore_axis_name="subcore")

    @functools.partial(pl.kernel, out_type=jax.ShapeDtypeStruct((n, LANES), table.dtype), mesh=mesh)
    def gather_kernel(x_hbm, i_hbm, o_hbm):
        def body(i_vmem, o_vmem):
            pltpu.sync_copy(x_hbm.at[i_vmem.at[0]], o_vmem)

        pltpu.emit_pipeline(
            body, grid=(n // w,),
            in_specs=[pl.BlockSpec((1, w), lambda i: (0, i))],
            out_specs=[pl.BlockSpec((w, LANES), lambda i: (i, 0))],
            core_axis_name=("core", "subcore"), dimension_semantics=(pltpu.PARALLEL,),
        )(i_hbm, o_hbm)

    return gather_kernel(table, idx.reshape(1, n))


def _sc_scatter_rows(rows, idx0, idx1, n_out):
    n = rows.shape[0]
    w = _sc_window(n)
    mesh = plsc.VectorSubcoreMesh(core_axis_name="core", subcore_axis_name="subcore")

    @functools.partial(pl.kernel, out_type=jax.ShapeDtypeStruct((n_out, LANES), rows.dtype), mesh=mesh)
    def scatter_kernel(x_hbm, i0_hbm, i1_hbm, o_hbm):
        def body(x_vmem, i0_vmem, i1_vmem):
            pltpu.sync_copy(x_vmem, o_hbm.at[i0_vmem.at[0]])
            pltpu.sync_copy(x_vmem, o_hbm.at[i1_vmem.at[0]])

        pltpu.emit_pipeline(
            body, grid=(n // w,),
            in_specs=[pl.BlockSpec((w, LANES), lambda i: (i, 0)),
                      pl.BlockSpec((1, w), lambda i: (0, i)),
                      pl.BlockSpec((1, w), lambda i: (0, i))],
            out_specs=[],
            core_axis_name=("core", "subcore"), dimension_semantics=(pltpu.PARALLEL,),
        )(x_hbm, i0_hbm, i1_hbm)

    return scatter_kernel(rows, idx0.reshape(1, n), idx1.reshape(1, n))


def _moe(h_packed, rec, counts, w_gate, w_up, w_down):
    t = rec.shape[1]
    padded = (counts + MOE_TILE - 1) // MOE_TILE * MOE_TILE
    pend = jnp.cumsum(padded)
    pstart = pend - padded
    nblk = -(-2 * t // MOE_TILE) + N_EXPERTS
    n_slots = nblk * MOE_TILE
    blk_start = jnp.arange(nblk, dtype=jnp.int32) * MOE_TILE
    blk_e = jnp.minimum(jnp.sum(blk_start[:, None] >= pend[None, :], axis=1), N_EXPERTS - 1).astype(jnp.int32)
    nvalid = (pend[-1] // MOE_TILE).astype(jnp.int32).reshape(1)
    chunk_base = (jnp.arange(PACK_CHUNKS, dtype=jnp.int32) * n_slots)[:, None]
    rows = lambda k: (chunk_base + (jnp.take(pstart, rec[INFO_EXPERT + k]).astype(jnp.int32)
                                    + rec[INFO_RANK + k])[None, :]).reshape(-1)
    idx0, idx1 = rows(0), rows(1)
    xs = _sc_scatter_rows(h_packed.reshape(PACK_CHUNKS * t, LANES), idx0, idx1, PACK_CHUNKS * n_slots)
    ys = _experts(xs.reshape(PACK_CHUNKS, n_slots, LANES), blk_e, nvalid, w_gate, w_up, w_down)
    y01 = _sc_gather_rows(ys.reshape(PACK_CHUNKS * n_slots, LANES), jnp.concatenate([idx0, idx1]))
    return y01.reshape(2, PACK_CHUNKS, t, LANES)


def _mixer(x, mods_mix, mods_ffn, attn_fn, s5_fn, wts, tm, t_all, row0, shared, counts):
    q, k, v, u = _inproj(x, mods_mix[0], mods_mix[1], wts['g_mix'], wts['w_in'], tm)
    o_attn = attn_fn(q, k, v)
    o_ssm, h_re, h_im = s5_fn(u)
    x1, shared, counts = _outproj(x, o_attn, o_ssm, mods_mix[2], mods_ffn[0], mods_ffn[1], wts['g_attn_out'],
                                  wts['g_ffn'], wts['w_out'], wts['w_router'], wts['b_router'], tm, t_all, row0,
                                  shared, counts)
    return x1, shared, counts, (k, v, h_re, h_im)


def kernel(x_prompt, x_sample, cache_k_win, cache_v_win, state_ssm_re, state_ssm_im, c_prompt, c_sample, g_mix, w_ada_mix, b_ada_mix, w_in, w_out, g_attn_out, g_ssm_out, ssm_lambda_re, ssm_lambda_im, ssm_log_dt, ssm_b_re, ssm_b_im, ssm_c_re, ssm_c_im, ssm_d, w_glu, b_glu, g_ffn, w_ada_ffn, b_ada_ffn, w_router_coarse, b_router_coarse, w_router_fine, b_router_fine, w_expert_gate, w_expert_up, w_expert_down, g_final):
    depth = g_mix.shape[0]
    assert depth == 1, "single-layer step"
    l = 0
    bp, sp, d = x_prompt.shape
    bs, ts, _ = x_sample.shape
    keep = min(max(w for w, _ in DILATED_PATTERNS), sp)

    c_all = jnp.concatenate([c_prompt, c_sample], axis=0).astype(F32)
    m_mix = _adaln(c_all, w_ada_mix[l], b_ada_mix[l])
    m_ffn = _adaln(c_all, w_ada_ffn[l], b_ada_ffn[l])

    def mods(m, lo, hi, per_token):
        parts = jnp.split(m[lo:hi], 3, axis=-1)
        if per_token:
            return tuple(jnp.repeat(p, ts, axis=0)[None] for p in parts)
        return tuple(p[:, None, :] for p in parts)

    pad = ROUTER_COLS - N_EXPERT_GROUPS - N_EXPERTS
    w_router = jnp.concatenate([w_router_coarse[l], w_router_fine[l], jnp.zeros((d, pad), F32)], axis=1)
    b_router = jnp.concatenate([b_router_coarse[l], b_router_fine[l], jnp.zeros((pad,), F32)])
    wts = {
        'g_mix': g_mix[l], 'w_in': w_in[l].astype(BF16), 'w_out': w_out[l].astype(BF16),
        'g_attn_out': g_attn_out[l], 'g_ffn': g_ffn[l], 'w_router': w_router, 'b_router': b_router,
        'w_gate': w_expert_gate[l], 'w_up': w_expert_up[l], 'w_down': w_expert_down[l],
    }
    a_re, a_im, b_mat, c_mat = _s5_params(ssm_lambda_re[l], ssm_lambda_im[l], ssm_log_dt[l], ssm_b_re[l],
                                          ssm_b_im[l], ssm_c_re[l], ssm_c_im[l])
    s5_w = (a_re, a_im, b_mat, c_mat, ssm_d[l].reshape(-1), w_glu[l].astype(BF16), b_glu[l], g_ssm_out[l])

    t_prompt, t_sample = bp * sp, bs * ts
    t_all = t_prompt + t_sample
    tm_p, tm_s = min(sp, ROW_TILE), min(t_sample, ROW_TILE)
    mods_ffn_p, mods_ffn_s = mods(m_ffn, 0, bp, False), mods(m_ffn, bp, bp + bs, True)

    x1p, shared, counts, (kp, vp, hrp, hip) = _mixer(
        x_prompt, mods(m_mix, 0, bp, False), mods_ffn_p, _attn_prompt, lambda u: _s5_prompt(u, *s5_w), wts,
        tm_p, t_all, 0, None, jnp.zeros((SUBLANES, ROUTER_COLS), F32))

    n_state = SSM_GROUPS * SSM_STATE
    ck = cache_k_win[l]
    cv = cache_v_win[l]

    def attn_s(q, k, v):
        r = lambda a: a.reshape(bs, ts, D_ATTN)
        return _attn_sample(r(q), r(k), r(v), ck, cv).reshape(1, bs * ts, D_ATTN)

    def s5_s(u):
        u_tm = u.reshape(bs, ts, D_SSM).transpose(1, 0, 2)
        o, hr, hi = _s5_sample(u_tm, state_ssm_re[l].reshape(bs, n_state), state_ssm_im[l].reshape(bs, n_state),
                               *s5_w)
        return o.transpose(1, 0, 2).reshape(1, bs * ts, D_SSM), hr, hi

    x1s, shared, counts, (ks, vs, hrs, his) = _mixer(
        x_sample.reshape(1, t_sample, d), mods(m_mix, bp, bp + bs, True), mods_ffn_s, attn_s, s5_s, wts,
        tm_s, t_all, t_prompt, shared, counts)

    h_packed, info, rec = shared
    counts = counts[0, N_EXPERT_GROUPS:N_EXPERT_GROUPS + N_EXPERTS].astype(jnp.int32)
    y01 = _moe(h_packed, rec, counts, wts['w_gate'], wts['w_up'], wts['w_down'])
    yp = _combine(x1p, y01, info, mods_ffn_p[2], g_final, tm_p, 0)
    ys = _combine(x1s, y01, info, mods_ffn_s[2], g_final, tm_s, t_prompt)

    heads = lambda a, b, s: a.reshape(1, b, s, N_HEADS, HEAD_DIM)
    state = lambda a, b: a.reshape(1, b, SSM_GROUPS, SSM_STATE)
    return (yp, ys.reshape(bs, ts, d),
            heads(kp[:, sp - keep:], bp, keep), heads(vp[:, sp - keep:], bp, keep), state(hrp, bp), state(hip, bp),
            heads(ks, bs, ts), heads(vs, bs, ts), state(hrs, bs), state(his, bs))
```

```python
import functools
import math

import numpy as np
import jax
import jax.numpy as jnp
from jax import lax
from jax.experimental import pallas as pl
from jax.experimental.pallas import tpu as pltpu
from jax.experimental.pallas import tpu_sc as plsc

F32 = jnp.float32
BF16 = jnp.bfloat16
HIGHEST = lax.Precision.HIGHEST

D_MODEL = 1024
D_ATTN = 512
D_SSM = 512
HEAD_DIM = 64
N_HEADS = 8
DILATED_PATTERNS = ((128, 1), (512, 4), (2048, 16))
SSM_GROUP_CH = 16
SSM_GROUPS = 32
SSM_STATE = 64
N_EXPERT_GROUPS = 4
EXPERTS_PER_GROUP = 8
N_EXPERTS = 32
D_EXPERT = 512
D_IN_PROJ = 3 * D_ATTN + D_SSM
EPS = 1e-6

LANES = 128
SUBLANES = 8
VMEM_LIMIT = 48 * 1024 * 1024

ATTN_BLOCK = 128
ATTN_UNROLL = 8
NEG_BIG = -1e30
SSM_TT = 128
SSM_PITCH = SSM_TT + 8
ROUTER_COLS = 128
MOE_TILE = 512
ROW_TILE = 512
SC_WINDOW = 128
PACK_CHUNKS = D_MODEL // 2 // LANES
INFO_EXPERT, INFO_RANK, INFO_GATE = 0, 2, 4


def _cparams(sem, vmem=VMEM_LIMIT):
    return pltpu.CompilerParams(dimension_semantics=sem, vmem_limit_bytes=vmem)


def _adaln_kernel(c_ref, w_ref, b_ref, o_ref):
    c = c_ref[...]
    s = c * jax.nn.sigmoid(c)
    o_ref[...] = jnp.dot(s, w_ref[...], precision=HIGHEST, preferred_element_type=F32) + b_ref[...]


def _adaln(c, w, b):
    r, d = c.shape
    n = w.shape[1]
    tn = 768
    return pl.pallas_call(
        _adaln_kernel,
        grid=(n // tn,),
        in_specs=[pl.BlockSpec((r, d), lambda j: (0, 0)),
                  pl.BlockSpec((d, tn), lambda j: (0, j)),
                  pl.BlockSpec((1, tn), lambda j: (0, j))],
        out_specs=pl.BlockSpec((r, tn), lambda j: (0, j)),
        out_shape=jax.ShapeDtypeStruct((r, n), F32),
        compiler_params=_cparams(("arbitrary",)),
    )(c, w, b.reshape(1, n))


def _inproj_kernel(x_ref, sh_ref, sc_ref, g_ref, w_ref, q_ref, k_ref, v_ref, u_ref):
    x = x_ref[...]
    ms = jnp.mean(x * x, axis=-1, keepdims=True)
    h = x * lax.rsqrt(ms + EPS) * g_ref[...]
    h = h * (1.0 + sc_ref[...]) + sh_ref[...]
    p = jnp.dot(h.astype(BF16), w_ref[...], preferred_element_type=F32)
    q_ref[...] = p[:, :D_ATTN] * (HEAD_DIM ** -0.5)
    k_ref[...] = p[:, D_ATTN:2 * D_ATTN]
    v_ref[...] = p[:, 2 * D_ATTN:3 * D_ATTN]
    u_ref[...] = p[:, 3 * D_ATTN:].astype(u_ref.dtype)


def _mod_spec(mod, tm):
    d = mod.shape[-1]
    if mod.shape[1] == 1:
        return pl.BlockSpec((None, 1, d), lambda b, i: (b, 0, 0))
    return pl.BlockSpec((None, tm, d), lambda b, i: (b, i, 0))


def _inproj(x, shift, scale, g, w_bf16, tm):
    nb, s, d = x.shape
    row = lambda n: pl.BlockSpec((None, tm, n), lambda b, i: (b, i, 0))
    out = jax.ShapeDtypeStruct((nb, s, D_ATTN), F32)
    return pl.pallas_call(
        _inproj_kernel,
        grid=(nb, s // tm),
        in_specs=[row(d), _mod_spec(shift, tm), _mod_spec(scale, tm),
                  pl.BlockSpec((1, d), lambda b, i: (0, 0)),
                  pl.BlockSpec((d, D_IN_PROJ), lambda b, i: (0, 0))],
        out_specs=[row(D_ATTN), row(D_ATTN), row(D_ATTN), row(D_SSM)],
        out_shape=[out, out, out, jax.ShapeDtypeStruct((nb, s, D_SSM), BF16)],
        compiler_params=_cparams(("parallel", "parallel")),
    )(x, shift, scale, g.reshape(1, d), w_bf16)


def _band_bias():
    qi = np.arange(ATTN_BLOCK)[:, None]
    kj = np.arange(ATTN_BLOCK)[None, :]
    cur = kj <= qi
    prev = kj >= qi
    to_bias = lambda m: np.tile(np.where(m, 0.0, NEG_BIG).astype(np.float32), (2, 1))
    return to_bias(np.concatenate([prev, cur], axis=1)), to_bias(cur)


def _attn_prompt_kernel(q_ref, k_ref, v_ref, bias2_ref, bias1_ref, o_ref, acc_scr, m_scr, l_scr):
    s_len = q_ref.shape[0]
    lane = lax.broadcasted_iota(jnp.int32, (1, LANES), 1)
    head0 = lane < HEAD_DIM

    def rows(ref, start, n, d):
        if d == 1:
            return ref[pl.ds(start, n), :]
        return ref[pl.ds(start, n, stride=d), :]

    def tiles(items, first):
        nk = ATTN_BLOCK if first else 2 * ATTN_BLOCK
        bias = bias1_ref[...] if first else bias2_ref[...]
        one = jnp.ones((), BF16)
        loaded = []
        for _, d, qstart in items:
            kstart = qstart if first else qstart - d * ATTN_BLOCK
            loaded.append((rows(q_ref, qstart, ATTN_BLOCK, d), rows(k_ref, kstart, nk, d),
                           rows(v_ref, kstart, nk, d)))
        results = []
        for qrows, krows, vrows in loaded:
            kb = krows.astype(BF16)
            vb = vrows.astype(BF16)
            q2 = jnp.concatenate([jnp.where(head0, qrows, 0.0), jnp.where(head0, 0.0, qrows)], axis=0).astype(BF16)
            s = lax.dot_general(q2, kb, (((1,), (1,)), ((), ())), preferred_element_type=F32) + bias
            m = jnp.max(s, axis=-1, keepdims=True)
            p = jnp.exp((s - m).astype(BF16))
            pv0 = jnp.dot(p[:ATTN_BLOCK], jnp.where(head0, vb, one), preferred_element_type=F32)
            pv1 = jnp.dot(p[ATTN_BLOCK:], jnp.where(head0, one, vb), preferred_element_type=F32)
            results.append((jnp.where(head0, pv0, pv1), jnp.where(head0, m[:ATTN_BLOCK], m[ATTN_BLOCK:]),
                            jnp.where(head0, pv1, pv0)))
        for (br, d, qstart), (acc, m, l) in zip(items, results):
            dst = pl.ds(qstart, ATTN_BLOCK) if d == 1 else pl.ds(qstart, ATTN_BLOCK, stride=d)
            acc_scr[br, dst, :] = acc
            m_scr[br, dst, :] = m
            l_scr[br, dst, :] = l

    def largest_divisor(n):
        return max(f for f in range(1, ATTN_UNROLL + 1) if n % f == 0)

    grouped = []
    for br, (window, d) in enumerate(DILATED_PATTERNS):
        assert window // d == ATTN_BLOCK
        if d <= ATTN_UNROLL:
            grouped += [(br, d, r) for r in range(d)]
        else:
            un = largest_divisor(d)

            def first_body(i, carry, br=br, d=d, un=un):
                tiles([(br, d, i * un + j) for j in range(un)], True)
                return carry

            lax.fori_loop(0, d // un, first_body, 0)
    for i in range(0, len(grouped), ATTN_UNROLL):
        tiles(grouped[i:i + ATTN_UNROLL], True)

    for br, (window, d) in enumerate(DILATED_PATTERNS):
        nblk = s_len // (d * ATTN_BLOCK)
        n_rest = d * (nblk - 1)
        if n_rest:
            un = largest_divisor(n_rest)

            def rest_body(i, carry, br=br, d=d, nblk=nblk, un=un):
                items = []
                for j in range(un):
                    n = i * un + j
                    qstart = n // (nblk - 1) + d * ATTN_BLOCK * (1 + n % (nblk - 1))
                    items.append((br, d, pl.multiple_of(qstart, ATTN_BLOCK) if d == 1 else qstart))
                tiles(items, False)
                return carry

            lax.fori_loop(0, n_rest // un, rest_body, 0)

    ch = 256

    def merge_body(i, carry):
        sl = pl.ds(pl.multiple_of(i * ch, ch), ch)
        ms = [m_scr[br, sl, :] for br in range(3)]
        mx = jnp.maximum(jnp.maximum(ms[0], ms[1]), ms[2])
        num = jnp.zeros((ch, LANES), F32)
        den = jnp.zeros((ch, LANES), F32)
        for br in range(3):
            e = jnp.exp(ms[br] - mx)
            num = num + e * acc_scr[br, sl, :]
            den = den + e * pltpu.roll(l_scr[br, sl, :], HEAD_DIM, 1)
        o_ref[sl, :] = (num / den).astype(o_ref.dtype)
        return carry

    lax.fori_loop(0, s_len // ch, merge_body, 0)


def _attn_prompt(q, k, v):
    nb, s, _ = q.shape
    assert s % (16 * ATTN_BLOCK) == 0
    bias2, bias1 = _band_bias()
    blk = pl.BlockSpec((None, s, LANES), lambda b, p: (b, 0, p))
    const = lambda a: pl.BlockSpec(a.shape, lambda b, p: (0, 0))
    return pl.pallas_call(
        _attn_prompt_kernel,
        grid=(nb, D_ATTN // LANES),
        in_specs=[blk, blk, blk, const(bias2), const(bias1)],
        out_specs=blk,
        out_shape=jax.ShapeDtypeStruct((nb, s, D_ATTN), BF16),
        scratch_shapes=[pltpu.VMEM((3, s, LANES), F32)] * 3,
        compiler_params=_cparams(("parallel", "parallel")),
    )(q, k, v, jnp.asarray(bias2), jnp.asarray(bias1))


def _sample_bias(t_new, w_buf):
    n_new = -(-t_new // LANES) * LANES
    t = np.arange(t_new)[:, None]
    idx = np.arange(w_buf + n_new)[None, :]
    dist = w_buf + t - idx
    mult = np.zeros(dist.shape, np.int64)
    for window, d in DILATED_PATTERNS:
        mult += ((dist >= 0) & (dist % d == 0) & (dist <= window)).astype(np.int64)
    bias = np.where(mult > 0, np.log(np.maximum(mult, 1)), NEG_BIG).astype(np.float32)
    bias = np.repeat(bias, N_HEADS, axis=0)
    return bias[:, :w_buf], bias[:, w_buf:]


def _attn_sample_kernel(q_ref, kn_ref, vn_ref, kt_ref, vt_ref, bias_ref, biasn_ref, o_ref):
    t_new = q_ref.shape[0]
    n_new = biasn_ref.shape[1]
    row_head = lax.broadcasted_iota(jnp.int32, (t_new * N_HEADS, D_ATTN), 0) & (N_HEADS - 1)
    col_head = lax.broadcasted_iota(jnp.int32, (t_new * N_HEADS, D_ATTN), 1) >> (HEAD_DIM.bit_length() - 1)
    own = row_head == col_head
    q = q_ref[...]
    q_rows = jnp.concatenate([jnp.broadcast_to(q[t:t + 1], (N_HEADS, D_ATTN)) for t in range(t_new)], axis=0)
    q_bd = jnp.where(own, q_rows, 0.0).astype(BF16)
    pad = jnp.zeros((n_new - t_new, D_ATTN), F32)
    kn = jnp.concatenate([kn_ref[...], pad], axis=0).astype(BF16)
    vn = jnp.concatenate([vn_ref[...], pad], axis=0).astype(BF16)
    nt = (((1,), (1,)), ((), ()))
    s_c = jnp.dot(q_bd, kt_ref[...].astype(BF16), preferred_element_type=F32) + bias_ref[...]
    s_n = lax.dot_general(q_bd, kn, nt, preferred_element_type=F32) + biasn_ref[...]
    m = jnp.maximum(jnp.max(s_c, axis=-1, keepdims=True), jnp.max(s_n, axis=-1, keepdims=True))
    p_c = jnp.exp(s_c - m)
    p_n = jnp.exp(s_n - m)
    l = jnp.sum(p_c, axis=-1, keepdims=True) + jnp.sum(p_n, axis=-1, keepdims=True)
    o = lax.dot_general(p_c.astype(BF16), vt_ref[...].astype(BF16), nt, preferred_element_type=F32)
    o = (o + jnp.dot(p_n.astype(BF16), vn, preferred_element_type=F32)) / l
    o = jnp.where(own, o, 0.0)
    o_ref[...] = jnp.concatenate(
        [jnp.sum(o[t * N_HEADS:(t + 1) * N_HEADS], axis=0, keepdims=True) for t in range(t_new)],
        axis=0).astype(o_ref.dtype)


def _attn_sample(q, k_new, v_new, cache_k, cache_v):
    nb, t_new, _ = q.shape
    w_buf = cache_k.shape[1]
    bias, bias_new = _sample_bias(t_new, w_buf)
    as_stored = lambda a: a.transpose(0, 2, 3, 1).reshape(nb, D_ATTN, w_buf)
    new = pl.BlockSpec((None, t_new, D_ATTN), lambda b: (b, 0, 0))
    buf = pl.BlockSpec((None, D_ATTN, w_buf), lambda b: (b, 0, 0))
    const = lambda a: pl.BlockSpec(a.shape, lambda b: (0, 0))
    return pl.pallas_call(
        _attn_sample_kernel,
        grid=(nb,),
        in_specs=[new, new, new, buf, buf, const(bias), const(bias_new)],
        out_specs=new,
        out_shape=jax.ShapeDtypeStruct((nb, t_new, D_ATTN), BF16),
        compiler_params=_cparams(("parallel",)),
    )(q, k_new, v_new, as_stored(cache_k), as_stored(cache_v), jnp.asarray(bias), jnp.asarray(bias_new))


def _s5_params(lam_re, lam_im, log_dt, b_re, b_im, c_re, c_im):
    f32 = F32
    dt = jnp.exp(log_dt.astype(f32))[:, None]
    lr, li = lam_re.astype(f32), lam_im.astype(f32)
    ea = jnp.exp(lr * dt)
    a_re, a_im = ea * jnp.cos(li * dt), ea * jnp.sin(li * dt)
    den = lr * lr + li * li
    co_re = ((a_re - 1.0) * lr + a_im * li) / den
    co_im = (a_im * lr - (a_re - 1.0) * li) / den
    bb_re = co_re[..., None] * b_re - co_im[..., None] * b_im
    bb_im = co_re[..., None] * b_im + co_im[..., None] * b_re
    eye = jnp.eye(8, dtype=f32)

    def b_blocks(bb):
        t = bb.reshape(4, 8, SSM_STATE, SSM_GROUP_CH)
        return jnp.einsum('ab,kapc->kacbp', eye, t).reshape(4, 8 * SSM_GROUP_CH, 8 * SSM_STATE)

    def c_blocks(cc):
        t = cc.reshape(4, 8, SSM_GROUP_CH, SSM_STATE)
        return jnp.einsum('ab,kacp->kbpac', eye, t).reshape(4, 8 * SSM_STATE, 8 * SSM_GROUP_CH)

    b_mat = jnp.concatenate([b_blocks(bb_re), b_blocks(bb_im)], axis=2).astype(BF16)
    c_mat = jnp.concatenate([c_blocks(c_re.astype(f32)), -c_blocks(c_im.astype(f32))], axis=1).astype(BF16)
    return a_re.reshape(-1), a_im.reshape(-1), b_mat, c_mat


def _gelu_tanh(y):
    return 0.5 * y * (1.0 + jnp.tanh(math.sqrt(2.0 / math.pi) * (y + 0.044715 * (y * y * y))))


def _s5_epilogue(y, u, d_ref, wglu_ref, bglu_ref, g_ref):
    y = y + d_ref[...] * u.astype(F32)
    z = _gelu_tanh(y)
    gate = jnp.dot(z.astype(BF16), wglu_ref[...], preferred_element_type=F32) + bglu_ref[...]
    out = z * jax.nn.sigmoid(gate)
    ms = jnp.mean(out * out, axis=-1, keepdims=True)
    return out * lax.rsqrt(ms + EPS) * g_ref[...]


def _s5_prompt_kernel(u_ref, are_ref, aim_ref, b_ref, c_ref, d_ref, wglu_ref, bglu_ref, g_ref,
                      o_ref, hre_ref, him_ref, scr, hst):
    nseq, tt, _ = u_ref.shape
    n_slab = D_SSM * SSM_STATE // SSM_GROUP_CH // LANES
    ti = pl.program_id(1)

    @pl.when(ti == 0)
    def _():
        hst[...] = jnp.zeros_like(hst)

    ub = u_ref[...].reshape(nseq * tt, D_SSM)
    for kc in range(4):
        bu = jnp.dot(ub[:, kc * LANES:(kc + 1) * LANES], b_ref[kc], preferred_element_type=F32)
        for part in range(2):
            for j in range(4):
                col = part * 512 + j * LANES
                for b in range(nseq):
                    scr[part * n_slab + 4 * kc + j, b * SSM_PITCH:b * SSM_PITCH + tt, :] = (
                        bu[b * tt:(b + 1) * tt, col:col + LANES])

    grp = 4
    for sg in range(n_slab // grp):
        slabs = [sg * grp + i for i in range(grp)]
        ar = [are_ref[s] for s in slabs]
        ai = [aim_ref[s] for s in slabs]

        def step(t, carry, slabs=slabs, ar=ar, ai=ai):
            hr, hi = carry
            nhr, nhi = [], []
            for i, s in enumerate(slabs):
                sel = pl.ds(t, nseq, stride=SSM_PITCH)
                br = scr[s, sel, :]
                bi = scr[n_slab + s, sel, :]
                r = ar[i] * hr[i] - ai[i] * hi[i] + br
                im = ar[i] * hi[i] + ai[i] * hr[i] + bi
                scr[s, sel, :] = r
                scr[n_slab + s, sel, :] = im
                nhr.append(r)
                nhi.append(im)
            return tuple(nhr), tuple(nhi)

        init = (tuple(hst[s] for s in slabs), tuple(hst[n_slab + s] for s in slabs))
        hr, hi = lax.fori_loop(0, tt, step, init, unroll=4)
        for i, s in enumerate(slabs):
            hst[s] = hr[i]
            hst[n_slab + s] = hi[i]

    ys = []
    for kc in range(4):
        slabs = [4 * kc + j for j in range(4)] + [n_slab + 4 * kc + j for j in range(4)]
        lhs = jnp.concatenate(
            [jnp.concatenate([scr[s, b * SSM_PITCH:b * SSM_PITCH + tt, :].astype(BF16) for s in slabs], axis=1)
             for b in range(nseq)], axis=0)
        ys.append(jnp.dot(lhs, c_ref[kc], preferred_element_type=F32))
    y = jnp.concatenate(ys, axis=1)
    out = _s5_epilogue(y, ub, d_ref, wglu_ref, bglu_ref, g_ref)
    o_ref[...] = out.reshape(nseq, tt, D_SSM).astype(o_ref.dtype)

    @pl.when(ti == pl.num_programs(1) - 1)
    def _():
        for s in range(n_slab):
            hre_ref[:, s * LANES:(s + 1) * LANES] = hst[s]
            him_ref[:, s * LANES:(s + 1) * LANES] = hst[n_slab + s]


def _s5_prompt(u, a_re, a_im, b_mat, c_mat, d, w_glu_bf16, b_glu, g_out):
    nb, s, _ = u.shape
    nseq = SUBLANES
    assert nb % nseq == 0 and s % SSM_TT == 0
    n_state = a_re.shape[0]
    n_slab = n_state // LANES
    bcast = lambda a: jnp.broadcast_to(a.reshape(n_slab, 1, LANES), (n_slab, nseq, LANES))
    const = lambda a: pl.BlockSpec(a.shape, lambda b, i: (0,) * a.ndim)
    args = (u, bcast(a_re), bcast(a_im), b_mat, c_mat, d.reshape(1, D_SSM), w_glu_bf16,
            b_glu.reshape(1, D_SSM), g_out.reshape(1, D_SSM))
    st = jax.ShapeDtypeStruct((nb, n_state), F32)
    return pl.pallas_call(
        _s5_prompt_kernel,
        grid=(nb // nseq, s // SSM_TT),
        in_specs=[pl.BlockSpec((nseq, SSM_TT, D_SSM), lambda b, i: (b, i, 0))] + [const(a) for a in args[1:]],
        out_specs=[pl.BlockSpec((nseq, SSM_TT, D_SSM), lambda b, i: (b, i, 0)),
                   pl.BlockSpec((nseq, n_state), lambda b, i: (b, 0)),
                   pl.BlockSpec((nseq, n_state), lambda b, i: (b, 0))],
        out_shape=[jax.ShapeDtypeStruct((nb, s, D_SSM), BF16), st, st],
        scratch_shapes=[pltpu.VMEM((2 * n_slab, nseq * SSM_PITCH, LANES), F32),
                        pltpu.VMEM((2 * n_slab, nseq, LANES), F32)],
        compiler_params=_cparams(("parallel", "arbitrary")),
    )(*args)


def _s5_sample_kernel(u_ref, h0re_ref, h0im_ref, are_ref, aim_ref, b_ref, c_ref, d_ref, wglu_ref, bglu_ref, g_ref,
                      o_ref, hre_ref, him_ref):
    t_new = u_ref.shape[0]
    hre = h0re_ref[...]
    him = h0im_ref[...]
    are, aim = are_ref[...], aim_ref[...]
    half = 4 * LANES
    for t in range(t_new):
        ub = u_ref[t]
        bus = [jnp.dot(ub[:, kc * LANES:(kc + 1) * LANES], b_ref[kc], preferred_element_type=F32) for kc in range(4)]
        bre = jnp.concatenate([bu[:, :half] for bu in bus], axis=1)
        bim = jnp.concatenate([bu[:, half:] for bu in bus], axis=1)
        hre, him = are * hre - aim * him + bre, are * him + aim * hre + bim
        ys = []
        for kc in range(4):
            lhs = jnp.concatenate([hre[:, kc * half:(kc + 1) * half], him[:, kc * half:(kc + 1) * half]], axis=1)
            ys.append(jnp.dot(lhs.astype(BF16), c_ref[kc], preferred_element_type=F32))
        y = jnp.concatenate(ys, axis=1)
        o_ref[t] = _s5_epilogue(y, ub, d_ref, wglu_ref, bglu_ref, g_ref).astype(o_ref.dtype)
    hre_ref[...] = hre
    him_ref[...] = him


def _s5_sample(u_tm, h0_re, h0_im, a_re, a_im, b_mat, c_mat, d, w_glu_bf16, b_glu, g_out):
    t_new, nb, _ = u_tm.shape
    n_state = a_re.shape[0]
    tb = 64
    assert nb % tb == 0
    const = lambda a: pl.BlockSpec(a.shape, lambda b: (0,) * a.ndim)
    args = (u_tm, h0_re, h0_im, a_re.reshape(1, n_state), a_im.reshape(1, n_state), b_mat, c_mat,
            d.reshape(1, D_SSM), w_glu_bf16, b_glu.reshape(1, D_SSM), g_out.reshape(1, D_SSM))
    st_spec = pl.BlockSpec((tb, n_state), lambda b: (b, 0))
    st = jax.ShapeDtypeStruct((nb, n_state), F32)
    return pl.pallas_call(
        _s5_sample_kernel,
        grid=(nb // tb,),
        in_specs=[pl.BlockSpec((t_new, tb, D_SSM), lambda b: (0, b, 0)), st_spec, st_spec]
                 + [const(a) for a in args[3:]],
        out_specs=[pl.BlockSpec((t_new, tb, D_SSM), lambda b: (0, b, 0)), st_spec, st_spec],
        out_shape=[jax.ShapeDtypeStruct((t_new, nb, D_SSM), BF16), st, st],
        compiler_params=_cparams(("parallel",)),
    )(*args)


def _store_packed(ref, val):
    half = val.shape[1] // 2
    bits = pltpu.bitcast(val.astype(BF16).astype(F32), jnp.uint32)
    words = bits[:, :half] | lax.shift_right_logical(bits[:, half:], jnp.uint32(16))
    for c in range(PACK_CHUNKS):
        ref[c] = words[:, c * LANES:(c + 1) * LANES]


def _load_packed(ref):
    hi, lo = [], []
    for c in range(PACK_CHUNKS):
        w = ref[c]
        hi.append(pltpu.bitcast(w & jnp.uint32(0xFFFF0000), F32))
        lo.append(pltpu.bitcast(lax.shift_left(w, jnp.uint32(16)), F32))
    return jnp.concatenate(hi + lo, axis=1)


def _route_tile(lg, tri_ref, carry):
    lane = lax.broadcasted_iota(jnp.int32, lg.shape, 1)
    lane_f = lane.astype(F32)
    none = float(ROUTER_COLS)
    ninf = float("-inf")
    first = lambda cond: jnp.min(jnp.where(cond, lane_f, none), axis=-1, keepdims=True)

    is_c = lane < N_EXPERT_GROUPS
    lc = jnp.where(is_c, lg, ninf)
    mc = jnp.max(lc, axis=-1, keepdims=True)
    p_grp = 1.0 / jnp.sum(jnp.exp(lc - mc), axis=-1, keepdims=True)
    grp = first(lc == mc)
    fine = lane - N_EXPERT_GROUPS
    fine_grp = lax.shift_right_arithmetic(fine, jnp.int32(EXPERTS_PER_GROUP.bit_length() - 1))
    in_grp = (fine >= 0) & (fine < N_EXPERTS) & (fine_grp.astype(F32) == grp)
    lf = jnp.where(in_grp, lg, ninf)
    v1 = jnp.max(lf, axis=-1, keepdims=True)
    i1 = first(lf == v1)
    lf2 = jnp.where(lane_f == i1, ninf, lf)
    v2 = jnp.max(lf2, axis=-1, keepdims=True)
    i2 = first(lf2 == v2)
    b = jnp.exp(v2 - v1)
    g0 = p_grp / (1.0 + b)
    g1 = p_grp * b / (1.0 + b)

    hit1 = lane_f == i1
    hit2 = lane_f == i2
    onehot = jnp.where(hit1 | hit2, 1.0, 0.0)
    before = jnp.dot(tri_ref[...], onehot.astype(BF16), preferred_element_type=F32) + carry[0:1, :]
    r1 = jnp.sum(jnp.where(hit1, before, 0.0), axis=-1, keepdims=True)
    r2 = jnp.sum(jnp.where(hit2, before, 0.0), axis=-1, keepdims=True)
    carry[...] = carry[...] + jnp.sum(onehot, axis=0, keepdims=True)

    as_int = lambda v: jnp.broadcast_to(v, lg.shape).astype(jnp.int32)
    as_bits = lambda v: pltpu.bitcast(jnp.broadcast_to(v, lg.shape), jnp.int32)
    info = jnp.zeros(lg.shape, jnp.int32)
    fields = ((INFO_EXPERT, as_int(i1 - N_EXPERT_GROUPS)), (INFO_EXPERT + 1, as_int(i2 - N_EXPERT_GROUPS)),
              (INFO_RANK, as_int(r1)), (INFO_RANK + 1, as_int(r2)), (INFO_GATE, as_bits(g0)),
              (INFO_GATE + 1, as_bits(g1)))
    for col, val in fields:
        info = jnp.where(lane == col, val, info)
    return info


def _outproj_kernel(x_ref, oa_ref, os_ref, gate_ref, sh_ref, sc_ref, ga_ref, gf_ref, w_ref, wr_hi_ref, wr_lo_ref,
                    br_ref, tri_ref, cnt_in_ref, *rest):
    x1_ref, h_ref, info_ref, rec_ref, cnt_ref, carry = rest[-6:]
    step = pl.program_id(0) * pl.num_programs(1) + pl.program_id(1)

    @pl.when(step == 0)
    def _():
        carry[...] = cnt_in_ref[...]

    oa = oa_ref[...].astype(F32)
    ms = jnp.mean(oa * oa, axis=-1, keepdims=True)
    na = oa * lax.rsqrt(ms + EPS) * ga_ref[...]
    merged = jnp.concatenate([na.astype(BF16), os_ref[...]], axis=-1)
    x1 = x_ref[...] + gate_ref[...] * jnp.dot(merged, w_ref[...], preferred_element_type=F32)
    x1_ref[...] = x1
    ms = jnp.mean(x1 * x1, axis=-1, keepdims=True)
    h = x1 * lax.rsqrt(ms + EPS) * gf_ref[...]
    h = h * (1.0 + sc_ref[...]) + sh_ref[...]
    h_hi = h.astype(BF16)
    _store_packed(h_ref, h)
    h_lo = (h - h_hi.astype(F32)).astype(BF16)
    lg = jnp.dot(h_hi, wr_hi_ref[...], preferred_element_type=F32)
    lg = lg + jnp.dot(h_hi, wr_lo_ref[...], preferred_element_type=F32)
    lg = lg + jnp.dot(h_lo, wr_hi_ref[...], preferred_element_type=F32)
    info = _route_tile(lg + br_ref[...], tri_ref, carry)
    info_ref[...] = info
    rec_ref[...] = pltpu.bitcast(pltpu.bitcast(info, F32).T[:SUBLANES], jnp.int32)

    @pl.when(step == pl.num_programs(0) * pl.num_programs(1) - 1)
    def _():
        cnt_ref[...] = carry[...]


def _outproj(x, o_attn, o_ssm, gate, shift, scale, g_attn, g_ffn, w_out_bf16, w_router, b_router, tm, t_all, row0,
             shared, counts):
    nb, s, d = x.shape
    assert row0 % tm == 0
    blk = lambda b, i: row0 // tm + b * (s // tm) + i
    row = lambda n: pl.BlockSpec((None, tm, n), lambda b, i: (b, i, 0))
    const = lambda a: pl.BlockSpec(a.shape, lambda b, i: (0,) * a.ndim)
    wr_hi = w_router.astype(BF16)
    wr_lo = (w_router - wr_hi.astype(F32)).astype(BF16)
    tri = jnp.asarray(np.tril(np.ones((tm, tm), np.float32), -1), BF16)
    consts = (g_attn.reshape(1, D_ATTN), g_ffn.reshape(1, d), w_out_bf16, wr_hi, wr_lo,
              b_router.reshape(1, ROUTER_COLS), tri, counts)
    n_in = 6 + len(consts)
    count_spec = pl.BlockSpec((SUBLANES, ROUTER_COLS), lambda b, i: (0, 0))
    out = pl.pallas_call(
        _outproj_kernel,
        grid=(nb, s // tm),
        in_specs=[row(d), row(D_ATTN), row(D_SSM), _mod_spec(gate, tm), _mod_spec(shift, tm), _mod_spec(scale, tm)]
                 + [const(a) for a in consts] + [pl.BlockSpec(memory_space=pl.ANY)] * len(shared or ()),
        out_specs=[row(d), pl.BlockSpec((PACK_CHUNKS, tm, LANES), lambda b, i: (0, blk(b, i), 0)),
                   pl.BlockSpec((tm, ROUTER_COLS), lambda b, i: (blk(b, i), 0)),
                   pl.BlockSpec((SUBLANES, tm), lambda b, i: (0, blk(b, i))), count_spec],
        out_shape=[jax.ShapeDtypeStruct((nb, s, d), F32),
                   jax.ShapeDtypeStruct((PACK_CHUNKS, t_all, LANES), jnp.uint32),
                   jax.ShapeDtypeStruct((t_all, ROUTER_COLS), jnp.int32),
                   jax.ShapeDtypeStruct((SUBLANES, t_all), jnp.int32),
                   jax.ShapeDtypeStruct((SUBLANES, ROUTER_COLS), F32)],
        scratch_shapes=[pltpu.VMEM((SUBLANES, ROUTER_COLS), F32)],
        input_output_aliases={n_in + j: 1 + j for j in range(len(shared or ()))},
        compiler_params=_cparams(("arbitrary", "arbitrary")),
    )(x, o_attn, o_ssm, gate, shift, scale, *consts, *(shared or ()))
    return out[0], tuple(out[1:4]), out[4]


def _expert_kernel(blk_e_ref, nvalid_ref, x_ref, wg_ref, wu_ref, wd_ref, y_ref, wg_s, wu_s, wd_s):
    i = pl.program_id(0)

    @pl.when((i == 0) | (blk_e_ref[i] != blk_e_ref[jnp.maximum(i - 1, 0)]))
    def _():
        wg_s[...] = wg_ref[...].astype(BF16)
        wu_s[...] = wu_ref[...].astype(BF16)
        wd_s[...] = wd_ref[...].astype(BF16)

    @pl.when(i < nvalid_ref[0])
    def _():
        x = _load_packed(x_ref).astype(BF16)
        a = jnp.dot(x, wg_s[...], preferred_element_type=F32)
        b = jnp.dot(x, wu_s[...], preferred_element_type=F32)
        hid = (a * jax.nn.sigmoid(a) * b).astype(BF16)
        _store_packed(y_ref, jnp.dot(hid, wd_s[...], preferred_element_type=F32))

    @pl.when(i >= nvalid_ref[0])
    def _():
        y_ref[...] = jnp.zeros_like(y_ref)


def _experts(xs, blk_e, nvalid, w_gate, w_up, w_down):
    nblk = xs.shape[1] // MOE_TILE
    d = w_gate.shape[1]
    wspec = lambda shp: pl.BlockSpec((None,) + shp, lambda i, be, nv: (be[i], 0, 0))
    tile = pl.BlockSpec((PACK_CHUNKS, MOE_TILE, LANES), lambda i, be, nv: (0, i, 0))
    return pl.pallas_call(
        _expert_kernel,
        grid_spec=pltpu.PrefetchScalarGridSpec(
            num_scalar_prefetch=2,
            grid=(nblk,),
            in_specs=[tile, wspec((d, D_EXPERT)), wspec((d, D_EXPERT)), wspec((D_EXPERT, d))],
            out_specs=tile,
            scratch_shapes=[pltpu.VMEM((d, D_EXPERT), BF16), pltpu.VMEM((d, D_EXPERT), BF16),
                            pltpu.VMEM((D_EXPERT, d), BF16)],
        ),
        out_shape=jax.ShapeDtypeStruct(xs.shape, jnp.uint32),
        compiler_params=_cparams(("arbitrary",)),
    )(blk_e, nvalid, xs, w_gate, w_up, w_down)


def _combine_kernel(x1_ref, y0_ref, y1_ref, info_ref, gate_ref, gfin_ref, o_ref):
    info = pltpu.bitcast(info_ref[...], F32)
    g0 = info[:, INFO_GATE:INFO_GATE + 1]
    g1 = info[:, INFO_GATE + 1:INFO_GATE + 2]
    moe = g0 * _load_packed(y0_ref) + g1 * _load_packed(y1_ref)
    x2 = x1_ref[...] + gate_ref[...] * moe
    ms = jnp.mean(x2 * x2, axis=-1, keepdims=True)
    o_ref[...] = x2 * lax.rsqrt(ms + EPS) * gfin_ref[...]


def _combine(x1, y01, info, gate, g_final, tm, row0):
    nb, s, d = x1.shape
    assert row0 % tm == 0
    blk = lambda b, i: row0 // tm + b * (s // tm) + i
    row = lambda n: pl.BlockSpec((None, tm, n), lambda b, i: (b, i, 0))
    packed = lambda k: pl.BlockSpec((None, PACK_CHUNKS, tm, LANES), lambda b, i: (k, 0, blk(b, i), 0))
    return pl.pallas_call(
        _combine_kernel,
        grid=(nb, s // tm),
        in_specs=[row(d), packed(0), packed(1), pl.BlockSpec((tm, ROUTER_COLS), lambda b, i: (blk(b, i), 0)),
                  _mod_spec(gate, tm), pl.BlockSpec((1, d), lambda b, i: (0, 0))],
        out_specs=row(d),
        out_shape=jax.ShapeDtypeStruct((nb, s, d), F32),
        compiler_params=_cparams(("parallel", "parallel")),
    )(x1, y01, y01, info, gate, g_final.reshape(1, d))


def _sc_window(n_rows):
    assert n_rows % SC_WINDOW == 0
    return SC_WINDOW


def _sc_gather_rows(table, idx):
    n = idx.shape[0]
    w = _sc_window(n)
    mesh = plsc.VectorSubcoreMesh(core_axis_name="core", subcore_axis_name="subcore")

    @functools.partial(pl.kernel, out_type=jax.ShapeDtypeStruct((n, LANES), table.dtype), mesh=mesh)
    def gather_kernel(x_hbm, i_hbm, o_hbm):
        def body(i_vmem, o_vmem):
            pltpu.sync_copy(x_hbm.at[i_vmem.at[0]], o_vmem)

        pltpu.emit_pipeline(
            body, grid=(n // w,),
            in_specs=[pl.BlockSpec((1, w), lambda i: (0, i))],
            out_specs=[pl.BlockSpec((w, LANES), lambda i: (i, 0))],
            core_axis_name=("core", "subcore"), dimension_semantics=(pltpu.PARALLEL,),
        )(i_hbm, o_hbm)

    return gather_kernel(table, idx.reshape(1, n))


def _sc_scatter_rows(rows, idx0, idx1, n_out):
    n = rows.shape[0]
    w = _sc_window(n)
    mesh = plsc.VectorSubcoreMesh(core_axis_name="core", subcore_axis_name="subcore")

    @functools.partial(pl.kernel, out_type=jax.ShapeDtypeStruct((n_out, LANES), rows.dtype), mesh=mesh)
    def scatter_kernel(x_hbm, i0_hbm, i1_hbm, o_hbm):
        def body(x_vmem, i0_vmem, i1_vmem):
            pltpu.sync_copy(x_vmem, o_hbm.at[i0_vmem.at[0]])
            pltpu.sync_copy(x_vmem, o_hbm.at[i1_vmem.at[0]])

        pltpu.emit_pipeline(
            body, grid=(n // w,),
            in_specs=[pl.BlockSpec((w, LANES), lambda i: (i, 0)),
                      pl.BlockSpec((1, w), lambda i: (0, i)),
                      pl.BlockSpec((1, w), lambda i: (0, i))],
            out_specs=[],
            core_axis_name=("core", "subcore"), dimension_semantics=(pltpu.PARALLEL,),
        )(x_hbm, i0_hbm, i1_hbm)

    return scatter_kernel(rows, idx0.reshape(1, n), idx1.reshape(1, n))


def _moe(h_packed, rec, counts, w_gate, w_up, w_down):
    t = rec.shape[1]
    padded = (counts + MOE_TILE - 1) // MOE_TILE * MOE_TILE
    pend = jnp.cumsum(padded)
    pstart = pend - padded
    nblk = -(-2 * t // MOE_TILE) + N_EXPERTS
    n_slots = nblk * MOE_TILE
    blk_start = jnp.arange(nblk, dtype=jnp.int32) * MOE_TILE
    blk_e = jnp.minimum(jnp.sum(blk_start[:, None] >= pend[None, :], axis=1), N_EXPERTS - 1).astype(jnp.int32)
    nvalid = (pend[-1] // MOE_TILE).astype(jnp.int32).reshape(1)
    chunk_base = (jnp.arange(PACK_CHUNKS, dtype=jnp.int32) * n_slots)[:, None]
    rows = lambda k: (chunk_base + (jnp.take(pstart, rec[INFO_EXPERT + k]).astype(jnp.int32)
                                    + rec[INFO_RANK + k])[None, :]).reshape(-1)
    idx0, idx1 = rows(0), rows(1)
    xs = _sc_scatter_rows(h_packed.reshape(PACK_CHUNKS * t, LANES), idx0, idx1, PACK_CHUNKS * n_slots)
    ys = _experts(xs.reshape(PACK_CHUNKS, n_slots, LANES), blk_e, nvalid, w_gate, w_up, w_down)
    y01 = _sc_gather_rows(ys.reshape(PACK_CHUNKS * n_slots, LANES), jnp.concatenate([idx0, idx1]))
    return y01.reshape(2, PACK_CHUNKS, t, LANES)


def _mixer(x, mods_mix, mods_ffn, attn_fn, s5_fn, wts, tm, t_all, row0, shared, counts):
    q, k, v, u = _inproj(x, mods_mix[0], mods_mix[1], wts['g_mix'], wts['w_in'], tm)
    o_attn = attn_fn(q, k, v)
    o_ssm, h_re, h_im = s5_fn(u)
    x1, shared, counts = _outproj(x, o_attn, o_ssm, mods_mix[2], mods_ffn[0], mods_ffn[1], wts['g_attn_out'],
                                  wts['g_ffn'], wts['w_out'], wts['w_router'], wts['b_router'], tm, t_all, row0,
                                  shared, counts)
    return x1, shared, counts, (k, v, h_re, h_im)


def kernel(x_prompt, x_sample, cache_k_win, cache_v_win, state_ssm_re, state_ssm_im, c_prompt, c_sample, g_mix, w_ada_mix, b_ada_mix, w_in, w_out, g_attn_out, g_ssm_out, ssm_lambda_re, ssm_lambda_im, ssm_log_dt, ssm_b_re, ssm_b_im, ssm_c_re, ssm_c_im, ssm_d, w_glu, b_glu, g_ffn, w_ada_ffn, b_ada_ffn, w_router_coarse, b_router_coarse, w_router_fine, b_router_fine, w_expert_gate, w_expert_up, w_expert_down, g_final):
    depth = g_mix.shape[0]
    assert depth == 1, "single-layer step"
    l = 0
    bp, sp, d = x_prompt.shape
    bs, ts, _ = x_sample.shape
    keep = min(max(w for w, _ in DILATED_PATTERNS), sp)

    c_all = jnp.concatenate([c_prompt, c_sample], axis=0).astype(F32)
    m_mix = _adaln(c_all, w_ada_mix[l], b_ada_mix[l])
    m_ffn = _adaln(c_all, w_ada_ffn[l], b_ada_ffn[l])

    def mods(m, lo, hi, per_token):
        parts = jnp.split(m[lo:hi], 3, axis=-1)
        if per_token:
            return tuple(jnp.repeat(p, ts, axis=0)[None] for p in parts)
        return tuple(p[:, None, :] for p in parts)

    pad = ROUTER_COLS - N_EXPERT_GROUPS - N_EXPERTS
    w_router = jnp.concatenate([w_router_coarse[l], w_router_fine[l], jnp.zeros((d, pad), F32)], axis=1)
    b_router = jnp.concatenate([b_router_coarse[l], b_router_fine[l], jnp.zeros((pad,), F32)])
    wts = {
        'g_mix': g_mix[l], 'w_in': w_in[l].astype(BF16), 'w_out': w_out[l].astype(BF16),
        'g_attn_out': g_attn_out[l], 'g_ffn': g_ffn[l], 'w_router': w_router, 'b_router': b_router,
        'w_gate': w_expert_gate[l], 'w_up': w_expert_up[l], 'w_down': w_expert_down[l],
    }
    a_re, a_im, b_mat, c_mat = _s5_params(ssm_lambda_re[l], ssm_lambda_im[l], ssm_log_dt[l], ssm_b_re[l],
                                          ssm_b_im[l], ssm_c_re[l], ssm_c_im[l])
    s5_w = (a_re, a_im, b_mat, c_mat, ssm_d[l].reshape(-1), w_glu[l].astype(BF16), b_glu[l], g_ssm_out[l])

    t_prompt, t_sample = bp * sp, bs * ts
    t_all = t_prompt + t_sample
    tm_p, tm_s = min(sp, ROW_TILE), min(t_sample, ROW_TILE)
    mods_ffn_p, mods_ffn_s = mods(m_ffn, 0, bp, False), mods(m_ffn, bp, bp + bs, True)

    x1p, shared, counts, (kp, vp, hrp, hip) = _mixer(
        x_prompt, mods(m_mix, 0, bp, False), mods_ffn_p, _attn_prompt, lambda u: _s5_prompt(u, *s5_w), wts,
        tm_p, t_all, 0, None, jnp.zeros((SUBLANES, ROUTER_COLS), F32))

    n_state = SSM_GROUPS * SSM_STATE
    ck = cache_k_win[l]
    cv = cache_v_win[l]

    def attn_s(q, k, v):
        r = lambda a: a.reshape(bs, ts, D_ATTN)
        return _attn_sample(r(q), r(k), r(v), ck, cv).reshape(1, bs * ts, D_ATTN)

    def s5_s(u):
        u_tm = u.reshape(bs, ts, D_SSM).transpose(1, 0, 2)
        o, hr, hi = _s5_sample(u_tm, state_ssm_re[l].reshape(bs, n_state), state_ssm_im[l].reshape(bs, n_state),
                               *s5_w)
        return o.transpose(1, 0, 2).reshape(1, bs * ts, D_SSM), hr, hi

    x1s, shared, counts, (ks, vs, hrs, his) = _mixer(
        x_sample.reshape(1, t_sample, d), mods(m_mix, bp, bp + bs, True), mods_ffn_s, attn_s, s5_s, wts,
        tm_s, t_all, t_prompt, shared, counts)

    h_packed, info, rec = shared
    counts = counts[0, N_EXPERT_GROUPS:N_EXPERT_GROUPS + N_EXPERTS].astype(jnp.int32)
    y01 = _moe(h_packed, rec, counts, wts['w_gate'], wts['w_up'], wts['w_down'])
    yp = _combine(x1p, y01, info, mods_ffn_p[2], g_final, tm_p, 0)
    ys = _combine(x1s, y01, info, mods_ffn_s[2], g_final, tm_s, t_prompt)

    heads = lambda a, b, s: a.reshape(1, b, s, N_HEADS, HEAD_DIM)
    state = lambda a, b: a.reshape(1, b, SSM_GROUPS, SSM_STATE)
    return (yp, ys.reshape(bs, ts, d),
            heads(kp[:, sp - keep:], bp, keep), heads(vp[:, sp - keep:], bp, keep), state(hrp, bp), state(hip, bp),
            heads(ks, bs, ts), heads(vs, bs, ts), state(hrs, bs), state(his, bs))
```

```python
import functools
import math

import numpy as np
import jax
import jax.numpy as jnp
from jax import lax
from jax.experimental import pallas as pl
from jax.experimental.pallas import tpu as pltpu
from jax.experimental.pallas import tpu_sc as plsc

F32 = jnp.float32
BF16 = jnp.bfloat16
HIGHEST = lax.Precision.HIGHEST

D_MODEL = 1024
D_ATTN = 512
D_SSM = 512
HEAD_DIM = 64
N_HEADS = 8
DILATED_PATTERNS = ((128, 1), (512, 4), (2048, 16))
SSM_GROUP_CH = 16
SSM_GROUPS = 32
SSM_STATE = 64
N_EXPERT_GROUPS = 4
EXPERTS_PER_GROUP = 8
N_EXPERTS = 32
D_EXPERT = 512
D_IN_PROJ = 3 * D_ATTN + D_SSM
EPS = 1e-6

LANES = 128
SUBLANES = 8
VMEM_LIMIT = 48 * 1024 * 1024

ATTN_BLOCK = 128
ATTN_UNROLL = 16
NEG_BIG = -1e30
SSM_TT = 128
SSM_PITCH = SSM_TT + 8
ROUTER_COLS = 128
MOE_TILE = 512
ROW_TILE = 512
SC_WINDOW = 128
PACK_CHUNKS = D_MODEL // 2 // LANES
INFO_EXPERT, INFO_RANK, INFO_GATE = 0, 2, 4


def _cparams(sem, vmem=VMEM_LIMIT):
    return pltpu.CompilerParams(dimension_semantics=sem, vmem_limit_bytes=vmem)


def _adaln_kernel(c_ref, w_ref, b_ref, o_ref):
    c = c_ref[...]
    s = c * jax.nn.sigmoid(c)
    o_ref[...] = jnp.dot(s, w_ref[...], precision=HIGHEST, preferred_element_type=F32) + b_ref[...]


def _adaln(c, w, b):
    r, d = c.shape
    n = w.shape[1]
    tn = 768
    return pl.pallas_call(
        _adaln_kernel,
        grid=(n // tn,),
        in_specs=[pl.BlockSpec((r, d), lambda j: (0, 0)),
                  pl.BlockSpec((d, tn), lambda j: (0, j)),
                  pl.BlockSpec((1, tn), lambda j: (0, j))],
        out_specs=pl.BlockSpec((r, tn), lambda j: (0, j)),
        out_shape=jax.ShapeDtypeStruct((r, n), F32),
        compiler_params=_cparams(("arbitrary",)),
    )(c, w, b.reshape(1, n))


def _inproj_kernel(x_ref, sh_ref, sc_ref, g_ref, w_ref, q_ref, k_ref, v_ref, u_ref):
    x = x_ref[...]
    ms = jnp.mean(x * x, axis=-1, keepdims=True)
    h = x * lax.rsqrt(ms + EPS) * g_ref[...]
    h = h * (1.0 + sc_ref[...]) + sh_ref[...]
    p = jnp.dot(h.astype(BF16), w_ref[...], preferred_element_type=F32)
    q_ref[...] = p[:, :D_ATTN] * (HEAD_DIM ** -0.5)
    k_ref[...] = p[:, D_ATTN:2 * D_ATTN]
    v_ref[...] = p[:, 2 * D_ATTN:3 * D_ATTN]
    u_ref[...] = p[:, 3 * D_ATTN:].astype(u_ref.dtype)


def _mod_spec(mod, tm):
    d = mod.shape[-1]
    if mod.shape[1] == 1:
        return pl.BlockSpec((None, 1, d), lambda b, i: (b, 0, 0))
    return pl.BlockSpec((None, tm, d), lambda b, i: (b, i, 0))


def _inproj(x, shift, scale, g, w_bf16, tm):
    nb, s, d = x.shape
    row = lambda n: pl.BlockSpec((None, tm, n), lambda b, i: (b, i, 0))
    out = jax.ShapeDtypeStruct((nb, s, D_ATTN), F32)
    return pl.pallas_call(
        _inproj_kernel,
        grid=(nb, s // tm),
        in_specs=[row(d), _mod_spec(shift, tm), _mod_spec(scale, tm),
                  pl.BlockSpec((1, d), lambda b, i: (0, 0)),
                  pl.BlockSpec((d, D_IN_PROJ), lambda b, i: (0, 0))],
        out_specs=[row(D_ATTN), row(D_ATTN), row(D_ATTN), row(D_SSM)],
        out_shape=[out, out, out, jax.ShapeDtypeStruct((nb, s, D_SSM), BF16)],
        compiler_params=_cparams(("parallel", "parallel")),
    )(x, shift, scale, g.reshape(1, d), w_bf16)


def _band_bias():
    qi = np.arange(ATTN_BLOCK)[:, None]
    kj = np.arange(ATTN_BLOCK)[None, :]
    cur = kj <= qi
    prev = kj >= qi
    to_bias = lambda m: np.tile(np.where(m, 0.0, NEG_BIG).astype(np.float32), (2, 1))
    return to_bias(np.concatenate([prev, cur], axis=1)), to_bias(cur)


def _attn_prompt_kernel(q_ref, k_ref, v_ref, bias2_ref, bias1_ref, o_ref, acc_scr, m_scr, l_scr):
    s_len = q_ref.shape[0]
    lane = lax.broadcasted_iota(jnp.int32, (1, LANES), 1)
    head0 = lane < HEAD_DIM

    def rows(ref, start, n, d):
        if d == 1:
            return ref[pl.ds(start, n), :]
        return ref[pl.ds(start, n, stride=d), :]

    def tiles(items, first):
        nk = ATTN_BLOCK if first else 2 * ATTN_BLOCK
        bias = bias1_ref[...] if first else bias2_ref[...]
        one = jnp.ones((), BF16)
        loaded = []
        for _, d, qstart in items:
            kstart = qstart if first else qstart - d * ATTN_BLOCK
            loaded.append((rows(q_ref, qstart, ATTN_BLOCK, d), rows(k_ref, kstart, nk, d),
                           rows(v_ref, kstart, nk, d)))
        results = []
        for qrows, krows, vrows in loaded:
            kb = krows.astype(BF16)
            vb = vrows.astype(BF16)
            q2 = jnp.concatenate([jnp.where(head0, qrows, 0.0), jnp.where(head0, 0.0, qrows)], axis=0).astype(BF16)
            s = lax.dot_general(q2, kb, (((1,), (1,)), ((), ())), preferred_element_type=F32) + bias
            m = jnp.max(s, axis=-1, keepdims=True)
            p = jnp.exp(s - m).astype(BF16)
            pv0 = jnp.dot(p[:ATTN_BLOCK], jnp.where(head0, vb, one), preferred_element_type=F32)
            pv1 = jnp.dot(p[ATTN_BLOCK:], jnp.where(head0, one, vb), preferred_element_type=F32)
            results.append((jnp.where(head0, pv0, pv1), jnp.where(head0, m[:ATTN_BLOCK], m[ATTN_BLOCK:]),
                            jnp.where(head0, pv1, pv0)))
        for (br, d, qstart), (acc, m, l) in zip(items, results):
            dst = pl.ds(qstart, ATTN_BLOCK) if d == 1 else pl.ds(qstart, ATTN_BLOCK, stride=d)
            acc_scr[br, dst, :] = acc
            m_scr[br, dst, :] = m
            l_scr[br, dst, :] = l

    def largest_divisor(n):
        return max(f for f in range(1, ATTN_UNROLL + 1) if n % f == 0)

    grouped = []
    for br, (window, d) in enumerate(DILATED_PATTERNS):
        assert window // d == ATTN_BLOCK
        if d <= ATTN_UNROLL:
            grouped += [(br, d, r) for r in range(d)]
        else:
            un = largest_divisor(d)

            def first_body(i, carry, br=br, d=d, un=un):
                tiles([(br, d, i * un + j) for j in range(un)], True)
                return carry

            lax.fori_loop(0, d // un, first_body, 0)
    for i in range(0, len(grouped), ATTN_UNROLL):
        tiles(grouped[i:i + ATTN_UNROLL], True)

    for br, (window, d) in enumerate(DILATED_PATTERNS):
        nblk = s_len // (d * ATTN_BLOCK)
        n_rest = d * (nblk - 1)
        if n_rest:
            un = largest_divisor(n_rest)

            def rest_body(i, carry, br=br, d=d, nblk=nblk, un=un):
                items = []
                for j in range(un):
                    n = i * un + j
                    qstart = n // (nblk - 1) + d * ATTN_BLOCK * (1 + n % (nblk - 1))
                    items.append((br, d, pl.multiple_of(qstart, ATTN_BLOCK) if d == 1 else qstart))
                tiles(items, False)
                return carry

            lax.fori_loop(0, n_rest // un, rest_body, 0)

    ch = 256

    def merge_body(i, carry):
        sl = pl.ds(pl.multiple_of(i * ch, ch), ch)
        ms = [m_scr[br, sl, :] for br in range(3)]
        mx = jnp.maximum(jnp.maximum(ms[0], ms[1]), ms[2])
        num = jnp.zeros((ch, LANES), F32)
        den = jnp.zeros((ch, LANES), F32)
        for br in range(3):
            e = jnp.exp(ms[br] - mx)
            num = num + e * acc_scr[br, sl, :]
            den = den + e * pltpu.roll(l_scr[br, sl, :], HEAD_DIM, 1)
        o_ref[sl, :] = (num / den).astype(o_ref.dtype)
        return carry

    lax.fori_loop(0, s_len // ch, merge_body, 0)


def _attn_prompt(q, k, v):
    nb, s, _ = q.shape
    assert s % (16 * ATTN_BLOCK) == 0
    bias2, bias1 = _band_bias()
    blk = pl.BlockSpec((None, s, LANES), lambda b, p: (b, 0, p))
    const = lambda a: pl.BlockSpec(a.shape, lambda b, p: (0, 0))
    return pl.pallas_call(
        _attn_prompt_kernel,
        grid=(nb, D_ATTN // LANES),
        in_specs=[blk, blk, blk, const(bias2), const(bias1)],
        out_specs=blk,
        out_shape=jax.ShapeDtypeStruct((nb, s, D_ATTN), BF16),
        scratch_shapes=[pltpu.VMEM((3, s, LANES), F32)] * 3,
        compiler_params=_cparams(("parallel", "parallel")),
    )(q, k, v, jnp.asarray(bias2), jnp.asarray(bias1))


def _sample_bias(t_new, w_buf):
    n_new = -(-t_new // LANES) * LANES
    t = np.arange(t_new)[:, None]
    idx = np.arange(w_buf + n_new)[None, :]
    dist = w_buf + t - idx
    mult = np.zeros(dist.shape, np.int64)
    for window, d in DILATED_PATTERNS:
        mult += ((dist >= 0) & (dist % d == 0) & (dist <= window)).astype(np.int64)
    bias = np.where(mult > 0, np.log(np.maximum(mult, 1)), NEG_BIG).astype(np.float32)
    bias = np.repeat(bias, N_HEADS, axis=0)
    return bias[:, :w_buf], bias[:, w_buf:]


def _attn_sample_kernel(q_ref, kn_ref, vn_ref, kt_ref, vt_ref, bias_ref, biasn_ref, o_ref):
    t_new = q_ref.shape[0]
    n_new = biasn_ref.shape[1]
    row_head = lax.broadcasted_iota(jnp.int32, (t_new * N_HEADS, D_ATTN), 0) & (N_HEADS - 1)
    col_head = lax.broadcasted_iota(jnp.int32, (t_new * N_HEADS, D_ATTN), 1) >> (HEAD_DIM.bit_length() - 1)
    own = row_head == col_head
    q = q_ref[...]
    q_rows = jnp.concatenate([jnp.broadcast_to(q[t:t + 1], (N_HEADS, D_ATTN)) for t in range(t_new)], axis=0)
    q_bd = jnp.where(own, q_rows, 0.0).astype(BF16)
    pad = jnp.zeros((n_new - t_new, D_ATTN), F32)
    kn = jnp.concatenate([kn_ref[...], pad], axis=0).astype(BF16)
    vn = jnp.concatenate([vn_ref[...], pad], axis=0).astype(BF16)
    nt = (((1,), (1,)), ((), ()))
    s_c = jnp.dot(q_bd, kt_ref[...].astype(BF16), preferred_element_type=F32) + bias_ref[...]
    s_n = lax.dot_general(q_bd, kn, nt, preferred_element_type=F32) + biasn_ref[...]
    m = jnp.maximum(jnp.max(s_c, axis=-1, keepdims=True), jnp.max(s_n, axis=-1, keepdims=True))
    p_c = jnp.exp(s_c - m)
    p_n = jnp.exp(s_n - m)
    l = jnp.sum(p_c, axis=-1, keepdims=True) + jnp.sum(p_n, axis=-1, keepdims=True)
    o = lax.dot_general(p_c.astype(BF16), vt_ref[...].astype(BF16), nt, preferred_element_type=F32)
    o = (o + jnp.dot(p_n.astype(BF16), vn, preferred_element_type=F32)) / l
    o = jnp.where(own, o, 0.0)
    o_ref[...] = jnp.concatenate(
        [jnp.sum(o[t * N_HEADS:(t + 1) * N_HEADS], axis=0, keepdims=True) for t in range(t_new)],
        axis=0).astype(o_ref.dtype)


def _attn_sample(q, k_new, v_new, cache_k, cache_v):
    nb, t_new, _ = q.shape
    w_buf = cache_k.shape[1]
    bias, bias_new = _sample_bias(t_new, w_buf)
    as_stored = lambda a: a.transpose(0, 2, 3, 1).reshape(nb, D_ATTN, w_buf)
    new = pl.BlockSpec((None, t_new, D_ATTN), lambda b: (b, 0, 0))
    buf = pl.BlockSpec((None, D_ATTN, w_buf), lambda b: (b, 0, 0))
    const = lambda a: pl.BlockSpec(a.shape, lambda b: (0, 0))
    return pl.pallas_call(
        _attn_sample_kernel,
        grid=(nb,),
        in_specs=[new, new, new, buf, buf, const(bias), const(bias_new)],
        out_specs=new,
        out_shape=jax.ShapeDtypeStruct((nb, t_new, D_ATTN), BF16),
        compiler_params=_cparams(("parallel",)),
    )(q, k_new, v_new, as_stored(cache_k), as_stored(cache_v), jnp.asarray(bias), jnp.asarray(bias_new))


def _s5_params(lam_re, lam_im, log_dt, b_re, b_im, c_re, c_im):
    f32 = F32
    dt = jnp.exp(log_dt.astype(f32))[:, None]
    lr, li = lam_re.astype(f32), lam_im.astype(f32)
    ea = jnp.exp(lr * dt)
    a_re, a_im = ea * jnp.cos(li * dt), ea * jnp.sin(li * dt)
    den = lr * lr + li * li
    co_re = ((a_re - 1.0) * lr + a_im * li) / den
    co_im = (a_im * lr - (a_re - 1.0) * li) / den
    bb_re = co_re[..., None] * b_re - co_im[..., None] * b_im
    bb_im = co_re[..., None] * b_im + co_im[..., None] * b_re
    eye = jnp.eye(8, dtype=f32)

    def b_blocks(bb):
        t = bb.reshape(4, 8, SSM_STATE, SSM_GROUP_CH)
        return jnp.einsum('ab,kapc->kacbp', eye, t).reshape(4, 8 * SSM_GROUP_CH, 8 * SSM_STATE)

    def c_blocks(cc):
        t = cc.reshape(4, 8, SSM_GROUP_CH, SSM_STATE)
        return jnp.einsum('ab,kacp->kbpac', eye, t).reshape(4, 8 * SSM_STATE, 8 * SSM_GROUP_CH)

    b_mat = jnp.concatenate([b_blocks(bb_re), b_blocks(bb_im)], axis=2).astype(BF16)
    c_mat = jnp.concatenate([c_blocks(c_re.astype(f32)), -c_blocks(c_im.astype(f32))], axis=1).astype(BF16)
    return a_re.reshape(-1), a_im.reshape(-1), b_mat, c_mat


def _gelu_tanh(y):
    return 0.5 * y * (1.0 + jnp.tanh(math.sqrt(2.0 / math.pi) * (y + 0.044715 * (y * y * y))))


def _s5_epilogue(y, u, d_ref, wglu_ref, bglu_ref, g_ref):
    y = y + d_ref[...] * u.astype(F32)
    z = _gelu_tanh(y)
    gate = jnp.dot(z.astype(BF16), wglu_ref[...], preferred_element_type=F32) + bglu_ref[...]
    out = z * jax.nn.sigmoid(gate)
    ms = jnp.mean(out * out, axis=-1, keepdims=True)
    return out * lax.rsqrt(ms + EPS) * g_ref[...]


def _s5_prompt_kernel(u_ref, are_ref, aim_ref, b_ref, c_ref, d_ref, wglu_ref, bglu_ref, g_ref,
                      o_ref, hre_ref, him_ref, scr, hst):
    nseq, tt, _ = u_ref.shape
    n_slab = D_SSM * SSM_STATE // SSM_GROUP_CH // LANES
    ti = pl.program_id(1)

    @pl.when(ti == 0)
    def _():
        hst[...] = jnp.zeros_like(hst)

    ub = u_ref[...].reshape(nseq * tt, D_SSM)
    for kc in range(4):
        bu = jnp.dot(ub[:, kc * LANES:(kc + 1) * LANES], b_ref[kc], preferred_element_type=F32)
        for part in range(2):
            for j in range(4):
                col = part * 512 + j * LANES
                for b in range(nseq):
                    scr[part * n_slab + 4 * kc + j, b * SSM_PITCH:b * SSM_PITCH + tt, :] = (
                        bu[b * tt:(b + 1) * tt, col:col + LANES])

    grp = 4
    for sg in range(n_slab // grp):
        slabs = [sg * grp + i for i in range(grp)]
        ar = [are_ref[s] for s in slabs]
        ai = [aim_ref[s] for s in slabs]

        def step(t, carry, slabs=slabs, ar=ar, ai=ai):
            hr, hi = carry
            nhr, nhi = [], []
            for i, s in enumerate(slabs):
                sel = pl.ds(t, nseq, stride=SSM_PITCH)
                br = scr[s, sel, :]
                bi = scr[n_slab + s, sel, :]
                r = ar[i] * hr[i] - ai[i] * hi[i] + br
                im = ar[i] * hi[i] + ai[i] * hr[i] + bi
                scr[s, sel, :] = r
                scr[n_slab + s, sel, :] = im
                nhr.append(r)
                nhi.append(im)
            return tuple(nhr), tuple(nhi)

        init = (tuple(hst[s] for s in slabs), tuple(hst[n_slab + s] for s in slabs))
        hr, hi = lax.fori_loop(0, tt, step, init, unroll=4)
        for i, s in enumerate(slabs):
            hst[s] = hr[i]
            hst[n_slab + s] = hi[i]

    ys = []
    for kc in range(4):
        slabs = [4 * kc + j for j in range(4)] + [n_slab + 4 * kc + j for j in range(4)]
        lhs = jnp.concatenate(
            [jnp.concatenate([scr[s, b * SSM_PITCH:b * SSM_PITCH + tt, :].astype(BF16) for s in slabs], axis=1)
             for b in range(nseq)], axis=0)
        ys.append(jnp.dot(lhs, c_ref[kc], preferred_element_type=F32))
    y = jnp.concatenate(ys, axis=1)
    out = _s5_epilogue(y, ub, d_ref, wglu_ref, bglu_ref, g_ref)
    o_ref[...] = out.reshape(nseq, tt, D_SSM).astype(o_ref.dtype)

    @pl.when(ti == pl.num_programs(1) - 1)
    def _():
        for s in range(n_slab):
            hre_ref[:, s * LANES:(s + 1) * LANES] = hst[s]
            him_ref[:, s * LANES:(s + 1) * LANES] = hst[n_slab + s]


def _s5_prompt(u, a_re, a_im, b_mat, c_mat, d, w_glu_bf16, b_glu, g_out):
    nb, s, _ = u.shape
    nseq = SUBLANES
    assert nb % nseq == 0 and s % SSM_TT == 0
    n_state = a_re.shape[0]
    n_slab = n_state // LANES
    bcast = lambda a: jnp.broadcast_to(a.reshape(n_slab, 1, LANES), (n_slab, nseq, LANES))
    const = lambda a: pl.BlockSpec(a.shape, lambda b, i: (0,) * a.ndim)
    args = (u, bcast(a_re), bcast(a_im), b_mat, c_mat, d.reshape(1, D_SSM), w_glu_bf16,
            b_glu.reshape(1, D_SSM), g_out.reshape(1, D_SSM))
    st = jax.ShapeDtypeStruct((nb, n_state), F32)
    return pl.pallas_call(
        _s5_prompt_kernel,
        grid=(nb // nseq, s // SSM_TT),
        in_specs=[pl.BlockSpec((nseq, SSM_TT, D_SSM), lambda b, i: (b, i, 0))] + [const(a) for a in args[1:]],
        out_specs=[pl.BlockSpec((nseq, SSM_TT, D_SSM), lambda b, i: (b, i, 0)),
                   pl.BlockSpec((nseq, n_state), lambda b, i: (b, 0)),
                   pl.BlockSpec((nseq, n_state), lambda b, i: (b, 0))],
        out_shape=[jax.ShapeDtypeStruct((nb, s, D_SSM), BF16), st, st],
        scratch_shapes=[pltpu.VMEM((2 * n_slab, nseq * SSM_PITCH, LANES), F32),
                        pltpu.VMEM((2 * n_slab, nseq, LANES), F32)],
        compiler_params=_cparams(("parallel", "arbitrary")),
    )(*args)


def _s5_sample_kernel(u_ref, h0re_ref, h0im_ref, are_ref, aim_ref, b_ref, c_ref, d_ref, wglu_ref, bglu_ref, g_ref,
                      o_ref, hre_ref, him_ref):
    t_new = u_ref.shape[0]
    hre = h0re_ref[...]
    him = h0im_ref[...]
    are, aim = are_ref[...], aim_ref[...]
    half = 4 * LANES
    for t in range(t_new):
        ub = u_ref[t]
        bus = [jnp.dot(ub[:, kc * LANES:(kc + 1) * LANES], b_ref[kc], preferred_element_type=F32) for kc in range(4)]
        bre = jnp.concatenate([bu[:, :half] for bu in bus], axis=1)
        bim = jnp.concatenate([bu[:, half:] for bu in bus], axis=1)
        hre, him = are * hre - aim * him + bre, are * him + aim * hre + bim
        ys = []
        for kc in range(4):
            lhs = jnp.concatenate([hre[:, kc * half:(kc + 1) * half], him[:, kc * half:(kc + 1) * half]], axis=1)
            ys.append(jnp.dot(lhs.astype(BF16), c_ref[kc], preferred_element_type=F32))
        y = jnp.concatenate(ys, axis=1)
        o_ref[t] = _s5_epilogue(y, ub, d_ref, wglu_ref, bglu_ref, g_ref).astype(o_ref.dtype)
    hre_ref[...] = hre
    him_ref[...] = him


def _s5_sample(u_tm, h0_re, h0_im, a_re, a_im, b_mat, c_mat, d, w_glu_bf16, b_glu, g_out):
    t_new, nb, _ = u_tm.shape
    n_state = a_re.shape[0]
    tb = 64
    assert nb % tb == 0
    const = lambda a: pl.BlockSpec(a.shape, lambda b: (0,) * a.ndim)
    args = (u_tm, h0_re, h0_im, a_re.reshape(1, n_state), a_im.reshape(1, n_state), b_mat, c_mat,
            d.reshape(1, D_SSM), w_glu_bf16, b_glu.reshape(1, D_SSM), g_out.reshape(1, D_SSM))
    st_spec = pl.BlockSpec((tb, n_state), lambda b: (b, 0))
    st = jax.ShapeDtypeStruct((nb, n_state), F32)
    return pl.pallas_call(
        _s5_sample_kernel,
        grid=(nb // tb,),
        in_specs=[pl.BlockSpec((t_new, tb, D_SSM), lambda b: (0, b, 0)), st_spec, st_spec]
                 + [const(a) for a in args[3:]],
        out_specs=[pl.BlockSpec((t_new, tb, D_SSM), lambda b: (0, b, 0)), st_spec, st_spec],
        out_shape=[jax.ShapeDtypeStruct((t_new, nb, D_SSM), BF16), st, st],
        compiler_params=_cparams(("parallel",)),
    )(*args)


def _store_packed(ref, val):
    half = val.shape[1] // 2
    bits = pltpu.bitcast(val.astype(BF16).astype(F32), jnp.uint32)
    words = bits[:, :half] | lax.shift_right_logical(bits[:, half:], jnp.uint32(16))
    for c in range(PACK_CHUNKS):
        ref[c] = words[:, c * LANES:(c + 1) * LANES]


def _load_packed(ref):
    hi, lo = [], []
    for c in range(PACK_CHUNKS):
        w = ref[c]
        hi.append(pltpu.bitcast(w & jnp.uint32(0xFFFF0000), F32))
        lo.append(pltpu.bitcast(lax.shift_left(w, jnp.uint32(16)), F32))
    return jnp.concatenate(hi + lo, axis=1)


def _route_tile(lg, tri_ref, carry):
    lane = lax.broadcasted_iota(jnp.int32, lg.shape, 1)
    lane_f = lane.astype(F32)
    none = float(ROUTER_COLS)
    ninf = float("-inf")
    first = lambda cond: jnp.min(jnp.where(cond, lane_f, none), axis=-1, keepdims=True)

    is_c = lane < N_EXPERT_GROUPS
    lc = jnp.where(is_c, lg, ninf)
    mc = jnp.max(lc, axis=-1, keepdims=True)
    p_grp = 1.0 / jnp.sum(jnp.exp(lc - mc), axis=-1, keepdims=True)
    grp = first(lc == mc)
    fine = lane - N_EXPERT_GROUPS
    fine_grp = lax.shift_right_arithmetic(fine, jnp.int32(EXPERTS_PER_GROUP.bit_length() - 1))
    in_grp = (fine >= 0) & (fine < N_EXPERTS) & (fine_grp.astype(F32) == grp)
    lf = jnp.where(in_grp, lg, ninf)
    v1 = jnp.max(lf, axis=-1, keepdims=True)
    i1 = first(lf == v1)
    lf2 = jnp.where(lane_f == i1, ninf, lf)
    v2 = jnp.max(lf2, axis=-1, keepdims=True)
    i2 = first(lf2 == v2)
    b = jnp.exp(v2 - v1)
    g0 = p_grp / (1.0 + b)
    g1 = p_grp * b / (1.0 + b)

    hit1 = lane_f == i1
    hit2 = lane_f == i2
    onehot = jnp.where(hit1 | hit2, 1.0, 0.0)
    before = jnp.dot(tri_ref[...], onehot.astype(BF16), preferred_element_type=F32) + carry[0:1, :]
    r1 = jnp.sum(jnp.where(hit1, before, 0.0), axis=-1, keepdims=True)
    r2 = jnp.sum(jnp.where(hit2, before, 0.0), axis=-1, keepdims=True)
    carry[...] = carry[...] + jnp.sum(onehot, axis=0, keepdims=True)

    as_int = lambda v: jnp.broadcast_to(v, lg.shape).astype(jnp.int32)
    as_bits = lambda v: pltpu.bitcast(jnp.broadcast_to(v, lg.shape), jnp.int32)
    info = jnp.zeros(lg.shape, jnp.int32)
    fields = ((INFO_EXPERT, as_int(i1 - N_EXPERT_GROUPS)), (INFO_EXPERT + 1, as_int(i2 - N_EXPERT_GROUPS)),
              (INFO_RANK, as_int(r1)), (INFO_RANK + 1, as_int(r2)), (INFO_GATE, as_bits(g0)),
              (INFO_GATE + 1, as_bits(g1)))
    for col, val in fields:
        info = jnp.where(lane == col, val, info)
    return info


def _outproj_kernel(x_ref, oa_ref, os_ref, gate_ref, sh_ref, sc_ref, ga_ref, gf_ref, w_ref, wr_hi_ref, wr_lo_ref,
                    br_ref, tri_ref, cnt_in_ref, *rest):
    x1_ref, h_ref, info_ref, rec_ref, cnt_ref, carry = rest[-6:]
    step = pl.program_id(0) * pl.num_programs(1) + pl.program_id(1)

    @pl.when(step == 0)
    def _():
        carry[...] = cnt_in_ref[...]

    oa = oa_ref[...].astype(F32)
    ms = jnp.mean(oa * oa, axis=-1, keepdims=True)
    na = oa * lax.rsqrt(ms + EPS) * ga_ref[...]
    merged = jnp.concatenate([na.astype(BF16), os_ref[...]], axis=-1)
    x1 = x_ref[...] + gate_ref[...] * jnp.dot(merged, w_ref[...], preferred_element_type=F32)
    x1_ref[...] = x1
    ms = jnp.mean(x1 * x1, axis=-1, keepdims=True)
    h = x1 * lax.rsqrt(ms + EPS) * gf_ref[...]
    h = h * (1.0 + sc_ref[...]) + sh_ref[...]
    h_hi = h.astype(BF16)
    _store_packed(h_ref, h)
    h_lo = (h - h_hi.astype(F32)).astype(BF16)
    lg = jnp.dot(h_hi, wr_hi_ref[...], preferred_element_type=F32)
    lg = lg + jnp.dot(h_hi, wr_lo_ref[...], preferred_element_type=F32)
    lg = lg + jnp.dot(h_lo, wr_hi_ref[...], preferred_element_type=F32)
    info = _route_tile(lg + br_ref[...], tri_ref, carry)
    info_ref[...] = info
    rec_ref[...] = pltpu.bitcast(pltpu.bitcast(info, F32).T[:SUBLANES], jnp.int32)

    @pl.when(step == pl.num_programs(0) * pl.num_programs(1) - 1)
    def _():
        cnt_ref[...] = carry[...]


def _outproj(x, o_attn, o_ssm, gate, shift, scale, g_attn, g_ffn, w_out_bf16, w_router, b_router, tm, t_all, row0,
             shared, counts):
    nb, s, d = x.shape
    assert row0 % tm == 0
    blk = lambda b, i: row0 // tm + b * (s // tm) + i
    row = lambda n: pl.BlockSpec((None, tm, n), lambda b, i: (b, i, 0))
    const = lambda a: pl.BlockSpec(a.shape, lambda b, i: (0,) * a.ndim)
    wr_hi = w_router.astype(BF16)
    wr_lo = (w_router - wr_hi.astype(F32)).astype(BF16)
    tri = jnp.asarray(np.tril(np.ones((tm, tm), np.float32), -1), BF16)
    consts = (g_attn.reshape(1, D_ATTN), g_ffn.reshape(1, d), w_out_bf16, wr_hi, wr_lo,
              b_router.reshape(1, ROUTER_COLS), tri, counts)
    n_in = 6 + len(consts)
    count_spec = pl.BlockSpec((SUBLANES, ROUTER_COLS), lambda b, i: (0, 0))
    out = pl.pallas_call(
        _outproj_kernel,
        grid=(nb, s // tm),
        in_specs=[row(d), row(D_ATTN), row(D_SSM), _mod_spec(gate, tm), _mod_spec(shift, tm), _mod_spec(scale, tm)]
                 + [const(a) for a in consts] + [pl.BlockSpec(memory_space=pl.ANY)] * len(shared or ()),
        out_specs=[row(d), pl.BlockSpec((PACK_CHUNKS, tm, LANES), lambda b, i: (0, blk(b, i), 0)),
                   pl.BlockSpec((tm, ROUTER_COLS), lambda b, i: (blk(b, i), 0)),
                   pl.BlockSpec((SUBLANES, tm), lambda b, i: (0, blk(b, i))), count_spec],
        out_shape=[jax.ShapeDtypeStruct((nb, s, d), F32),
                   jax.ShapeDtypeStruct((PACK_CHUNKS, t_all, LANES), jnp.uint32),
                   jax.ShapeDtypeStruct((t_all, ROUTER_COLS), jnp.int32),
                   jax.ShapeDtypeStruct((SUBLANES, t_all), jnp.int32),
                   jax.ShapeDtypeStruct((SUBLANES, ROUTER_COLS), F32)],
        scratch_shapes=[pltpu.VMEM((SUBLANES, ROUTER_COLS), F32)],
        input_output_aliases={n_in + j: 1 + j for j in range(len(shared or ()))},
        compiler_params=_cparams(("arbitrary", "arbitrary")),
    )(x, o_attn, o_ssm, gate, shift, scale, *consts, *(shared or ()))
    return out[0], tuple(out[1:4]), out[4]


def _expert_kernel(blk_e_ref, nvalid_ref, x_ref, wg_ref, wu_ref, wd_ref, y_ref, wg_s, wu_s, wd_s):
    i = pl.program_id(0)

    @pl.when((i == 0) | (blk_e_ref[i] != blk_e_ref[jnp.maximum(i - 1, 0)]))
    def _():
        wg_s[...] = wg_ref[...].astype(BF16)
        wu_s[...] = wu_ref[...].astype(BF16)
        wd_s[...] = wd_ref[...].astype(BF16)

    @pl.when(i < nvalid_ref[0])
    def _():
        x = _load_packed(x_ref).astype(BF16)
        a = jnp.dot(x, wg_s[...], preferred_element_type=F32)
        b = jnp.dot(x, wu_s[...], preferred_element_type=F32)
        hid = (a * jax.nn.sigmoid(a) * b).astype(BF16)
        _store_packed(y_ref, jnp.dot(hid, wd_s[...], preferred_element_type=F32))

    @pl.when(i >= nvalid_ref[0])
    def _():
        y_ref[...] = jnp.zeros_like(y_ref)


def _experts(xs, blk_e, nvalid, w_gate, w_up, w_down):
    nblk = xs.shape[1] // MOE_TILE
    d = w_gate.shape[1]
    wspec = lambda shp: pl.BlockSpec((None,) + shp, lambda i, be, nv: (be[i], 0, 0))
    tile = pl.BlockSpec((PACK_CHUNKS, MOE_TILE, LANES), lambda i, be, nv: (0, i, 0))
    return pl.pallas_call(
        _expert_kernel,
        grid_spec=pltpu.PrefetchScalarGridSpec(
            num_scalar_prefetch=2,
            grid=(nblk,),
            in_specs=[tile, wspec((d, D_EXPERT)), wspec((d, D_EXPERT)), wspec((D_EXPERT, d))],
            out_specs=tile,
            scratch_shapes=[pltpu.VMEM((d, D_EXPERT), BF16), pltpu.VMEM((d, D_EXPERT), BF16),
                            pltpu.VMEM((D_EXPERT, d), BF16)],
        ),
        out_shape=jax.ShapeDtypeStruct(xs.shape, jnp.uint32),
        compiler_params=_cparams(("arbitrary",)),
    )(blk_e, nvalid, xs, w_gate, w_up, w_down)


def _combine_kernel(x1_ref, y0_ref, y1_ref, info_ref, gate_ref, gfin_ref, o_ref):
    info = pltpu.bitcast(info_ref[...], F32)
    g0 = info[:, INFO_GATE:INFO_GATE + 1]
    g1 = info[:, INFO_GATE + 1:INFO_GATE + 2]
    moe = g0 * _load_packed(y0_ref) + g1 * _load_packed(y1_ref)
    x2 = x1_ref[...] + gate_ref[...] * moe
    ms = jnp.mean(x2 * x2, axis=-1, keepdims=True)
    o_ref[...] = x2 * lax.rsqrt(ms + EPS) * gfin_ref[...]


def _combine(x1, y01, info, gate, g_final, tm, row0):
    nb, s, d = x1.shape
    assert row0 % tm == 0
    blk = lambda b, i: row0 // tm + b * (s // tm) + i
    row = lambda n: pl.BlockSpec((None, tm, n), lambda b, i: (b, i, 0))
    packed = lambda k: pl.BlockSpec((None, PACK_CHUNKS, tm, LANES), lambda b, i: (k, 0, blk(b, i), 0))
    return pl.pallas_call(
        _combine_kernel,
        grid=(nb, s // tm),
        in_specs=[row(d), packed(0), packed(1), pl.BlockSpec((tm, ROUTER_COLS), lambda b, i: (blk(b, i), 0)),
                  _mod_spec(gate, tm), pl.BlockSpec((1, d), lambda b, i: (0, 0))],
        out_specs=row(d),
        out_shape=jax.ShapeDtypeStruct((nb, s, d), F32),
        compiler_params=_cparams(("parallel", "parallel")),
    )(x1, y01, y01, info, gate, g_final.reshape(1, d))


def _sc_window(n_rows):
    assert n_rows % SC_WINDOW == 0
    return SC_WINDOW


def _sc_gather_rows(table, idx):
    n = idx.shape[0]
    w = _sc_window(n)
    mesh = plsc.VectorSubcoreMesh(core_axis_name="core", subcore_axis_name="subcore")

    @functools.partial(pl.kernel, out_type=jax.ShapeDtypeStruct((n, LANES), table.dtype), mesh=mesh)
    def gather_kernel(x_hbm, i_hbm, o_hbm):
        def body(i_vmem, o_vmem):
            pltpu.sync_copy(x_hbm.at[i_vmem.at[0]], o_vmem)

        pltpu.emit_pipeline(
            body, grid=(n // w,),
            in_specs=[pl.BlockSpec((1, w), lambda i: (0, i))],
            out_specs=[pl.BlockSpec((w, LANES), lambda i: (i, 0))],
            core_axis_name=("core", "subcore"), dimension_semantics=(pltpu.PARALLEL,),
        )(i_hbm, o_hbm)

    return gather_kernel(table, idx.reshape(1, n))


def _sc_scatter_rows(rows, idx0, idx1, n_out):
    n = rows.shape[0]
    w = _sc_window(n)
    mesh = plsc.VectorSubcoreMesh(core_axis_name="core", subcore_axis_name="subcore")

    @functools.partial(pl.kernel, out_type=jax.ShapeDtypeStruct((n_out, LANES), rows.dtype), mesh=mesh)
    def scatter_kernel(x_hbm, i0_hbm, i1_hbm, o_hbm):
        def body(x_vmem, i0_vmem, i1_vmem):
            pltpu.sync_copy(x_vmem, o_hbm.at[i0_vmem.at[0]])
            pltpu.sync_copy(x_vmem, o_hbm.at[i1_vmem.at[0]])

        pltpu.emit_pipeline(
            body, grid=(n // w,),
            in_specs=[pl.BlockSpec((w, LANES), lambda i: (i, 0)),
                      pl.BlockSpec((1, w), lambda i: (0, i)),
                      pl.BlockSpec((1, w), lambda i: (0, i))],
            out_specs=[],
            core_axis_name=("core", "subcore"), dimension_semantics=(pltpu.PARALLEL,),
        )(x_hbm, i0_hbm, i1_hbm)

    return scatter_kernel(rows, idx0.reshape(1, n), idx1.reshape(1, n))


def _moe(h_packed, rec, counts, w_gate, w_up, w_down):
    t = rec.shape[1]
    padded = (counts + MOE_TILE - 1) // MOE_TILE * MOE_TILE
    pend = jnp.cumsum(padded)
    pstart = pend - padded
    nblk = -(-2 * t // MOE_TILE) + N_EXPERTS
    n_slots = nblk * MOE_TILE
    blk_start = jnp.arange(nblk, dtype=jnp.int32) * MOE_TILE
    blk_e = jnp.minimum(jnp.sum(blk_start[:, None] >= pend[None, :], axis=1), N_EXPERTS - 1).astype(jnp.int32)
    nvalid = (pend[-1] // MOE_TILE).astype(jnp.int32).reshape(1)
    chunk_base = (jnp.arange(PACK_CHUNKS, dtype=jnp.int32) * n_slots)[:, None]
    rows = lambda k: (chunk_base + (jnp.take(pstart, rec[INFO_EXPERT + k]).astype(jnp.int32)
                                    + rec[INFO_RANK + k])[None, :]).reshape(-1)
    idx0, idx1 = rows(0), rows(1)
    xs = _sc_scatter_rows(h_packed.reshape(PACK_CHUNKS * t, LANES), idx0, idx1, PACK_CHUNKS * n_slots)
    ys = _experts(xs.reshape(PACK_CHUNKS, n_slots, LANES), blk_e, nvalid, w_gate, w_up, w_down)
    y01 = _sc_gather_rows(ys.reshape(PACK_CHUNKS * n_slots, LANES), jnp.concatenate([idx0, idx1]))
    return y01.reshape(2, PACK_CHUNKS, t, LANES)


def _mixer(x, mods_mix, mods_ffn, attn_fn, s5_fn, wts, tm, t_all, row0, shared, counts):
    q, k, v, u = _inproj(x, mods_mix[0], mods_mix[1], wts['g_mix'], wts['w_in'], tm)
    o_attn = attn_fn(q, k, v)
    o_ssm, h_re, h_im = s5_fn(u)
    x1, shared, counts = _outproj(x, o_attn, o_ssm, mods_mix[2], mods_ffn[0], mods_ffn[1], wts['g_attn_out'],
                                  wts['g_ffn'], wts['w_out'], wts['w_router'], wts['b_router'], tm, t_all, row0,
                                  shared, counts)
    return x1, shared, counts, (k, v, h_re, h_im)


def kernel(x_prompt, x_sample, cache_k_win, cache_v_win, state_ssm_re, state_ssm_im, c_prompt, c_sample, g_mix, w_ada_mix, b_ada_mix, w_in, w_out, g_attn_out, g_ssm_out, ssm_lambda_re, ssm_lambda_im, ssm_log_dt, ssm_b_re, ssm_b_im, ssm_c_re, ssm_c_im, ssm_d, w_glu, b_glu, g_ffn, w_ada_ffn, b_ada_ffn, w_router_coarse, b_router_coarse, w_router_fine, b_router_fine, w_expert_gate, w_expert_up, w_expert_down, g_final):
    depth = g_mix.shape[0]
    assert depth == 1, "single-layer step"
    l = 0
    bp, sp, d = x_prompt.shape
    bs, ts, _ = x_sample.shape
    keep = min(max(w for w, _ in DILATED_PATTERNS), sp)

    c_all = jnp.concatenate([c_prompt, c_sample], axis=0).astype(F32)
    m_mix = _adaln(c_all, w_ada_mix[l], b_ada_mix[l])
    m_ffn = _adaln(c_all, w_ada_ffn[l], b_ada_ffn[l])

    def mods(m, lo, hi, per_token):
        parts = jnp.split(m[lo:hi], 3, axis=-1)
        if per_token:
            return tuple(jnp.repeat(p, ts, axis=0)[None] for p in parts)
        return tuple(p[:, None, :] for p in parts)

    pad = ROUTER_COLS - N_EXPERT_GROUPS - N_EXPERTS
    w_router = jnp.concatenate([w_router_coarse[l], w_router_fine[l], jnp.zeros((d, pad), F32)], axis=1)
    b_router = jnp.concatenate([b_router_coarse[l], b_router_fine[l], jnp.zeros((pad,), F32)])
    wts = {
        'g_mix': g_mix[l], 'w_in': w_in[l].astype(BF16), 'w_out': w_out[l].astype(BF16),
        'g_attn_out': g_attn_out[l], 'g_ffn': g_ffn[l], 'w_router': w_router, 'b_router': b_router,
        'w_gate': w_expert_gate[l], 'w_up': w_expert_up[l], 'w_down': w_expert_down[l],
    }
    a_re, a_im, b_mat, c_mat = _s5_params(ssm_lambda_re[l], ssm_lambda_im[l], ssm_log_dt[l], ssm_b_re[l],
                                          ssm_b_im[l], ssm_c_re[l], ssm_c_im[l])
    s5_w = (a_re, a_im, b_mat, c_mat, ssm_d[l].reshape(-1), w_glu[l].astype(BF16), b_glu[l], g_ssm_out[l])

    t_prompt, t_sample = bp * sp, bs * ts
    t_all = t_prompt + t_sample
    tm_p, tm_s = min(sp, ROW_TILE), min(t_sample, ROW_TILE)
    mods_ffn_p, mods_ffn_s = mods(m_ffn, 0, bp, False), mods(m_ffn, bp, bp + bs, True)

    x1p, shared, counts, (kp, vp, hrp, hip) = _mixer(
        x_prompt, mods(m_mix, 0, bp, False), mods_ffn_p, _attn_prompt, lambda u: _s5_prompt(u, *s5_w), wts,
        tm_p, t_all, 0, None, jnp.zeros((SUBLANES, ROUTER_COLS), F32))

    n_state = SSM_GROUPS * SSM_STATE
    ck = cache_k_win[l]
    cv = cache_v_win[l]

    def attn_s(q, k, v):
        r = lambda a: a.reshape(bs, ts, D_ATTN)
        return _attn_sample(r(q), r(k), r(v), ck, cv).reshape(1, bs * ts, D_ATTN)

    def s5_s(u):
        u_tm = u.reshape(bs, ts, D_SSM).transpose(1, 0, 2)
        o, hr, hi = _s5_sample(u_tm, state_ssm_re[l].reshape(bs, n_state), state_ssm_im[l].reshape(bs, n_state),
                               *s5_w)
        return o.transpose(1, 0, 2).reshape(1, bs * ts, D_SSM), hr, hi

    x1s, shared, counts, (ks, vs, hrs, his) = _mixer(
        x_sample.reshape(1, t_sample, d), mods(m_mix, bp, bp + bs, True), mods_ffn_s, attn_s, s5_s, wts,
        tm_s, t_all, t_prompt, shared, counts)

    h_packed, info, rec = shared
    counts = counts[0, N_EXPERT_GROUPS:N_EXPERT_GROUPS + N_EXPERTS].astype(jnp.int32)
    y01 = _moe(h_packed, rec, counts, wts['w_gate'], wts['w_up'], wts['w_down'])
    yp = _combine(x1p, y01, info, mods_ffn_p[2], g_final, tm_p, 0)
    ys = _combine(x1s, y01, info, mods_ffn_s[2], g_final, tm_s, t_prompt)

    heads = lambda a, b, s: a.reshape(1, b, s, N_HEADS, HEAD_DIM)
    state = lambda a, b: a.reshape(1, b, SSM_GROUPS, SSM_STATE)
    return (yp, ys.reshape(bs, ts, d),
            heads(kp[:, sp - keep:], bp, keep), heads(vp[:, sp - keep:], bp, keep), state(hrp, bp), state(hip, bp),
            heads(ks, bs, ts), heads(vs, bs, ts), state(hrs, bs), state(his, bs))
```

```python
import functools
import math

import numpy as np
import jax
import jax.numpy as jnp
from jax import lax
from jax.experimental import pallas as pl
from jax.experimental.pallas import tpu as pltpu
from jax.experimental.pallas import tpu_sc as plsc

F32 = jnp.float32
BF16 = jnp.bfloat16
HIGHEST = lax.Precision.HIGHEST

D_MODEL = 1024
D_ATTN = 512
D_SSM = 512
HEAD_DIM = 64
N_HEADS = 8
DILATED_PATTERNS = ((128, 1), (512, 4), (2048, 16))
SSM_GROUP_CH = 16
SSM_GROUPS = 32
SSM_STATE = 64
N_EXPERT_GROUPS = 4
EXPERTS_PER_GROUP = 8
N_EXPERTS = 32
D_EXPERT = 512
D_IN_PROJ = 3 * D_ATTN + D_SSM
EPS = 1e-6

LANES = 128
SUBLANES = 8
VMEM_LIMIT = 48 * 1024 * 1024

ATTN_BLOCK = 128
ATTN_UNROLL = 24
NEG_BIG = -1e30
SSM_TT = 128
SSM_PITCH = SSM_TT + 8
ROUTER_COLS = 128
MOE_TILE = 512
ROW_TILE = 512
SC_WINDOW = 128
PACK_CHUNKS = D_MODEL // 2 // LANES
INFO_EXPERT, INFO_RANK, INFO_GATE = 0, 2, 4


def _cparams(sem, vmem=VMEM_LIMIT):
    return pltpu.CompilerParams(dimension_semantics=sem, vmem_limit_bytes=vmem)


def _adaln_kernel(c_ref, w_ref, b_ref, o_ref):
    c = c_ref[...]
    s = c * jax.nn.sigmoid(c)
    o_ref[...] = jnp.dot(s, w_ref[...], precision=HIGHEST, preferred_element_type=F32) + b_ref[...]


def _adaln(c, w, b):
    r, d = c.shape
    n = w.shape[1]
    tn = 768
    return pl.pallas_call(
        _adaln_kernel,
        grid=(n // tn,),
        in_specs=[pl.BlockSpec((r, d), lambda j: (0, 0)),
                  pl.BlockSpec((d, tn), lambda j: (0, j)),
                  pl.BlockSpec((1, tn), lambda j: (0, j))],
        out_specs=pl.BlockSpec((r, tn), lambda j: (0, j)),
        out_shape=jax.ShapeDtypeStruct((r, n), F32),
        compiler_params=_cparams(("arbitrary",)),
    )(c, w, b.reshape(1, n))


def _inproj_kernel(x_ref, sh_ref, sc_ref, g_ref, w_ref, q_ref, k_ref, v_ref, u_ref):
    x = x_ref[...]
    ms = jnp.mean(x * x, axis=-1, keepdims=True)
    h = x * lax.rsqrt(ms + EPS) * g_ref[...]
    h = h * (1.0 + sc_ref[...]) + sh_ref[...]
    p = jnp.dot(h.astype(BF16), w_ref[...], preferred_element_type=F32)
    q_ref[...] = p[:, :D_ATTN] * (HEAD_DIM ** -0.5)
    k_ref[...] = p[:, D_ATTN:2 * D_ATTN]
    v_ref[...] = p[:, 2 * D_ATTN:3 * D_ATTN]
    u_ref[...] = p[:, 3 * D_ATTN:].astype(u_ref.dtype)


def _mod_spec(mod, tm):
    d = mod.shape[-1]
    if mod.shape[1] == 1:
        return pl.BlockSpec((None, 1, d), lambda b, i: (b, 0, 0))
    return pl.BlockSpec((None, tm, d), lambda b, i: (b, i, 0))


def _inproj(x, shift, scale, g, w_bf16, tm):
    nb, s, d = x.shape
    row = lambda n: pl.BlockSpec((None, tm, n), lambda b, i: (b, i, 0))
    out = jax.ShapeDtypeStruct((nb, s, D_ATTN), F32)
    return pl.pallas_call(
        _inproj_kernel,
        grid=(nb, s // tm),
        in_specs=[row(d), _mod_spec(shift, tm), _mod_spec(scale, tm),
                  pl.BlockSpec((1, d), lambda b, i: (0, 0)),
                  pl.BlockSpec((d, D_IN_PROJ), lambda b, i: (0, 0))],
        out_specs=[row(D_ATTN), row(D_ATTN), row(D_ATTN), row(D_SSM)],
        out_shape=[out, out, out, jax.ShapeDtypeStruct((nb, s, D_SSM), BF16)],
        compiler_params=_cparams(("parallel", "parallel")),
    )(x, shift, scale, g.reshape(1, d), w_bf16)


def _band_bias():
    qi = np.arange(ATTN_BLOCK)[:, None]
    kj = np.arange(ATTN_BLOCK)[None, :]
    cur = kj <= qi
    prev = kj >= qi
    to_bias = lambda m: np.tile(np.where(m, 0.0, NEG_BIG).astype(np.float32), (2, 1))
    return to_bias(np.concatenate([prev, cur], axis=1)), to_bias(cur)


def _attn_prompt_kernel(q_ref, k_ref, v_ref, bias2_ref, bias1_ref, o_ref, acc_scr, m_scr, l_scr):
    s_len = q_ref.shape[0]
    lane = lax.broadcasted_iota(jnp.int32, (1, LANES), 1)
    head0 = lane < HEAD_DIM

    def rows(ref, start, n, d):
        if d == 1:
            return ref[pl.ds(start, n), :]
        return ref[pl.ds(start, n, stride=d), :]

    def tiles(items):
        one = jnp.ones((), BF16)
        loaded = []
        for _, d, qstart, first in items:
            nk = ATTN_BLOCK if first else 2 * ATTN_BLOCK
            kstart = qstart if first else qstart - d * ATTN_BLOCK
            loaded.append((rows(q_ref, qstart, ATTN_BLOCK, d), rows(k_ref, kstart, nk, d),
                           rows(v_ref, kstart, nk, d), bias1_ref if first else bias2_ref))
        results = []
        for qrows, krows, vrows, bias_ref in loaded:
            kb = krows.astype(BF16)
            vb = vrows.astype(BF16)
            q2 = jnp.concatenate([jnp.where(head0, qrows, 0.0), jnp.where(head0, 0.0, qrows)], axis=0).astype(BF16)
            s = lax.dot_general(q2, kb, (((1,), (1,)), ((), ())), preferred_element_type=F32) + bias_ref[...]
            m = jnp.max(s, axis=-1, keepdims=True)
            p = jnp.exp(s - m).astype(BF16)
            pv0 = jnp.dot(p[:ATTN_BLOCK], jnp.where(head0, vb, one), preferred_element_type=F32)
            pv1 = jnp.dot(p[ATTN_BLOCK:], jnp.where(head0, one, vb), preferred_element_type=F32)
            results.append((jnp.where(head0, pv0, pv1), jnp.where(head0, m[:ATTN_BLOCK], m[ATTN_BLOCK:]),
                            jnp.where(head0, pv1, pv0)))
        for (br, d, qstart, _), (acc, m, l) in zip(items, results):
            dst = pl.ds(qstart, ATTN_BLOCK) if d == 1 else pl.ds(qstart, ATTN_BLOCK, stride=d)
            acc_scr[br, dst, :] = acc
            m_scr[br, dst, :] = m
            l_scr[br, dst, :] = l

    items = []
    for br, (window, d) in enumerate(DILATED_PATTERNS):
        assert window // d == ATTN_BLOCK
        for r in range(d):
            for jb in range(s_len // (d * ATTN_BLOCK)):
                items.append((br, d, r + d * ATTN_BLOCK * jb, jb == 0))
    for i in range(0, len(items), ATTN_UNROLL):
        tiles(items[i:i + ATTN_UNROLL])

    ch = 256

    def merge_body(i, carry):
        sl = pl.ds(pl.multiple_of(i * ch, ch), ch)
        ms = [m_scr[br, sl, :] for br in range(3)]
        mx = jnp.maximum(jnp.maximum(ms[0], ms[1]), ms[2])
        num = jnp.zeros((ch, LANES), F32)
        den = jnp.zeros((ch, LANES), F32)
        for br in range(3):
            e = jnp.exp(ms[br] - mx)
            num = num + e * acc_scr[br, sl, :]
            den = den + e * pltpu.roll(l_scr[br, sl, :], HEAD_DIM, 1)
        o_ref[sl, :] = (num / den).astype(o_ref.dtype)
        return carry

    lax.fori_loop(0, s_len // ch, merge_body, 0)


def _attn_prompt(q, k, v):
    nb, s, _ = q.shape
    assert s % (16 * ATTN_BLOCK) == 0
    bias2, bias1 = _band_bias()
    blk = pl.BlockSpec((None, s, LANES), lambda b, p: (b, 0, p))
    const = lambda a: pl.BlockSpec(a.shape, lambda b, p: (0, 0))
    return pl.pallas_call(
        _attn_prompt_kernel,
        grid=(nb, D_ATTN // LANES),
        in_specs=[blk, blk, blk, const(bias2), const(bias1)],
        out_specs=blk,
        out_shape=jax.ShapeDtypeStruct((nb, s, D_ATTN), BF16),
        scratch_shapes=[pltpu.VMEM((3, s, LANES), F32)] * 3,
        compiler_params=_cparams(("parallel", "parallel")),
    )(q, k, v, jnp.asarray(bias2), jnp.asarray(bias1))


def _sample_bias(t_new, w_buf):
    n_new = -(-t_new // LANES) * LANES
    t = np.arange(t_new)[:, None]
    idx = np.arange(w_buf + n_new)[None, :]
    dist = w_buf + t - idx
    mult = np.zeros(dist.shape, np.int64)
    for window, d in DILATED_PATTERNS:
        mult += ((dist >= 0) & (dist % d == 0) & (dist <= window)).astype(np.int64)
    bias = np.where(mult > 0, np.log(np.maximum(mult, 1)), NEG_BIG).astype(np.float32)
    bias = np.repeat(bias, N_HEADS, axis=0)
    return bias[:, :w_buf], bias[:, w_buf:]


def _attn_sample_kernel(q_ref, kn_ref, vn_ref, kt_ref, vt_ref, bias_ref, biasn_ref, o_ref):
    t_new = q_ref.shape[0]
    n_new = biasn_ref.shape[1]
    row_head = lax.broadcasted_iota(jnp.int32, (t_new * N_HEADS, D_ATTN), 0) & (N_HEADS - 1)
    col_head = lax.broadcasted_iota(jnp.int32, (t_new * N_HEADS, D_ATTN), 1) >> (HEAD_DIM.bit_length() - 1)
    own = row_head == col_head
    q = q_ref[...]
    q_rows = jnp.concatenate([jnp.broadcast_to(q[t:t + 1], (N_HEADS, D_ATTN)) for t in range(t_new)], axis=0)
    q_bd = jnp.where(own, q_rows, 0.0).astype(BF16)
    pad = jnp.zeros((n_new - t_new, D_ATTN), F32)
    kn = jnp.concatenate([kn_ref[...], pad], axis=0).astype(BF16)
    vn = jnp.concatenate([vn_ref[...], pad], axis=0).astype(BF16)
    nt = (((1,), (1,)), ((), ()))
    s_c = jnp.dot(q_bd, kt_ref[...].astype(BF16), preferred_element_type=F32) + bias_ref[...]
    s_n = lax.dot_general(q_bd, kn, nt, preferred_element_type=F32) + biasn_ref[...]
    m = jnp.maximum(jnp.max(s_c, axis=-1, keepdims=True), jnp.max(s_n, axis=-1, keepdims=True))
    p_c = jnp.exp(s_c - m)
    p_n = jnp.exp(s_n - m)
    l = jnp.sum(p_c, axis=-1, keepdims=True) + jnp.sum(p_n, axis=-1, keepdims=True)
    o = lax.dot_general(p_c.astype(BF16), vt_ref[...].astype(BF16), nt, preferred_element_type=F32)
    o = (o + jnp.dot(p_n.astype(BF16), vn, preferred_element_type=F32)) / l
    o = jnp.where(own, o, 0.0)
    o_ref[...] = jnp.concatenate(
        [jnp.sum(o[t * N_HEADS:(t + 1) * N_HEADS], axis=0, keepdims=True) for t in range(t_new)],
        axis=0).astype(o_ref.dtype)


def _attn_sample(q, k_new, v_new, cache_k, cache_v):
    nb, t_new, _ = q.shape
    w_buf = cache_k.shape[1]
    bias, bias_new = _sample_bias(t_new, w_buf)
    as_stored = lambda a: a.transpose(0, 2, 3, 1).reshape(nb, D_ATTN, w_buf)
    new = pl.BlockSpec((None, t_new, D_ATTN), lambda b: (b, 0, 0))
    buf = pl.BlockSpec((None, D_ATTN, w_buf), lambda b: (b, 0, 0))
    const = lambda a: pl.BlockSpec(a.shape, lambda b: (0, 0))
    return pl.pallas_call(
        _attn_sample_kernel,
        grid=(nb,),
        in_specs=[new, new, new, buf, buf, const(bias), const(bias_new)],
        out_specs=new,
        out_shape=jax.ShapeDtypeStruct((nb, t_new, D_ATTN), BF16),
        compiler_params=_cparams(("parallel",)),
    )(q, k_new, v_new, as_stored(cache_k), as_stored(cache_v), jnp.asarray(bias), jnp.asarray(bias_new))


def _s5_params(lam_re, lam_im, log_dt, b_re, b_im, c_re, c_im):
    f32 = F32
    dt = jnp.exp(log_dt.astype(f32))[:, None]
    lr, li = lam_re.astype(f32), lam_im.astype(f32)
    ea = jnp.exp(lr * dt)
    a_re, a_im = ea * jnp.cos(li * dt), ea * jnp.sin(li * dt)
    den = lr * lr + li * li
    co_re = ((a_re - 1.0) * lr + a_im * li) / den
    co_im = (a_im * lr - (a_re - 1.0) * li) / den
    bb_re = co_re[..., None] * b_re - co_im[..., None] * b_im
    bb_im = co_re[..., None] * b_im + co_im[..., None] * b_re
    eye = jnp.eye(8, dtype=f32)

    def b_blocks(bb):
        t = bb.reshape(4, 8, SSM_STATE, SSM_GROUP_CH)
        return jnp.einsum('ab,kapc->kacbp', eye, t).reshape(4, 8 * SSM_GROUP_CH, 8 * SSM_STATE)

    def c_blocks(cc):
        t = cc.reshape(4, 8, SSM_GROUP_CH, SSM_STATE)
        return jnp.einsum('ab,kacp->kbpac', eye, t).reshape(4, 8 * SSM_STATE, 8 * SSM_GROUP_CH)

    b_mat = jnp.concatenate([b_blocks(bb_re), b_blocks(bb_im)], axis=2).astype(BF16)
    c_mat = jnp.concatenate([c_blocks(c_re.astype(f32)), -c_blocks(c_im.astype(f32))], axis=1).astype(BF16)
    return a_re.reshape(-1), a_im.reshape(-1), b_mat, c_mat


def _gelu_tanh(y):
    return 0.5 * y * (1.0 + jnp.tanh(math.sqrt(2.0 / math.pi) * (y + 0.044715 * (y * y * y))))


def _s5_epilogue(y, u, d_ref, wglu_ref, bglu_ref, g_ref):
    y = y + d_ref[...] * u.astype(F32)
    z = _gelu_tanh(y)
    gate = jnp.dot(z.astype(BF16), wglu_ref[...], preferred_element_type=F32) + bglu_ref[...]
    out = z * jax.nn.sigmoid(gate)
    ms = jnp.mean(out * out, axis=-1, keepdims=True)
    return out * lax.rsqrt(ms + EPS) * g_ref[...]


def _s5_prompt_kernel(u_ref, are_ref, aim_ref, b_ref, c_ref, d_ref, wglu_ref, bglu_ref, g_ref,
                      o_ref, hre_ref, him_ref, scr, hst):
    nseq, tt, _ = u_ref.shape
    n_slab = D_SSM * SSM_STATE // SSM_GROUP_CH // LANES
    ti = pl.program_id(1)

    @pl.when(ti == 0)
    def _():
        hst[...] = jnp.zeros_like(hst)

    ub = u_ref[...].reshape(nseq * tt, D_SSM)
    for kc in range(4):
        bu = jnp.dot(ub[:, kc * LANES:(kc + 1) * LANES], b_ref[kc], preferred_element_type=F32)
        for part in range(2):
            for j in range(4):
                col = part * 512 + j * LANES
                for b in range(nseq):
                    scr[part * n_slab + 4 * kc + j, b * SSM_PITCH:b * SSM_PITCH + tt, :] = (
                        bu[b * tt:(b + 1) * tt, col:col + LANES])

    grp = 4
    for sg in range(n_slab // grp):
        slabs = [sg * grp + i for i in range(grp)]
        ar = [are_ref[s] for s in slabs]
        ai = [aim_ref[s] for s in slabs]

        def step(t, carry, slabs=slabs, ar=ar, ai=ai):
            hr, hi = carry
            nhr, nhi = [], []
            for i, s in enumerate(slabs):
                sel = pl.ds(t, nseq, stride=SSM_PITCH)
                br = scr[s, sel, :]
                bi = scr[n_slab + s, sel, :]
                r = ar[i] * hr[i] - ai[i] * hi[i] + br
                im = ar[i] * hi[i] + ai[i] * hr[i] + bi
                scr[s, sel, :] = r
                scr[n_slab + s, sel, :] = im
                nhr.append(r)
                nhi.append(im)
            return tuple(nhr), tuple(nhi)

        init = (tuple(hst[s] for s in slabs), tuple(hst[n_slab + s] for s in slabs))
        hr, hi = lax.fori_loop(0, tt, step, init, unroll=4)
        for i, s in enumerate(slabs):
            hst[s] = hr[i]
            hst[n_slab + s] = hi[i]

    ys = []
    for kc in range(4):
        slabs = [4 * kc + j for j in range(4)] + [n_slab + 4 * kc + j for j in range(4)]
        lhs = jnp.concatenate(
            [jnp.concatenate([scr[s, b * SSM_PITCH:b * SSM_PITCH + tt, :].astype(BF16) for s in slabs], axis=1)
             for b in range(nseq)], axis=0)
        ys.append(jnp.dot(lhs, c_ref[kc], preferred_element_type=F32))
    y = jnp.concatenate(ys, axis=1)
    out = _s5_epilogue(y, ub, d_ref, wglu_ref, bglu_ref, g_ref)
    o_ref[...] = out.reshape(nseq, tt, D_SSM).astype(o_ref.dtype)

    @pl.when(ti == pl.num_programs(1) - 1)
    def _():
        for s in range(n_slab):
            hre_ref[:, s * LANES:(s + 1) * LANES] = hst[s]
            him_ref[:, s * LANES:(s + 1) * LANES] = hst[n_slab + s]


def _s5_prompt(u, a_re, a_im, b_mat, c_mat, d, w_glu_bf16, b_glu, g_out):
    nb, s, _ = u.shape
    nseq = SUBLANES
    assert nb % nseq == 0 and s % SSM_TT == 0
    n_state = a_re.shape[0]
    n_slab = n_state // LANES
    bcast = lambda a: jnp.broadcast_to(a.reshape(n_slab, 1, LANES), (n_slab, nseq, LANES))
    const = lambda a: pl.BlockSpec(a.shape, lambda b, i: (0,) * a.ndim)
    args = (u, bcast(a_re), bcast(a_im), b_mat, c_mat, d.reshape(1, D_SSM), w_glu_bf16,
            b_glu.reshape(1, D_SSM), g_out.reshape(1, D_SSM))
    st = jax.ShapeDtypeStruct((nb, n_state), F32)
    return pl.pallas_call(
        _s5_prompt_kernel,
        grid=(nb // nseq, s // SSM_TT),
        in_specs=[pl.BlockSpec((nseq, SSM_TT, D_SSM), lambda b, i: (b, i, 0))] + [const(a) for a in args[1:]],
        out_specs=[pl.BlockSpec((nseq, SSM_TT, D_SSM), lambda b, i: (b, i, 0)),
                   pl.BlockSpec((nseq, n_state), lambda b, i: (b, 0)),
                   pl.BlockSpec((nseq, n_state), lambda b, i: (b, 0))],
        out_shape=[jax.ShapeDtypeStruct((nb, s, D_SSM), BF16), st, st],
        scratch_shapes=[pltpu.VMEM((2 * n_slab, nseq * SSM_PITCH, LANES), F32),
                        pltpu.VMEM((2 * n_slab, nseq, LANES), F32)],
        compiler_params=_cparams(("parallel", "arbitrary")),
    )(*args)


def _s5_sample_kernel(u_ref, h0re_ref, h0im_ref, are_ref, aim_ref, b_ref, c_ref, d_ref, wglu_ref, bglu_ref, g_ref,
                      o_ref, hre_ref, him_ref):
    t_new = u_ref.shape[0]
    hre = h0re_ref[...]
    him = h0im_ref[...]
    are, aim = are_ref[...], aim_ref[...]
    half = 4 * LANES
    for t in range(t_new):
        ub = u_ref[t]
        bus = [jnp.dot(ub[:, kc * LANES:(kc + 1) * LANES], b_ref[kc], preferred_element_type=F32) for kc in range(4)]
        bre = jnp.concatenate([bu[:, :half] for bu in bus], axis=1)
        bim = jnp.concatenate([bu[:, half:] for bu in bus], axis=1)
        hre, him = are * hre - aim * him + bre, are * him + aim * hre + bim
        ys = []
        for kc in range(4):
            lhs = jnp.concatenate([hre[:, kc * half:(kc + 1) * half], him[:, kc * half:(kc + 1) * half]], axis=1)
            ys.append(jnp.dot(lhs.astype(BF16), c_ref[kc], preferred_element_type=F32))
        y = jnp.concatenate(ys, axis=1)
        o_ref[t] = _s5_epilogue(y, ub, d_ref, wglu_ref, bglu_ref, g_ref).astype(o_ref.dtype)
    hre_ref[...] = hre
    him_ref[...] = him


def _s5_sample(u_tm, h0_re, h0_im, a_re, a_im, b_mat, c_mat, d, w_glu_bf16, b_glu, g_out):
    t_new, nb, _ = u_tm.shape
    n_state = a_re.shape[0]
    tb = 64
    assert nb % tb == 0
    const = lambda a: pl.BlockSpec(a.shape, lambda b: (0,) * a.ndim)
    args = (u_tm, h0_re, h0_im, a_re.reshape(1, n_state), a_im.reshape(1, n_state), b_mat, c_mat,
            d.reshape(1, D_SSM), w_glu_bf16, b_glu.reshape(1, D_SSM), g_out.reshape(1, D_SSM))
    st_spec = pl.BlockSpec((tb, n_state), lambda b: (b, 0))
    st = jax.ShapeDtypeStruct((nb, n_state), F32)
    return pl.pallas_call(
        _s5_sample_kernel,
        grid=(nb // tb,),
        in_specs=[pl.BlockSpec((t_new, tb, D_SSM), lambda b: (0, b, 0)), st_spec, st_spec]
                 + [const(a) for a in args[3:]],
        out_specs=[pl.BlockSpec((t_new, tb, D_SSM), lambda b: (0, b, 0)), st_spec, st_spec],
        out_shape=[jax.ShapeDtypeStruct((t_new, nb, D_SSM), BF16), st, st],
        compiler_params=_cparams(("parallel",)),
    )(*args)


def _store_packed(ref, val):
    half = val.shape[1] // 2
    bits = pltpu.bitcast(val.astype(BF16).astype(F32), jnp.uint32)
    words = bits[:, :half] | lax.shift_right_logical(bits[:, half:], jnp.uint32(16))
    for c in range(PACK_CHUNKS):
        ref[c] = words[:, c * LANES:(c + 1) * LANES]


def _load_packed(ref):
    hi, lo = [], []
    for c in range(PACK_CHUNKS):
        w = ref[c]
        hi.append(pltpu.bitcast(w & jnp.uint32(0xFFFF0000), F32))
        lo.append(pltpu.bitcast(lax.shift_left(w, jnp.uint32(16)), F32))
    return jnp.concatenate(hi + lo, axis=1)


def _route_tile(lg, tri_ref, carry):
    lane = lax.broadcasted_iota(jnp.int32, lg.shape, 1)
    lane_f = lane.astype(F32)
    none = float(ROUTER_COLS)
    ninf = float("-inf")
    first = lambda cond: jnp.min(jnp.where(cond, lane_f, none), axis=-1, keepdims=True)

    is_c = lane < N_EXPERT_GROUPS
    lc = jnp.where(is_c, lg, ninf)
    mc = jnp.max(lc, axis=-1, keepdims=True)
    p_grp = 1.0 / jnp.sum(jnp.exp(lc - mc), axis=-1, keepdims=True)
    grp = first(lc == mc)
    fine = lane - N_EXPERT_GROUPS
    fine_grp = lax.shift_right_arithmetic(fine, jnp.int32(EXPERTS_PER_GROUP.bit_length() - 1))
    in_grp = (fine >= 0) & (fine < N_EXPERTS) & (fine_grp.astype(F32) == grp)
    lf = jnp.where(in_grp, lg, ninf)
    v1 = jnp.max(lf, axis=-1, keepdims=True)
    i1 = first(lf == v1)
    lf2 = jnp.where(lane_f == i1, ninf, lf)
    v2 = jnp.max(lf2, axis=-1, keepdims=True)
    i2 = first(lf2 == v2)
    b = jnp.exp(v2 - v1)
    g0 = p_grp / (1.0 + b)
    g1 = p_grp * b / (1.0 + b)

    hit1 = lane_f == i1
    hit2 = lane_f == i2
    onehot = jnp.where(hit1 | hit2, 1.0, 0.0)
    before = jnp.dot(tri_ref[...], onehot.astype(BF16), preferred_element_type=F32) + carry[0:1, :]
    r1 = jnp.sum(jnp.where(hit1, before, 0.0), axis=-1, keepdims=True)
    r2 = jnp.sum(jnp.where(hit2, before, 0.0), axis=-1, keepdims=True)
    carry[...] = carry[...] + jnp.sum(onehot, axis=0, keepdims=True)

    as_int = lambda v: jnp.broadcast_to(v, lg.shape).astype(jnp.int32)
    as_bits = lambda v: pltpu.bitcast(jnp.broadcast_to(v, lg.shape), jnp.int32)
    info = jnp.zeros(lg.shape, jnp.int32)
    fields = ((INFO_EXPERT, as_int(i1 - N_EXPERT_GROUPS)), (INFO_EXPERT + 1, as_int(i2 - N_EXPERT_GROUPS)),
              (INFO_RANK, as_int(r1)), (INFO_RANK + 1, as_int(r2)), (INFO_GATE, as_bits(g0)),
              (INFO_GATE + 1, as_bits(g1)))
    for col, val in fields:
        info = jnp.where(lane == col, val, info)
    return info


def _outproj_kernel(x_ref, oa_ref, os_ref, gate_ref, sh_ref, sc_ref, ga_ref, gf_ref, w_ref, wr_hi_ref, wr_lo_ref,
                    br_ref, tri_ref, cnt_in_ref, *rest):
    x1_ref, h_ref, info_ref, rec_ref, cnt_ref, carry = rest[-6:]
    step = pl.program_id(0) * pl.num_programs(1) + pl.program_id(1)

    @pl.when(step == 0)
    def _():
        carry[...] = cnt_in_ref[...]

    oa = oa_ref[...].astype(F32)
    ms = jnp.mean(oa * oa, axis=-1, keepdims=True)
    na = oa * lax.rsqrt(ms + EPS) * ga_ref[...]
    merged = jnp.concatenate([na.astype(BF16), os_ref[...]], axis=-1)
    x1 = x_ref[...] + gate_ref[...] * jnp.dot(merged, w_ref[...], preferred_element_type=F32)
    x1_ref[...] = x1
    ms = jnp.mean(x1 * x1, axis=-1, keepdims=True)
    h = x1 * lax.rsqrt(ms + EPS) * gf_ref[...]
    h = h * (1.0 + sc_ref[...]) + sh_ref[...]
    h_hi = h.astype(BF16)
    _store_packed(h_ref, h)
    h_lo = (h - h_hi.astype(F32)).astype(BF16)
    lg = jnp.dot(h_hi, wr_hi_ref[...], preferred_element_type=F32)
    lg = lg + jnp.dot(h_hi, wr_lo_ref[...], preferred_element_type=F32)
    lg = lg + jnp.dot(h_lo, wr_hi_ref[...], preferred_element_type=F32)
    info = _route_tile(lg + br_ref[...], tri_ref, carry)
    info_ref[...] = info
    rec_ref[...] = pltpu.bitcast(pltpu.bitcast(info, F32).T[:SUBLANES], jnp.int32)

    @pl.when(step == pl.num_programs(0) * pl.num_programs(1) - 1)
    def _():
        cnt_ref[...] = carry[...]


def _outproj(x, o_attn, o_ssm, gate, shift, scale, g_attn, g_ffn, w_out_bf16, w_router, b_router, tm, t_all, row0,
             shared, counts):
    nb, s, d = x.shape
    assert row0 % tm == 0
    blk = lambda b, i: row0 // tm + b * (s // tm) + i
    row = lambda n: pl.BlockSpec((None, tm, n), lambda b, i: (b, i, 0))
    const = lambda a: pl.BlockSpec(a.shape, lambda b, i: (0,) * a.ndim)
    wr_hi = w_router.astype(BF16)
    wr_lo = (w_router - wr_hi.astype(F32)).astype(BF16)
    tri = jnp.asarray(np.tril(np.ones((tm, tm), np.float32), -1), BF16)
    consts = (g_attn.reshape(1, D_ATTN), g_ffn.reshape(1, d), w_out_bf16, wr_hi, wr_lo,
              b_router.reshape(1, ROUTER_COLS), tri, counts)
    n_in = 6 + len(consts)
    count_spec = pl.BlockSpec((SUBLANES, ROUTER_COLS), lambda b, i: (0, 0))
    out = pl.pallas_call(
        _outproj_kernel,
        grid=(nb, s // tm),
        in_specs=[row(d), row(D_ATTN), row(D_SSM), _mod_spec(gate, tm), _mod_spec(shift, tm), _mod_spec(scale, tm)]
                 + [const(a) for a in consts] + [pl.BlockSpec(memory_space=pl.ANY)] * len(shared or ()),
        out_specs=[row(d), pl.BlockSpec((PACK_CHUNKS, tm, LANES), lambda b, i: (0, blk(b, i), 0)),
                   pl.BlockSpec((tm, ROUTER_COLS), lambda b, i: (blk(b, i), 0)),
                   pl.BlockSpec((SUBLANES, tm), lambda b, i: (0, blk(b, i))), count_spec],
        out_shape=[jax.ShapeDtypeStruct((nb, s, d), F32),
                   jax.ShapeDtypeStruct((PACK_CHUNKS, t_all, LANES), jnp.uint32),
                   jax.ShapeDtypeStruct((t_all, ROUTER_COLS), jnp.int32),
                   jax.ShapeDtypeStruct((SUBLANES, t_all), jnp.int32),
                   jax.ShapeDtypeStruct((SUBLANES, ROUTER_COLS), F32)],
        scratch_shapes=[pltpu.VMEM((SUBLANES, ROUTER_COLS), F32)],
        input_output_aliases={n_in + j: 1 + j for j in range(len(shared or ()))},
        compiler_params=_cparams(("arbitrary", "arbitrary")),
    )(x, o_attn, o_ssm, gate, shift, scale, *consts, *(shared or ()))
    return out[0], tuple(out[1:4]), out[4]


def _expert_kernel(blk_e_ref, nvalid_ref, x_ref, wg_ref, wu_ref, wd_ref, y_ref, wg_s, wu_s, wd_s):
    i = pl.program_id(0)

    @pl.when((i == 0) | (blk_e_ref[i] != blk_e_ref[jnp.maximum(i - 1, 0)]))
    def _():
        wg_s[...] = wg_ref[...].astype(BF16)
        wu_s[...] = wu_ref[...].astype(BF16)
        wd_s[...] = wd_ref[...].astype(BF16)

    @pl.when(i < nvalid_ref[0])
    def _():
        x = _load_packed(x_ref).astype(BF16)
        a = jnp.dot(x, wg_s[...], preferred_element_type=F32)
        b = jnp.dot(x, wu_s[...], preferred_element_type=F32)
        hid = (a * jax.nn.sigmoid(a) * b).astype(BF16)
        _store_packed(y_ref, jnp.dot(hid, wd_s[...], preferred_element_type=F32))

    @pl.when(i >= nvalid_ref[0])
    def _():
        y_ref[...] = jnp.zeros_like(y_ref)


def _experts(xs, blk_e, nvalid, w_gate, w_up, w_down):
    nblk = xs.shape[1] // MOE_TILE
    d = w_gate.shape[1]
    wspec = lambda shp: pl.BlockSpec((None,) + shp, lambda i, be, nv: (be[i], 0, 0))
    tile = pl.BlockSpec((PACK_CHUNKS, MOE_TILE, LANES), lambda i, be, nv: (0, i, 0))
    return pl.pallas_call(
        _expert_kernel,
        grid_spec=pltpu.PrefetchScalarGridSpec(
            num_scalar_prefetch=2,
            grid=(nblk,),
            in_specs=[tile, wspec((d, D_EXPERT)), wspec((d, D_EXPERT)), wspec((D_EXPERT, d))],
            out_specs=tile,
            scratch_shapes=[pltpu.VMEM((d, D_EXPERT), BF16), pltpu.VMEM((d, D_EXPERT), BF16),
                            pltpu.VMEM((D_EXPERT, d), BF16)],
        ),
        out_shape=jax.ShapeDtypeStruct(xs.shape, jnp.uint32),
        compiler_params=_cparams(("arbitrary",)),
    )(blk_e, nvalid, xs, w_gate, w_up, w_down)


def _combine_kernel(x1_ref, y0_ref, y1_ref, info_ref, gate_ref, gfin_ref, o_ref):
    info = pltpu.bitcast(info_ref[...], F32)
    g0 = info[:, INFO_GATE:INFO_GATE + 1]
    g1 = info[:, INFO_GATE + 1:INFO_GATE + 2]
    moe = g0 * _load_packed(y0_ref) + g1 * _load_packed(y1_ref)
    x2 = x1_ref[...] + gate_ref[...] * moe
    ms = jnp.mean(x2 * x2, axis=-1, keepdims=True)
    o_ref[...] = x2 * lax.rsqrt(ms + EPS) * gfin_ref[...]


def _combine(x1, y01, info, gate, g_final, tm, row0):
    nb, s, d = x1.shape
    assert row0 % tm == 0
    blk = lambda b, i: row0 // tm + b * (s // tm) + i
    row = lambda n: pl.BlockSpec((None, tm, n), lambda b, i: (b, i, 0))
    packed = lambda k: pl.BlockSpec((None, PACK_CHUNKS, tm, LANES), lambda b, i: (k, 0, blk(b, i), 0))
    return pl.pallas_call(
        _combine_kernel,
        grid=(nb, s // tm),
        in_specs=[row(d), packed(0), packed(1), pl.BlockSpec((tm, ROUTER_COLS), lambda b, i: (blk(b, i), 0)),
                  _mod_spec(gate, tm), pl.BlockSpec((1, d), lambda b, i: (0, 0))],
        out_specs=row(d),
        out_shape=jax.ShapeDtypeStruct((nb, s, d), F32),
        compiler_params=_cparams(("parallel", "parallel")),
    )(x1, y01, y01, info, gate, g_final.reshape(1, d))


def _sc_window(n_rows):
    assert n_rows % SC_WINDOW == 0
    return SC_WINDOW


def _sc_gather_rows(table, idx):
    n = idx.shape[0]
    w = _sc_window(n)
    mesh = plsc.VectorSubcoreMesh(core_axis_name="core", subcore_axis_name="subcore")

    @functools.partial(pl.kernel, out_type=jax.ShapeDtypeStruct((n, LANES), table.dtype), mesh=mesh)
    def gather_kernel(x_hbm, i_hbm, o_hbm):
        def body(i_vmem, o_vmem):
            pltpu.sync_copy(x_hbm.at[i_vmem.at[0]], o_vmem)

        pltpu.emit_pipeline(
            body, grid=(n // w,),
            in_specs=[pl.BlockSpec((1, w), lambda i: (0, i))],
            out_specs=[pl.BlockSpec((w, LANES), lambda i: (i, 0))],
            core_axis_name=("core", "subcore"), dimension_semantics=(pltpu.PARALLEL,),
        )(i_hbm, o_hbm)

    return gather_kernel(table, idx.reshape(1, n))


def _sc_scatter_rows(rows, idx0, idx1, n_out):
    n = rows.shape[0]
    w = _sc_window(n)
    mesh = plsc.VectorSubcoreMesh(core_axis_name="core", subcore_axis_name="subcore")

    @functools.partial(pl.kernel, out_type=jax.ShapeDtypeStruct((n_out, LANES), rows.dtype), mesh=mesh)
    def scatter_kernel(x_hbm, i0_hbm, i1_hbm, o_hbm):
        def body(x_vmem, i0_vmem, i1_vmem):
            pltpu.sync_copy(x_vmem, o_hbm.at[i0_vmem.at[0]])
            pltpu.sync_copy(x_vmem, o_hbm.at[i1_vmem.at[0]])

        pltpu.emit_pipeline(
            body, grid=(n // w,),
            in_specs=[pl.BlockSpec((w, LANES), lambda i: (i, 0)),
                      pl.BlockSpec((1, w), lambda i: (0, i)),
                      pl.BlockSpec((1, w), lambda i: (0, i))],
            out_specs=[],
            core_axis_name=("core", "subcore"), dimension_semantics=(pltpu.PARALLEL,),
        )(x_hbm, i0_hbm, i1_hbm)

    return scatter_kernel(rows, idx0.reshape(1, n), idx1.reshape(1, n))


def _moe(h_packed, rec, counts, w_gate, w_up, w_down):
    t = rec.shape[1]
    padded = (counts + MOE_TILE - 1) // MOE_TILE * MOE_TILE
    pend = jnp.cumsum(padded)
    pstart = pend - padded
    nblk = -(-2 * t // MOE_TILE) + N_EXPERTS
    n_slots = nblk * MOE_TILE
    blk_start = jnp.arange(nblk, dtype=jnp.int32) * MOE_TILE
    blk_e = jnp.minimum(jnp.sum(blk_start[:, None] >= pend[None, :], axis=1), N_EXPERTS - 1).astype(jnp.int32)
    nvalid = (pend[-1] // MOE_TILE).astype(jnp.int32).reshape(1)
    chunk_base = (jnp.arange(PACK_CHUNKS, dtype=jnp.int32) * n_slots)[:, None]
    rows = lambda k: (chunk_base + (jnp.take(pstart, rec[INFO_EXPERT + k]).astype(jnp.int32)
                                    + rec[INFO_RANK + k])[None, :]).reshape(-1)
    idx0, idx1 = rows(0), rows(1)
    xs = _sc_scatter_rows(h_packed.reshape(PACK_CHUNKS * t, LANES), idx0, idx1, PACK_CHUNKS * n_slots)
    ys = _experts(xs.reshape(PACK_CHUNKS, n_slots, LANES), blk_e, nvalid, w_gate, w_up, w_down)
    y01 = _sc_gather_rows(ys.reshape(PACK_CHUNKS * n_slots, LANES), jnp.concatenate([idx0, idx1]))
    return y01.reshape(2, PACK_CHUNKS, t, LANES)


def _mixer(x, mods_mix, mods_ffn, attn_fn, s5_fn, wts, tm, t_all, row0, shared, counts):
    q, k, v, u = _inproj(x, mods_mix[0], mods_mix[1], wts['g_mix'], wts['w_in'], tm)
    o_attn = attn_fn(q, k, v)
    o_ssm, h_re, h_im = s5_fn(u)
    x1, shared, counts = _outproj(x, o_attn, o_ssm, mods_mix[2], mods_ffn[0], mods_ffn[1], wts['g_attn_out'],
                                  wts['g_ffn'], wts['w_out'], wts['w_router'], wts['b_router'], tm, t_all, row0,
                                  shared, counts)
    return x1, shared, counts, (k, v, h_re, h_im)


def kernel(x_prompt, x_sample, cache_k_win, cache_v_win, state_ssm_re, state_ssm_im, c_prompt, c_sample, g_mix, w_ada_mix, b_ada_mix, w_in, w_out, g_attn_out, g_ssm_out, ssm_lambda_re, ssm_lambda_im, ssm_log_dt, ssm_b_re, ssm_b_im, ssm_c_re, ssm_c_im, ssm_d, w_glu, b_glu, g_ffn, w_ada_ffn, b_ada_ffn, w_router_coarse, b_router_coarse, w_router_fine, b_router_fine, w_expert_gate, w_expert_up, w_expert_down, g_final):
    depth = g_mix.shape[0]
    assert depth == 1, "single-layer step"
    l = 0
    bp, sp, d = x_prompt.shape
    bs, ts, _ = x_sample.shape
    keep = min(max(w for w, _ in DILATED_PATTERNS), sp)

    c_all = jnp.concatenate([c_prompt, c_sample], axis=0).astype(F32)
    m_mix = _adaln(c_all, w_ada_mix[l], b_ada_mix[l])
    m_ffn = _adaln(c_all, w_ada_ffn[l], b_ada_ffn[l])

    def mods(m, lo, hi, per_token):
        parts = jnp.split(m[lo:hi], 3, axis=-1)
        if per_token:
            return tuple(jnp.repeat(p, ts, axis=0)[None] for p in parts)
        return tuple(p[:, None, :] for p in parts)

    pad = ROUTER_COLS - N_EXPERT_GROUPS - N_EXPERTS
    w_router = jnp.concatenate([w_router_coarse[l], w_router_fine[l], jnp.zeros((d, pad), F32)], axis=1)
    b_router = jnp.concatenate([b_router_coarse[l], b_router_fine[l], jnp.zeros((pad,), F32)])
    wts = {
        'g_mix': g_mix[l], 'w_in': w_in[l].astype(BF16), 'w_out': w_out[l].astype(BF16),
        'g_attn_out': g_attn_out[l], 'g_ffn': g_ffn[l], 'w_router': w_router, 'b_router': b_router,
        'w_gate': w_expert_gate[l], 'w_up': w_expert_up[l], 'w_down': w_expert_down[l],
    }
    a_re, a_im, b_mat, c_mat = _s5_params(ssm_lambda_re[l], ssm_lambda_im[l], ssm_log_dt[l], ssm_b_re[l],
                                          ssm_b_im[l], ssm_c_re[l], ssm_c_im[l])
    s5_w = (a_re, a_im, b_mat, c_mat, ssm_d[l].reshape(-1), w_glu[l].astype(BF16), b_glu[l], g_ssm_out[l])

    t_prompt, t_sample = bp * sp, bs * ts
    t_all = t_prompt + t_sample
    tm_p, tm_s = min(sp, ROW_TILE), min(t_sample, ROW_TILE)
    mods_ffn_p, mods_ffn_s = mods(m_ffn, 0, bp, False), mods(m_ffn, bp, bp + bs, True)

    x1p, shared, counts, (kp, vp, hrp, hip) = _mixer(
        x_prompt, mods(m_mix, 0, bp, False), mods_ffn_p, _attn_prompt, lambda u: _s5_prompt(u, *s5_w), wts,
        tm_p, t_all, 0, None, jnp.zeros((SUBLANES, ROUTER_COLS), F32))

    n_state = SSM_GROUPS * SSM_STATE
    ck = cache_k_win[l]
    cv = cache_v_win[l]

    def attn_s(q, k, v):
        r = lambda a: a.reshape(bs, ts, D_ATTN)
        return _attn_sample(r(q), r(k), r(v), ck, cv).reshape(1, bs * ts, D_ATTN)

    def s5_s(u):
        u_tm = u.reshape(bs, ts, D_SSM).transpose(1, 0, 2)
        o, hr, hi = _s5_sample(u_tm, state_ssm_re[l].reshape(bs, n_state), state_ssm_im[l].reshape(bs, n_state),
                               *s5_w)
        return o.transpose(1, 0, 2).reshape(1, bs * ts, D_SSM), hr, hi

    x1s, shared, counts, (ks, vs, hrs, his) = _mixer(
        x_sample.reshape(1, t_sample, d), mods(m_mix, bp, bp + bs, True), mods_ffn_s, attn_s, s5_s, wts,
        tm_s, t_all, t_prompt, shared, counts)

    h_packed, info, rec = shared
    counts = counts[0, N_EXPERT_GROUPS:N_EXPERT_GROUPS + N_EXPERTS].astype(jnp.int32)
    y01 = _moe(h_packed, rec, counts, wts['w_gate'], wts['w_up'], wts['w_down'])
    yp = _combine(x1p, y01, info, mods_ffn_p[2], g_final, tm_p, 0)
    ys = _combine(x1s, y01, info, mods_ffn_s[2], g_final, tm_s, t_prompt)

    heads = lambda a, b, s: a.reshape(1, b, s, N_HEADS, HEAD_DIM)
    state = lambda a, b: a.reshape(1, b, SSM_GROUPS, SSM_STATE)
    return (yp, ys.reshape(bs, ts, d),
            heads(kp[:, sp - keep:], bp, keep), heads(vp[:, sp - keep:], bp, keep), state(hrp, bp), state(hip, bp),
            heads(ks, bs, ts), heads(vs, bs, ts), state(hrs, bs), state(his, bs))
```

```python
import functools
import math

import numpy as np
import jax
import jax.numpy as jnp
from jax import lax
from jax.experimental import pallas as pl
from jax.experimental.pallas import tpu as pltpu
from jax.experimental.pallas import tpu_sc as plsc

F32 = jnp.float32
BF16 = jnp.bfloat16
HIGHEST = lax.Precision.HIGHEST

D_MODEL = 1024
D_ATTN = 512
D_SSM = 512
HEAD_DIM = 64
N_HEADS = 8
DILATED_PATTERNS = ((128, 1), (512, 4), (2048, 16))
SSM_GROUP_CH = 16
SSM_GROUPS = 32
SSM_STATE = 64
N_EXPERT_GROUPS = 4
EXPERTS_PER_GROUP = 8
N_EXPERTS = 32
D_EXPERT = 512
D_IN_PROJ = 3 * D_ATTN + D_SSM
EPS = 1e-6

LANES = 128
SUBLANES = 8
VMEM_LIMIT = 48 * 1024 * 1024

ATTN_BLOCK = 128
ATTN_UNROLL = 48
NEG_BIG = -1e30
SSM_TT = 128
SSM_PITCH = SSM_TT + 8
ROUTER_COLS = 128
MOE_TILE = 512
ROW_TILE = 512
SC_WINDOW = 128
PACK_CHUNKS = D_MODEL // 2 // LANES
INFO_EXPERT, INFO_RANK, INFO_GATE = 0, 2, 4


def _cparams(sem, vmem=VMEM_LIMIT):
    return pltpu.CompilerParams(dimension_semantics=sem, vmem_limit_bytes=vmem)


def _adaln_kernel(c_ref, w_ref, b_ref, o_ref):
    c = c_ref[...]
    s = c * jax.nn.sigmoid(c)
    o_ref[...] = jnp.dot(s, w_ref[...], precision=HIGHEST, preferred_element_type=F32) + b_ref[...]


def _adaln(c, w, b):
    r, d = c.shape
    n = w.shape[1]
    tn = 768
    return pl.pallas_call(
        _adaln_kernel,
        grid=(n // tn,),
        in_specs=[pl.BlockSpec((r, d), lambda j: (0, 0)),
                  pl.BlockSpec((d, tn), lambda j: (0, j)),
                  pl.BlockSpec((1, tn), lambda j: (0, j))],
        out_specs=pl.BlockSpec((r, tn), lambda j: (0, j)),
        out_shape=jax.ShapeDtypeStruct((r, n), F32),
        compiler_params=_cparams(("arbitrary",)),
    )(c, w, b.reshape(1, n))


def _inproj_kernel(x_ref, sh_ref, sc_ref, g_ref, w_ref, q_ref, k_ref, v_ref, u_ref):
    x = x_ref[...]
    ms = jnp.mean(x * x, axis=-1, keepdims=True)
    h = x * lax.rsqrt(ms + EPS) * g_ref[...]
    h = h * (1.0 + sc_ref[...]) + sh_ref[...]
    p = jnp.dot(h.astype(BF16), w_ref[...], preferred_element_type=F32)
    q_ref[...] = p[:, :D_ATTN] * (HEAD_DIM ** -0.5)
    k_ref[...] = p[:, D_ATTN:2 * D_ATTN]
    v_ref[...] = p[:, 2 * D_ATTN:3 * D_ATTN]
    u_ref[...] = p[:, 3 * D_ATTN:].astype(u_ref.dtype)


def _mod_spec(mod, tm):
    d = mod.shape[-1]
    if mod.shape[1] == 1:
        return pl.BlockSpec((None, 1, d), lambda b, i: (b, 0, 0))
    return pl.BlockSpec((None, tm, d), lambda b, i: (b, i, 0))


def _inproj(x, shift, scale, g, w_bf16, tm):
    nb, s, d = x.shape
    row = lambda n: pl.BlockSpec((None, tm, n), lambda b, i: (b, i, 0))
    out = jax.ShapeDtypeStruct((nb, s, D_ATTN), F32)
    return pl.pallas_call(
        _inproj_kernel,
        grid=(nb, s // tm),
        in_specs=[row(d), _mod_spec(shift, tm), _mod_spec(scale, tm),
                  pl.BlockSpec((1, d), lambda b, i: (0, 0)),
                  pl.BlockSpec((d, D_IN_PROJ), lambda b, i: (0, 0))],
        out_specs=[row(D_ATTN), row(D_ATTN), row(D_ATTN), row(D_SSM)],
        out_shape=[out, out, out, jax.ShapeDtypeStruct((nb, s, D_SSM), BF16)],
        compiler_params=_cparams(("parallel", "parallel")),
    )(x, shift, scale, g.reshape(1, d), w_bf16)


def _band_bias():
    qi = np.arange(ATTN_BLOCK)[:, None]
    kj = np.arange(ATTN_BLOCK)[None, :]
    cur = kj <= qi
    prev = kj >= qi
    to_bias = lambda m: np.tile(np.where(m, 0.0, NEG_BIG).astype(np.float32), (2, 1))
    return to_bias(np.concatenate([prev, cur], axis=1)), to_bias(cur)


def _attn_prompt_kernel(q_ref, k_ref, v_ref, bias2_ref, bias1_ref, o_ref, acc_scr, m_scr, l_scr):
    s_len = q_ref.shape[0]
    lane = lax.broadcasted_iota(jnp.int32, (1, LANES), 1)
    head0 = lane < HEAD_DIM

    def rows(ref, start, n, d):
        if d == 1:
            return ref[pl.ds(start, n), :]
        return ref[pl.ds(start, n, stride=d), :]

    def tiles(items):
        one = jnp.ones((), BF16)
        loaded = []
        for _, d, qstart, first in items:
            nk = ATTN_BLOCK if first else 2 * ATTN_BLOCK
            kstart = qstart if first else qstart - d * ATTN_BLOCK
            loaded.append((rows(q_ref, qstart, ATTN_BLOCK, d), rows(k_ref, kstart, nk, d),
                           rows(v_ref, kstart, nk, d), bias1_ref if first else bias2_ref))
        results = []
        for qrows, krows, vrows, bias_ref in loaded:
            kb = krows.astype(BF16)
            vb = vrows.astype(BF16)
            q2 = jnp.concatenate([jnp.where(head0, qrows, 0.0), jnp.where(head0, 0.0, qrows)], axis=0).astype(BF16)
            s = lax.dot_general(q2, kb, (((1,), (1,)), ((), ())), preferred_element_type=F32) + bias_ref[...]
            m = jnp.max(s, axis=-1, keepdims=True)
            p = jnp.exp(s - m).astype(BF16)
            pv0 = jnp.dot(p[:ATTN_BLOCK], jnp.where(head0, vb, one), preferred_element_type=F32)
            pv1 = jnp.dot(p[ATTN_BLOCK:], jnp.where(head0, one, vb), preferred_element_type=F32)
            results.append((jnp.where(head0, pv0, pv1), jnp.where(head0, m[:ATTN_BLOCK], m[ATTN_BLOCK:]),
                            jnp.where(head0, pv1, pv0)))
        for (br, d, qstart, _), (acc, m, l) in zip(items, results):
            dst = pl.ds(qstart, ATTN_BLOCK) if d == 1 else pl.ds(qstart, ATTN_BLOCK, stride=d)
            acc_scr[br, dst, :] = acc
            m_scr[br, dst, :] = m
            l_scr[br, dst, :] = l

    items = []
    for br, (window, d) in enumerate(DILATED_PATTERNS):
        assert window // d == ATTN_BLOCK
        for r in range(d):
            for jb in range(s_len // (d * ATTN_BLOCK)):
                items.append((br, d, r + d * ATTN_BLOCK * jb, jb == 0))
    for i in range(0, len(items), ATTN_UNROLL):
        tiles(items[i:i + ATTN_UNROLL])

    ch = 256

    def merge_body(i, carry):
        sl = pl.ds(pl.multiple_of(i * ch, ch), ch)
        ms = [m_scr[br, sl, :] for br in range(3)]
        mx = jnp.maximum(jnp.maximum(ms[0], ms[1]), ms[2])
        num = jnp.zeros((ch, LANES), F32)
        den = jnp.zeros((ch, LANES), F32)
        for br in range(3):
            e = jnp.exp(ms[br] - mx)
            num = num + e * acc_scr[br, sl, :]
            den = den + e * pltpu.roll(l_scr[br, sl, :], HEAD_DIM, 1)
        o_ref[sl, :] = (num / den).astype(o_ref.dtype)
        return carry

    lax.fori_loop(0, s_len // ch, merge_body, 0)


def _attn_prompt(q, k, v):
    nb, s, _ = q.shape
    assert s % (16 * ATTN_BLOCK) == 0
    bias2, bias1 = _band_bias()
    blk = pl.BlockSpec((None, s, LANES), lambda b, p: (b, 0, p))
    const = lambda a: pl.BlockSpec(a.shape, lambda b, p: (0, 0))
    return pl.pallas_call(
        _attn_prompt_kernel,
        grid=(nb, D_ATTN // LANES),
        in_specs=[blk, blk, blk, const(bias2), const(bias1)],
        out_specs=blk,
        out_shape=jax.ShapeDtypeStruct((nb, s, D_ATTN), BF16),
        scratch_shapes=[pltpu.VMEM((3, s, LANES), F32)] * 3,
        compiler_params=_cparams(("parallel", "parallel")),
    )(q, k, v, jnp.asarray(bias2), jnp.asarray(bias1))


def _sample_bias(t_new, w_buf):
    n_new = -(-t_new // LANES) * LANES
    t = np.arange(t_new)[:, None]
    idx = np.arange(w_buf + n_new)[None, :]
    dist = w_buf + t - idx
    mult = np.zeros(dist.shape, np.int64)
    for window, d in DILATED_PATTERNS:
        mult += ((dist >= 0) & (dist % d == 0) & (dist <= window)).astype(np.int64)
    bias = np.where(mult > 0, np.log(np.maximum(mult, 1)), NEG_BIG).astype(np.float32)
    bias = np.repeat(bias, N_HEADS, axis=0)
    return bias[:, :w_buf], bias[:, w_buf:]


def _attn_sample_kernel(q_ref, kn_ref, vn_ref, kt_ref, vt_ref, bias_ref, biasn_ref, o_ref):
    t_new = q_ref.shape[0]
    n_new = biasn_ref.shape[1]
    row_head = lax.broadcasted_iota(jnp.int32, (t_new * N_HEADS, D_ATTN), 0) & (N_HEADS - 1)
    col_head = lax.broadcasted_iota(jnp.int32, (t_new * N_HEADS, D_ATTN), 1) >> (HEAD_DIM.bit_length() - 1)
    own = row_head == col_head
    q = q_ref[...]
    q_rows = jnp.concatenate([jnp.broadcast_to(q[t:t + 1], (N_HEADS, D_ATTN)) for t in range(t_new)], axis=0)
    q_bd = jnp.where(own, q_rows, 0.0).astype(BF16)
    pad = jnp.zeros((n_new - t_new, D_ATTN), F32)
    kn = jnp.concatenate([kn_ref[...], pad], axis=0).astype(BF16)
    vn = jnp.concatenate([vn_ref[...], pad], axis=0).astype(BF16)
    nt = (((1,), (1,)), ((), ()))
    s_c = jnp.dot(q_bd, kt_ref[...].astype(BF16), preferred_element_type=F32) + bias_ref[...]
    s_n = lax.dot_general(q_bd, kn, nt, preferred_element_type=F32) + biasn_ref[...]
    m = jnp.maximum(jnp.max(s_c, axis=-1, keepdims=True), jnp.max(s_n, axis=-1, keepdims=True))
    p_c = jnp.exp(s_c - m)
    p_n = jnp.exp(s_n - m)
    l = jnp.sum(p_c, axis=-1, keepdims=True) + jnp.sum(p_n, axis=-1, keepdims=True)
    o = lax.dot_general(p_c.astype(BF16), vt_ref[...].astype(BF16), nt, preferred_element_type=F32)
    o = (o + jnp.dot(p_n.astype(BF16), vn, preferred_element_type=F32)) / l
    o = jnp.where(own, o, 0.0)
    o_ref[...] = jnp.concatenate(
        [jnp.sum(o[t * N_HEADS:(t + 1) * N_HEADS], axis=0, keepdims=True) for t in range(t_new)],
        axis=0).astype(o_ref.dtype)


def _attn_sample(q, k_new, v_new, cache_k, cache_v):
    nb, t_new, _ = q.shape
    w_buf = cache_k.shape[1]
    bias, bias_new = _sample_bias(t_new, w_buf)
    as_stored = lambda a: a.transpose(0, 2, 3, 1).reshape(nb, D_ATTN, w_buf)
    new = pl.BlockSpec((None, t_new, D_ATTN), lambda b: (b, 0, 0))
    buf = pl.BlockSpec((None, D_ATTN, w_buf), lambda b: (b, 0, 0))
    const = lambda a: pl.BlockSpec(a.shape, lambda b: (0, 0))
    return pl.pallas_call(
        _attn_sample_kernel,
        grid=(nb,),
        in_specs=[new, new, new, buf, buf, const(bias), const(bias_new)],
        out_specs=new,
        out_shape=jax.ShapeDtypeStruct((nb, t_new, D_ATTN), BF16),
        compiler_params=_cparams(("parallel",)),
    )(q, k_new, v_new, as_stored(cache_k), as_stored(cache_v), jnp.asarray(bias), jnp.asarray(bias_new))


def _s5_params(lam_re, lam_im, log_dt, b_re, b_im, c_re, c_im):
    f32 = F32
    dt = jnp.exp(log_dt.astype(f32))[:, None]
    lr, li = lam_re.astype(f32), lam_im.astype(f32)
    ea = jnp.exp(lr * dt)
    a_re, a_im = ea * jnp.cos(li * dt), ea * jnp.sin(li * dt)
    den = lr * lr + li * li
    co_re = ((a_re - 1.0) * lr + a_im * li) / den
    co_im = (a_im * lr - (a_re - 1.0) * li) / den
    bb_re = co_re[..., None] * b_re - co_im[..., None] * b_im
    bb_im = co_re[..., None] * b_im + co_im[..., None] * b_re
    eye = jnp.eye(8, dtype=f32)

    def b_blocks(bb):
        t = bb.reshape(4, 8, SSM_STATE, SSM_GROUP_CH)
        return jnp.einsum('ab,kapc->kacbp', eye, t).reshape(4, 8 * SSM_GROUP_CH, 8 * SSM_STATE)

    def c_blocks(cc):
        t = cc.reshape(4, 8, SSM_GROUP_CH, SSM_STATE)
        return jnp.einsum('ab,kacp->kbpac', eye, t).reshape(4, 8 * SSM_STATE, 8 * SSM_GROUP_CH)

    b_mat = jnp.concatenate([b_blocks(bb_re), b_blocks(bb_im)], axis=2).astype(BF16)
    c_mat = jnp.concatenate([c_blocks(c_re.astype(f32)), -c_blocks(c_im.astype(f32))], axis=1).astype(BF16)
    return a_re.reshape(-1), a_im.reshape(-1), b_mat, c_mat


def _gelu_tanh(y):
    return 0.5 * y * (1.0 + jnp.tanh(math.sqrt(2.0 / math.pi) * (y + 0.044715 * (y * y * y))))


def _s5_epilogue(y, u, d_ref, wglu_ref, bglu_ref, g_ref):
    y = y + d_ref[...] * u.astype(F32)
    z = _gelu_tanh(y)
    gate = jnp.dot(z.astype(BF16), wglu_ref[...], preferred_element_type=F32) + bglu_ref[...]
    out = z * jax.nn.sigmoid(gate)
    ms = jnp.mean(out * out, axis=-1, keepdims=True)
    return out * lax.rsqrt(ms + EPS) * g_ref[...]


def _s5_prompt_kernel(u_ref, are_ref, aim_ref, b_ref, c_ref, d_ref, wglu_ref, bglu_ref, g_ref,
                      o_ref, hre_ref, him_ref, scr, hst):
    nseq, tt, _ = u_ref.shape
    n_slab = D_SSM * SSM_STATE // SSM_GROUP_CH // LANES
    ti = pl.program_id(1)

    @pl.when(ti == 0)
    def _():
        hst[...] = jnp.zeros_like(hst)

    ub = u_ref[...].reshape(nseq * tt, D_SSM)
    for kc in range(4):
        bu = jnp.dot(ub[:, kc * LANES:(kc + 1) * LANES], b_ref[kc], preferred_element_type=F32)
        for part in range(2):
            for j in range(4):
                col = part * 512 + j * LANES
                for b in range(nseq):
                    scr[part * n_slab + 4 * kc + j, b * SSM_PITCH:b * SSM_PITCH + tt, :] = (
                        bu[b * tt:(b + 1) * tt, col:col + LANES])

    grp = 4
    for sg in range(n_slab // grp):
        slabs = [sg * grp + i for i in range(grp)]
        ar = [are_ref[s] for s in slabs]
        ai = [aim_ref[s] for s in slabs]

        def step(t, carry, slabs=slabs, ar=ar, ai=ai):
            hr, hi = carry
            nhr, nhi = [], []
            for i, s in enumerate(slabs):
                sel = pl.ds(t, nseq, stride=SSM_PITCH)
                br = scr[s, sel, :]
                bi = scr[n_slab + s, sel, :]
                r = ar[i] * hr[i] - ai[i] * hi[i] + br
                im = ar[i] * hi[i] + ai[i] * hr[i] + bi
                scr[s, sel, :] = r
                scr[n_slab + s, sel, :] = im
                nhr.append(r)
                nhi.append(im)
            return tuple(nhr), tuple(nhi)

        init = (tuple(hst[s] for s in slabs), tuple(hst[n_slab + s] for s in slabs))
        hr, hi = lax.fori_loop(0, tt, step, init, unroll=4)
        for i, s in enumerate(slabs):
            hst[s] = hr[i]
            hst[n_slab + s] = hi[i]

    ys = []
    for kc in range(4):
        slabs = [4 * kc + j for j in range(4)] + [n_slab + 4 * kc + j for j in range(4)]
        lhs = jnp.concatenate(
            [jnp.concatenate([scr[s, b * SSM_PITCH:b * SSM_PITCH + tt, :].astype(BF16) for s in slabs], axis=1)
             for b in range(nseq)], axis=0)
        ys.append(jnp.dot(lhs, c_ref[kc], preferred_element_type=F32))
    y = jnp.concatenate(ys, axis=1)
    out = _s5_epilogue(y, ub, d_ref, wglu_ref, bglu_ref, g_ref)
    o_ref[...] = out.reshape(nseq, tt, D_SSM).astype(o_ref.dtype)

    @pl.when(ti == pl.num_programs(1) - 1)
    def _():
        for s in range(n_slab):
            hre_ref[:, s * LANES:(s + 1) * LANES] = hst[s]
            him_ref[:, s * LANES:(s + 1) * LANES] = hst[n_slab + s]


def _s5_prompt(u, a_re, a_im, b_mat, c_mat, d, w_glu_bf16, b_glu, g_out):
    nb, s, _ = u.shape
    nseq = SUBLANES
    assert nb % nseq == 0 and s % SSM_TT == 0
    n_state = a_re.shape[0]
    n_slab = n_state // LANES
    bcast = lambda a: jnp.broadcast_to(a.reshape(n_slab, 1, LANES), (n_slab, nseq, LANES))
    const = lambda a: pl.BlockSpec(a.shape, lambda b, i: (0,) * a.ndim)
    args = (u, bcast(a_re), bcast(a_im), b_mat, c_mat, d.reshape(1, D_SSM), w_glu_bf16,
            b_glu.reshape(1, D_SSM), g_out.reshape(1, D_SSM))
    st = jax.ShapeDtypeStruct((nb, n_state), F32)
    return pl.pallas_call(
        _s5_prompt_kernel,
        grid=(nb // nseq, s // SSM_TT),
        in_specs=[pl.BlockSpec((nseq, SSM_TT, D_SSM), lambda b, i: (b, i, 0))] + [const(a) for a in args[1:]],
        out_specs=[pl.BlockSpec((nseq, SSM_TT, D_SSM), lambda b, i: (b, i, 0)),
                   pl.BlockSpec((nseq, n_state), lambda b, i: (b, 0)),
                   pl.BlockSpec((nseq, n_state), lambda b, i: (b, 0))],
        out_shape=[jax.ShapeDtypeStruct((nb, s, D_SSM), BF16), st, st],
        scratch_shapes=[pltpu.VMEM((2 * n_slab, nseq * SSM_PITCH, LANES), F32),
                        pltpu.VMEM((2 * n_slab, nseq, LANES), F32)],
        compiler_params=_cparams(("parallel", "arbitrary")),
    )(*args)


def _s5_sample_kernel(u_ref, h0re_ref, h0im_ref, are_ref, aim_ref, b_ref, c_ref, d_ref, wglu_ref, bglu_ref, g_ref,
                      o_ref, hre_ref, him_ref):
    t_new = u_ref.shape[0]
    hre = h0re_ref[...]
    him = h0im_ref[...]
    are, aim = are_ref[...], aim_ref[...]
    half = 4 * LANES
    for t in range(t_new):
        ub = u_ref[t]
        bus = [jnp.dot(ub[:, kc * LANES:(kc + 1) * LANES], b_ref[kc], preferred_element_type=F32) for kc in range(4)]
        bre = jnp.concatenate([bu[:, :half] for bu in bus], axis=1)
        bim = jnp.concatenate([bu[:, half:] for bu in bus], axis=1)
        hre, him = are * hre - aim * him + bre, are * him + aim * hre + bim
        ys = []
        for kc in range(4):
            lhs = jnp.concatenate([hre[:, kc * half:(kc + 1) * half], him[:, kc * half:(kc + 1) * half]], axis=1)
            ys.append(jnp.dot(lhs.astype(BF16), c_ref[kc], preferred_element_type=F32))
        y = jnp.concatenate(ys, axis=1)
        o_ref[t] = _s5_epilogue(y, ub, d_ref, wglu_ref, bglu_ref, g_ref).astype(o_ref.dtype)
    hre_ref[...] = hre
    him_ref[...] = him


def _s5_sample(u_tm, h0_re, h0_im, a_re, a_im, b_mat, c_mat, d, w_glu_bf16, b_glu, g_out):
    t_new, nb, _ = u_tm.shape
    n_state = a_re.shape[0]
    tb = 64
    assert nb % tb == 0
    const = lambda a: pl.BlockSpec(a.shape, lambda b: (0,) * a.ndim)
    args = (u_tm, h0_re, h0_im, a_re.reshape(1, n_state), a_im.reshape(1, n_state), b_mat, c_mat,
            d.reshape(1, D_SSM), w_glu_bf16, b_glu.reshape(1, D_SSM), g_out.reshape(1, D_SSM))
    st_spec = pl.BlockSpec((tb, n_state), lambda b: (b, 0))
    st = jax.ShapeDtypeStruct((nb, n_state), F32)
    return pl.pallas_call(
        _s5_sample_kernel,
        grid=(nb // tb,),
        in_specs=[pl.BlockSpec((t_new, tb, D_SSM), lambda b: (0, b, 0)), st_spec, st_spec]
                 + [const(a) for a in args[3:]],
        out_specs=[pl.BlockSpec((t_new, tb, D_SSM), lambda b: (0, b, 0)), st_spec, st_spec],
        out_shape=[jax.ShapeDtypeStruct((t_new, nb, D_SSM), BF16), st, st],
        compiler_params=_cparams(("parallel",)),
    )(*args)


def _store_packed(ref, val):
    half = val.shape[1] // 2
    bits = pltpu.bitcast(val.astype(BF16).astype(F32), jnp.uint32)
    words = bits[:, :half] | lax.shift_right_logical(bits[:, half:], jnp.uint32(16))
    for c in range(PACK_CHUNKS):
        ref[c] = words[:, c * LANES:(c + 1) * LANES]


def _load_packed(ref):
    hi, lo = [], []
    for c in range(PACK_CHUNKS):
        w = ref[c]
        hi.append(pltpu.bitcast(w & jnp.uint32(0xFFFF0000), F32))
        lo.append(pltpu.bitcast(lax.shift_left(w, jnp.uint32(16)), F32))
    return jnp.concatenate(hi + lo, axis=1)


def _route_tile(lg, tri_ref, carry):
    lane = lax.broadcasted_iota(jnp.int32, lg.shape, 1)
    lane_f = lane.astype(F32)
    none = float(ROUTER_COLS)
    ninf = float("-inf")
    first = lambda cond: jnp.min(jnp.where(cond, lane_f, none), axis=-1, keepdims=True)

    is_c = lane < N_EXPERT_GROUPS
    lc = jnp.where(is_c, lg, ninf)
    mc = jnp.max(lc, axis=-1, keepdims=True)
    p_grp = 1.0 / jnp.sum(jnp.exp(lc - mc), axis=-1, keepdims=True)
    grp = first(lc == mc)
    fine = lane - N_EXPERT_GROUPS
    fine_grp = lax.shift_right_arithmetic(fine, jnp.int32(EXPERTS_PER_GROUP.bit_length() - 1))
    in_grp = (fine >= 0) & (fine < N_EXPERTS) & (fine_grp.astype(F32) == grp)
    lf = jnp.where(in_grp, lg, ninf)
    v1 = jnp.max(lf, axis=-1, keepdims=True)
    i1 = first(lf == v1)
    lf2 = jnp.where(lane_f == i1, ninf, lf)
    v2 = jnp.max(lf2, axis=-1, keepdims=True)
    i2 = first(lf2 == v2)
    b = jnp.exp(v2 - v1)
    g0 = p_grp / (1.0 + b)
    g1 = p_grp * b / (1.0 + b)

    hit1 = lane_f == i1
    hit2 = lane_f == i2
    onehot = jnp.where(hit1 | hit2, 1.0, 0.0)
    before = jnp.dot(tri_ref[...], onehot.astype(BF16), preferred_element_type=F32) + carry[0:1, :]
    r1 = jnp.sum(jnp.where(hit1, before, 0.0), axis=-1, keepdims=True)
    r2 = jnp.sum(jnp.where(hit2, before, 0.0), axis=-1, keepdims=True)
    carry[...] = carry[...] + jnp.sum(onehot, axis=0, keepdims=True)

    as_int = lambda v: jnp.broadcast_to(v, lg.shape).astype(jnp.int32)
    as_bits = lambda v: pltpu.bitcast(jnp.broadcast_to(v, lg.shape), jnp.int32)
    info = jnp.zeros(lg.shape, jnp.int32)
    fields = ((INFO_EXPERT, as_int(i1 - N_EXPERT_GROUPS)), (INFO_EXPERT + 1, as_int(i2 - N_EXPERT_GROUPS)),
              (INFO_RANK, as_int(r1)), (INFO_RANK + 1, as_int(r2)), (INFO_GATE, as_bits(g0)),
              (INFO_GATE + 1, as_bits(g1)))
    for col, val in fields:
        info = jnp.where(lane == col, val, info)
    return info


def _outproj_kernel(x_ref, oa_ref, os_ref, gate_ref, sh_ref, sc_ref, ga_ref, gf_ref, w_ref, wr_hi_ref, wr_lo_ref,
                    br_ref, tri_ref, cnt_in_ref, *rest):
    x1_ref, h_ref, info_ref, rec_ref, cnt_ref, carry = rest[-6:]
    step = pl.program_id(0) * pl.num_programs(1) + pl.program_id(1)

    @pl.when(step == 0)
    def _():
        carry[...] = cnt_in_ref[...]

    oa = oa_ref[...].astype(F32)
    ms = jnp.mean(oa * oa, axis=-1, keepdims=True)
    na = oa * lax.rsqrt(ms + EPS) * ga_ref[...]
    merged = jnp.concatenate([na.astype(BF16), os_ref[...]], axis=-1)
    x1 = x_ref[...] + gate_ref[...] * jnp.dot(merged, w_ref[...], preferred_element_type=F32)
    x1_ref[...] = x1
    ms = jnp.mean(x1 * x1, axis=-1, keepdims=True)
    h = x1 * lax.rsqrt(ms + EPS) * gf_ref[...]
    h = h * (1.0 + sc_ref[...]) + sh_ref[...]
    h_hi = h.astype(BF16)
    _store_packed(h_ref, h)
    h_lo = (h - h_hi.astype(F32)).astype(BF16)
    lg = jnp.dot(h_hi, wr_hi_ref[...], preferred_element_type=F32)
    lg = lg + jnp.dot(h_hi, wr_lo_ref[...], preferred_element_type=F32)
    lg = lg + jnp.dot(h_lo, wr_hi_ref[...], preferred_element_type=F32)
    info = _route_tile(lg + br_ref[...], tri_ref, carry)
    info_ref[...] = info
    rec_ref[...] = pltpu.bitcast(pltpu.bitcast(info, F32).T[:SUBLANES], jnp.int32)

    @pl.when(step == pl.num_programs(0) * pl.num_programs(1) - 1)
    def _():
        cnt_ref[...] = carry[...]


def _outproj(x, o_attn, o_ssm, gate, shift, scale, g_attn, g_ffn, w_out_bf16, w_router, b_router, tm, t_all, row0,
             shared, counts):
    nb, s, d = x.shape
    assert row0 % tm == 0
    blk = lambda b, i: row0 // tm + b * (s // tm) + i
    row = lambda n: pl.BlockSpec((None, tm, n), lambda b, i: (b, i, 0))
    const = lambda a: pl.BlockSpec(a.shape, lambda b, i: (0,) * a.ndim)
    wr_hi = w_router.astype(BF16)
    wr_lo = (w_router - wr_hi.astype(F32)).astype(BF16)
    tri = jnp.asarray(np.tril(np.ones((tm, tm), np.float32), -1), BF16)
    consts = (g_attn.reshape(1, D_ATTN), g_ffn.reshape(1, d), w_out_bf16, wr_hi, wr_lo,
              b_router.reshape(1, ROUTER_COLS), tri, counts)
    n_in = 6 + len(consts)
    count_spec = pl.BlockSpec((SUBLANES, ROUTER_COLS), lambda b, i: (0, 0))
    out = pl.pallas_call(
        _outproj_kernel,
        grid=(nb, s // tm),
        in_specs=[row(d), row(D_ATTN), row(D_SSM), _mod_spec(gate, tm), _mod_spec(shift, tm), _mod_spec(scale, tm)]
                 + [const(a) for a in consts] + [pl.BlockSpec(memory_space=pl.ANY)] * len(shared or ()),
        out_specs=[row(d), pl.BlockSpec((PACK_CHUNKS, tm, LANES), lambda b, i: (0, blk(b, i), 0)),
                   pl.BlockSpec((tm, ROUTER_COLS), lambda b, i: (blk(b, i), 0)),
                   pl.BlockSpec((SUBLANES, tm), lambda b, i: (0, blk(b, i))), count_spec],
        out_shape=[jax.ShapeDtypeStruct((nb, s, d), F32),
                   jax.ShapeDtypeStruct((PACK_CHUNKS, t_all, LANES), jnp.uint32),
                   jax.ShapeDtypeStruct((t_all, ROUTER_COLS), jnp.int32),
                   jax.ShapeDtypeStruct((SUBLANES, t_all), jnp.int32),
                   jax.ShapeDtypeStruct((SUBLANES, ROUTER_COLS), F32)],
        scratch_shapes=[pltpu.VMEM((SUBLANES, ROUTER_COLS), F32)],
        input_output_aliases={n_in + j: 1 + j for j in range(len(shared or ()))},
        compiler_params=_cparams(("arbitrary", "arbitrary")),
    )(x, o_attn, o_ssm, gate, shift, scale, *consts, *(shared or ()))
    return out[0], tuple(out[1:4]), out[4]


def _expert_kernel(blk_e_ref, nvalid_ref, x_ref, wg_ref, wu_ref, wd_ref, y_ref, wg_s, wu_s, wd_s):
    i = pl.program_id(0)

    @pl.when((i == 0) | (blk_e_ref[i] != blk_e_ref[jnp.maximum(i - 1, 0)]))
    def _():
        wg_s[...] = wg_ref[...].astype(BF16)
        wu_s[...] = wu_ref[...].astype(BF16)
        wd_s[...] = wd_ref[...].astype(BF16)

    @pl.when(i < nvalid_ref[0])
    def _():
        x = _load_packed(x_ref).astype(BF16)
        a = jnp.dot(x, wg_s[...], preferred_element_type=F32)
        b = jnp.dot(x, wu_s[...], preferred_element_type=F32)
        hid = (a * jax.nn.sigmoid(a) * b).astype(BF16)
        _store_packed(y_ref, jnp.dot(hid, wd_s[...], preferred_element_type=F32))

    @pl.when(i >= nvalid_ref[0])
    def _():
        y_ref[...] = jnp.zeros_like(y_ref)


def _experts(xs, blk_e, nvalid, w_gate, w_up, w_down):
    nblk = xs.shape[1] // MOE_TILE
    d = w_gate.shape[1]
    wspec = lambda shp: pl.BlockSpec((None,) + shp, lambda i, be, nv: (be[i], 0, 0))
    tile = pl.BlockSpec((PACK_CHUNKS, MOE_TILE, LANES), lambda i, be, nv: (0, i, 0))
    return pl.pallas_call(
        _expert_kernel,
        grid_spec=pltpu.PrefetchScalarGridSpec(
            num_scalar_prefetch=2,
            grid=(nblk,),
            in_specs=[tile, wspec((d, D_EXPERT)), wspec((d, D_EXPERT)), wspec((D_EXPERT, d))],
            out_specs=tile,
            scratch_shapes=[pltpu.VMEM((d, D_EXPERT), BF16), pltpu.VMEM((d, D_EXPERT), BF16),
                            pltpu.VMEM((D_EXPERT, d), BF16)],
        ),
        out_shape=jax.ShapeDtypeStruct(xs.shape, jnp.uint32),
        compiler_params=_cparams(("arbitrary",)),
    )(blk_e, nvalid, xs, w_gate, w_up, w_down)


def _combine_kernel(x1_ref, y0_ref, y1_ref, info_ref, gate_ref, gfin_ref, o_ref):
    info = pltpu.bitcast(info_ref[...], F32)
    g0 = info[:, INFO_GATE:INFO_GATE + 1]
    g1 = info[:, INFO_GATE + 1:INFO_GATE + 2]
    moe = g0 * _load_packed(y0_ref) + g1 * _load_packed(y1_ref)
    x2 = x1_ref[...] + gate_ref[...] * moe
    ms = jnp.mean(x2 * x2, axis=-1, keepdims=True)
    o_ref[...] = x2 * lax.rsqrt(ms + EPS) * gfin_ref[...]


def _combine(x1, y01, info, gate, g_final, tm, row0):
    nb, s, d = x1.shape
    assert row0 % tm == 0
    blk = lambda b, i: row0 // tm + b * (s // tm) + i
    row = lambda n: pl.BlockSpec((None, tm, n), lambda b, i: (b, i, 0))
    packed = lambda k: pl.BlockSpec((None, PACK_CHUNKS, tm, LANES), lambda b, i: (k, 0, blk(b, i), 0))
    return pl.pallas_call(
        _combine_kernel,
        grid=(nb, s // tm),
        in_specs=[row(d), packed(0), packed(1), pl.BlockSpec((tm, ROUTER_COLS), lambda b, i: (blk(b, i), 0)),
                  _mod_spec(gate, tm), pl.BlockSpec((1, d), lambda b, i: (0, 0))],
        out_specs=row(d),
        out_shape=jax.ShapeDtypeStruct((nb, s, d), F32),
        compiler_params=_cparams(("parallel", "parallel")),
    )(x1, y01, y01, info, gate, g_final.reshape(1, d))


def _sc_window(n_rows):
    assert n_rows % SC_WINDOW == 0
    return SC_WINDOW


def _sc_gather_rows(table, idx):
    n = idx.shape[0]
    w = _sc_window(n)
    mesh = plsc.VectorSubcoreMesh(core_axis_name="core", subcore_axis_name="subcore")

    @functools.partial(pl.kernel, out_type=jax.ShapeDtypeStruct((n, LANES), table.dtype), mesh=mesh)
    def gather_kernel(x_hbm, i_hbm, o_hbm):
        def body(i_vmem, o_vmem):
            pltpu.sync_copy(x_hbm.at[i_vmem.at[0]], o_vmem)

        pltpu.emit_pipeline(
            body, grid=(n // w,),
            in_specs=[pl.BlockSpec((1, w), lambda i: (0, i))],
            out_specs=[pl.BlockSpec((w, LANES), lambda i: (i, 0))],
            core_axis_name=("core", "subcore"), dimension_semantics=(pltpu.PARALLEL,),
        )(i_hbm, o_hbm)

    return gather_kernel(table, idx.reshape(1, n))


def _sc_scatter_rows(rows, idx0, idx1, n_out):
    n = rows.shape[0]
    w = _sc_window(n)
    mesh = plsc.VectorSubcoreMesh(core_axis_name="core", subcore_axis_name="subcore")

    @functools.partial(pl.kernel, out_type=jax.ShapeDtypeStruct((n_out, LANES), rows.dtype), mesh=mesh)
    def scatter_kernel(x_hbm, i0_hbm, i1_hbm, o_hbm):
        def body(x_vmem, i0_vmem, i1_vmem):
            pltpu.sync_copy(x_vmem, o_hbm.at[i0_vmem.at[0]])
            pltpu.sync_copy(x_vmem, o_hbm.at[i1_vmem.at[0]])

        pltpu.emit_pipeline(
            body, grid=(n // w,),
            in_specs=[pl.BlockSpec((w, LANES), lambda i: (i, 0)),
                      pl.BlockSpec((1, w), lambda i: (0, i)),
                      pl.BlockSpec((1, w), lambda i: (0, i))],
            out_specs=[],
            core_axis_name=("core", "subcore"), dimension_semantics=(pltpu.PARALLEL,),
        )(x_hbm, i0_hbm, i1_hbm)

    return scatter_kernel(rows, idx0.reshape(1, n), idx1.reshape(1, n))


def _moe(h_packed, rec, counts, w_gate, w_up, w_down):
    t = rec.shape[1]
    padded = (counts + MOE_TILE - 1) // MOE_TILE * MOE_TILE
    pend = jnp.cumsum(padded)
    pstart = pend - padded
    nblk = -(-2 * t // MOE_TILE) + N_EXPERTS
    n_slots = nblk * MOE_TILE
    blk_start = jnp.arange(nblk, dtype=jnp.int32) * MOE_TILE
    blk_e = jnp.minimum(jnp.sum(blk_start[:, None] >= pend[None, :], axis=1), N_EXPERTS - 1).astype(jnp.int32)
    nvalid = (pend[-1] // MOE_TILE).astype(jnp.int32).reshape(1)
    chunk_base = (jnp.arange(PACK_CHUNKS, dtype=jnp.int32) * n_slots)[:, None]
    rows = lambda k: (chunk_base + (jnp.take(pstart, rec[INFO_EXPERT + k]).astype(jnp.int32)
                                    + rec[INFO_RANK + k])[None, :]).reshape(-1)
    idx0, idx1 = rows(0), rows(1)
    xs = _sc_scatter_rows(h_packed.reshape(PACK_CHUNKS * t, LANES), idx0, idx1, PACK_CHUNKS * n_slots)
    ys = _experts(xs.reshape(PACK_CHUNKS, n_slots, LANES), blk_e, nvalid, w_gate, w_up, w_down)
    y01 = _sc_gather_rows(ys.reshape(PACK_CHUNKS * n_slots, LANES), jnp.concatenate([idx0, idx1]))
    return y01.reshape(2, PACK_CHUNKS, t, LANES)


def _mixer(x, mods_mix, mods_ffn, attn_fn, s5_fn, wts, tm, t_all, row0, shared, counts):
    q, k, v, u = _inproj(x, mods_mix[0], mods_mix[1], wts['g_mix'], wts['w_in'], tm)
    o_attn = attn_fn(q, k, v)
    o_ssm, h_re, h_im = s5_fn(u)
    x1, shared, counts = _outproj(x, o_attn, o_ssm, mods_mix[2], mods_ffn[0], mods_ffn[1], wts['g_attn_out'],
                                  wts['g_ffn'], wts['w_out'], wts['w_router'], wts['b_router'], tm, t_all, row0,
                                  shared, counts)
    return x1, shared, counts, (k, v, h_re, h_im)


def kernel(x_prompt, x_sample, cache_k_win, cache_v_win, state_ssm_re, state_ssm_im, c_prompt, c_sample, g_mix, w_ada_mix, b_ada_mix, w_in, w_out, g_attn_out, g_ssm_out, ssm_lambda_re, ssm_lambda_im, ssm_log_dt, ssm_b_re, ssm_b_im, ssm_c_re, ssm_c_im, ssm_d, w_glu, b_glu, g_ffn, w_ada_ffn, b_ada_ffn, w_router_coarse, b_router_coarse, w_router_fine, b_router_fine, w_expert_gate, w_expert_up, w_expert_down, g_final):
    depth = g_mix.shape[0]
    assert depth == 1, "single-layer step"
    l = 0
    bp, sp, d = x_prompt.shape
    bs, ts, _ = x_sample.shape
    keep = min(max(w for w, _ in DILATED_PATTERNS), sp)

    c_all = jnp.concatenate([c_prompt, c_sample], axis=0).astype(F32)
    m_mix = _adaln(c_all, w_ada_mix[l], b_ada_mix[l])
    m_ffn = _adaln(c_all, w_ada_ffn[l], b_ada_ffn[l])

    def mods(m, lo, hi, per_token):
        parts = jnp.split(m[lo:hi], 3, axis=-1)
        if per_token:
            return tuple(jnp.repeat(p, ts, axis=0)[None] for p in parts)
        return tuple(p[:, None, :] for p in parts)

    pad = ROUTER_COLS - N_EXPERT_GROUPS - N_EXPERTS
    w_router = jnp.concatenate([w_router_coarse[l], w_router_fine[l], jnp.zeros((d, pad), F32)], axis=1)
    b_router = jnp.concatenate([b_router_coarse[l], b_router_fine[l], jnp.zeros((pad,), F32)])
    wts = {
        'g_mix': g_mix[l], 'w_in': w_in[l].astype(BF16), 'w_out': w_out[l].astype(BF16),
        'g_attn_out': g_attn_out[l], 'g_ffn': g_ffn[l], 'w_router': w_router, 'b_router': b_router,
        'w_gate': w_expert_gate[l], 'w_up': w_expert_up[l], 'w_down': w_expert_down[l],
    }
    a_re, a_im, b_mat, c_mat = _s5_params(ssm_lambda_re[l], ssm_lambda_im[l], ssm_log_dt[l], ssm_b_re[l],
                                          ssm_b_im[l], ssm_c_re[l], ssm_c_im[l])
    s5_w = (a_re, a_im, b_mat, c_mat, ssm_d[l].reshape(-1), w_glu[l].astype(BF16), b_glu[l], g_ssm_out[l])

    t_prompt, t_sample = bp * sp, bs * ts
    t_all = t_prompt + t_sample
    tm_p, tm_s = min(sp, ROW_TILE), min(t_sample, ROW_TILE)
    mods_ffn_p, mods_ffn_s = mods(m_ffn, 0, bp, False), mods(m_ffn, bp, bp + bs, True)

    x1p, shared, counts, (kp, vp, hrp, hip) = _mixer(
        x_prompt, mods(m_mix, 0, bp, False), mods_ffn_p, _attn_prompt, lambda u: _s5_prompt(u, *s5_w), wts,
        tm_p, t_all, 0, None, jnp.zeros((SUBLANES, ROUTER_COLS), F32))

    n_state = SSM_GROUPS * SSM_STATE
    ck = cache_k_win[l]
    cv = cache_v_win[l]

    def attn_s(q, k, v):
        r = lambda a: a.reshape(bs, ts, D_ATTN)
        return _attn_sample(r(q), r(k), r(v), ck, cv).reshape(1, bs * ts, D_ATTN)

    def s5_s(u):
        u_tm = u.reshape(bs, ts, D_SSM).transpose(1, 0, 2)
        o, hr, hi = _s5_sample(u_tm, state_ssm_re[l].reshape(bs, n_state), state_ssm_im[l].reshape(bs, n_state),
                               *s5_w)
        return o.transpose(1, 0, 2).reshape(1, bs * ts, D_SSM), hr, hi

    x1s, shared, counts, (ks, vs, hrs, his) = _mixer(
        x_sample.reshape(1, t_sample, d), mods(m_mix, bp, bp + bs, True), mods_ffn_s, attn_s, s5_s, wts,
        tm_s, t_all, t_prompt, shared, counts)

    h_packed, info, rec = shared
    counts = counts[0, N_EXPERT_GROUPS:N_EXPERT_GROUPS + N_EXPERTS].astype(jnp.int32)
    y01 = _moe(h_packed, rec, counts, wts['w_gate'], wts['w_up'], wts['w_down'])
    yp = _combine(x1p, y01, info, mods_ffn_p[2], g_final, tm_p, 0)
    ys = _combine(x1s, y01, info, mods_ffn_s[2], g_final, tm_s, t_prompt)

    heads = lambda a, b, s: a.reshape(1, b, s, N_HEADS, HEAD_DIM)
    state = lambda a, b: a.reshape(1, b, SSM_GROUPS, SSM_STATE)
    return (yp, ys.reshape(bs, ts, d),
            heads(kp[:, sp - keep:], bp, keep), heads(vp[:, sp - keep:], bp, keep), state(hrp, bp), state(hip, bp),
            heads(ks, bs, ts), heads(vs, bs, ts), state(hrs, bs), state(his, bs))
```
